```python
import functools
import jax
import jax.numpy as jnp
from jax import lax
import numpy as np

D_MODEL = 1024
BATCH = 16
SEQ = 256
DEPTH = 4
DEC_BATCH = 2
DEC_SEQ = 1024
PAST_LEN = 256

GRID_W = 64
ROPE_BASE = 10000.0
NORM_EPS = 1e-6
QBLK = 128

GLA_HEADS = 4
GLA_DK = 32
GLA_DV = 64
GLA_GATE_RANK = 16
GLA_GATE_NORM = 16.0
GLA_CHUNK = 64

MLA_HEADS = 4
MLA_Q_RANK = 256
MLA_KV_RANK = 128
MLA_NOPE = 64
MLA_ROPE = 32
MLA_DV = 64

RET_HEADS = 4
RET_DK = 64
RET_DV = 64
RET_CHUNK = 64

GQA_HEADS = 4
GQA_KV_HEADS = 2
GQA_HD = 64
WINDOW = 128
WBLK = 128

MIX_W = GLA_HEADS * GLA_DV + MLA_HEADS * MLA_DV + RET_HEADS * RET_DV + GQA_HEADS * GQA_HD
IN_SPLITS = (
    GLA_HEADS * GLA_DK, GLA_HEADS * GLA_DK, GLA_HEADS * GLA_DV, GLA_HEADS * GLA_DV, GLA_GATE_RANK, GLA_GATE_RANK,
    MLA_Q_RANK, MLA_KV_RANK, MLA_ROPE,
    RET_HEADS * RET_DK, RET_HEADS * RET_DK, RET_HEADS * RET_DV, RET_HEADS * RET_DV,
    GQA_HEADS * GQA_HD, GQA_KV_HEADS * GQA_HD, GQA_KV_HEADS * GQA_HD,
)
IN_WIDTH = sum(IN_SPLITS)

D_FF = 2816
N_EXPERTS = 8
TOP_K = 2
D_FF_EXPERT = 3584
N_DENSE = (DEPTH + 1) // 2
N_MOE = DEPTH // 2

kernel_name = 'hybrid_prefix_diffusion_step'


def rmsnorm(x, g):
    xf = x.astype(jnp.float32)
    y = xf * lax.rsqrt(jnp.mean(xf * xf, axis=-1, keepdims=True) + NORM_EPS)
    return (y * g.astype(jnp.float32)).astype(x.dtype)


def head_rms(x):
    xf = x.astype(jnp.float32)
    return xf * lax.rsqrt(jnp.mean(xf * xf, axis=-1, keepdims=True) + NORM_EPS)


def split_heads(x, n):
    b, t, _ = x.shape
    return x.reshape(b, t, n, -1).transpose(0, 2, 1, 3)


def merge_heads(x):
    b, n, t, d = x.shape
    return x.transpose(0, 2, 1, 3).reshape(b, t, n * d)


def axial_rope(x):
    t, dr = x.shape[-2], x.shape[-1]
    n_rows = t // GRID_W
    rows = jnp.repeat(jnp.arange(n_rows, dtype=jnp.float32), GRID_W)
    cols = (jnp.arange(t) % GRID_W).astype(jnp.float32)
    quarter = dr // 4
    inv = jnp.power(ROPE_BASE, -jnp.arange(quarter, dtype=jnp.float32) / quarter)
    ang = jnp.concatenate([rows[:, None] * inv, cols[:, None] * inv], axis=-1)
    cos, sin = jnp.cos(ang), jnp.sin(ang)
    xf = x.astype(jnp.float32)
    x1, x2 = xf[..., : dr // 2], xf[..., dr // 2:]
    return jnp.concatenate([x1 * cos - x2 * sin, x1 * sin + x2 * cos], axis=-1).astype(x.dtype)


def softmax_with_sink(s, sink):
    if sink is None:
        return jax.nn.softmax(s, axis=-1)
    sk = sink.astype(jnp.float32).reshape((1,) + sink.shape + (1,) * (s.ndim - 3))
    m = jnp.maximum(jnp.max(s, axis=-1, keepdims=True), sk)
    p = jnp.exp(s - m)
    return p / (jnp.sum(p, axis=-1, keepdims=True) + jnp.exp(sk - m))


def dense_attention(q, k, v, sink):
    b, hk, g, tq, d = q.shape
    scale = d ** -0.5
    nq = tq // QBLK
    qb = jnp.moveaxis(q.reshape(b, hk, g, nq, QBLK, d), 3, 0)

    def one_block(q_blk):
        s = jnp.einsum('bkgqd,bktd->bkgqt', q_blk, k, preferred_element_type=jnp.float32) * scale
        p = softmax_with_sink(s, sink).astype(v.dtype)
        return jnp.einsum('bkgqt,bktv->bkgqv', p, v)

    o = lax.map(one_block, qb)
    return jnp.moveaxis(o, 0, 3).reshape(b, hk, g, tq, v.shape[-1])


def window_attention(q, k, v, kc, vc, sink):
    b, hk, g, t, d = q.shape
    scale = d ** -0.5
    nb = t // WBLK
    qb = q.reshape(b, hk, g, nb, WBLK, d)

    def band(x):
        xp = jnp.pad(x, ((0, 0), (0, 0), (WBLK, WBLK), (0, 0))).reshape(b, hk, nb + 2, WBLK, x.shape[-1])
        return jnp.concatenate([xp[:, :, :-2], xp[:, :, 1:-1], xp[:, :, 2:]], axis=3)

    kb, vb = band(k), band(v)
    s_loc = jnp.einsum('bkgnqd,bknjd->bkgnqj', qb, kb, preferred_element_type=jnp.float32) * scale
    qpos = jnp.arange(nb)[:, None, None] * WBLK + jnp.arange(WBLK)[None, :, None]
    kpos = (jnp.arange(nb)[:, None, None] - 1) * WBLK + jnp.arange(3 * WBLK)[None, None, :]
    valid = (jnp.abs(qpos - kpos) <= WINDOW) & (kpos >= 0) & (kpos < t)
    s_loc = jnp.where(valid, s_loc, -jnp.inf)
    s_ctx = jnp.einsum('bkgnqd,bkld->bkgnql', qb, kc, preferred_element_type=jnp.float32) * scale
    p = softmax_with_sink(jnp.concatenate([s_loc, s_ctx], axis=-1), sink).astype(v.dtype)
    o = (jnp.einsum('bkgnqj,bknjv->bkgnqv', p[..., : 3 * WBLK], vb)
         + jnp.einsum('bkgnql,bklv->bkgnqv', p[..., 3 * WBLK:], vc))
    return o.reshape(b, hk, g, t, v.shape[-1])


def chunked_recurrence(q, k, v, log_a, s0, chunk):
    b, h, t, dk = q.shape
    dv = v.shape[-1]
    n = t // chunk
    f32 = jnp.float32
    qc = q.astype(f32).reshape(b, h, n, chunk, dk)
    kc = k.astype(f32).reshape(b, h, n, chunk, dk)
    vc = v.astype(f32).reshape(b, h, n, chunk, dv)
    bc = jnp.cumsum(log_a.astype(f32).reshape(b, h, n, chunk, -1), axis=3)
    b_last = bc[:, :, :, -1:, :]
    causal = jnp.tril(jnp.ones((chunk, chunk), dtype=bool))
    diff = bc[:, :, :, :, None, :] - bc[:, :, :, None, :, :]
    dec = jnp.exp(jnp.where(causal[:, :, None], diff, -jnp.inf))
    if log_a.shape[-1] == 1:
        att = jnp.einsum('bhnid,bhnjd->bhnij', qc, kc) * dec[..., 0]
    else:
        att = jnp.einsum('bhnid,bhnjd,bhnijd->bhnij', qc, kc, dec)
    o_intra = jnp.einsum('bhnij,bhnjv->bhniv', att, vc)
    q_in = qc * jnp.exp(bc)
    kv_chunk = jnp.einsum('bhnjd,bhnjv->bhndv', kc * jnp.exp(b_last - bc), vc)
    a_chunk = jnp.exp(b_last[:, :, :, 0, :])

    def step(s, inp):
        q_n, kv_n, a_n = inp
        o_n = jnp.einsum('bhid,bhdv->bhiv', q_n, s)
        return a_n[..., None] * s + kv_n, o_n

    xs = (jnp.moveaxis(q_in, 2, 0), jnp.moveaxis(kv_chunk, 2, 0), jnp.moveaxis(a_chunk, 2, 0))
    s_t, o_inter = lax.scan(step, s0.astype(f32), xs)
    o = o_intra + jnp.moveaxis(o_inter, 0, 2)
    return o.reshape(b, h, t, dv), s_t


def bidir_recurrence(q, k, v, la_f, la_b, s0_f, s0_b, chunk):
    o_f, s_f = chunked_recurrence(q, k, v, la_f, s0_f, chunk)
    flip = lambda a: jnp.flip(a, axis=2)
    o_b, s_b = chunked_recurrence(flip(q), flip(k), flip(v), flip(la_b), s0_b, chunk)
    return o_f + flip(o_b), s_f, s_b


def mla_keys(ckv, krope, w_kv_up):
    kv = split_heads(ckv @ w_kv_up, MLA_HEADS)
    b, h, t, _ = kv.shape
    k = jnp.concatenate([kv[..., :MLA_NOPE], jnp.broadcast_to(krope[:, None], (b, h, t, MLA_ROPE))], axis=-1)
    return k, kv[..., MLA_NOPE:]


def token_mixer(h, lp, ctx):
    latent = ctx is not None
    b, t, _ = h.shape
    f32 = jnp.float32
    pts = []
    acc = 0
    for s in IN_SPLITS[:-1]:
        acc += s
        pts.append(acc)
    (gq, gk, gv, gog, glr_f, glr_b, mcq, mckv, mkr,
     rq, rk, rv, rg, aq, ak, av) = jnp.split(h @ lp['w_in'], pts, axis=-1)

    la_gla = [split_heads(jax.nn.log_sigmoid((lr @ lp['gla_gate_w'][d] + lp['gla_gate_b'][d]).astype(f32))
                          / GLA_GATE_NORM, GLA_HEADS) for d, lr in enumerate((glr_f, glr_b))]
    s0 = ctx['gla'] if latent else jnp.zeros((b, 2, GLA_HEADS, GLA_DK, GLA_DV), f32)
    o, sg_f, sg_b = bidir_recurrence(split_heads(gq, GLA_HEADS), split_heads(gk, GLA_HEADS) * GLA_DK ** -0.5,
                                     split_heads(gv, GLA_HEADS), la_gla[0], la_gla[1], s0[:, 0], s0[:, 1], GLA_CHUNK)
    o_gla = merge_heads(head_rms(o)).astype(h.dtype) * jax.nn.silu(gog)

    qh = split_heads(rmsnorm(mcq, lp['mla_q_norm_g']) @ lp['mla_w_q_up'], MLA_HEADS)
    q_nope, q_rope = qh[..., :MLA_NOPE], qh[..., MLA_NOPE:]
    ckv = rmsnorm(mckv, lp['mla_kv_norm_g'])
    krope = mkr
    if latent:
        q_rope, krope = axial_rope(q_rope), axial_rope(krope)
    q_m = jnp.concatenate([q_nope, q_rope], axis=-1)
    k_m, v_m = mla_keys(ckv, krope, lp['mla_w_kv_up'])
    if latent:
        k_c, v_c = mla_keys(ctx['ckv'], ctx['krope'], lp['mla_w_kv_up'])
        k_m = jnp.concatenate([k_m, k_c], axis=2)
        v_m = jnp.concatenate([v_m, v_c], axis=2)
    o_mla = merge_heads(dense_attention(q_m[:, :, None], k_m, v_m, None)[:, :, 0])

    rqh, rkh = split_heads(rq, RET_HEADS), split_heads(rk, RET_HEADS)
    if latent:
        rqh, rkh = axial_rope(rqh), axial_rope(rkh)
    lg = jax.nn.log_sigmoid(lp['ret_decay'].astype(f32))
    la_ret = [jnp.broadcast_to(lg[d][None, :, None, None], (b, RET_HEADS, t, 1)) for d in range(2)]
    s0 = ctx['ret'] if latent else jnp.zeros((b, 2, RET_HEADS, RET_DK, RET_DV), f32)
    o, sr_f, sr_b = bidir_recurrence(rqh, rkh * RET_DK ** -0.5, split_heads(rv, RET_HEADS),
                                     la_ret[0], la_ret[1], s0[:, 0], s0[:, 1], RET_CHUNK)
    o_ret = merge_heads(head_rms(o)).astype(h.dtype) * jax.nn.silu(rg)

    qa, ka, va = split_heads(aq, GQA_HEADS), split_heads(ak, GQA_KV_HEADS), split_heads(av, GQA_KV_HEADS)
    if latent:
        qa, ka = axial_rope(qa), axial_rope(ka)
    grp = GQA_HEADS // GQA_KV_HEADS
    qa = qa.reshape(b, GQA_KV_HEADS, grp, t, GQA_HD)
    sink = lp['gqa_sink'].reshape(GQA_KV_HEADS, grp)
    if latent:
        o = window_attention(qa, ka, va, ctx['gk'], ctx['gv'], sink)
    else:
        o = dense_attention(qa, ka, va, sink)
    o_gqa = merge_heads(o.reshape(b, GQA_HEADS, t, GQA_HD))

    out = jnp.concatenate([o_gla, o_mla, o_ret, o_gqa], axis=-1) @ lp['w_out']
    if latent:
        return out, None
    new_ctx = {'gla': jnp.stack([sg_f, sg_b], axis=1), 'ret': jnp.stack([sr_f, sr_b], axis=1),
               'ckv': ckv, 'krope': krope, 'gk': ka, 'gv': va}
    return out, new_ctx


def swiglu(h, wg, wu, wd):
    return (jax.nn.silu(h @ wg) * (h @ wu)) @ wd


def moe_swiglu(h, router, wg, wu, wd):
    logits = (h @ router).astype(jnp.float32)
    top_v, top_i = lax.top_k(logits, TOP_K)
    w = jax.nn.softmax(top_v, axis=-1)
    gate = jnp.sum(jax.nn.one_hot(top_i, N_EXPERTS, dtype=jnp.float32) * w[..., None], axis=-2).astype(h.dtype)
    y = jnp.zeros_like(h)
    for e in range(N_EXPERTS):
        y = y + gate[..., e:e + 1] * swiglu(h, wg[e], wu[e], wd[e])
    return y


def modulation(cond, w_mod, b_mod):
    m = (jax.nn.silu(cond) @ w_mod + b_mod).reshape(-1, 1, 6 * D_MODEL)
    return jnp.split(m, 6, axis=-1)


def trunk_layer(x, mod, g1, g2, lp, ffn, ctx):
    sh1, sc1, gt1, sh2, sc2, gt2 = mod
    mix, new_ctx = token_mixer(rmsnorm(x, g1) * (1 + sc1) + sh1, lp, ctx)
    x = x + gt1 * mix
    x = x + gt2 * ffn(rmsnorm(x, g2) * (1 + sc2) + sh2)
    return x, new_ctx


def setup_inputs(seed: int = 0) -> dict:
    key = jax.random.key(seed)
    keys = iter(jax.random.split(key, 48))
    f32 = jnp.float32
    D = D_MODEL

    def nrm(shape, scale=1.0):
        return jax.random.normal(next(keys), shape, f32) * scale

    def gain(shape):
        return 1.0 + nrm(shape, 0.02)

    ret_base = jnp.log(jnp.exp2(5.0 + jnp.arange(RET_HEADS, dtype=f32)) - 1.0)
    return {
        'x_prompt': nrm((BATCH, SEQ, D)),
        'x_sample': nrm((DEC_BATCH, DEC_SEQ, D)),
        'state_gla': nrm((DEC_BATCH, DEPTH, 2, GLA_HEADS, GLA_DK, GLA_DV), 0.5),
        'state_ret': nrm((DEC_BATCH, DEPTH, 2, RET_HEADS, RET_DK, RET_DV), 0.5),
        'cache_mla_ckv': nrm((DEC_BATCH, DEPTH, PAST_LEN, MLA_KV_RANK)),
        'cache_mla_krope': nrm((DEC_BATCH, DEPTH, PAST_LEN, MLA_ROPE)),
        'cache_gqa_k': nrm((DEC_BATCH, DEPTH, GQA_KV_HEADS, PAST_LEN, GQA_HD)),
        'cache_gqa_v': nrm((DEC_BATCH, DEPTH, GQA_KV_HEADS, PAST_LEN, GQA_HD)),
        'c': nrm((DEC_BATCH, D)),
        'c_ctx': nrm((D,)),
        'norm1_g': gain((DEPTH, D)),
        'norm2_g': gain((DEPTH, D)),
        'final_norm_g': gain((D,)),
        'w_mod': nrm((DEPTH, D, 6 * D), 0.5 * D ** -0.5),
        'b_mod': nrm((DEPTH, 6 * D), 0.01),
        'w_in': nrm((DEPTH, D, IN_WIDTH), D ** -0.5),
        'w_out': nrm((DEPTH, MIX_W, D), MIX_W ** -0.5),
        'gla_gate_w': nrm((DEPTH, 2, GLA_GATE_RANK, GLA_HEADS * GLA_DK), GLA_GATE_RANK ** -0.5),
        'gla_gate_b': nrm((DEPTH, 2, GLA_HEADS * GLA_DK), 0.1),
        'mla_q_norm_g': gain((DEPTH, MLA_Q_RANK)),
        'mla_w_q_up': nrm((DEPTH, MLA_Q_RANK, MLA_HEADS * (MLA_NOPE + MLA_ROPE)), MLA_Q_RANK ** -0.5),
        'mla_kv_norm_g': gain((DEPTH, MLA_KV_RANK)),
        'mla_w_kv_up': nrm((DEPTH, MLA_KV_RANK, MLA_HEADS * (MLA_NOPE + MLA_DV)), MLA_KV_RANK ** -0.5),
        'ret_decay': ret_base + nrm((DEPTH, 2, RET_HEADS), 0.1),
        'gqa_sink': nrm((DEPTH, GQA_HEADS), 0.5),
        'ffn_w_gate': nrm((N_DENSE, D, D_FF), D ** -0.5),
        'ffn_w_up': nrm((N_DENSE, D, D_FF), D ** -0.5),
        'ffn_w_down': nrm((N_DENSE, D_FF, D), D_FF ** -0.5),
        'moe_router': nrm((N_MOE, D, N_EXPERTS), D ** -0.5),
        'moe_w_gate': nrm((N_MOE, N_EXPERTS, D, D_FF_EXPERT), D ** -0.5),
        'moe_w_up': nrm((N_MOE, N_EXPERTS, D, D_FF_EXPERT), D ** -0.5),
        'moe_w_down': nrm((N_MOE, N_EXPERTS, D_FF_EXPERT, D), D_FF_EXPERT ** -0.5),
    }


def reference(x_prompt, x_sample, state_gla, state_ret, cache_mla_ckv, cache_mla_krope, cache_gqa_k, cache_gqa_v,
              c, c_ctx, norm1_g, norm2_g, final_norm_g, w_mod, b_mod, w_in, w_out, gla_gate_w, gla_gate_b,
              mla_q_norm_g, mla_w_q_up, mla_kv_norm_g, mla_w_kv_up, ret_decay, gqa_sink,
              ffn_w_gate, ffn_w_up, ffn_w_down, moe_router, moe_w_gate, moe_w_up, moe_w_down):
    xp, xs = x_prompt, x_sample
    gla_l, ret_l, ckv_l, kr_l, gk_l, gv_l = [], [], [], [], [], []
    for l in range(DEPTH):
        lp = {'w_in': w_in[l], 'w_out': w_out[l], 'gla_gate_w': gla_gate_w[l], 'gla_gate_b': gla_gate_b[l],
              'mla_q_norm_g': mla_q_norm_g[l], 'mla_w_q_up': mla_w_q_up[l], 'mla_kv_norm_g': mla_kv_norm_g[l],
              'mla_w_kv_up': mla_w_kv_up[l], 'ret_decay': ret_decay[l], 'gqa_sink': gqa_sink[l]}
        j = l // 2
        if l % 2 == 0:
            ffn = functools.partial(swiglu, wg=ffn_w_gate[j], wu=ffn_w_up[j], wd=ffn_w_down[j])
        else:
            ffn = functools.partial(moe_swiglu, router=moe_router[j], wg=moe_w_gate[j], wu=moe_w_up[j],
                                    wd=moe_w_down[j])
        xp, ctx = trunk_layer(xp, modulation(c_ctx, w_mod[l], b_mod[l]), norm1_g[l], norm2_g[l], lp, ffn, None)
        gla_l.append(ctx['gla'].astype(x_prompt.dtype))
        ret_l.append(ctx['ret'].astype(x_prompt.dtype))
        ckv_l.append(ctx['ckv'])
        kr_l.append(ctx['krope'])
        gk_l.append(ctx['gk'])
        gv_l.append(ctx['gv'])
        cache = {'gla': state_gla[:, l], 'ret': state_ret[:, l], 'ckv': cache_mla_ckv[:, l],
                 'krope': cache_mla_krope[:, l], 'gk': cache_gqa_k[:, l], 'gv': cache_gqa_v[:, l]}
        xs, _ = trunk_layer(xs, modulation(c, w_mod[l], b_mod[l]), norm1_g[l], norm2_g[l], lp, ffn, cache)
    y_prompt = rmsnorm(xp, final_norm_g)
    y_sample = rmsnorm(xs, final_norm_g)
    new_state_gla = jnp.stack(gla_l, axis=1)
    new_state_ret = jnp.stack(ret_l, axis=1)
    new_cache_mla_ckv = jnp.stack(ckv_l, axis=1)
    new_cache_mla_krope = jnp.stack(kr_l, axis=1)
    new_cache_gqa_k = jnp.stack(gk_l, axis=1)
    new_cache_gqa_v = jnp.stack(gv_l, axis=1)
    return (y_prompt, y_sample, new_state_gla, new_state_ret, new_cache_mla_ckv, new_cache_mla_krope,
            new_cache_gqa_k, new_cache_gqa_v)
```

```python
import functools

import numpy as np
import jax
import jax.numpy as jnp
from jax import lax
from jax.experimental import pallas as pl
from jax.experimental.pallas import tpu as pltpu

F32 = jnp.float32
BF16 = jnp.bfloat16
HIGHEST = lax.Precision.HIGHEST

D = 1024
N_CTX, T_CTX = 16, 256
N_LAT, T_LAT = 2, 1024
PAST = 256
DEPTH = 4
M_CTX = N_CTX * T_CTX
M_LAT = N_LAT * T_LAT
M_ALL = M_CTX + M_LAT
GRID_W = 64
ROPE_BASE = 10000.0
EPS = 1e-6

GLA_H, GLA_DK, GLA_DV, GLA_RANK, GLA_NORM, GLA_C = 4, 32, 64, 16, 16.0, 64
MLA_H, MLA_QR, MLA_KVR, MLA_NOPE, MLA_ROPE, MLA_DV = 4, 256, 128, 64, 32, 64
RET_H, RET_DK, RET_DV = 4, 64, 64
GQA_H, GQA_KV, GQA_HD, WINDOW = 4, 2, 64, 128
D_FF, N_EXP, D_FFE = 2816, 8, 3584

W_GLA, W_MLA, W_RET, W_GQA = 896, 512, 1024, 512
LANE = 128
NEG = -1e30
QB = 256
VMEM_LIMIT = 56 * 1024 * 1024


def _cparams(*sem):
    return pltpu.CompilerParams(dimension_semantics=sem, vmem_limit_bytes=VMEM_LIMIT)


def _mm(a, b):
    return jnp.dot(a.astype(BF16), b.astype(BF16), preferred_element_type=F32)


def _mm_nt(a, b):
    return lax.dot_general(a.astype(BF16), b.astype(BF16), (((1,), (1,)), ((), ())), preferred_element_type=F32)


def _mm_tn(a, b):
    return lax.dot_general(a.astype(BF16), b.astype(BF16), (((0,), (0,)), ((), ())), preferred_element_type=F32)


def _mm_f32(a, b):
    return jnp.dot(a, b, precision=HIGHEST, preferred_element_type=F32)


def _silu(x):
    return x * (1.0 / (1.0 + jnp.exp(-x)))


def _log_sigmoid(x):
    return jnp.minimum(x, 0.0) - jnp.log1p(jnp.exp(-jnp.abs(x)))


def _rms(x):
    return x * lax.rsqrt(jnp.mean(x * x, axis=-1, keepdims=True) + EPS)


def _mod_row(tile, tm):
    return jnp.maximum((tile * tm) // T_LAT - (M_CTX // T_LAT - 1), 0)


def _mod_kernel(c_ref, w_ref, b_ref, o_ref):
    o_ref[...] = _mm(_silu(c_ref[...]), w_ref[...]) + b_ref[...]


def _modulation(cond, w_mod, b_mod):
    tn = 1536
    return pl.pallas_call(
        _mod_kernel,
        out_shape=jax.ShapeDtypeStruct((DEPTH, 8, 6 * D), F32),
        grid=(DEPTH, 6 * D // tn),
        in_specs=[pl.BlockSpec((8, D), lambda l, j: (0, 0)),
                  pl.BlockSpec((None, D, tn), lambda l, j: (l, 0, j)),
                  pl.BlockSpec((None, 1, tn), lambda l, j: (l, 0, j))],
        out_specs=pl.BlockSpec((None, 8, tn), lambda l, j: (l, 0, j)),
        compiler_params=_cparams("parallel", "parallel"),
        name="modulation",
    )(cond, w_mod, b_mod.reshape(DEPTH, 1, 6 * D))


def _inproj_kernel(x_ref, g_ref, sh_ref, sc_ref, wa_ref, wb_ref, wc_ref, wd_ref, oa_ref, ob_ref, oc_ref, od_ref):
    h = (_rms(x_ref[...]) * g_ref[...] * (1.0 + sc_ref[...]) + sh_ref[...]).astype(BF16)
    oa_ref[...] = jnp.dot(h, wa_ref[...], preferred_element_type=F32)
    ob_ref[...] = jnp.dot(h, wb_ref[...], preferred_element_type=F32)
    oc_ref[...] = jnp.dot(h, wc_ref[...], preferred_element_type=F32)
    od_ref[...] = jnp.dot(h, wd_ref[...], preferred_element_type=F32)


def _mod_spec(j, tm):
    return pl.BlockSpec((None, 1, D), lambda i: (_mod_row(i, tm) * 6 + j, 0, 0))


def _inproj(x, g1, modl, wa, wb, wc, wd):
    tm = 512
    full = lambda w: pl.BlockSpec((D, w), lambda i: (0, 0))
    rows = lambda w: pl.BlockSpec((tm, w), lambda i: (i, 0))
    widths = (W_GLA, W_MLA, W_RET, W_GQA)
    return pl.pallas_call(
        _inproj_kernel,
        out_shape=[jax.ShapeDtypeStruct((M_ALL, w), F32) for w in widths],
        grid=(M_ALL // tm,),
        in_specs=[rows(D), pl.BlockSpec((1, D), lambda i: (0, 0)), _mod_spec(0, tm), _mod_spec(1, tm)]
        + [full(w) for w in widths],
        out_specs=[rows(w) for w in widths],
        compiler_params=_cparams("parallel"),
        name="inproj",
    )(x, g1, modl, modl, wa, wb, wc, wd)


def _head_rms_gate(o, gate):
    r = lax.broadcasted_iota(jnp.int32, (256, 256), 0) // 64
    c = lax.broadcasted_iota(jnp.int32, (256, 256), 1) // 64
    group_mean = jnp.where(r == c, 1.0 / 64.0, 0.0).astype(F32)
    ms = _mm_f32(o * o, group_mean)
    return o * lax.rsqrt(ms + EPS) * _silu(gate)


def _rope(x, cos, sin_lo, sin_hi, half):
    w = x.shape[-1]
    return x * cos + pltpu.roll(x, w - half, 1) * sin_lo + pltpu.roll(x, half, 1) * sin_hi


def _rope_tables(t, head_dim, width):
    half = head_dim // 2
    quarter = head_dim // 4
    pos = np.arange(t)
    rows = (pos // GRID_W).astype(np.float32)
    cols = (pos % GRID_W).astype(np.float32)
    inv = np.power(np.float32(ROPE_BASE), -np.arange(quarter, dtype=np.float32) / np.float32(quarter)).astype(np.float32)
    ang = np.concatenate([rows[:, None] * inv, cols[:, None] * inv], axis=-1).astype(np.float32)
    lane = np.arange(width)
    a = ang[:, lane % half]
    cos, sin = np.cos(a).astype(np.float32), np.sin(a).astype(np.float32)
    low = (lane % head_dim) < half
    return (jnp.asarray(cos), jnp.asarray(np.where(low[None], -sin, 0.0).astype(np.float32)),
            jnp.asarray(np.where(low[None], 0.0, sin).astype(np.float32)))


def _gla_kernel(t, latent, *refs):
    if latent:
        p_ref, gw_ref, gb_ref, s0_ref, o_ref, la_f, la_b, of_s, ob_s = refs
    else:
        p_ref, gw_ref, gb_ref, o_ref, st_ref, la_f, la_b, of_s, ob_s = refs
    c = GLA_C
    n = t // c
    la_f[...] = _log_sigmoid(_mm_f32(p_ref[:, 768:784], gw_ref[0]) + gb_ref[0]) / GLA_NORM
    la_b[...] = _log_sigmoid(_mm_f32(p_ref[:, 784:800], gw_ref[1]) + gb_ref[1]) / GLA_NORM

    ri = lax.broadcasted_iota(jnp.int32, (c, c), 0)
    ci = lax.broadcasted_iota(jnp.int32, (c, c), 1)
    scale = GLA_DK ** -0.5

    def chunk(row0, la_ref, fwd, state):
        keep = (ci <= ri) if fwd else (ci >= ri)
        q = p_ref[pl.ds(row0, c), 0:128]
        k = p_ref[pl.ds(row0, c), 128:256] * scale
        v = p_ref[pl.ds(row0, c), 256:512]
        bc = _mm_f32(keep.astype(F32), la_ref[pl.ds(row0, c), :])
        tot = bc[c - 1:c, :] if fwd else bc[0:1, :]
        mid = bc[c // 2 - 1:c // 2, :] if fwd else bc[c // 2:c // 2 + 1, :]
        qe, ke = q * jnp.exp(bc - mid), k * jnp.exp(mid - bc)
        q_in, k_out, a = q * jnp.exp(bc), k * jnp.exp(tot - bc), jnp.exp(tot)
        outs, new_state = [], []
        for h in range(GLA_H):
            sl = slice(h * GLA_DK, (h + 1) * GLA_DK)
            vh = v[:, h * GLA_DV:(h + 1) * GLA_DV]
            att = jnp.where(keep, _mm_nt(qe[:, sl], ke[:, sl]), 0.0)
            outs.append(_mm(att, vh) + _mm_nt(q_in[:, sl], state[h]))
            new_state.append(state[h] * a[:, sl] + _mm_tn(vh, k_out[:, sl]))
        return jnp.concatenate(outs, axis=-1), tuple(new_state)

    def body(i, carry):
        sf, sb = carry
        rf = pl.multiple_of(i * c, c)
        rb = pl.multiple_of((n - 1 - i) * c, c)
        o_f, sf = chunk(rf, la_f, True, sf)
        of_s[pl.ds(rf, c), :] = o_f
        o_b, sb = chunk(rb, la_b, False, sb)
        ob_s[pl.ds(rb, c), :] = o_b
        return sf, sb

    if latent:
        init = (tuple(s0_ref[0, h] for h in range(GLA_H)), tuple(s0_ref[1, h] for h in range(GLA_H)))
    else:
        zero = jnp.zeros((GLA_DV, GLA_DK), F32)
        init = ((zero,) * GLA_H, (zero,) * GLA_H)
    sf, sb = lax.fori_loop(0, n, body, init)
    if not latent:
        for h in range(GLA_H):
            st_ref[0, h] = sf[h]
            st_ref[1, h] = sb[h]
    for r in range(t // QB):
        rows = slice(r * QB, (r + 1) * QB)
        o_ref[rows, :] = _head_rms_gate(of_s[rows, :] + ob_s[rows, :], p_ref[rows, 512:768]).astype(BF16)


def _gla(p, gate_w, gate_b, s0_t, latent):
    t, nseq, row0 = (T_LAT, N_LAT, M_CTX // T_LAT) if latent else (T_CTX, N_CTX, 0)
    st_shape = (2, GLA_H, GLA_DV, GLA_DK)
    in_specs = [pl.BlockSpec((t, W_GLA), lambda b: (row0 + b, 0)),
                pl.BlockSpec((2, GLA_RANK, GLA_H * GLA_DK), lambda b: (0, 0, 0)),
                pl.BlockSpec((2, 1, GLA_H * GLA_DK), lambda b: (0, 0, 0))]
    args = [p, gate_w, gate_b]
    o_shape = jax.ShapeDtypeStruct((nseq * t, 256), BF16)
    o_spec = pl.BlockSpec((t, 256), lambda b: (b, 0))
    if latent:
        in_specs.append(pl.BlockSpec((None,) + st_shape, lambda b: (b, 0, 0, 0, 0)))
        args.append(s0_t)
        out_shape, out_specs = o_shape, o_spec
    else:
        out_shape = [o_shape, jax.ShapeDtypeStruct((nseq,) + st_shape, F32)]
        out_specs = [o_spec, pl.BlockSpec((None,) + st_shape, lambda b: (b, 0, 0, 0, 0))]
    return pl.pallas_call(
        functools.partial(_gla_kernel, t, latent),
        out_shape=out_shape, grid=(nseq,), in_specs=in_specs, out_specs=out_specs,
        scratch_shapes=[pltpu.VMEM((t, 128), F32), pltpu.VMEM((t, 128), F32),
                        pltpu.VMEM((t, 256), F32), pltpu.VMEM((t, 256), F32)],
        compiler_params=_cparams("parallel"),
        name="gla_latent" if latent else "gla_context",
    )(*args)


def _ret_kernel(t, latent, *refs):
    if latent:
        p_ref, dec_ref, cos_ref, slo_ref, shi_ref, s0_ref, o_ref, q_s, k_s = refs
    else:
        p_ref, dec_ref, o_ref, st_ref = refs
    lg = _log_sigmoid(dec_ref[...])
    scale = RET_DK ** -0.5
    if latent:
        q_s[...] = _rope(p_ref[:, 0:256], cos_ref[...], slo_ref[...], shi_ref[...], RET_DK // 2)
        k_s[...] = _rope(p_ref[:, 256:512], cos_ref[...], slo_ref[...], shi_ref[...], RET_DK // 2) * scale
        q_of = lambda rows, sl: q_s[rows, sl]
        k_of = lambda sl: k_s[:, sl]
    else:
        q_of = lambda rows, sl: p_ref[rows, sl]
        k_of = lambda sl: p_ref[:, 256 + sl.start:256 + sl.stop] * scale
    col = lax.broadcasted_iota(jnp.int32, (QB, t), 1)
    for r in range(t // QB):
        rows = slice(r * QB, (r + 1) * QB)
        row = lax.broadcasted_iota(jnp.int32, (QB, t), 0) + r * QB
        diff = (row - col).astype(F32)
        pos = (lax.broadcasted_iota(jnp.int32, (QB, 1), 0) + r * QB).astype(F32)
        outs = []
        for h in range(RET_H):
            sl = slice(h * RET_DK, (h + 1) * RET_DK)
            lg_f, lg_b = lg[0:1, h * 64:h * 64 + 1], lg[1:2, h * 64:h * 64 + 1]
            decay = (jnp.where(diff >= 0, jnp.exp(jnp.maximum(diff, 0.0) * lg_f), 0.0)
                     + jnp.where(diff <= 0, jnp.exp(jnp.maximum(-diff, 0.0) * lg_b), 0.0))
            q = q_of(rows, sl)
            v = p_ref[:, 512 + h * RET_DV:512 + (h + 1) * RET_DV]
            o = _mm(_mm_nt(q, k_of(sl)) * decay, v)
            if latent:
                o = o + _mm(q * jnp.exp((pos + 1.0) * lg_f), s0_ref[0, h])
                o = o + _mm(q * jnp.exp((float(t) - pos) * lg_b), s0_ref[1, h])
            outs.append(o)
        o_ref[rows, :] = _head_rms_gate(jnp.concatenate(outs, axis=-1), p_ref[rows, 768:1024]).astype(BF16)
    if not latent:
        j = lax.broadcasted_iota(jnp.int32, (t, 1), 0).astype(F32)
        for h in range(RET_H):
            sl = slice(h * RET_DK, (h + 1) * RET_DK)
            lg_f, lg_b = lg[0:1, h * 64:h * 64 + 1], lg[1:2, h * 64:h * 64 + 1]
            v = p_ref[:, 512 + h * RET_DV:512 + (h + 1) * RET_DV]
            k = k_of(sl)
            st_ref[0, h] = _mm_tn(k * jnp.exp((float(t - 1) - j) * lg_f), v)
            st_ref[1, h] = _mm_tn(k * jnp.exp(j * lg_b), v)


def _ret(p, dec_lanes, tables, s0, latent):
    t, nseq, row0 = (T_LAT, N_LAT, M_CTX // T_LAT) if latent else (T_CTX, N_CTX, 0)
    st_shape = (2, RET_H, RET_DK, RET_DV)
    in_specs = [pl.BlockSpec((t, W_RET), lambda b: (row0 + b, 0)), pl.BlockSpec((2, 256), lambda b: (0, 0))]
    args = [p, dec_lanes]
    o_shape = jax.ShapeDtypeStruct((nseq * t, 256), BF16)
    o_spec = pl.BlockSpec((t, 256), lambda b: (b, 0))
    scratch = []
    if latent:
        in_specs += [pl.BlockSpec((t, 256), lambda b: (0, 0))] * 3
        in_specs.append(pl.BlockSpec((None,) + st_shape, lambda b: (b, 0, 0, 0, 0)))
        args += list(tables) + [s0]
        out_shape, out_specs = o_shape, o_spec
        scratch = [pltpu.VMEM((t, 256), F32), pltpu.VMEM((t, 256), F32)]
    else:
        out_shape = [o_shape, jax.ShapeDtypeStruct((nseq,) + st_shape, F32)]
        out_specs = [o_spec, pl.BlockSpec((None,) + st_shape, lambda b: (b, 0, 0, 0, 0))]
    return pl.pallas_call(
        functools.partial(_ret_kernel, t, latent),
        out_shape=out_shape, grid=(nseq,), in_specs=in_specs, out_specs=out_specs, scratch_shapes=scratch,
        compiler_params=_cparams("parallel"),
        name="ret_latent" if latent else "ret_context",
    )(*args)


def _mla_kernel(t, latent, *refs):
    if latent:
        (p_ref, gq_ref, gkv_ref, wq_ref, wkv_ref, cq_ref, slq_ref, shq_ref, ck_ref, slk_ref, shk_ref,
         cckv_ref, ckr_ref, o_ref, qn_s, qr_s, kn_s, kr_s, v_s) = refs
    else:
        p_ref, gq_ref, gkv_ref, wq_ref, wkv_ref, o_ref, ckv_ref, qn_s, qr_s, kn_s, kr_s, v_s = refs
    nk = t + (PAST if latent else 0)
    qh = _mm(_rms(p_ref[:, 0:256]) * gq_ref[...], wq_ref[...])
    ckv = _rms(p_ref[:, 256:384]) * gkv_ref[...]
    kv = _mm(ckv, wkv_ref[...])
    qn_s[...] = qh[:, 0:256]
    kn_s[0:t, :] = kv[:, 0:256]
    v_s[0:t, :] = kv[:, 256:512]
    if latent:
        qr_s[...] = _rope(qh[:, 256:384], cq_ref[...], slq_ref[...], shq_ref[...], MLA_ROPE // 2)
        kr_s[0:t, :] = _rope(p_ref[:, 384:512], ck_ref[...], slk_ref[...], shk_ref[...], MLA_ROPE // 2)
        kvc = _mm(cckv_ref[...], wkv_ref[...])
        kn_s[t:nk, :] = kvc[:, 0:256]
        v_s[t:nk, :] = kvc[:, 256:512]
        kr_s[t:nk, :] = ckr_ref[...]
    else:
        qr_s[...] = qh[:, 256:384]
        kr_s[...] = p_ref[:, 384:512]
        ckv_ref[...] = ckv
    scale = (MLA_NOPE + MLA_ROPE) ** -0.5
    for r in range(t // QB):
        rows = slice(r * QB, (r + 1) * QB)
        outs = []
        for h in range(MLA_H):
            s = (_mm_nt(qn_s[rows, h * 64:(h + 1) * 64], kn_s[:, h * 64:(h + 1) * 64])
                 + _mm_nt(qr_s[rows, h * 32:(h + 1) * 32], kr_s[:, 0:MLA_ROPE])) * scale
            e = jnp.exp(s - jnp.max(s, axis=-1, keepdims=True))
            outs.append(_mm(e, v_s[:, h * 64:(h + 1) * 64]) / jnp.sum(e, axis=-1, keepdims=True))
        o_ref[rows, :] = jnp.concatenate(outs, axis=-1).astype(BF16)


def _mla(p, gq, gkv, wq, wkv, tables_q, tables_k, cache_ckv, cache_kr, latent):
    t, nseq, row0 = (T_LAT, N_LAT, M_CTX // T_LAT) if latent else (T_CTX, N_CTX, 0)
    nk = t + (PAST if latent else 0)
    const = lambda shape: pl.BlockSpec(shape, lambda b: (0,) * len(shape))
    in_specs = [pl.BlockSpec((t, W_MLA), lambda b: (row0 + b, 0)), const((1, MLA_QR)), const((1, MLA_KVR)),
                const((MLA_QR, 384)), const((MLA_KVR, 512))]
    args = [p, gq, gkv, wq, wkv]
    o_shape = jax.ShapeDtypeStruct((nseq * t, 256), BF16)
    o_spec = pl.BlockSpec((t, 256), lambda b: (b, 0))
    if latent:
        in_specs += [const((t, 128))] * 6
        in_specs += [pl.BlockSpec((None, PAST, MLA_KVR), lambda b: (b, 0, 0)),
                     pl.BlockSpec((None, PAST, 128), lambda b: (b, 0, 0))]
        args += list(tables_q) + list(tables_k) + [cache_ckv, cache_kr]
        out_shape, out_specs = o_shape, o_spec
    else:
        out_shape = [o_shape, jax.ShapeDtypeStruct((nseq * t, MLA_KVR), F32)]
        out_specs = [o_spec, pl.BlockSpec((t, MLA_KVR), lambda b: (b, 0))]
    return pl.pallas_call(
        functools.partial(_mla_kernel, t, latent),
        out_shape=out_shape, grid=(nseq,), in_specs=in_specs, out_specs=out_specs,
        scratch_shapes=[pltpu.VMEM((t, 256), F32), pltpu.VMEM((t, 128), F32), pltpu.VMEM((nk, 256), F32),
                        pltpu.VMEM((nk, 128), F32), pltpu.VMEM((nk, 256), F32)],
        compiler_params=_cparams("parallel"),
        name="mla_latent" if latent else "mla_context",
    )(*args)


def _gqa_kernel(t, latent, *refs):
    if latent:
        p_ref, sink_ref, cos_ref, slo_ref, shi_ref, ck_ref, cv_ref, o_ref, q_s, k_s = refs
    else:
        p_ref, sink_ref, o_ref = refs
    scale = GQA_HD ** -0.5
    grp = GQA_H // GQA_KV
    if latent:
        q_s[...] = _rope(p_ref[:, 0:256], cos_ref[...], slo_ref[...], shi_ref[...], GQA_HD // 2)
        k_s[...] = _rope(p_ref[:, 256:384], cos_ref[:, 0:128], slo_ref[:, 0:128], shi_ref[:, 0:128], GQA_HD // 2)
    col = lax.broadcasted_iota(jnp.int32, (QB, t), 1)
    for r in range(t // QB):
        rows = slice(r * QB, (r + 1) * QB)
        if latent:
            row = lax.broadcasted_iota(jnp.int32, (QB, t), 0) + r * QB
            near = jnp.abs(row - col) <= WINDOW
        outs = []
        for h in range(GQA_H):
            kvh = h // grp
            ksl = slice(kvh * GQA_HD, (kvh + 1) * GQA_HD)
            sk = sink_ref[0:1, h * 64:h * 64 + 1]
            v = p_ref[:, 384 + kvh * GQA_HD:384 + (kvh + 1) * GQA_HD]
            if latent:
                q = q_s[rows, h * 64:(h + 1) * 64]
                s_loc = jnp.where(near, _mm_nt(q, k_s[:, ksl]) * scale, NEG)
                s_ctx = _mm_nt(q, ck_ref[kvh]) * scale
                m = jnp.maximum(jnp.maximum(jnp.max(s_loc, axis=-1, keepdims=True),
                                            jnp.max(s_ctx, axis=-1, keepdims=True)), sk)
                e_loc, e_ctx = jnp.exp(s_loc - m), jnp.exp(s_ctx - m)
                den = jnp.sum(e_loc, axis=-1, keepdims=True) + jnp.sum(e_ctx, axis=-1, keepdims=True) + jnp.exp(sk - m)
                outs.append((_mm(e_loc, v) + _mm(e_ctx, cv_ref[kvh])) / den)
            else:
                q = p_ref[rows, h * 64:(h + 1) * 64]
                s = _mm_nt(q, p_ref[:, 256 + ksl.start:256 + ksl.stop]) * scale
                m = jnp.maximum(jnp.max(s, axis=-1, keepdims=True), sk)
                e = jnp.exp(s - m)
                outs.append(_mm(e, v) / (jnp.sum(e, axis=-1, keepdims=True) + jnp.exp(sk - m)))
        o_ref[rows, :] = jnp.concatenate(outs, axis=-1).astype(BF16)


def _gqa(p, sink_lanes, tables, cache_k, cache_v, latent):
    t, nseq, row0 = (T_LAT, N_LAT, M_CTX // T_LAT) if latent else (T_CTX, N_CTX, 0)
    in_specs = [pl.BlockSpec((t, W_GQA), lambda b: (row0 + b, 0)), pl.BlockSpec((1, 256), lambda b: (0, 0))]
    args = [p, sink_lanes]
    scratch = []
    if latent:
        in_specs += [pl.BlockSpec((t, 256), lambda b: (0, 0))] * 3
        in_specs += [pl.BlockSpec((None, GQA_KV, PAST, GQA_HD), lambda b: (b, 0, 0, 0))] * 2
        args += list(tables) + [cache_k, cache_v]
        scratch = [pltpu.VMEM((t, 256), F32), pltpu.VMEM((t, 128), F32)]
    return pl.pallas_call(
        functools.partial(_gqa_kernel, t, latent),
        out_shape=jax.ShapeDtypeStruct((nseq * t, 256), BF16), grid=(nseq,), in_specs=in_specs,
        out_specs=pl.BlockSpec((t, 256), lambda b: (b, 0)), scratch_shapes=scratch,
        compiler_params=_cparams("parallel"),
        name="gqa_latent" if latent else "gqa_context",
    )(*args)


def _outproj_kernel(x_ref, oa_ref, ob_ref, oc_ref, od_ref, w_ref, gt_ref, g_ref, sh_ref, sc_ref, xo_ref, h_ref):
    mix = (jnp.dot(oa_ref[...], w_ref[0:256, :], preferred_element_type=F32)
           + jnp.dot(ob_ref[...], w_ref[256:512, :], preferred_element_type=F32)
           + jnp.dot(oc_ref[...], w_ref[512:768, :], preferred_element_type=F32)
           + jnp.dot(od_ref[...], w_ref[768:1024, :], preferred_element_type=F32))
    x = x_ref[...] + gt_ref[...] * mix
    xo_ref[...] = x
    h_ref[...] = (_rms(x) * g_ref[...] * (1.0 + sc_ref[...]) + sh_ref[...]).astype(BF16)


def _outproj(x, oa, ob, oc, od, w_out, g2, modl):
    tm = 512
    rows = lambda w: pl.BlockSpec((tm, w), lambda i: (i, 0))
    return pl.pallas_call(
        _outproj_kernel,
        out_shape=[jax.ShapeDtypeStruct((M_ALL, D), F32), jax.ShapeDtypeStruct((M_ALL, D), BF16)],
        grid=(M_ALL // tm,),
        in_specs=[rows(D), rows(256), rows(256), rows(256), rows(256), pl.BlockSpec((D, D), lambda i: (0, 0)),
                  _mod_spec(2, tm), pl.BlockSpec((1, D), lambda i: (0, 0)), _mod_spec(3, tm), _mod_spec(4, tm)],
        out_specs=[rows(D), rows(D)],
        compiler_params=_cparams("parallel"),
        name="outproj",
    )(x, oa, ob, oc, od, w_out, modl, g2, modl, modl)


def _ffn_kernel(h_ref, x_ref, gt_ref, wg_ref, wu_ref, wd_ref, o_ref, acc_ref):
    f = pl.program_id(1)

    @pl.when(f == 0)
    def _():
        acc_ref[...] = jnp.zeros_like(acc_ref)

    h = h_ref[...]
    g = jnp.dot(h, wg_ref[...].astype(BF16), preferred_element_type=F32)
    u = jnp.dot(h, wu_ref[...].astype(BF16), preferred_element_type=F32)
    acc_ref[...] += jnp.dot((_silu(g) * u).astype(BF16), wd_ref[...].astype(BF16), preferred_element_type=F32)

    @pl.when(f == pl.num_programs(1) - 1)
    def _():
        o_ref[...] = x_ref[...] + gt_ref[...] * acc_ref[...]


def _ffn(h, x, modl, wg, wu, wd):
    tm, tf = 1024, 256
    return pl.pallas_call(
        _ffn_kernel,
        out_shape=jax.ShapeDtypeStruct((M_ALL, D), F32),
        grid=(M_ALL // tm, D_FF // tf),
        in_specs=[pl.BlockSpec((tm, D), lambda i, f: (i, 0)), pl.BlockSpec((tm, D), lambda i, f: (i, 0)),
                  pl.BlockSpec((None, 1, D), lambda i, f: (_mod_row(i, tm) * 6 + 5, 0, 0)),
                  pl.BlockSpec((D, tf), lambda i, f: (0, f)), pl.BlockSpec((D, tf), lambda i, f: (0, f)),
                  pl.BlockSpec((tf, D), lambda i, f: (f, 0))],
        out_specs=pl.BlockSpec((tm, D), lambda i, f: (i, 0)),
        scratch_shapes=[pltpu.VMEM((tm, D), F32)],
        compiler_params=_cparams("parallel", "arbitrary"),
        name="ffn_dense",
    )(h, x, modl, wg, wu, wd)


def _moe_kernel(h_ref, x_ref, gt_ref, r_ref, wg_ref, wu_ref, wd_ref, o_ref, acc_ref, gate_ref):
    e = pl.program_id(1)
    f = pl.program_id(2)
    lane = lax.broadcasted_iota(jnp.int32, gate_ref.shape, 1)

    @pl.when((e == 0) & (f == 0))
    def _():
        acc_ref[...] = jnp.zeros_like(acc_ref)
        logits = jnp.where(lane < N_EXP, jnp.dot(h_ref[...], r_ref[...], preferred_element_type=F32), NEG)
        m1 = jnp.max(logits, axis=-1, keepdims=True)
        i1 = jnp.min(jnp.where(logits == m1, lane, LANE), axis=-1, keepdims=True)
        rest = jnp.where(lane == i1, NEG, logits)
        m2 = jnp.max(rest, axis=-1, keepdims=True)
        i2 = jnp.min(jnp.where(rest == m2, lane, LANE), axis=-1, keepdims=True)
        e2 = jnp.exp(m2 - m1)
        gate_ref[...] = jnp.where(lane == i1, 1.0 / (1.0 + e2), 0.0) + jnp.where(lane == i2, e2 / (1.0 + e2), 0.0)

    h = h_ref[...]
    gate = jnp.sum(jnp.where(lane == e, gate_ref[...], 0.0), axis=-1, keepdims=True)
    g = jnp.dot(h, wg_ref[...].astype(BF16), preferred_element_type=F32)
    u = jnp.dot(h, wu_ref[...].astype(BF16), preferred_element_type=F32)
    acc_ref[...] += jnp.dot((gate * (_silu(g) * u)).astype(BF16), wd_ref[...].astype(BF16),
                            preferred_element_type=F32)

    @pl.when((e == pl.num_programs(1) - 1) & (f == pl.num_programs(2) - 1))
    def _():
        o_ref[...] = x_ref[...] + gt_ref[...] * acc_ref[...]


def _moe(h, x, modl, router, wg, wu, wd):
    tm, tf = 1024, 512
    return pl.pallas_call(
        _moe_kernel,
        out_shape=jax.ShapeDtypeStruct((M_ALL, D), F32),
        grid=(M_ALL // tm, N_EXP, D_FFE // tf),
        in_specs=[pl.BlockSpec((tm, D), lambda i, e, f: (i, 0)), pl.BlockSpec((tm, D), lambda i, e, f: (i, 0)),
                  pl.BlockSpec((None, 1, D), lambda i, e, f: (_mod_row(i, tm) * 6 + 5, 0, 0)),
                  pl.BlockSpec((D, LANE), lambda i, e, f: (0, 0)),
                  pl.BlockSpec((None, D, tf), lambda i, e, f: (e, 0, f)),
                  pl.BlockSpec((None, D, tf), lambda i, e, f: (e, 0, f)),
                  pl.BlockSpec((None, tf, D), lambda i, e, f: (e, f, 0))],
        out_specs=pl.BlockSpec((tm, D), lambda i, e, f: (i, 0)),
        scratch_shapes=[pltpu.VMEM((tm, D), F32), pltpu.VMEM((tm, LANE), F32)],
        compiler_params=_cparams("parallel", "arbitrary", "arbitrary"),
        name="ffn_moe",
    )(h, x, modl, router, wg, wu, wd)


def _final_kernel(x_ref, g_ref, o_ref):
    o_ref[...] = _rms(x_ref[...]) * g_ref[...]


def _final_norm(x, g):
    tm = 1024
    return pl.pallas_call(
        _final_kernel,
        out_shape=jax.ShapeDtypeStruct((M_ALL, D), F32),
        grid=(M_ALL // tm,),
        in_specs=[pl.BlockSpec((tm, D), lambda i: (i, 0)), pl.BlockSpec((1, D), lambda i: (0, 0))],
        out_specs=pl.BlockSpec((tm, D), lambda i: (i, 0)),
        compiler_params=_cparams("parallel"),
        name="final_norm",
    )(x, g)


def _pad_cols(w, width):
    return jnp.pad(w, ((0, 0), (0, 0), (0, width - w.shape[-1])))


def kernel(x_prompt, x_sample, state_gla, state_ret, cache_mla_ckv, cache_mla_krope, cache_gqa_k, cache_gqa_v,
           c, c_ctx, norm1_g, norm2_g, final_norm_g, w_mod, b_mod, w_in, w_out, gla_gate_w, gla_gate_b,
           mla_q_norm_g, mla_w_q_up, mla_kv_norm_g, mla_w_kv_up, ret_decay, gqa_sink,
           ffn_w_gate, ffn_w_up, ffn_w_down, moe_router, moe_w_gate, moe_w_up, moe_w_down):
    x = jnp.concatenate([x_prompt.reshape(M_CTX, D), x_sample.reshape(M_LAT, D)], axis=0)

    cond = jnp.concatenate([c_ctx[None], c, jnp.zeros((8 - 1 - N_LAT, D), F32)], axis=0)
    mod = _modulation(cond, w_mod, b_mod)
    mod = mod[:, :1 + N_LAT].reshape(DEPTH, (1 + N_LAT) * 6, 1, D)

    w_gla = _pad_cols(w_in[:, :, 0:800], W_GLA).astype(BF16)
    w_mla = _pad_cols(w_in[:, :, 800:1216], W_MLA).astype(BF16)
    w_ret = w_in[:, :, 1216:2240].astype(BF16)
    w_gqa = w_in[:, :, 2240:2752].astype(BF16)
    w_out_b = w_out.astype(BF16)
    wq = mla_w_q_up.reshape(DEPTH, MLA_QR, MLA_H, MLA_NOPE + MLA_ROPE)
    wq = jnp.concatenate([wq[..., :MLA_NOPE].reshape(DEPTH, MLA_QR, MLA_H * MLA_NOPE),
                          wq[..., MLA_NOPE:].reshape(DEPTH, MLA_QR, MLA_H * MLA_ROPE)], axis=-1).astype(BF16)
    wkv = mla_w_kv_up.reshape(DEPTH, MLA_KVR, MLA_H, MLA_NOPE + MLA_DV)
    wkv = jnp.concatenate([wkv[..., :MLA_NOPE].reshape(DEPTH, MLA_KVR, MLA_H * MLA_NOPE),
                           wkv[..., MLA_NOPE:].reshape(DEPTH, MLA_KVR, MLA_H * MLA_DV)], axis=-1).astype(BF16)
    router = jnp.pad(moe_router, ((0, 0), (0, 0), (0, LANE - N_EXP))).astype(BF16)
    dec_lanes = jnp.repeat(ret_decay, RET_DV, axis=-1)
    sink_lanes = jnp.repeat(gqa_sink, GQA_HD, axis=-1).reshape(DEPTH, 1, 256)
    gate_b = gla_gate_b.reshape(DEPTH, 2, 1, GLA_H * GLA_DK)
    s0_gla = jnp.swapaxes(state_gla, -1, -2)
    cache_kr = jnp.pad(cache_mla_krope, ((0, 0), (0, 0), (0, 0), (0, 128 - MLA_ROPE)))

    rope64 = _rope_tables(T_LAT, 64, 256)
    rope32_q = _rope_tables(T_LAT, 32, 128)
    ck, sl, sh = _rope_tables(T_LAT, 32, 128)
    live = jnp.asarray((np.arange(128) < MLA_ROPE).astype(np.float32))[None]
    rope32_k = (ck * live, sl * live, sh * live)

    gla_l, ret_l, ckv_l, kr_l, gk_l, gv_l = [], [], [], [], [], []
    for l in range(DEPTH):
        modl = mod[l]
        p_gla, p_mla, p_ret, p_gqa = _inproj(x, norm1_g[l][None], modl, w_gla[l], w_mla[l], w_ret[l], w_gqa[l])

        o_gla_c, st_gla = _gla(p_gla, gla_gate_w[l], gate_b[l], None, False)
        o_gla_s = _gla(p_gla, gla_gate_w[l], gate_b[l], s0_gla[:, l], True)
        o_ret_c, st_ret = _ret(p_ret, dec_lanes[l], None, None, False)
        o_ret_s = _ret(p_ret, dec_lanes[l], rope64, state_ret[:, l], True)
        gq, gkv = mla_q_norm_g[l][None], mla_kv_norm_g[l][None]
        o_mla_c, ckv = _mla(p_mla, gq, gkv, wq[l], wkv[l], None, None, None, None, False)
        o_mla_s = _mla(p_mla, gq, gkv, wq[l], wkv[l], rope32_q, rope32_k, cache_mla_ckv[:, l], cache_kr[:, l], True)
        o_gqa_c = _gqa(p_gqa, sink_lanes[l], None, None, None, False)
        o_gqa_s = _gqa(p_gqa, sink_lanes[l], rope64, cache_gqa_k[:, l], cache_gqa_v[:, l], True)

        cat = lambda a, b: jnp.concatenate([a, b], axis=0)
        x, h2 = _outproj(x, cat(o_gla_c, o_gla_s), cat(o_mla_c, o_mla_s), cat(o_ret_c, o_ret_s),
                         cat(o_gqa_c, o_gqa_s), w_out_b[l], norm2_g[l][None], modl)
        j = l // 2
        if l % 2 == 0:
            x = _ffn(h2, x, modl, ffn_w_gate[j], ffn_w_up[j], ffn_w_down[j])
        else:
            x = _moe(h2, x, modl, router[j], moe_w_gate[j], moe_w_up[j], moe_w_down[j])

        gla_l.append(jnp.swapaxes(st_gla, -1, -2))
        ret_l.append(st_ret)
        ckv_l.append(ckv.reshape(N_CTX, T_CTX, MLA_KVR))
        kr_l.append(p_mla[:M_CTX, 384:384 + MLA_ROPE].reshape(N_CTX, T_CTX, MLA_ROPE))
        kv = p_gqa[:M_CTX, 256:512].reshape(N_CTX, T_CTX, 2, GQA_KV, GQA_HD)
        gk_l.append(kv[:, :, 0].transpose(0, 2, 1, 3))
        gv_l.append(kv[:, :, 1].transpose(0, 2, 1, 3))

    y = _final_norm(x, final_norm_g[None])
    return (y[:M_CTX].reshape(N_CTX, T_CTX, D), y[M_CTX:].reshape(N_LAT, T_LAT, D),
            jnp.stack(gla_l, axis=1), jnp.stack(ret_l, axis=1), jnp.stack(ckv_l, axis=1), jnp.stack(kr_l, axis=1),
            jnp.stack(gk_l, axis=1), jnp.stack(gv_l, axis=1))
```

```python
import functools

import numpy as np
import jax
import jax.numpy as jnp
from jax import lax
from jax.experimental import pallas as pl
from jax.experimental.pallas import tpu as pltpu

F32 = jnp.float32
BF16 = jnp.bfloat16
HIGHEST = lax.Precision.HIGHEST

D = 1024
N_CTX, T_CTX = 16, 256
N_LAT, T_LAT = 2, 1024
PAST = 256
DEPTH = 4
M_CTX = N_CTX * T_CTX
M_LAT = N_LAT * T_LAT
M_ALL = M_CTX + M_LAT
GRID_W = 64
ROPE_BASE = 10000.0
EPS = 1e-6

GLA_H, GLA_DK, GLA_DV, GLA_RANK, GLA_NORM, GLA_C = 4, 32, 64, 16, 16.0, 64
MLA_H, MLA_QR, MLA_KVR, MLA_NOPE, MLA_ROPE, MLA_DV = 4, 256, 128, 64, 32, 64
RET_H, RET_DK, RET_DV = 4, 64, 64
GQA_H, GQA_KV, GQA_HD, WINDOW = 4, 2, 64, 128
D_FF, N_EXP, D_FFE = 2816, 8, 3584

W_GLA, W_MLA, W_RET, W_GQA = 896, 512, 1024, 512
LANE = 128
NEG = -1e30
QB = 256
VMEM_LIMIT = 56 * 1024 * 1024


def _cparams(*sem):
    return pltpu.CompilerParams(dimension_semantics=sem, vmem_limit_bytes=VMEM_LIMIT)


def _mm(a, b):
    return jnp.dot(a.astype(BF16), b.astype(BF16), preferred_element_type=F32)


def _mm_nt(a, b):
    return lax.dot_general(a.astype(BF16), b.astype(BF16), (((1,), (1,)), ((), ())), preferred_element_type=F32)


def _mm_tn(a, b):
    return lax.dot_general(a.astype(BF16), b.astype(BF16), (((0,), (0,)), ((), ())), preferred_element_type=F32)


def _mm_f32(a, b):
    return jnp.dot(a, b, precision=HIGHEST, preferred_element_type=F32)


def _silu(x):
    return x * (1.0 / (1.0 + jnp.exp(-x)))


def _log_sigmoid(x):
    return jnp.minimum(x, 0.0) - jnp.log1p(jnp.exp(-jnp.abs(x)))


def _rms(x):
    return x * lax.rsqrt(jnp.mean(x * x, axis=-1, keepdims=True) + EPS)


def _mod_row(tile, tm):
    return jnp.maximum((tile * tm) // T_LAT - (M_CTX // T_LAT - 1), 0)


def _mod_kernel(c_ref, w_ref, b_ref, o_ref):
    o_ref[...] = _mm(_silu(c_ref[...]), w_ref[...]) + b_ref[...]


def _modulation(cond, w_mod, b_mod):
    tn = 1536
    return pl.pallas_call(
        _mod_kernel,
        out_shape=jax.ShapeDtypeStruct((DEPTH, 8, 6 * D), F32),
        grid=(DEPTH, 6 * D // tn),
        in_specs=[pl.BlockSpec((8, D), lambda l, j: (0, 0)),
                  pl.BlockSpec((None, D, tn), lambda l, j: (l, 0, j)),
                  pl.BlockSpec((None, 1, tn), lambda l, j: (l, 0, j))],
        out_specs=pl.BlockSpec((None, 8, tn), lambda l, j: (l, 0, j)),
        compiler_params=_cparams("parallel", "parallel"),
        name="modulation",
    )(cond, w_mod, b_mod.reshape(DEPTH, 1, 6 * D))


def _inproj_kernel(x_ref, g_ref, sh_ref, sc_ref, wa_ref, wb_ref, wc_ref, wd_ref, oa_ref, ob_ref, oc_ref, od_ref):
    h = (_rms(x_ref[...]) * g_ref[...] * (1.0 + sc_ref[...]) + sh_ref[...]).astype(BF16)
    oa_ref[...] = jnp.dot(h, wa_ref[...], preferred_element_type=F32)
    ob_ref[...] = jnp.dot(h, wb_ref[...], preferred_element_type=F32)
    oc_ref[...] = jnp.dot(h, wc_ref[...], preferred_element_type=F32)
    od_ref[...] = jnp.dot(h, wd_ref[...], preferred_element_type=F32)


def _mod_spec(j, tm):
    return pl.BlockSpec((None, 1, D), lambda i: (_mod_row(i, tm) * 6 + j, 0, 0))


def _inproj(x, g1, modl, wa, wb, wc, wd):
    tm = 512
    full = lambda w: pl.BlockSpec((D, w), lambda i: (0, 0))
    rows = lambda w: pl.BlockSpec((tm, w), lambda i: (i, 0))
    widths = (W_GLA, W_MLA, W_RET, W_GQA)
    return pl.pallas_call(
        _inproj_kernel,
        out_shape=[jax.ShapeDtypeStruct((M_ALL, w), F32) for w in widths],
        grid=(M_ALL // tm,),
        in_specs=[rows(D), pl.BlockSpec((1, D), lambda i: (0, 0)), _mod_spec(0, tm), _mod_spec(1, tm)]
        + [full(w) for w in widths],
        out_specs=[rows(w) for w in widths],
        compiler_params=_cparams("parallel"),
        name="inproj",
    )(x, g1, modl, modl, wa, wb, wc, wd)


def _head_rms_gate(o, gate):
    r = lax.broadcasted_iota(jnp.int32, (256, 256), 0) // 64
    c = lax.broadcasted_iota(jnp.int32, (256, 256), 1) // 64
    group_mean = jnp.where(r == c, 1.0 / 64.0, 0.0).astype(F32)
    ms = _mm_f32(o * o, group_mean)
    return o * lax.rsqrt(ms + EPS) * _silu(gate)


def _rope(x, cos, sin_lo, sin_hi, half):
    w = x.shape[-1]
    return x * cos + pltpu.roll(x, w - half, 1) * sin_lo + pltpu.roll(x, half, 1) * sin_hi


def _rope_tables(t, head_dim, width):
    half = head_dim // 2
    quarter = head_dim // 4
    pos = np.arange(t)
    rows = (pos // GRID_W).astype(np.float32)
    cols = (pos % GRID_W).astype(np.float32)
    inv = np.power(np.float32(ROPE_BASE), -np.arange(quarter, dtype=np.float32) / np.float32(quarter)).astype(np.float32)
    ang = np.concatenate([rows[:, None] * inv, cols[:, None] * inv], axis=-1).astype(np.float32)
    lane = np.arange(width)
    a = ang[:, lane % half]
    cos, sin = np.cos(a).astype(np.float32), np.sin(a).astype(np.float32)
    low = (lane % head_dim) < half
    return (jnp.asarray(cos), jnp.asarray(np.where(low[None], -sin, 0.0).astype(np.float32)),
            jnp.asarray(np.where(low[None], 0.0, sin).astype(np.float32)))


def _gla_kernel(t, latent, *refs):
    if latent:
        p_ref, gw_ref, gb_ref, s0_ref, o_ref, la_f, la_b, of_s, ob_s = refs
    else:
        p_ref, gw_ref, gb_ref, o_ref, st_ref, la_f, la_b, of_s, ob_s = refs
    c = GLA_C
    n = t // c
    la_f[...] = _log_sigmoid(_mm_f32(p_ref[:, 768:784], gw_ref[0]) + gb_ref[0]) / GLA_NORM
    la_b[...] = _log_sigmoid(_mm_f32(p_ref[:, 784:800], gw_ref[1]) + gb_ref[1]) / GLA_NORM

    ri = lax.broadcasted_iota(jnp.int32, (c, c), 0)
    ci = lax.broadcasted_iota(jnp.int32, (c, c), 1)
    scale = GLA_DK ** -0.5

    def chunk(row0, la_ref, fwd, state):
        keep = (ci <= ri) if fwd else (ci >= ri)
        q = p_ref[pl.ds(row0, c), 0:128]
        k = p_ref[pl.ds(row0, c), 128:256] * scale
        v = p_ref[pl.ds(row0, c), 256:512]
        bc = _mm_f32(keep.astype(F32), la_ref[pl.ds(row0, c), :])
        tot = bc[c - 1:c, :] if fwd else bc[0:1, :]
        mid = bc[c // 2 - 1:c // 2, :] if fwd else bc[c // 2:c // 2 + 1, :]
        qe, ke = q * jnp.exp(bc - mid), k * jnp.exp(mid - bc)
        q_in, k_out, a = q * jnp.exp(bc), k * jnp.exp(tot - bc), jnp.exp(tot)
        outs, new_state = [], []
        for h in range(GLA_H):
            sl = slice(h * GLA_DK, (h + 1) * GLA_DK)
            vh = v[:, h * GLA_DV:(h + 1) * GLA_DV]
            att = jnp.where(keep, _mm_nt(qe[:, sl], ke[:, sl]), 0.0)
            outs.append(_mm(att, vh) + _mm_nt(q_in[:, sl], state[h]))
            new_state.append(state[h] * a[:, sl] + _mm_tn(vh, k_out[:, sl]))
        return jnp.concatenate(outs, axis=-1), tuple(new_state)

    def body(i, carry):
        sf, sb = carry
        rf = pl.multiple_of(i * c, c)
        rb = pl.multiple_of((n - 1 - i) * c, c)
        o_f, sf = chunk(rf, la_f, True, sf)
        of_s[pl.ds(rf, c), :] = o_f
        o_b, sb = chunk(rb, la_b, False, sb)
        ob_s[pl.ds(rb, c), :] = o_b
        return sf, sb

    if latent:
        init = (tuple(s0_ref[0, h] for h in range(GLA_H)), tuple(s0_ref[1, h] for h in range(GLA_H)))
    else:
        zero = jnp.zeros((GLA_DV, GLA_DK), F32)
        init = ((zero,) * GLA_H, (zero,) * GLA_H)
    sf, sb = lax.fori_loop(0, n, body, init)
    if not latent:
        for h in range(GLA_H):
            st_ref[0, h] = sf[h]
            st_ref[1, h] = sb[h]
    for r in range(t // QB):
        rows = slice(r * QB, (r + 1) * QB)
        o_ref[rows, :] = _head_rms_gate(of_s[rows, :] + ob_s[rows, :], p_ref[rows, 512:768]).astype(BF16)


def _gla(p, gate_w, gate_b, s0_t, latent):
    t, nseq, row0 = (T_LAT, N_LAT, M_CTX // T_LAT) if latent else (T_CTX, N_CTX, 0)
    st_shape = (2, GLA_H, GLA_DV, GLA_DK)
    in_specs = [pl.BlockSpec((t, W_GLA), lambda b: (row0 + b, 0)),
                pl.BlockSpec((2, GLA_RANK, GLA_H * GLA_DK), lambda b: (0, 0, 0)),
                pl.BlockSpec((2, 1, GLA_H * GLA_DK), lambda b: (0, 0, 0))]
    args = [p, gate_w, gate_b]
    o_shape = jax.ShapeDtypeStruct((nseq * t, 256), BF16)
    o_spec = pl.BlockSpec((t, 256), lambda b: (b, 0))
    if latent:
        in_specs.append(pl.BlockSpec((None,) + st_shape, lambda b: (b, 0, 0, 0, 0)))
        args.append(s0_t)
        out_shape, out_specs = o_shape, o_spec
    else:
        out_shape = [o_shape, jax.ShapeDtypeStruct((nseq,) + st_shape, F32)]
        out_specs = [o_spec, pl.BlockSpec((None,) + st_shape, lambda b: (b, 0, 0, 0, 0))]
    return pl.pallas_call(
        functools.partial(_gla_kernel, t, latent),
        out_shape=out_shape, grid=(nseq,), in_specs=in_specs, out_specs=out_specs,
        scratch_shapes=[pltpu.VMEM((t, 128), F32), pltpu.VMEM((t, 128), F32),
                        pltpu.VMEM((t, 256), F32), pltpu.VMEM((t, 256), F32)],
        compiler_params=_cparams("parallel"),
        name="gla_latent" if latent else "gla_context",
    )(*args)


def _ret_kernel(t, latent, *refs):
    if latent:
        p_ref, dec_ref, cos_ref, slo_ref, shi_ref, s0_ref, o_ref, q_s, k_s = refs
    else:
        p_ref, dec_ref, o_ref, st_ref = refs
    lg = _log_sigmoid(dec_ref[...])
    scale = RET_DK ** -0.5
    if latent:
        q_s[...] = _rope(p_ref[:, 0:256], cos_ref[...], slo_ref[...], shi_ref[...], RET_DK // 2)
        k_s[...] = _rope(p_ref[:, 256:512], cos_ref[...], slo_ref[...], shi_ref[...], RET_DK // 2) * scale
        q_of = lambda rows, sl: q_s[rows, sl]
        k_of = lambda sl: k_s[:, sl]
    else:
        q_of = lambda rows, sl: p_ref[rows, sl]
        k_of = lambda sl: p_ref[:, 256 + sl.start:256 + sl.stop] * scale
    col = lax.broadcasted_iota(jnp.int32, (QB, t), 1)
    for r in range(t // QB):
        rows = slice(r * QB, (r + 1) * QB)
        row = lax.broadcasted_iota(jnp.int32, (QB, t), 0) + r * QB
        diff = (row - col).astype(F32)
        pos = (lax.broadcasted_iota(jnp.int32, (QB, 1), 0) + r * QB).astype(F32)
        outs = []
        for h in range(RET_H):
            sl = slice(h * RET_DK, (h + 1) * RET_DK)
            lg_f, lg_b = lg[0:1, h * 64:h * 64 + 1], lg[1:2, h * 64:h * 64 + 1]
            decay = (jnp.where(diff >= 0, jnp.exp(jnp.maximum(diff, 0.0) * lg_f), 0.0)
                     + jnp.where(diff <= 0, jnp.exp(jnp.maximum(-diff, 0.0) * lg_b), 0.0))
            q = q_of(rows, sl)
            v = p_ref[:, 512 + h * RET_DV:512 + (h + 1) * RET_DV]
            o = _mm(_mm_nt(q, k_of(sl)) * decay, v)
            if latent:
                o = o + _mm(q * jnp.exp((pos + 1.0) * lg_f), s0_ref[0, h])
                o = o + _mm(q * jnp.exp((float(t) - pos) * lg_b), s0_ref[1, h])
            outs.append(o)
        o_ref[rows, :] = _head_rms_gate(jnp.concatenate(outs, axis=-1), p_ref[rows, 768:1024]).astype(BF16)
    if not latent:
        j = lax.broadcasted_iota(jnp.int32, (t, 1), 0).astype(F32)
        for h in range(RET_H):
            sl = slice(h * RET_DK, (h + 1) * RET_DK)
            lg_f, lg_b = lg[0:1, h * 64:h * 64 + 1], lg[1:2, h * 64:h * 64 + 1]
            v = p_ref[:, 512 + h * RET_DV:512 + (h + 1) * RET_DV]
            k = k_of(sl)
            st_ref[0, h] = _mm_tn(k * jnp.exp((float(t - 1) - j) * lg_f), v)
            st_ref[1, h] = _mm_tn(k * jnp.exp(j * lg_b), v)


def _ret(p, dec_lanes, tables, s0, latent):
    t, nseq, row0 = (T_LAT, N_LAT, M_CTX // T_LAT) if latent else (T_CTX, N_CTX, 0)
    st_shape = (2, RET_H, RET_DK, RET_DV)
    in_specs = [pl.BlockSpec((t, W_RET), lambda b: (row0 + b, 0)), pl.BlockSpec((2, 256), lambda b: (0, 0))]
    args = [p, dec_lanes]
    o_shape = jax.ShapeDtypeStruct((nseq * t, 256), BF16)
    o_spec = pl.BlockSpec((t, 256), lambda b: (b, 0))
    scratch = []
    if latent:
        in_specs += [pl.BlockSpec((t, 256), lambda b: (0, 0))] * 3
        in_specs.append(pl.BlockSpec((None,) + st_shape, lambda b: (b, 0, 0, 0, 0)))
        args += list(tables) + [s0]
        out_shape, out_specs = o_shape, o_spec
        scratch = [pltpu.VMEM((t, 256), F32), pltpu.VMEM((t, 256), F32)]
    else:
        out_shape = [o_shape, jax.ShapeDtypeStruct((nseq,) + st_shape, F32)]
        out_specs = [o_spec, pl.BlockSpec((None,) + st_shape, lambda b: (b, 0, 0, 0, 0))]
    return pl.pallas_call(
        functools.partial(_ret_kernel, t, latent),
        out_shape=out_shape, grid=(nseq,), in_specs=in_specs, out_specs=out_specs, scratch_shapes=scratch,
        compiler_params=_cparams("parallel"),
        name="ret_latent" if latent else "ret_context",
    )(*args)


def _mla_kernel(t, latent, *refs):
    if latent:
        (p_ref, gq_ref, gkv_ref, wq_ref, wkv_ref, cq_ref, slq_ref, shq_ref, ck_ref, slk_ref, shk_ref,
         cckv_ref, ckr_ref, o_ref, qn_s, qr_s, kn_s, kr_s, v_s) = refs
    else:
        p_ref, gq_ref, gkv_ref, wq_ref, wkv_ref, o_ref, ckv_ref, qn_s, qr_s, kn_s, kr_s, v_s = refs
    nk = t + (PAST if latent else 0)
    qh = _mm(_rms(p_ref[:, 0:256]) * gq_ref[...], wq_ref[...])
    ckv = _rms(p_ref[:, 256:384]) * gkv_ref[...]
    kv = _mm(ckv, wkv_ref[...])
    qn_s[...] = qh[:, 0:256]
    kn_s[0:t, :] = kv[:, 0:256]
    v_s[0:t, :] = kv[:, 256:512]
    if latent:
        qr_s[...] = _rope(qh[:, 256:384], cq_ref[...], slq_ref[...], shq_ref[...], MLA_ROPE // 2)
        kr_s[0:t, :] = _rope(p_ref[:, 384:512], ck_ref[...], slk_ref[...], shk_ref[...], MLA_ROPE // 2)
        kvc = _mm(cckv_ref[...], wkv_ref[...])
        kn_s[t:nk, :] = kvc[:, 0:256]
        v_s[t:nk, :] = kvc[:, 256:512]
        kr_s[t:nk, :] = ckr_ref[...]
    else:
        qr_s[...] = qh[:, 256:384]
        kr_s[...] = p_ref[:, 384:512]
        ckv_ref[...] = ckv
    scale = (MLA_NOPE + MLA_ROPE) ** -0.5
    for r in range(t // QB):
        rows = slice(r * QB, (r + 1) * QB)
        outs = []
        for h in range(MLA_H):
            s = (_mm_nt(qn_s[rows, h * 64:(h + 1) * 64], kn_s[:, h * 64:(h + 1) * 64])
                 + _mm_nt(qr_s[rows, h * 32:(h + 1) * 32], kr_s[:, 0:MLA_ROPE])) * scale
            e = jnp.exp(s - jnp.max(s, axis=-1, keepdims=True))
            outs.append(_mm(e, v_s[:, h * 64:(h + 1) * 64]) / jnp.sum(e, axis=-1, keepdims=True))
        o_ref[rows, :] = jnp.concatenate(outs, axis=-1).astype(BF16)


def _mla(p, gq, gkv, wq, wkv, tables_q, tables_k, cache_ckv, cache_kr, latent):
    t, nseq, row0 = (T_LAT, N_LAT, M_CTX // T_LAT) if latent else (T_CTX, N_CTX, 0)
    nk = t + (PAST if latent else 0)
    const = lambda shape: pl.BlockSpec(shape, lambda b: (0,) * len(shape))
    in_specs = [pl.BlockSpec((t, W_MLA), lambda b: (row0 + b, 0)), const((1, MLA_QR)), const((1, MLA_KVR)),
                const((MLA_QR, 384)), const((MLA_KVR, 512))]
    args = [p, gq, gkv, wq, wkv]
    o_shape = jax.ShapeDtypeStruct((nseq * t, 256), BF16)
    o_spec = pl.BlockSpec((t, 256), lambda b: (b, 0))
    if latent:
        in_specs += [const((t, 128))] * 6
        in_specs += [pl.BlockSpec((None, PAST, MLA_KVR), lambda b: (b, 0, 0)),
                     pl.BlockSpec((None, PAST, 128), lambda b: (b, 0, 0))]
        args += list(tables_q) + list(tables_k) + [cache_ckv, cache_kr]
        out_shape, out_specs = o_shape, o_spec
    else:
        out_shape = [o_shape, jax.ShapeDtypeStruct((nseq * t, MLA_KVR), F32)]
        out_specs = [o_spec, pl.BlockSpec((t, MLA_KVR), lambda b: (b, 0))]
    return pl.pallas_call(
        functools.partial(_mla_kernel, t, latent),
        out_shape=out_shape, grid=(nseq,), in_specs=in_specs, out_specs=out_specs,
        scratch_shapes=[pltpu.VMEM((t, 256), F32), pltpu.VMEM((t, 128), F32), pltpu.VMEM((nk, 256), F32),
                        pltpu.VMEM((nk, 128), F32), pltpu.VMEM((nk, 256), F32)],
        compiler_params=_cparams("parallel"),
        name="mla_latent" if latent else "mla_context",
    )(*args)


def _gqa_kernel(t, latent, *refs):
    if latent:
        p_ref, sink_ref, cos_ref, slo_ref, shi_ref, ck_ref, cv_ref, o_ref, q_s, k_s = refs
    else:
        p_ref, sink_ref, o_ref = refs
    scale = GQA_HD ** -0.5
    grp = GQA_H // GQA_KV
    if latent:
        q_s[...] = _rope(p_ref[:, 0:256], cos_ref[...], slo_ref[...], shi_ref[...], GQA_HD // 2)
        k_s[...] = _rope(p_ref[:, 256:384], cos_ref[:, 0:128], slo_ref[:, 0:128], shi_ref[:, 0:128], GQA_HD // 2)
    col = lax.broadcasted_iota(jnp.int32, (QB, t), 1)
    for r in range(t // QB):
        rows = slice(r * QB, (r + 1) * QB)
        if latent:
            row = lax.broadcasted_iota(jnp.int32, (QB, t), 0) + r * QB
            near = jnp.abs(row - col) <= WINDOW
        outs = []
        for h in range(GQA_H):
            kvh = h // grp
            ksl = slice(kvh * GQA_HD, (kvh + 1) * GQA_HD)
            sk = sink_ref[0:1, h * 64:h * 64 + 1]
            v = p_ref[:, 384 + kvh * GQA_HD:384 + (kvh + 1) * GQA_HD]
            if latent:
                q = q_s[rows, h * 64:(h + 1) * 64]
                s_loc = jnp.where(near, _mm_nt(q, k_s[:, ksl]) * scale, NEG)
                s_ctx = _mm_nt(q, ck_ref[kvh]) * scale
                m = jnp.maximum(jnp.maximum(jnp.max(s_loc, axis=-1, keepdims=True),
                                            jnp.max(s_ctx, axis=-1, keepdims=True)), sk)
                e_loc, e_ctx = jnp.exp(s_loc - m), jnp.exp(s_ctx - m)
                den = jnp.sum(e_loc, axis=-1, keepdims=True) + jnp.sum(e_ctx, axis=-1, keepdims=True) + jnp.exp(sk - m)
                outs.append((_mm(e_loc, v) + _mm(e_ctx, cv_ref[kvh])) / den)
            else:
                q = p_ref[rows, h * 64:(h + 1) * 64]
                s = _mm_nt(q, p_ref[:, 256 + ksl.start:256 + ksl.stop]) * scale
                m = jnp.maximum(jnp.max(s, axis=-1, keepdims=True), sk)
                e = jnp.exp(s - m)
                outs.append(_mm(e, v) / (jnp.sum(e, axis=-1, keepdims=True) + jnp.exp(sk - m)))
        o_ref[rows, :] = jnp.concatenate(outs, axis=-1).astype(BF16)


def _gqa(p, sink_lanes, tables, cache_k, cache_v, latent):
    t, nseq, row0 = (T_LAT, N_LAT, M_CTX // T_LAT) if latent else (T_CTX, N_CTX, 0)
    in_specs = [pl.BlockSpec((t, W_GQA), lambda b: (row0 + b, 0)), pl.BlockSpec((1, 256), lambda b: (0, 0))]
    args = [p, sink_lanes]
    scratch = []
    if latent:
        in_specs += [pl.BlockSpec((t, 256), lambda b: (0, 0))] * 3
        in_specs += [pl.BlockSpec((None, GQA_KV, PAST, GQA_HD), lambda b: (b, 0, 0, 0))] * 2
        args += list(tables) + [cache_k, cache_v]
        scratch = [pltpu.VMEM((t, 256), F32), pltpu.VMEM((t, 128), F32)]
    return pl.pallas_call(
        functools.partial(_gqa_kernel, t, latent),
        out_shape=jax.ShapeDtypeStruct((nseq * t, 256), BF16), grid=(nseq,), in_specs=in_specs,
        out_specs=pl.BlockSpec((t, 256), lambda b: (b, 0)), scratch_shapes=scratch,
        compiler_params=_cparams("parallel"),
        name="gqa_latent" if latent else "gqa_context",
    )(*args)


def _outproj_kernel(route, x_ref, oa_ref, ob_ref, oc_ref, od_ref, w_ref, gt_ref, g_ref, sh_ref, sc_ref, *refs):
    mix = (jnp.dot(oa_ref[...], w_ref[0:256, :], preferred_element_type=F32)
           + jnp.dot(ob_ref[...], w_ref[256:512, :], preferred_element_type=F32)
           + jnp.dot(oc_ref[...], w_ref[512:768, :], preferred_element_type=F32)
           + jnp.dot(od_ref[...], w_ref[768:1024, :], preferred_element_type=F32))
    x = x_ref[...] + gt_ref[...] * mix
    h = (_rms(x) * g_ref[...] * (1.0 + sc_ref[...]) + sh_ref[...]).astype(BF16)
    if not route:
        xo_ref, h_ref = refs
    else:
        r_ref, xo_ref, h_ref, gate_ref, sel_ref = refs
        lane = lax.broadcasted_iota(jnp.int32, gate_ref.shape, 1)
        logits = jnp.where(lane < N_EXP, jnp.dot(h, r_ref[...], preferred_element_type=F32), NEG)
        m1 = jnp.max(logits, axis=-1, keepdims=True)
        i1 = jnp.min(jnp.where(logits == m1, lane, LANE), axis=-1, keepdims=True)
        rest = jnp.where(lane == i1, NEG, logits)
        m2 = jnp.max(rest, axis=-1, keepdims=True)
        i2 = jnp.min(jnp.where(rest == m2, lane, LANE), axis=-1, keepdims=True)
        e2 = jnp.exp(m2 - m1)
        gate_ref[...] = jnp.where(lane == i1, 1.0 / (1.0 + e2), 0.0) + jnp.where(lane == i2, e2 / (1.0 + e2), 0.0)
        sel_ref[...] = jnp.where((lane == i1) | (lane == i2), 1, 0)
    xo_ref[...] = x
    h_ref[...] = h


def _outproj(x, oa, ob, oc, od, w_out, g2, modl, router):
    tm = 512
    route = router is not None
    rows = lambda w: pl.BlockSpec((tm, w), lambda i: (i, 0))
    in_specs = [rows(D), rows(256), rows(256), rows(256), rows(256), pl.BlockSpec((D, D), lambda i: (0, 0)),
                _mod_spec(2, tm), pl.BlockSpec((1, D), lambda i: (0, 0)), _mod_spec(3, tm), _mod_spec(4, tm)]
    args = [x, oa, ob, oc, od, w_out, modl, g2, modl, modl]
    out_shape = [jax.ShapeDtypeStruct((M_ALL, D), F32), jax.ShapeDtypeStruct((M_ALL, D), BF16)]
    out_specs = [rows(D), rows(D)]
    if route:
        in_specs.append(pl.BlockSpec((D, LANE), lambda i: (0, 0)))
        args.append(router)
        out_shape += [jax.ShapeDtypeStruct((M_ALL, LANE), F32), jax.ShapeDtypeStruct((M_ALL, LANE), jnp.int32)]
        out_specs += [rows(LANE), rows(LANE)]
    return pl.pallas_call(
        functools.partial(_outproj_kernel, route),
        out_shape=out_shape, grid=(M_ALL // tm,), in_specs=in_specs, out_specs=out_specs,
        compiler_params=_cparams("parallel"),
        name="outproj_route" if route else "outproj",
    )(*args)


def _ffn_kernel(h_ref, x_ref, gt_ref, wg_ref, wu_ref, wd_ref, o_ref, acc_ref):
    f = pl.program_id(1)

    @pl.when(f == 0)
    def _():
        acc_ref[...] = jnp.zeros_like(acc_ref)

    h = h_ref[...]
    g = jnp.dot(h, wg_ref[...].astype(BF16), preferred_element_type=F32)
    u = jnp.dot(h, wu_ref[...].astype(BF16), preferred_element_type=F32)
    acc_ref[...] += jnp.dot((_silu(g) * u).astype(BF16), wd_ref[...].astype(BF16), preferred_element_type=F32)

    @pl.when(f == pl.num_programs(1) - 1)
    def _():
        o_ref[...] = x_ref[...] + gt_ref[...] * acc_ref[...]


def _ffn(h, x, modl, wg, wu, wd, j):
    tm, tf = 1024, 256
    return pl.pallas_call(
        _ffn_kernel,
        out_shape=jax.ShapeDtypeStruct((M_ALL, D), F32),
        grid=(M_ALL // tm, D_FF // tf),
        in_specs=[pl.BlockSpec((tm, D), lambda i, f: (i, 0)), pl.BlockSpec((tm, D), lambda i, f: (i, 0)),
                  pl.BlockSpec((None, 1, D), lambda i, f: (_mod_row(i, tm) * 6 + 5, 0, 0)),
                  pl.BlockSpec((None, D, tf), lambda i, f: (j, 0, f)),
                  pl.BlockSpec((None, D, tf), lambda i, f: (j, 0, f)),
                  pl.BlockSpec((None, tf, D), lambda i, f: (j, f, 0))],
        out_specs=pl.BlockSpec((tm, D), lambda i, f: (i, 0)),
        scratch_shapes=[pltpu.VMEM((tm, D), F32)],
        compiler_params=_cparams("parallel", "arbitrary"),
        name="ffn_dense",
    )(h, x, modl, wg, wu, wd)


SUP, SUB, CHUNK = 2048, 256, 256
N_SUB = SUP // SUB
S_MAX = 2 * M_ALL // SUP + N_EXP
P_SLOT = S_MAX * SUP
TC = 256
WIN = TC + 16


def _moe_expert_kernel(se_ref, nt_ref, sblk_ref, clo_ref, chi_ref, tok_ref, h_ref, wg_ref, wu_ref, wd_ref, o_ref,
                       xs_s, acc_s, wg_s, wu_s, wd_s):
    s = pl.program_id(0)
    f = pl.program_id(1)
    n = nt_ref[s]

    @pl.when((f == 0) & (n == 0))
    def _():
        o_ref[...] = jnp.zeros_like(o_ref)

    @pl.when((f == 0) & (n > 0))
    def _():
        lane = lax.broadcasted_iota(jnp.int32, (SUB, CHUNK), 1)

        def gather(j, carry):
            tok = tok_ref[pl.ds(pl.multiple_of(j * SUB, SUB), SUB), :]
            tok = jnp.concatenate([tok] * (CHUNK // LANE), axis=1)
            acc_s[j] = jnp.zeros((SUB, D), F32)

            def chunk(c, carry):
                onehot = jnp.where(tok - c * CHUNK == lane, 1.0, 0.0).astype(BF16)
                rows = h_ref[pl.ds(pl.multiple_of(c * CHUNK, CHUNK), CHUNK), :]
                acc_s[j] += jnp.dot(onehot, rows, preferred_element_type=F32)
                return carry

            g = s * N_SUB + j
            lax.fori_loop(clo_ref[g], chi_ref[g] + 1, chunk, 0)
            xs_s[j] = acc_s[j].astype(BF16)
            acc_s[j] = jnp.zeros((SUB, D), F32)
            return carry

        lax.fori_loop(0, n, gather, 0)

    @pl.when(n > 0)
    def _():
        wg_s[...] = wg_ref[...].astype(BF16)
        wu_s[...] = wu_ref[...].astype(BF16)
        wd_s[...] = wd_ref[...].astype(BF16)

        def sub(j, carry):
            x = xs_s[j]
            g = jnp.dot(x, wg_s[...], preferred_element_type=F32)
            u = jnp.dot(x, wu_s[...], preferred_element_type=F32)
            acc_s[j] += jnp.dot((_silu(g) * u).astype(BF16), wd_s[...], preferred_element_type=F32)
            return carry

        lax.fori_loop(0, n, sub, 0)

    @pl.when((f == pl.num_programs(1) - 1) & (n > 0))
    def _():
        for j in range(N_SUB):
            rows = slice(j * SUB, (j + 1) * SUB)

            @pl.when(j < n)
            def _():
                o_ref[rows, :] = acc_s[j].astype(BF16)

            @pl.when(j >= n)
            def _():
                o_ref[rows, :] = jnp.zeros((SUB, D), BF16)


def _moe_experts(h, tokb, meta, wg, wu, wd, layer):
    tf = 512
    nf = D_FFE // tf
    se, nt, sblk, clo, chi = meta

    def w_up(s, f, se, nt, sblk, clo, chi):
        return (layer, se[s], 0, jnp.where(nt[s] > 0, f, nf - 1))

    def w_down(s, f, se, nt, sblk, clo, chi):
        return (layer, se[s], jnp.where(nt[s] > 0, f, nf - 1), 0)

    grid_spec = pltpu.PrefetchScalarGridSpec(
        num_scalar_prefetch=5,
        grid=(S_MAX, nf),
        in_specs=[pl.BlockSpec((SUP, LANE), lambda s, f, se, nt, sblk, clo, chi: (sblk[s], 0)),
                  pl.BlockSpec((M_ALL, D), lambda s, f, *_: (0, 0), pipeline_mode=pl.Buffered(1)),
                  pl.BlockSpec((None, None, D, tf), w_up), pl.BlockSpec((None, None, D, tf), w_up),
                  pl.BlockSpec((None, None, tf, D), w_down)],
        out_specs=pl.BlockSpec((SUP, D), lambda s, f, *_: (s, 0)),
        scratch_shapes=[pltpu.VMEM((N_SUB, SUB, D), BF16), pltpu.VMEM((N_SUB, SUB, D), F32),
                        pltpu.VMEM((D, tf), BF16), pltpu.VMEM((D, tf), BF16), pltpu.VMEM((tf, D), BF16)],
    )
    return pl.pallas_call(
        _moe_expert_kernel,
        out_shape=jax.ShapeDtypeStruct((P_SLOT, D), BF16),
        grid_spec=grid_spec,
        compiler_params=pltpu.CompilerParams(dimension_semantics=("arbitrary", "arbitrary"),
                                             vmem_limit_bytes=60 * 1024 * 1024),
        name="moe_experts",
    )(se, nt, sblk, clo, chi, tokb, h, wg, wu, wd)


def _moe_combine_kernel(off_ref, x_ref, gt_ref, gate_ref, pos_ref, *refs):
    win_refs, o_ref = refs[:N_EXP], refs[N_EXP]
    i = pl.program_id(0)
    lane = lax.broadcasted_iota(jnp.int32, (TC, WIN), 1)
    y = jnp.zeros((TC, D), F32)
    for e in range(N_EXP):
        rel = pos_ref[:, e:e + 1] - off_ref[i * N_EXP + e] * 16
        onehot = jnp.where(rel == lane, 1.0, 0.0).astype(BF16)
        y = y + gate_ref[:, e:e + 1] * jnp.dot(onehot, win_refs[e][...], preferred_element_type=F32)
    o_ref[...] = x_ref[...] + gt_ref[...] * y


def _moe_combine(x, modl, gates, pos, off, slots):
    def win_spec(e):
        return pl.BlockSpec((pl.Element(WIN), pl.Element(D)), lambda i, off: (off[i * N_EXP + e] * 16, 0))

    grid_spec = pltpu.PrefetchScalarGridSpec(
        num_scalar_prefetch=1,
        grid=(M_ALL // TC,),
        in_specs=[pl.BlockSpec((TC, D), lambda i, off: (i, 0)),
                  pl.BlockSpec((None, 1, D), lambda i, off: (_mod_row(i, TC) * 6 + 5, 0, 0)),
                  pl.BlockSpec((TC, LANE), lambda i, off: (i, 0)), pl.BlockSpec((TC, LANE), lambda i, off: (i, 0))]
        + [win_spec(e) for e in range(N_EXP)],
        out_specs=pl.BlockSpec((TC, D), lambda i, off: (i, 0)),
    )
    return pl.pallas_call(
        _moe_combine_kernel,
        out_shape=jax.ShapeDtypeStruct((M_ALL, D), F32),
        grid_spec=grid_spec,
        compiler_params=_cparams("arbitrary"),
        name="moe_combine",
    )(off, x, modl, gates, pos, *([slots] * N_EXP))


def _moe_plan(sel):
    i32 = jnp.int32
    sel8 = sel[:, :N_EXP]
    csum = jnp.cumsum(sel8, axis=0)
    rank = csum - sel8
    n_e = csum[-1]
    ns_e = (n_e + SUP - 1) // SUP
    end_e = jnp.cumsum(ns_e)
    start_e = end_e - ns_e
    n_used = end_e[-1]
    pos = jnp.where(sel8 > 0, start_e[None, :] * SUP + rank, -1)
    s_ids = jnp.arange(S_MAX, dtype=i32)
    sblk = jnp.minimum(s_ids, n_used - 1)
    se = jnp.sum((end_e[None, :] <= sblk[:, None]).astype(i32), axis=1)
    nv = jnp.clip(n_e[se] - (sblk - start_e[se]) * SUP, 0, SUP)
    nt = jnp.where(s_ids < n_used, (nv + SUB - 1) // SUB, 0)
    t_ids = jnp.broadcast_to(jnp.arange(M_ALL, dtype=i32)[:, None], (M_ALL, N_EXP))
    tok = jnp.full((P_SLOT,), -1, i32).at[jnp.where(sel8 > 0, pos, P_SLOT).reshape(-1)].set(
        t_ids.reshape(-1), mode="drop")
    tk = tok.reshape(P_SLOT // SUB, SUB)
    clo = jnp.maximum(tk[:, 0], 0) // CHUNK
    chi = jnp.max(tk, axis=1) // CHUNK
    tokb = jnp.broadcast_to(tok[:, None], (P_SLOT, LANE))
    before = jnp.concatenate([jnp.zeros((1, N_EXP), i32), csum[TC - 1::TC][:-1]], axis=0)
    base = start_e[None, :] * SUP + before
    off = jnp.minimum(base // 16, (P_SLOT - WIN) // 16).reshape(-1)
    pos128 = jnp.pad(pos, ((0, 0), (0, LANE - N_EXP)), constant_values=-1)
    return (se.astype(i32), nt.astype(i32), sblk.astype(i32), clo.astype(i32), chi.astype(i32)), tokb, pos128, \
        off.astype(i32)


def _moe(h, x, modl, gates, sel, wg, wu, wd, layer):
    meta, tokb, pos, off = _moe_plan(sel)
    slots = _moe_experts(h, tokb, meta, wg, wu, wd, layer)
    return _moe_combine(x, modl, gates, pos, off, slots)


def _final_kernel(x_ref, g_ref, o_ref):
    o_ref[...] = _rms(x_ref[...]) * g_ref[...]


def _final_norm(x, g):
    tm = 1024
    return pl.pallas_call(
        _final_kernel,
        out_shape=jax.ShapeDtypeStruct((M_ALL, D), F32),
        grid=(M_ALL // tm,),
        in_specs=[pl.BlockSpec((tm, D), lambda i: (i, 0)), pl.BlockSpec((1, D), lambda i: (0, 0))],
        out_specs=pl.BlockSpec((tm, D), lambda i: (i, 0)),
        compiler_params=_cparams("parallel"),
        name="final_norm",
    )(x, g)


def _pad_cols(w, width):
    return jnp.pad(w, ((0, 0), (0, 0), (0, width - w.shape[-1])))


def kernel(x_prompt, x_sample, state_gla, state_ret, cache_mla_ckv, cache_mla_krope, cache_gqa_k, cache_gqa_v,
           c, c_ctx, norm1_g, norm2_g, final_norm_g, w_mod, b_mod, w_in, w_out, gla_gate_w, gla_gate_b,
           mla_q_norm_g, mla_w_q_up, mla_kv_norm_g, mla_w_kv_up, ret_decay, gqa_sink,
           ffn_w_gate, ffn_w_up, ffn_w_down, moe_router, moe_w_gate, moe_w_up, moe_w_down):
    x = jnp.concatenate([x_prompt.reshape(M_CTX, D), x_sample.reshape(M_LAT, D)], axis=0)

    cond = jnp.concatenate([c_ctx[None], c, jnp.zeros((8 - 1 - N_LAT, D), F32)], axis=0)
    mod = _modulation(cond, w_mod, b_mod)
    mod = mod[:, :1 + N_LAT].reshape(DEPTH, (1 + N_LAT) * 6, 1, D)

    w_gla = _pad_cols(w_in[:, :, 0:800], W_GLA).astype(BF16)
    w_mla = _pad_cols(w_in[:, :, 800:1216], W_MLA).astype(BF16)
    w_ret = w_in[:, :, 1216:2240].astype(BF16)
    w_gqa = w_in[:, :, 2240:2752].astype(BF16)
    w_out_b = w_out.astype(BF16)
    wq = mla_w_q_up.reshape(DEPTH, MLA_QR, MLA_H, MLA_NOPE + MLA_ROPE)
    wq = jnp.concatenate([wq[..., :MLA_NOPE].reshape(DEPTH, MLA_QR, MLA_H * MLA_NOPE),
                          wq[..., MLA_NOPE:].reshape(DEPTH, MLA_QR, MLA_H * MLA_ROPE)], axis=-1).astype(BF16)
    wkv = mla_w_kv_up.reshape(DEPTH, MLA_KVR, MLA_H, MLA_NOPE + MLA_DV)
    wkv = jnp.concatenate([wkv[..., :MLA_NOPE].reshape(DEPTH, MLA_KVR, MLA_H * MLA_NOPE),
                           wkv[..., MLA_NOPE:].reshape(DEPTH, MLA_KVR, MLA_H * MLA_DV)], axis=-1).astype(BF16)
    router = jnp.pad(moe_router, ((0, 0), (0, 0), (0, LANE - N_EXP))).astype(BF16)
    dec_lanes = jnp.repeat(ret_decay, RET_DV, axis=-1)
    sink_lanes = jnp.repeat(gqa_sink, GQA_HD, axis=-1).reshape(DEPTH, 1, 256)
    gate_b = gla_gate_b.reshape(DEPTH, 2, 1, GLA_H * GLA_DK)
    s0_gla = jnp.swapaxes(state_gla, -1, -2)
    cache_kr = jnp.pad(cache_mla_krope, ((0, 0), (0, 0), (0, 0), (0, 128 - MLA_ROPE)))

    rope64 = _rope_tables(T_LAT, 64, 256)
    rope32_q = _rope_tables(T_LAT, 32, 128)
    ck, sl, sh = _rope_tables(T_LAT, 32, 128)
    live = jnp.asarray((np.arange(128) < MLA_ROPE).astype(np.float32))[None]
    rope32_k = (ck * live, sl * live, sh * live)

    gla_l, ret_l, ckv_l, kr_l, gk_l, gv_l = [], [], [], [], [], []
    for l in range(DEPTH):
        modl = mod[l]
        p_gla, p_mla, p_ret, p_gqa = _inproj(x, norm1_g[l][None], modl, w_gla[l], w_mla[l], w_ret[l], w_gqa[l])

        o_gla_c, st_gla = _gla(p_gla, gla_gate_w[l], gate_b[l], None, False)
        o_gla_s = _gla(p_gla, gla_gate_w[l], gate_b[l], s0_gla[:, l], True)
        o_ret_c, st_ret = _ret(p_ret, dec_lanes[l], None, None, False)
        o_ret_s = _ret(p_ret, dec_lanes[l], rope64, state_ret[:, l], True)
        gq, gkv = mla_q_norm_g[l][None], mla_kv_norm_g[l][None]
        o_mla_c, ckv = _mla(p_mla, gq, gkv, wq[l], wkv[l], None, None, None, None, False)
        o_mla_s = _mla(p_mla, gq, gkv, wq[l], wkv[l], rope32_q, rope32_k, cache_mla_ckv[:, l], cache_kr[:, l], True)
        o_gqa_c = _gqa(p_gqa, sink_lanes[l], None, None, None, False)
        o_gqa_s = _gqa(p_gqa, sink_lanes[l], rope64, cache_gqa_k[:, l], cache_gqa_v[:, l], True)

        cat = lambda a, b: jnp.concatenate([a, b], axis=0)
        j = l // 2
        mixed = (cat(o_gla_c, o_gla_s), cat(o_mla_c, o_mla_s), cat(o_ret_c, o_ret_s), cat(o_gqa_c, o_gqa_s))
        if l % 2 == 0:
            x, h2 = _outproj(x, *mixed, w_out_b[l], norm2_g[l][None], modl, None)
            x = _ffn(h2, x, modl, ffn_w_gate, ffn_w_up, ffn_w_down, j)
        else:
            x, h2, gates, sel = _outproj(x, *mixed, w_out_b[l], norm2_g[l][None], modl, router[j])
            x = _moe(h2, x, modl, gates, sel, moe_w_gate, moe_w_up, moe_w_down, j)

        gla_l.append(jnp.swapaxes(st_gla, -1, -2))
        ret_l.append(st_ret)
        ckv_l.append(ckv.reshape(N_CTX, T_CTX, MLA_KVR))
        kr_l.append(p_mla[:M_CTX, 384:384 + MLA_ROPE].reshape(N_CTX, T_CTX, MLA_ROPE))
        kv = p_gqa[:M_CTX, 256:512].reshape(N_CTX, T_CTX, 2, GQA_KV, GQA_HD)
        gk_l.append(kv[:, :, 0].transpose(0, 2, 1, 3))
        gv_l.append(kv[:, :, 1].transpose(0, 2, 1, 3))

    y = _final_norm(x, final_norm_g[None])
    return (y[:M_CTX].reshape(N_CTX, T_CTX, D), y[M_CTX:].reshape(N_LAT, T_LAT, D),
            jnp.stack(gla_l, axis=1), jnp.stack(ret_l, axis=1), jnp.stack(ckv_l, axis=1), jnp.stack(kr_l, axis=1),
            jnp.stack(gk_l, axis=1), jnp.stack(gv_l, axis=1))
```

```python
import functools

import numpy as np
import jax
import jax.numpy as jnp
from jax import lax
from jax.experimental import pallas as pl
from jax.experimental.pallas import tpu as pltpu

F32 = jnp.float32
BF16 = jnp.bfloat16
HIGHEST = lax.Precision.HIGHEST

D = 1024
N_CTX, T_CTX = 16, 256
N_LAT, T_LAT = 2, 1024
PAST = 256
DEPTH = 4
M_CTX = N_CTX * T_CTX
M_LAT = N_LAT * T_LAT
M_ALL = M_CTX + M_LAT
GRID_W = 64
ROPE_BASE = 10000.0
EPS = 1e-6

GLA_H, GLA_DK, GLA_DV, GLA_RANK, GLA_NORM, GLA_C = 4, 32, 64, 16, 16.0, 64
MLA_H, MLA_QR, MLA_KVR, MLA_NOPE, MLA_ROPE, MLA_DV = 4, 256, 128, 64, 32, 64
RET_H, RET_DK, RET_DV = 4, 64, 64
GQA_H, GQA_KV, GQA_HD, WINDOW = 4, 2, 64, 128
D_FF, N_EXP, D_FFE = 2816, 8, 3584

W_GLA, W_MLA, W_RET, W_GQA = 896, 512, 1024, 512
LANE = 128
NEG = -1e30
QB = 256
VMEM_LIMIT = 56 * 1024 * 1024


def _cparams(*sem):
    return pltpu.CompilerParams(dimension_semantics=sem, vmem_limit_bytes=VMEM_LIMIT)


def _mm(a, b):
    return jnp.dot(a.astype(BF16), b.astype(BF16), preferred_element_type=F32)


def _mm_nt(a, b):
    return lax.dot_general(a.astype(BF16), b.astype(BF16), (((1,), (1,)), ((), ())), preferred_element_type=F32)


def _mm_tn(a, b):
    return lax.dot_general(a.astype(BF16), b.astype(BF16), (((0,), (0,)), ((), ())), preferred_element_type=F32)


def _mm_f32(a, b):
    return jnp.dot(a, b, precision=HIGHEST, preferred_element_type=F32)


def _silu(x):
    return x * (1.0 / (1.0 + jnp.exp(-x)))


def _log_sigmoid(x):
    return jnp.minimum(x, 0.0) - jnp.log1p(jnp.exp(-jnp.abs(x)))


def _rms(x):
    return x * lax.rsqrt(jnp.mean(x * x, axis=-1, keepdims=True) + EPS)


def _mod_row(tile, tm):
    return jnp.maximum((tile * tm) // T_LAT - (M_CTX // T_LAT - 1), 0)


def _mod_kernel(c_ref, w_ref, b_ref, o_ref):
    o_ref[...] = _mm(_silu(c_ref[...]), w_ref[...]) + b_ref[...]


def _modulation(cond, w_mod, b_mod):
    tn = 1536
    return pl.pallas_call(
        _mod_kernel,
        out_shape=jax.ShapeDtypeStruct((DEPTH, 8, 6 * D), F32),
        grid=(DEPTH, 6 * D // tn),
        in_specs=[pl.BlockSpec((8, D), lambda l, j: (0, 0)),
                  pl.BlockSpec((None, D, tn), lambda l, j: (l, 0, j)),
                  pl.BlockSpec((None, 1, tn), lambda l, j: (l, 0, j))],
        out_specs=pl.BlockSpec((None, 8, tn), lambda l, j: (l, 0, j)),
        compiler_params=_cparams("parallel", "parallel"),
        name="modulation",
    )(cond, w_mod, b_mod.reshape(DEPTH, 1, 6 * D))


def _inproj_kernel(x_ref, g_ref, sh_ref, sc_ref, wa_ref, wb_ref, wc_ref, wd_ref, oa_ref, ob_ref, oc_ref, od_ref):
    h = (_rms(x_ref[...]) * g_ref[...] * (1.0 + sc_ref[...]) + sh_ref[...]).astype(BF16)
    oa_ref[...] = jnp.dot(h, wa_ref[...], preferred_element_type=F32)
    ob_ref[...] = jnp.dot(h, wb_ref[...], preferred_element_type=F32)
    oc_ref[...] = jnp.dot(h, wc_ref[...], preferred_element_type=F32)
    od_ref[...] = jnp.dot(h, wd_ref[...], preferred_element_type=F32)


def _mod_spec(j, tm):
    return pl.BlockSpec((None, 1, D), lambda i: (_mod_row(i, tm) * 6 + j, 0, 0))


def _inproj(x, g1, modl, wa, wb, wc, wd):
    tm = 512
    full = lambda w: pl.BlockSpec((D, w), lambda i: (0, 0))
    rows = lambda w: pl.BlockSpec((tm, w), lambda i: (i, 0))
    widths = (W_GLA, W_MLA, W_RET, W_GQA)
    return pl.pallas_call(
        _inproj_kernel,
        out_shape=[jax.ShapeDtypeStruct((M_ALL, w), F32) for w in widths],
        grid=(M_ALL // tm,),
        in_specs=[rows(D), pl.BlockSpec((1, D), lambda i: (0, 0)), _mod_spec(0, tm), _mod_spec(1, tm)]
        + [full(w) for w in widths],
        out_specs=[rows(w) for w in widths],
        compiler_params=_cparams("parallel"),
        name="inproj",
    )(x, g1, modl, modl, wa, wb, wc, wd)


def _head_rms_gate(o, gate):
    r = lax.broadcasted_iota(jnp.int32, (256, 256), 0) // 64
    c = lax.broadcasted_iota(jnp.int32, (256, 256), 1) // 64
    group_mean = jnp.where(r == c, 1.0 / 64.0, 0.0).astype(BF16)
    sq = o * o
    sq_hi = sq.astype(BF16)
    sq_lo = (sq - sq_hi.astype(F32)).astype(BF16)
    ms = (jnp.dot(sq_hi, group_mean, preferred_element_type=F32)
          + jnp.dot(sq_lo, group_mean, preferred_element_type=F32))
    return o * lax.rsqrt(ms + EPS) * _silu(gate)


def _rope(x, cos, sin_lo, sin_hi, half):
    w = x.shape[-1]
    return x * cos + pltpu.roll(x, w - half, 1) * sin_lo + pltpu.roll(x, half, 1) * sin_hi


def _rope_tables(t, head_dim, width):
    half = head_dim // 2
    quarter = head_dim // 4
    pos = np.arange(t)
    rows = (pos // GRID_W).astype(np.float32)
    cols = (pos % GRID_W).astype(np.float32)
    inv = np.power(np.float32(ROPE_BASE), -np.arange(quarter, dtype=np.float32) / np.float32(quarter)).astype(np.float32)
    ang = np.concatenate([rows[:, None] * inv, cols[:, None] * inv], axis=-1).astype(np.float32)
    lane = np.arange(width)
    a = ang[:, lane % half]
    cos, sin = np.cos(a).astype(np.float32), np.sin(a).astype(np.float32)
    low = (lane % head_dim) < half
    return (jnp.asarray(cos), jnp.asarray(np.where(low[None], -sin, 0.0).astype(np.float32)),
            jnp.asarray(np.where(low[None], 0.0, sin).astype(np.float32)))


def _gla_kernel(t, latent, *refs):
    if latent:
        p_ref, gw_ref, gb_ref, s0_ref, o_ref, la_f, la_b, of_s, ob_s, st_s = refs
    else:
        p_ref, gw_ref, gb_ref, o_ref, st_ref, la_f, la_b, of_s, ob_s, st_s = refs
    c = GLA_C
    n = t // c
    hd = GLA_H * GLA_DK
    la_f[...] = _log_sigmoid(_mm_f32(p_ref[:, 768:784], gw_ref[0]) + gb_ref[0]) / GLA_NORM
    la_b[...] = _log_sigmoid(_mm_f32(p_ref[:, 784:800], gw_ref[1]) + gb_ref[1]) / GLA_NORM

    ri = lax.broadcasted_iota(jnp.int32, (GLA_H * c, c), 0) % c
    ci = lax.broadcasted_iota(jnp.int32, (GLA_H * c, c), 1)
    tri_r = lax.broadcasted_iota(jnp.int32, (c, c), 0)
    tri_c = lax.broadcasted_iota(jnp.int32, (c, c), 1)
    head_rows = lax.broadcasted_iota(jnp.int32, (GLA_H * c, hd), 0) // c
    own_dk = (head_rows == lax.broadcasted_iota(jnp.int32, (GLA_H * c, hd), 1) // GLA_DK).astype(F32)
    own_dv = (lax.broadcasted_iota(jnp.int32, (GLA_H * c, GLA_H * GLA_DV), 0) // c
              == lax.broadcasted_iota(jnp.int32, (GLA_H * c, GLA_H * GLA_DV), 1) // GLA_DV).astype(F32)
    scale = GLA_DK ** -0.5
    if latent:
        st_s[...] = s0_ref[...]
    else:
        st_s[...] = jnp.zeros_like(st_s)

    def chunk(row0, la_ref, d):
        fwd = d == 0
        keep = (ci <= ri) if fwd else (ci >= ri)
        tri = ((tri_c <= tri_r) if fwd else (tri_c >= tri_r)).astype(BF16)
        q = p_ref[pl.ds(row0, c), 0:128]
        k = p_ref[pl.ds(row0, c), 128:256] * scale
        v = p_ref[pl.ds(row0, c), 256:512]
        la = la_ref[pl.ds(row0, c), :]
        la_hi = la.astype(BF16)
        la_lo = (la - la_hi.astype(F32)).astype(BF16)
        bc = (jnp.dot(tri, la_hi, preferred_element_type=F32) + jnp.dot(tri, la_lo, preferred_element_type=F32))
        tot = bc[c - 1:c, :] if fwd else bc[0:1, :]
        mid = bc[c // 2 - 1:c // 2, :] if fwd else bc[c // 2:c // 2 + 1, :]
        qe, ke = q * jnp.exp(bc - mid), k * jnp.exp(mid - bc)
        q_in, k_out, a = q * jnp.exp(bc), k * jnp.exp(tot - bc), jnp.exp(tot)
        q_rows = jnp.concatenate([qe] * GLA_H, axis=0) * own_dk
        att = jnp.where(keep, _mm_nt(q_rows, ke), 0.0)
        o_all = _mm(att, v) * own_dv
        o = o_all[0:c] + o_all[c:2 * c] + o_all[2 * c:3 * c] + o_all[3 * c:4 * c]
        st = st_s[d]
        o = o + _mm_nt(q_in, st)
        st_s[d] = st * a + _mm_tn(v, k_out) * own_dk
        return o

    unroll = 4

    def body(i, carry):
        for u in range(unroll):
            rf = pl.multiple_of((i * unroll + u) * c, c)
            rb = pl.multiple_of((n - 1 - i * unroll - u) * c, c)
            of_s[pl.ds(rf, c), :] = chunk(rf, la_f, 0)
            ob_s[pl.ds(rb, c), :] = chunk(rb, la_b, 1)
        return carry

    lax.fori_loop(0, n // unroll, body, 0)
    if not latent:
        st_ref[...] = st_s[...]
    for r in range(t // QB):
        rows = slice(r * QB, (r + 1) * QB)
        o_ref[rows, :] = _head_rms_gate(of_s[rows, :] + ob_s[rows, :], p_ref[rows, 512:768]).astype(BF16)


def _gla(p, gate_w, gate_b, s0_bd, latent):
    t, nseq, row0 = (T_LAT, N_LAT, M_CTX // T_LAT) if latent else (T_CTX, N_CTX, 0)
    st_shape = (2, GLA_H * GLA_DV, GLA_H * GLA_DK)
    in_specs = [pl.BlockSpec((t, W_GLA), lambda b: (row0 + b, 0)),
                pl.BlockSpec((2, GLA_RANK, GLA_H * GLA_DK), lambda b: (0, 0, 0)),
                pl.BlockSpec((2, 1, GLA_H * GLA_DK), lambda b: (0, 0, 0))]
    args = [p, gate_w, gate_b]
    o_shape = jax.ShapeDtypeStruct((nseq * t, 256), BF16)
    o_spec = pl.BlockSpec((t, 256), lambda b: (b, 0))
    if latent:
        in_specs.append(pl.BlockSpec((None,) + st_shape, lambda b: (b, 0, 0, 0)))
        args.append(s0_bd)
        out_shape, out_specs = o_shape, o_spec
    else:
        out_shape = [o_shape, jax.ShapeDtypeStruct((nseq,) + st_shape, F32)]
        out_specs = [o_spec, pl.BlockSpec((None,) + st_shape, lambda b: (b, 0, 0, 0))]
    return pl.pallas_call(
        functools.partial(_gla_kernel, t, latent),
        out_shape=out_shape, grid=(nseq,), in_specs=in_specs, out_specs=out_specs,
        scratch_shapes=[pltpu.VMEM((t, 128), F32), pltpu.VMEM((t, 128), F32),
                        pltpu.VMEM((t, 256), F32), pltpu.VMEM((t, 256), F32), pltpu.VMEM(st_shape, F32)],
        compiler_params=_cparams("parallel"),
        name="gla_latent" if latent else "gla_context",
    )(*args)


def _ret_kernel(t, latent, *refs):
    if latent:
        p_ref, dec_ref, cos_ref, slo_ref, shi_ref, s0_ref, o_ref, q_s, k_s = refs
    else:
        p_ref, dec_ref, o_ref, st_ref = refs
    lg = _log_sigmoid(dec_ref[...])
    scale = RET_DK ** -0.5
    if latent:
        q_s[...] = _rope(p_ref[:, 0:256], cos_ref[...], slo_ref[...], shi_ref[...], RET_DK // 2)
        k_s[...] = _rope(p_ref[:, 256:512], cos_ref[...], slo_ref[...], shi_ref[...], RET_DK // 2) * scale
        q_of = lambda rows, sl: q_s[rows, sl]
        k_of = lambda sl: k_s[:, sl]
    else:
        q_of = lambda rows, sl: p_ref[rows, sl]
        k_of = lambda sl: p_ref[:, 256 + sl.start:256 + sl.stop] * scale
    col = lax.broadcasted_iota(jnp.int32, (QB, t), 1)
    for r in range(t // QB):
        rows = slice(r * QB, (r + 1) * QB)
        row = lax.broadcasted_iota(jnp.int32, (QB, t), 0) + r * QB
        diff = (row - col).astype(F32)
        pos = (lax.broadcasted_iota(jnp.int32, (QB, 1), 0) + r * QB).astype(F32)
        outs = []
        for h in range(RET_H):
            sl = slice(h * RET_DK, (h + 1) * RET_DK)
            lg_f, lg_b = lg[0:1, h * 64:h * 64 + 1], lg[1:2, h * 64:h * 64 + 1]
            decay = (jnp.where(diff >= 0, jnp.exp(jnp.maximum(diff, 0.0) * lg_f), 0.0)
                     + jnp.where(diff <= 0, jnp.exp(jnp.maximum(-diff, 0.0) * lg_b), 0.0))
            q = q_of(rows, sl)
            v = p_ref[:, 512 + h * RET_DV:512 + (h + 1) * RET_DV]
            o = _mm(_mm_nt(q, k_of(sl)) * decay, v)
            if latent:
                o = o + _mm(q * jnp.exp((pos + 1.0) * lg_f), s0_ref[0, h])
                o = o + _mm(q * jnp.exp((float(t) - pos) * lg_b), s0_ref[1, h])
            outs.append(o)
        o_ref[rows, :] = _head_rms_gate(jnp.concatenate(outs, axis=-1), p_ref[rows, 768:1024]).astype(BF16)
    if not latent:
        j = lax.broadcasted_iota(jnp.int32, (t, 1), 0).astype(F32)
        for h in range(RET_H):
            sl = slice(h * RET_DK, (h + 1) * RET_DK)
            lg_f, lg_b = lg[0:1, h * 64:h * 64 + 1], lg[1:2, h * 64:h * 64 + 1]
            v = p_ref[:, 512 + h * RET_DV:512 + (h + 1) * RET_DV]
            k = k_of(sl)
            st_ref[0, h] = _mm_tn(k * jnp.exp((float(t - 1) - j) * lg_f), v)
            st_ref[1, h] = _mm_tn(k * jnp.exp(j * lg_b), v)


def _ret(p, dec_lanes, tables, s0, latent):
    t, nseq, row0 = (T_LAT, N_LAT, M_CTX // T_LAT) if latent else (T_CTX, N_CTX, 0)
    st_shape = (2, RET_H, RET_DK, RET_DV)
    in_specs = [pl.BlockSpec((t, W_RET), lambda b: (row0 + b, 0)), pl.BlockSpec((2, 256), lambda b: (0, 0))]
    args = [p, dec_lanes]
    o_shape = jax.ShapeDtypeStruct((nseq * t, 256), BF16)
    o_spec = pl.BlockSpec((t, 256), lambda b: (b, 0))
    scratch = []
    if latent:
        in_specs += [pl.BlockSpec((t, 256), lambda b: (0, 0))] * 3
        in_specs.append(pl.BlockSpec((None,) + st_shape, lambda b: (b, 0, 0, 0, 0)))
        args += list(tables) + [s0]
        out_shape, out_specs = o_shape, o_spec
        scratch = [pltpu.VMEM((t, 256), F32), pltpu.VMEM((t, 256), F32)]
    else:
        out_shape = [o_shape, jax.ShapeDtypeStruct((nseq,) + st_shape, F32)]
        out_specs = [o_spec, pl.BlockSpec((None,) + st_shape, lambda b: (b, 0, 0, 0, 0))]
    return pl.pallas_call(
        functools.partial(_ret_kernel, t, latent),
        out_shape=out_shape, grid=(nseq,), in_specs=in_specs, out_specs=out_specs, scratch_shapes=scratch,
        compiler_params=_cparams("parallel"),
        name="ret_latent" if latent else "ret_context",
    )(*args)


def _mla_kernel(t, latent, *refs):
    if latent:
        (p_ref, gq_ref, gkv_ref, wq_ref, wkv_ref, cq_ref, slq_ref, shq_ref, ck_ref, slk_ref, shk_ref,
         cckv_ref, ckr_ref, o_ref, qn_s, qr_s, kn_s, kr_s, v_s) = refs
    else:
        p_ref, gq_ref, gkv_ref, wq_ref, wkv_ref, o_ref, ckv_ref, qn_s, qr_s, kn_s, kr_s, v_s = refs
    nk = t + (PAST if latent else 0)
    qh = _mm(_rms(p_ref[:, 0:256]) * gq_ref[...], wq_ref[...])
    ckv = _rms(p_ref[:, 256:384]) * gkv_ref[...]
    kv = _mm(ckv, wkv_ref[...])
    qn_s[...] = qh[:, 0:256]
    kn_s[0:t, :] = kv[:, 0:256]
    v_s[0:t, :] = kv[:, 256:512]
    if latent:
        qr_s[...] = _rope(qh[:, 256:384], cq_ref[...], slq_ref[...], shq_ref[...], MLA_ROPE // 2)
        kr_s[0:t, :] = _rope(p_ref[:, 384:512], ck_ref[...], slk_ref[...], shk_ref[...], MLA_ROPE // 2)
        kvc = _mm(cckv_ref[...], wkv_ref[...])
        kn_s[t:nk, :] = kvc[:, 0:256]
        v_s[t:nk, :] = kvc[:, 256:512]
        kr_s[t:nk, :] = ckr_ref[...]
    else:
        qr_s[...] = qh[:, 256:384]
        kr_s[...] = p_ref[:, 384:512]
        ckv_ref[...] = ckv
    scale = (MLA_NOPE + MLA_ROPE) ** -0.5
    for r in range(t // QB):
        rows = slice(r * QB, (r + 1) * QB)
        outs = []
        for h in range(MLA_H):
            s = (_mm_nt(qn_s[rows, h * 64:(h + 1) * 64], kn_s[:, h * 64:(h + 1) * 64])
                 + _mm_nt(qr_s[rows, h * 32:(h + 1) * 32], kr_s[:, 0:MLA_ROPE])) * scale
            e = jnp.exp(s - jnp.max(s, axis=-1, keepdims=True))
            outs.append(_mm(e, v_s[:, h * 64:(h + 1) * 64]) / jnp.sum(e, axis=-1, keepdims=True))
        o_ref[rows, :] = jnp.concatenate(outs, axis=-1).astype(BF16)


def _mla(p, gq, gkv, wq, wkv, tables_q, tables_k, cache_ckv, cache_kr, latent):
    t, nseq, row0 = (T_LAT, N_LAT, M_CTX // T_LAT) if latent else (T_CTX, N_CTX, 0)
    nk = t + (PAST if latent else 0)
    const = lambda shape: pl.BlockSpec(shape, lambda b: (0,) * len(shape))
    in_specs = [pl.BlockSpec((t, W_MLA), lambda b: (row0 + b, 0)), const((1, MLA_QR)), const((1, MLA_KVR)),
                const((MLA_QR, 384)), const((MLA_KVR, 512))]
    args = [p, gq, gkv, wq, wkv]
    o_shape = jax.ShapeDtypeStruct((nseq * t, 256), BF16)
    o_spec = pl.BlockSpec((t, 256), lambda b: (b, 0))
    if latent:
        in_specs += [const((t, 128))] * 6
        in_specs += [pl.BlockSpec((None, PAST, MLA_KVR), lambda b: (b, 0, 0)),
                     pl.BlockSpec((None, PAST, 128), lambda b: (b, 0, 0))]
        args += list(tables_q) + list(tables_k) + [cache_ckv, cache_kr]
        out_shape, out_specs = o_shape, o_spec
    else:
        out_shape = [o_shape, jax.ShapeDtypeStruct((nseq * t, MLA_KVR), F32)]
        out_specs = [o_spec, pl.BlockSpec((t, MLA_KVR), lambda b: (b, 0))]
    return pl.pallas_call(
        functools.partial(_mla_kernel, t, latent),
        out_shape=out_shape, grid=(nseq,), in_specs=in_specs, out_specs=out_specs,
        scratch_shapes=[pltpu.VMEM((t, 256), F32), pltpu.VMEM((t, 128), F32), pltpu.VMEM((nk, 256), F32),
                        pltpu.VMEM((nk, 128), F32), pltpu.VMEM((nk, 256), F32)],
        compiler_params=_cparams("parallel"),
        name="mla_latent" if latent else "mla_context",
    )(*args)


def _gqa_kernel(t, latent, *refs):
    if latent:
        p_ref, sink_ref, cos_ref, slo_ref, shi_ref, ck_ref, cv_ref, o_ref, q_s, k_s = refs
    else:
        p_ref, sink_ref, o_ref = refs
    scale = GQA_HD ** -0.5
    grp = GQA_H // GQA_KV
    if latent:
        q_s[...] = _rope(p_ref[:, 0:256], cos_ref[...], slo_ref[...], shi_ref[...], GQA_HD // 2)
        k_s[...] = _rope(p_ref[:, 256:384], cos_ref[:, 0:128], slo_ref[:, 0:128], shi_ref[:, 0:128], GQA_HD // 2)
    col = lax.broadcasted_iota(jnp.int32, (QB, t), 1)
    for r in range(t // QB):
        rows = slice(r * QB, (r + 1) * QB)
        if latent:
            row = lax.broadcasted_iota(jnp.int32, (QB, t), 0) + r * QB
            near = jnp.abs(row - col) <= WINDOW
        outs = []
        for h in range(GQA_H):
            kvh = h // grp
            ksl = slice(kvh * GQA_HD, (kvh + 1) * GQA_HD)
            sk = sink_ref[0:1, h * 64:h * 64 + 1]
            v = p_ref[:, 384 + kvh * GQA_HD:384 + (kvh + 1) * GQA_HD]
            if latent:
                q = q_s[rows, h * 64:(h + 1) * 64]
                s_loc = jnp.where(near, _mm_nt(q, k_s[:, ksl]) * scale, NEG)
                s_ctx = _mm_nt(q, ck_ref[kvh]) * scale
                m = jnp.maximum(jnp.maximum(jnp.max(s_loc, axis=-1, keepdims=True),
                                            jnp.max(s_ctx, axis=-1, keepdims=True)), sk)
                e_loc, e_ctx = jnp.exp(s_loc - m), jnp.exp(s_ctx - m)
                den = jnp.sum(e_loc, axis=-1, keepdims=True) + jnp.sum(e_ctx, axis=-1, keepdims=True) + jnp.exp(sk - m)
                outs.append((_mm(e_loc, v) + _mm(e_ctx, cv_ref[kvh])) / den)
            else:
                q = p_ref[rows, h * 64:(h + 1) * 64]
                s = _mm_nt(q, p_ref[:, 256 + ksl.start:256 + ksl.stop]) * scale
                m = jnp.maximum(jnp.max(s, axis=-1, keepdims=True), sk)
                e = jnp.exp(s - m)
                outs.append(_mm(e, v) / (jnp.sum(e, axis=-1, keepdims=True) + jnp.exp(sk - m)))
        o_ref[rows, :] = jnp.concatenate(outs, axis=-1).astype(BF16)


def _gqa(p, sink_lanes, tables, cache_k, cache_v, latent):
    t, nseq, row0 = (T_LAT, N_LAT, M_CTX // T_LAT) if latent else (T_CTX, N_CTX, 0)
    in_specs = [pl.BlockSpec((t, W_GQA), lambda b: (row0 + b, 0)), pl.BlockSpec((1, 256), lambda b: (0, 0))]
    args = [p, sink_lanes]
    scratch = []
    if latent:
        in_specs += [pl.BlockSpec((t, 256), lambda b: (0, 0))] * 3
        in_specs += [pl.BlockSpec((None, GQA_KV, PAST, GQA_HD), lambda b: (b, 0, 0, 0))] * 2
        args += list(tables) + [cache_k, cache_v]
        scratch = [pltpu.VMEM((t, 256), F32), pltpu.VMEM((t, 128), F32)]
    return pl.pallas_call(
        functools.partial(_gqa_kernel, t, latent),
        out_shape=jax.ShapeDtypeStruct((nseq * t, 256), BF16), grid=(nseq,), in_specs=in_specs,
        out_specs=pl.BlockSpec((t, 256), lambda b: (b, 0)), scratch_shapes=scratch,
        compiler_params=_cparams("parallel"),
        name="gqa_latent" if latent else "gqa_context",
    )(*args)


def _outproj_kernel(route, tm, x_ref, *refs):
    ctx_refs, lat_refs = refs[0:4], refs[4:8]
    w_ref, gt_ref, g_ref, sh_ref, sc_ref = refs[8:13]
    refs = refs[13:]
    is_lat = pl.program_id(0) >= M_CTX // tm
    mix = jnp.zeros((tm, D), F32)
    for m in range(4):
        o = jnp.where(is_lat, lat_refs[m][...], ctx_refs[m][...])
        mix = mix + jnp.dot(o, w_ref[m * 256:(m + 1) * 256, :], preferred_element_type=F32)
    x = x_ref[...] + gt_ref[...] * mix
    h = (_rms(x) * g_ref[...] * (1.0 + sc_ref[...]) + sh_ref[...]).astype(BF16)
    if not route:
        xo_ref, h_ref = refs
    else:
        r_ref, xo_ref, h_ref, gate_ref, sel_ref = refs
        lane = lax.broadcasted_iota(jnp.int32, gate_ref.shape, 1)
        logits = jnp.where(lane < N_EXP, jnp.dot(h, r_ref[...], preferred_element_type=F32), NEG)
        m1 = jnp.max(logits, axis=-1, keepdims=True)
        i1 = jnp.min(jnp.where(logits == m1, lane, LANE), axis=-1, keepdims=True)
        rest = jnp.where(lane == i1, NEG, logits)
        m2 = jnp.max(rest, axis=-1, keepdims=True)
        i2 = jnp.min(jnp.where(rest == m2, lane, LANE), axis=-1, keepdims=True)
        e2 = jnp.exp(m2 - m1)
        gate_ref[...] = jnp.where(lane == i1, 1.0 / (1.0 + e2), 0.0) + jnp.where(lane == i2, e2 / (1.0 + e2), 0.0)
        sel_ref[...] = jnp.where((lane == i1) | (lane == i2), 1, 0)
    xo_ref[...] = x
    h_ref[...] = h


def _outproj(x, o_ctx, o_lat, w_out, g2, modl, router):
    tm = 512
    n_ctx = M_CTX // tm
    route = router is not None
    rows = lambda w: pl.BlockSpec((tm, w), lambda i: (i, 0))
    ctx_spec = pl.BlockSpec((tm, 256), lambda i: (jnp.minimum(i, n_ctx - 1), 0))
    lat_spec = pl.BlockSpec((tm, 256), lambda i: (jnp.maximum(i - n_ctx, 0), 0))
    in_specs = [rows(D)] + [ctx_spec] * 4 + [lat_spec] * 4 + [
        pl.BlockSpec((D, D), lambda i: (0, 0)),
        _mod_spec(2, tm), pl.BlockSpec((1, D), lambda i: (0, 0)), _mod_spec(3, tm), _mod_spec(4, tm)]
    args = [x, *o_ctx, *o_lat, w_out, modl, g2, modl, modl]
    out_shape = [jax.ShapeDtypeStruct((M_ALL, D), F32), jax.ShapeDtypeStruct((M_ALL, D), BF16)]
    out_specs = [rows(D), rows(D)]
    if route:
        in_specs.append(pl.BlockSpec((D, LANE), lambda i: (0, 0)))
        args.append(router)
        out_shape += [jax.ShapeDtypeStruct((M_ALL, LANE), F32), jax.ShapeDtypeStruct((M_ALL, LANE), jnp.int32)]
        out_specs += [rows(LANE), rows(LANE)]
    return pl.pallas_call(
        functools.partial(_outproj_kernel, route, tm),
        out_shape=out_shape, grid=(M_ALL // tm,), in_specs=in_specs, out_specs=out_specs,
        compiler_params=_cparams("parallel"),
        name="outproj_route" if route else "outproj",
    )(*args)


def _ffn_kernel(h_ref, x_ref, gt_ref, wg_ref, wu_ref, wd_ref, o_ref, acc_ref):
    f = pl.program_id(1)

    @pl.when(f == 0)
    def _():
        acc_ref[...] = jnp.zeros_like(acc_ref)

    h = h_ref[...]
    g = jnp.dot(h, wg_ref[...].astype(BF16), preferred_element_type=F32)
    u = jnp.dot(h, wu_ref[...].astype(BF16), preferred_element_type=F32)
    acc_ref[...] += jnp.dot((_silu(g) * u).astype(BF16), wd_ref[...].astype(BF16), preferred_element_type=F32)

    @pl.when(f == pl.num_programs(1) - 1)
    def _():
        o_ref[...] = x_ref[...] + gt_ref[...] * acc_ref[...]


def _ffn(h, x, modl, wg, wu, wd, j):
    tm, tf = 1024, 256
    return pl.pallas_call(
        _ffn_kernel,
        out_shape=jax.ShapeDtypeStruct((M_ALL, D), F32),
        grid=(M_ALL // tm, D_FF // tf),
        in_specs=[pl.BlockSpec((tm, D), lambda i, f: (i, 0)), pl.BlockSpec((tm, D), lambda i, f: (i, 0)),
                  pl.BlockSpec((None, 1, D), lambda i, f: (_mod_row(i, tm) * 6 + 5, 0, 0)),
                  pl.BlockSpec((None, D, tf), lambda i, f: (j, 0, f)),
                  pl.BlockSpec((None, D, tf), lambda i, f: (j, 0, f)),
                  pl.BlockSpec((None, tf, D), lambda i, f: (j, f, 0))],
        out_specs=pl.BlockSpec((tm, D), lambda i, f: (i, 0)),
        scratch_shapes=[pltpu.VMEM((tm, D), F32)],
        compiler_params=_cparams("parallel", "arbitrary"),
        name="ffn_dense",
    )(h, x, modl, wg, wu, wd)


SUP, SUB, CHUNK = 2048, 256, 256
N_SUB = SUP // SUB
S_MAX = 2 * M_ALL // SUP + N_EXP
P_SLOT = S_MAX * SUP
TC = 256
WIN = TC + 16


def _moe_expert_kernel(se_ref, nt_ref, sblk_ref, clo_ref, chi_ref, pos_ref, h_ref, wg_ref, wu_ref, wd_ref, o_ref,
                       xs_s, acc_s, wg_s, wu_s, wd_s):
    s = pl.program_id(0)
    f = pl.program_id(1)
    n = nt_ref[s]

    @pl.when((f == 0) & (n == 0))
    def _():
        o_ref[...] = jnp.zeros_like(o_ref)

    @pl.when((f == 0) & (n > 0))
    def _():
        row = lax.broadcasted_iota(jnp.int32, (SUB, CHUNK), 0)

        def gather(j, carry):
            row0 = (s * N_SUB + j) * SUB
            acc_s[j] = jnp.zeros((SUB, D), F32)

            def chunk(c, carry):
                tpos = pos_ref[:, pl.ds(pl.multiple_of(c * CHUNK, CHUNK), CHUNK)]
                onehot = jnp.where(tpos - row0 == row, 1.0, 0.0).astype(BF16)
                rows = h_ref[pl.ds(pl.multiple_of(c * CHUNK, CHUNK), CHUNK), :]
                acc_s[j] += jnp.dot(onehot, rows, preferred_element_type=F32)
                return carry

            g = s * N_SUB + j
            lax.fori_loop(clo_ref[g], chi_ref[g] + 1, chunk, 0)
            xs_s[j] = acc_s[j].astype(BF16)
            acc_s[j] = jnp.zeros((SUB, D), F32)
            return carry

        lax.fori_loop(0, n, gather, 0)

    @pl.when(n > 0)
    def _():
        wg_s[...] = wg_ref[...].astype(BF16)
        wu_s[...] = wu_ref[...].astype(BF16)
        wd_s[...] = wd_ref[...].astype(BF16)

        def sub(j, carry):
            x = xs_s[j]
            g = jnp.dot(x, wg_s[...], preferred_element_type=F32)
            u = jnp.dot(x, wu_s[...], preferred_element_type=F32)
            acc_s[j] += jnp.dot((_silu(g) * u).astype(BF16), wd_s[...], preferred_element_type=F32)
            return carry

        lax.fori_loop(0, n, sub, 0)

    @pl.when((f == pl.num_programs(1) - 1) & (n > 0))
    def _():
        for j in range(N_SUB):
            rows = slice(j * SUB, (j + 1) * SUB)

            @pl.when(j < n)
            def _():
                o_ref[rows, :] = acc_s[j].astype(BF16)

            @pl.when(j >= n)
            def _():
                o_ref[rows, :] = jnp.zeros((SUB, D), BF16)


def _moe_experts(h, pos_t, meta, wg, wu, wd, layer):
    tf = 512
    nf = D_FFE // tf
    se, nt, sblk, clo, chi = meta

    def w_up(s, f, se, nt, sblk, clo, chi):
        return (layer, se[s], 0, jnp.where(nt[s] > 0, f, nf - 1))

    def w_down(s, f, se, nt, sblk, clo, chi):
        return (layer, se[s], jnp.where(nt[s] > 0, f, nf - 1), 0)

    grid_spec = pltpu.PrefetchScalarGridSpec(
        num_scalar_prefetch=5,
        grid=(S_MAX, nf),
        in_specs=[pl.BlockSpec((None, 1, M_ALL), lambda s, f, se, nt, sblk, clo, chi: (se[s], 0, 0)),
                  pl.BlockSpec((M_ALL, D), lambda s, f, *_: (0, 0), pipeline_mode=pl.Buffered(1)),
                  pl.BlockSpec((None, None, D, tf), w_up), pl.BlockSpec((None, None, D, tf), w_up),
                  pl.BlockSpec((None, None, tf, D), w_down)],
        out_specs=pl.BlockSpec((SUP, D), lambda s, f, *_: (s, 0)),
        scratch_shapes=[pltpu.VMEM((N_SUB, SUB, D), BF16), pltpu.VMEM((N_SUB, SUB, D), F32),
                        pltpu.VMEM((D, tf), BF16), pltpu.VMEM((D, tf), BF16), pltpu.VMEM((tf, D), BF16)],
    )
    return pl.pallas_call(
        _moe_expert_kernel,
        out_shape=jax.ShapeDtypeStruct((P_SLOT, D), BF16),
        grid_spec=grid_spec,
        compiler_params=pltpu.CompilerParams(dimension_semantics=("arbitrary", "arbitrary"),
                                             vmem_limit_bytes=60 * 1024 * 1024),
        name="moe_experts",
    )(se, nt, sblk, clo, chi, pos_t, h, wg, wu, wd)


def _moe_combine_kernel(off_ref, x_ref, gt_ref, gate_ref, pos_ref, *refs):
    win_refs, o_ref = refs[:N_EXP], refs[N_EXP]
    i = pl.program_id(0)
    lane = lax.broadcasted_iota(jnp.int32, (TC, WIN), 1)
    y = jnp.zeros((TC, D), F32)
    for e in range(N_EXP):
        rel = pos_ref[:, e:e + 1] - off_ref[i * N_EXP + e] * 16
        onehot = jnp.where(rel == lane, 1.0, 0.0).astype(BF16)
        y = y + gate_ref[:, e:e + 1] * jnp.dot(onehot, win_refs[e][...], preferred_element_type=F32)
    o_ref[...] = x_ref[...] + gt_ref[...] * y


def _moe_combine(x, modl, gates, pos, off, slots):
    def win_spec(e):
        return pl.BlockSpec((pl.Element(WIN), pl.Element(D)), lambda i, off: (off[i * N_EXP + e] * 16, 0))

    grid_spec = pltpu.PrefetchScalarGridSpec(
        num_scalar_prefetch=1,
        grid=(M_ALL // TC,),
        in_specs=[pl.BlockSpec((TC, D), lambda i, off: (i, 0)),
                  pl.BlockSpec((None, 1, D), lambda i, off: (_mod_row(i, TC) * 6 + 5, 0, 0)),
                  pl.BlockSpec((TC, LANE), lambda i, off: (i, 0)), pl.BlockSpec((TC, LANE), lambda i, off: (i, 0))]
        + [win_spec(e) for e in range(N_EXP)],
        out_specs=pl.BlockSpec((TC, D), lambda i, off: (i, 0)),
    )
    return pl.pallas_call(
        _moe_combine_kernel,
        out_shape=jax.ShapeDtypeStruct((M_ALL, D), F32),
        grid_spec=grid_spec,
        compiler_params=_cparams("arbitrary"),
        name="moe_combine",
    )(off, x, modl, gates, pos, *([slots] * N_EXP))


def _moe_plan(sel):
    i32 = jnp.int32
    sel8 = sel[:, :N_EXP]
    csum = jnp.cumsum(sel8, axis=0)
    rank = csum - sel8
    n_e = csum[-1]
    ns_e = (n_e + SUP - 1) // SUP
    end_e = jnp.cumsum(ns_e)
    start_e = end_e - ns_e
    n_used = end_e[-1]
    pos = jnp.where(sel8 > 0, start_e[None, :] * SUP + rank, -1)
    s_ids = jnp.arange(S_MAX, dtype=i32)
    sblk = jnp.minimum(s_ids, n_used - 1)
    se = jnp.sum((end_e[None, :] <= sblk[:, None]).astype(i32), axis=1)
    nv = jnp.clip(n_e[se] - (sblk - start_e[se]) * SUP, 0, SUP)
    nt = jnp.where(s_ids < n_used, (nv + SUB - 1) // SUB, 0)
    g_ids = jnp.arange(S_MAX * N_SUB, dtype=i32)
    s_g, e_g = g_ids // N_SUB, se[g_ids // N_SUB]
    r0 = (sblk[s_g] - start_e[e_g]) * SUP + (g_ids % N_SUB) * SUB
    r1 = jnp.minimum(r0 + SUB, n_e[e_g])
    live = (s_g < n_used) & (r0 < n_e[e_g])
    cs_g = jnp.take(csum, e_g, axis=1)
    t_first = jnp.sum((cs_g <= r0[None, :]).astype(i32), axis=0)
    t_last = jnp.sum((cs_g < r1[None, :]).astype(i32), axis=0)
    clo = jnp.where(live, t_first // CHUNK, 0)
    chi = jnp.where(live, t_last // CHUNK, -1)
    pos_t = pos.T.reshape(N_EXP, 1, M_ALL)
    before = jnp.concatenate([jnp.zeros((1, N_EXP), i32), csum[TC - 1::TC][:-1]], axis=0)
    base = start_e[None, :] * SUP + before
    off = jnp.minimum(base // 16, (P_SLOT - WIN) // 16).reshape(-1)
    pos128 = jnp.pad(pos, ((0, 0), (0, LANE - N_EXP)), constant_values=-1)
    meta = (se.astype(i32), nt.astype(i32), sblk.astype(i32), clo.astype(i32), chi.astype(i32))
    return meta, pos_t, pos128, off.astype(i32)


def _moe(h, x, modl, gates, sel, wg, wu, wd, layer):
    meta, pos_t, pos, off = _moe_plan(sel)
    slots = _moe_experts(h, pos_t, meta, wg, wu, wd, layer)
    return _moe_combine(x, modl, gates, pos, off, slots)


def _final_kernel(x_ref, g_ref, o_ref):
    o_ref[...] = _rms(x_ref[...]) * g_ref[...]


def _final_norm(x, g):
    tm = 1024
    return pl.pallas_call(
        _final_kernel,
        out_shape=jax.ShapeDtypeStruct((M_ALL, D), F32),
        grid=(M_ALL // tm,),
        in_specs=[pl.BlockSpec((tm, D), lambda i: (i, 0)), pl.BlockSpec((1, D), lambda i: (0, 0))],
        out_specs=pl.BlockSpec((tm, D), lambda i: (i, 0)),
        compiler_params=_cparams("parallel"),
        name="final_norm",
    )(x, g)


def _pad_cols(w, width):
    return jnp.pad(w, ((0, 0), (0, 0), (0, width - w.shape[-1])))


def kernel(x_prompt, x_sample, state_gla, state_ret, cache_mla_ckv, cache_mla_krope, cache_gqa_k, cache_gqa_v,
           c, c_ctx, norm1_g, norm2_g, final_norm_g, w_mod, b_mod, w_in, w_out, gla_gate_w, gla_gate_b,
           mla_q_norm_g, mla_w_q_up, mla_kv_norm_g, mla_w_kv_up, ret_decay, gqa_sink,
           ffn_w_gate, ffn_w_up, ffn_w_down, moe_router, moe_w_gate, moe_w_up, moe_w_down):
    x = jnp.concatenate([x_prompt.reshape(M_CTX, D), x_sample.reshape(M_LAT, D)], axis=0)

    cond = jnp.concatenate([c_ctx[None], c, jnp.zeros((8 - 1 - N_LAT, D), F32)], axis=0)
    mod = _modulation(cond, w_mod, b_mod)
    mod = mod[:, :1 + N_LAT].reshape(DEPTH, (1 + N_LAT) * 6, 1, D)

    w_gla = _pad_cols(w_in[:, :, 0:800], W_GLA).astype(BF16)
    w_mla = _pad_cols(w_in[:, :, 800:1216], W_MLA).astype(BF16)
    w_ret = w_in[:, :, 1216:2240].astype(BF16)
    w_gqa = w_in[:, :, 2240:2752].astype(BF16)
    w_out_b = w_out.astype(BF16)
    wq = mla_w_q_up.reshape(DEPTH, MLA_QR, MLA_H, MLA_NOPE + MLA_ROPE)
    wq = jnp.concatenate([wq[..., :MLA_NOPE].reshape(DEPTH, MLA_QR, MLA_H * MLA_NOPE),
                          wq[..., MLA_NOPE:].reshape(DEPTH, MLA_QR, MLA_H * MLA_ROPE)], axis=-1).astype(BF16)
    wkv = mla_w_kv_up.reshape(DEPTH, MLA_KVR, MLA_H, MLA_NOPE + MLA_DV)
    wkv = jnp.concatenate([wkv[..., :MLA_NOPE].reshape(DEPTH, MLA_KVR, MLA_H * MLA_NOPE),
                           wkv[..., MLA_NOPE:].reshape(DEPTH, MLA_KVR, MLA_H * MLA_DV)], axis=-1).astype(BF16)
    router = jnp.pad(moe_router, ((0, 0), (0, 0), (0, LANE - N_EXP))).astype(BF16)
    dec_lanes = jnp.repeat(ret_decay, RET_DV, axis=-1)
    sink_lanes = jnp.repeat(gqa_sink, GQA_HD, axis=-1).reshape(DEPTH, 1, 256)
    gate_b = gla_gate_b.reshape(DEPTH, 2, 1, GLA_H * GLA_DK)
    eye = jnp.eye(GLA_H, dtype=F32)
    s0_gla = jnp.einsum('bldhkv,hg->bldhvgk', state_gla, eye).reshape(
        N_LAT, DEPTH, 2, GLA_H * GLA_DV, GLA_H * GLA_DK)
    cache_kr = jnp.pad(cache_mla_krope, ((0, 0), (0, 0), (0, 0), (0, 128 - MLA_ROPE)))

    rope64 = _rope_tables(T_LAT, 64, 256)
    rope32_q = _rope_tables(T_LAT, 32, 128)
    ck, sl, sh = _rope_tables(T_LAT, 32, 128)
    live = jnp.asarray((np.arange(128) < MLA_ROPE).astype(np.float32))[None]
    rope32_k = (ck * live, sl * live, sh * live)

    gla_l, ret_l, ckv_l, kr_l, gk_l, gv_l = [], [], [], [], [], []
    for l in range(DEPTH):
        modl = mod[l]
        p_gla, p_mla, p_ret, p_gqa = _inproj(x, norm1_g[l][None], modl, w_gla[l], w_mla[l], w_ret[l], w_gqa[l])

        o_gla_c, st_gla = _gla(p_gla, gla_gate_w[l], gate_b[l], None, False)
        o_gla_s = _gla(p_gla, gla_gate_w[l], gate_b[l], s0_gla[:, l], True)
        o_ret_c, st_ret = _ret(p_ret, dec_lanes[l], None, None, False)
        o_ret_s = _ret(p_ret, dec_lanes[l], rope64, state_ret[:, l], True)
        gq, gkv = mla_q_norm_g[l][None], mla_kv_norm_g[l][None]
        o_mla_c, ckv = _mla(p_mla, gq, gkv, wq[l], wkv[l], None, None, None, None, False)
        o_mla_s = _mla(p_mla, gq, gkv, wq[l], wkv[l], rope32_q, rope32_k, cache_mla_ckv[:, l], cache_kr[:, l], True)
        o_gqa_c = _gqa(p_gqa, sink_lanes[l], None, None, None, False)
        o_gqa_s = _gqa(p_gqa, sink_lanes[l], rope64, cache_gqa_k[:, l], cache_gqa_v[:, l], True)

        j = l // 2
        o_ctx = (o_gla_c, o_mla_c, o_ret_c, o_gqa_c)
        o_lat = (o_gla_s, o_mla_s, o_ret_s, o_gqa_s)
        if l % 2 == 0:
            x, h2 = _outproj(x, o_ctx, o_lat, w_out_b[l], norm2_g[l][None], modl, None)
            x = _ffn(h2, x, modl, ffn_w_gate, ffn_w_up, ffn_w_down, j)
        else:
            x, h2, gates, sel = _outproj(x, o_ctx, o_lat, w_out_b[l], norm2_g[l][None], modl, router[j])
            x = _moe(h2, x, modl, gates, sel, moe_w_gate, moe_w_up, moe_w_down, j)

        st_blocks = st_gla.reshape(N_CTX, 2, GLA_H, GLA_DV, GLA_H, GLA_DK)
        gla_l.append(jnp.stack([st_blocks[:, :, h, :, h, :] for h in range(GLA_H)], axis=2).swapaxes(-1, -2))
        ret_l.append(st_ret)
        ckv_l.append(ckv.reshape(N_CTX, T_CTX, MLA_KVR))
        kr_l.append(p_mla[:M_CTX, 384:384 + MLA_ROPE].reshape(N_CTX, T_CTX, MLA_ROPE))
        kv = p_gqa[:M_CTX, 256:512].reshape(N_CTX, T_CTX, 2, GQA_KV, GQA_HD)
        gk_l.append(kv[:, :, 0].transpose(0, 2, 1, 3))
        gv_l.append(kv[:, :, 1].transpose(0, 2, 1, 3))

    y = _final_norm(x, final_norm_g[None])
    return (y[:M_CTX].reshape(N_CTX, T_CTX, D), y[M_CTX:].reshape(N_LAT, T_LAT, D),
            jnp.stack(gla_l, axis=1), jnp.stack(ret_l, axis=1), jnp.stack(ckv_l, axis=1), jnp.stack(kr_l, axis=1),
            jnp.stack(gk_l, axis=1), jnp.stack(gv_l, axis=1))
```

```python
import functools

import numpy as np
import jax
import jax.numpy as jnp
from jax import lax
from jax.experimental import pallas as pl
from jax.experimental.pallas import tpu as pltpu

F32 = jnp.float32
BF16 = jnp.bfloat16
HIGHEST = lax.Precision.HIGHEST

D = 1024
N_CTX, T_CTX = 16, 256
N_LAT, T_LAT = 2, 1024
PAST = 256
DEPTH = 4
M_CTX = N_CTX * T_CTX
M_LAT = N_LAT * T_LAT
M_ALL = M_CTX + M_LAT
GRID_W = 64
ROPE_BASE = 10000.0
EPS = 1e-6

GLA_H, GLA_DK, GLA_DV, GLA_RANK, GLA_NORM, GLA_C = 4, 32, 64, 16, 16.0, 64
MLA_H, MLA_QR, MLA_KVR, MLA_NOPE, MLA_ROPE, MLA_DV = 4, 256, 128, 64, 32, 64
RET_H, RET_DK, RET_DV = 4, 64, 64
GQA_H, GQA_KV, GQA_HD, WINDOW = 4, 2, 64, 128
D_FF, N_EXP, D_FFE = 2816, 8, 3584

W_GLA, W_MLA, W_RET, W_GQA = 896, 512, 1024, 512
LANE = 128
NEG = -1e30
QB = 256
VMEM_LIMIT = 56 * 1024 * 1024


def _cparams(*sem):
    return pltpu.CompilerParams(dimension_semantics=sem, vmem_limit_bytes=VMEM_LIMIT)


def _mm(a, b):
    return jnp.dot(a.astype(BF16), b.astype(BF16), preferred_element_type=F32)


def _mm_nt(a, b):
    return lax.dot_general(a.astype(BF16), b.astype(BF16), (((1,), (1,)), ((), ())), preferred_element_type=F32)


def _mm_tn(a, b):
    return lax.dot_general(a.astype(BF16), b.astype(BF16), (((0,), (0,)), ((), ())), preferred_element_type=F32)


def _mm_f32(a, b):
    return jnp.dot(a, b, precision=HIGHEST, preferred_element_type=F32)


def _silu(x):
    return x * (1.0 / (1.0 + jnp.exp(-x)))


def _log_sigmoid(x):
    return jnp.minimum(x, 0.0) - jnp.log1p(jnp.exp(-jnp.abs(x)))


def _rms(x):
    return x * lax.rsqrt(jnp.mean(x * x, axis=-1, keepdims=True) + EPS)


def _mod_row(tile, tm):
    return jnp.maximum((tile * tm) // T_LAT - (M_CTX // T_LAT - 1), 0)


def _mod_kernel(c_ref, w_ref, b_ref, o_ref):
    o_ref[...] = _mm(_silu(c_ref[...]), w_ref[...]) + b_ref[...]


def _modulation(cond, w_mod, b_mod):
    tn = 1536
    return pl.pallas_call(
        _mod_kernel,
        out_shape=jax.ShapeDtypeStruct((DEPTH, 8, 6 * D), F32),
        grid=(DEPTH, 6 * D // tn),
        in_specs=[pl.BlockSpec((8, D), lambda l, j: (0, 0)),
                  pl.BlockSpec((None, D, tn), lambda l, j: (l, 0, j)),
                  pl.BlockSpec((None, 1, tn), lambda l, j: (l, 0, j))],
        out_specs=pl.BlockSpec((None, 8, tn), lambda l, j: (l, 0, j)),
        compiler_params=_cparams("parallel", "parallel"),
        name="modulation",
    )(cond, w_mod, b_mod.reshape(DEPTH, 1, 6 * D))


def _inproj_kernel(split, tm, *refs):
    if split:
        xc_ref, xl_ref = refs[0:2]
        x = jnp.where(pl.program_id(0) >= M_CTX // tm, xl_ref[...], xc_ref[...])
        refs = refs[2:]
    else:
        x = refs[0][...]
        refs = refs[1:]
    g_ref, sh_ref, sc_ref, wa_ref, wb_ref, wc_ref, wd_ref, oa_ref, ob_ref, oc_ref, od_ref = refs
    h = (_rms(x) * g_ref[...] * (1.0 + sc_ref[...]) + sh_ref[...]).astype(BF16)
    oa_ref[...] = jnp.dot(h, wa_ref[...], preferred_element_type=F32)
    ob_ref[...] = jnp.dot(h, wb_ref[...], preferred_element_type=F32)
    oc_ref[...] = jnp.dot(h, wc_ref[...], preferred_element_type=F32)
    od_ref[...] = jnp.dot(h, wd_ref[...], preferred_element_type=F32)


def _mod_spec(l, j, tm):
    return pl.BlockSpec((None, None, 1, D), lambda i, *_: (l, _mod_row(i, tm) * 6 + j, 0, 0))


def _layer_spec(l, *shape):
    return pl.BlockSpec((None,) + shape, lambda *_: (l,) + (0,) * len(shape))


def _row_split_specs(tm, width):
    n_ctx = M_CTX // tm
    return [pl.BlockSpec((tm, width), lambda i: (jnp.minimum(i, n_ctx - 1), 0)),
            pl.BlockSpec((tm, width), lambda i: (jnp.maximum(i - n_ctx, 0), 0))]


def _inproj(xs, g1, mod, ws, l):
    tm = 512
    split = len(xs) == 2
    rows = lambda w: pl.BlockSpec((tm, w), lambda i: (i, 0))
    widths = (W_GLA, W_MLA, W_RET, W_GQA)
    x_specs = _row_split_specs(tm, D) if split else [rows(D)]
    return pl.pallas_call(
        functools.partial(_inproj_kernel, split, tm),
        out_shape=[jax.ShapeDtypeStruct((M_ALL, w), F32) for w in widths],
        grid=(M_ALL // tm,),
        in_specs=x_specs + [_layer_spec(l, 1, D), _mod_spec(l, 0, tm), _mod_spec(l, 1, tm)]
        + [_layer_spec(l, D, w) for w in widths],
        out_specs=[rows(w) for w in widths],
        compiler_params=_cparams("parallel"),
        name="inproj",
    )(*xs, g1, mod, mod, *ws)


def _head_rms_gate(o, gate):
    r = lax.broadcasted_iota(jnp.int32, (256, 256), 0) // 64
    c = lax.broadcasted_iota(jnp.int32, (256, 256), 1) // 64
    group_mean = jnp.where(r == c, 1.0 / 64.0, 0.0).astype(BF16)
    sq = o * o
    sq_hi = sq.astype(BF16)
    sq_lo = (sq - sq_hi.astype(F32)).astype(BF16)
    ms = (jnp.dot(sq_hi, group_mean, preferred_element_type=F32)
          + jnp.dot(sq_lo, group_mean, preferred_element_type=F32))
    return o * lax.rsqrt(ms + EPS) * _silu(gate)


def _rope(x, cos, sin_lo, sin_hi, half):
    w = x.shape[-1]
    return x * cos + pltpu.roll(x, w - half, 1) * sin_lo + pltpu.roll(x, half, 1) * sin_hi


def _rope_tables(t, head_dim, width):
    half = head_dim // 2
    quarter = head_dim // 4
    pos = np.arange(t)
    rows = (pos // GRID_W).astype(np.float32)
    cols = (pos % GRID_W).astype(np.float32)
    inv = np.power(np.float32(ROPE_BASE), -np.arange(quarter, dtype=np.float32) / np.float32(quarter)).astype(np.float32)
    ang = np.concatenate([rows[:, None] * inv, cols[:, None] * inv], axis=-1).astype(np.float32)
    lane = np.arange(width)
    a = ang[:, lane % half]
    cos, sin = np.cos(a).astype(np.float32), np.sin(a).astype(np.float32)
    low = (lane % head_dim) < half
    return (jnp.asarray(cos), jnp.asarray(np.where(low[None], -sin, 0.0).astype(np.float32)),
            jnp.asarray(np.where(low[None], 0.0, sin).astype(np.float32)))


def _gla_kernel(t, latent, *refs):
    if latent:
        p_ref, gw_ref, gb_ref, s0_ref, o_ref, la_f, la_b, of_s, ob_s, st_s = refs
    else:
        p_ref, gw_ref, gb_ref, o_ref, st_ref, la_f, la_b, of_s, ob_s, st_s = refs
    c = GLA_C
    n = t // c
    hd = GLA_H * GLA_DK
    la_f[...] = _log_sigmoid(_mm_f32(p_ref[:, 768:784], gw_ref[0]) + gb_ref[0]) / GLA_NORM
    la_b[...] = _log_sigmoid(_mm_f32(p_ref[:, 784:800], gw_ref[1]) + gb_ref[1]) / GLA_NORM

    ri = lax.broadcasted_iota(jnp.int32, (GLA_H * c, c), 0) % c
    ci = lax.broadcasted_iota(jnp.int32, (GLA_H * c, c), 1)
    tri_r = lax.broadcasted_iota(jnp.int32, (c, c), 0)
    tri_c = lax.broadcasted_iota(jnp.int32, (c, c), 1)
    head_rows = lax.broadcasted_iota(jnp.int32, (GLA_H * c, hd), 0) // c
    own_dk = (head_rows == lax.broadcasted_iota(jnp.int32, (GLA_H * c, hd), 1) // GLA_DK).astype(F32)
    own_dv = (lax.broadcasted_iota(jnp.int32, (GLA_H * c, GLA_H * GLA_DV), 0) // c
              == lax.broadcasted_iota(jnp.int32, (GLA_H * c, GLA_H * GLA_DV), 1) // GLA_DV).astype(F32)
    scale = GLA_DK ** -0.5
    if latent:
        st_s[...] = s0_ref[...]
    else:
        st_s[...] = jnp.zeros_like(st_s)

    def chunk(row0, la_ref, d):
        fwd = d == 0
        keep = (ci <= ri) if fwd else (ci >= ri)
        tri = ((tri_c <= tri_r) if fwd else (tri_c >= tri_r)).astype(BF16)
        q = p_ref[pl.ds(row0, c), 0:128]
        k = p_ref[pl.ds(row0, c), 128:256] * scale
        v = p_ref[pl.ds(row0, c), 256:512]
        la = la_ref[pl.ds(row0, c), :]
        la_hi = la.astype(BF16)
        la_lo = (la - la_hi.astype(F32)).astype(BF16)
        bc = (jnp.dot(tri, la_hi, preferred_element_type=F32) + jnp.dot(tri, la_lo, preferred_element_type=F32))
        tot = bc[c - 1:c, :] if fwd else bc[0:1, :]
        mid = bc[c // 2 - 1:c // 2, :] if fwd else bc[c // 2:c // 2 + 1, :]
        qe, ke = q * jnp.exp(bc - mid), k * jnp.exp(mid - bc)
        q_in, k_out, a = q * jnp.exp(bc), k * jnp.exp(tot - bc), jnp.exp(tot)
        q_rows = jnp.concatenate([qe] * GLA_H, axis=0) * own_dk
        att = jnp.where(keep, _mm_nt(q_rows, ke), 0.0)
        o_all = _mm(att, v) * own_dv
        o = o_all[0:c] + o_all[c:2 * c] + o_all[2 * c:3 * c] + o_all[3 * c:4 * c]
        st = st_s[d]
        o = o + _mm_nt(q_in, st)
        st_s[d] = st * a + _mm_tn(v, k_out) * own_dk
        return o

    unroll = 4

    def body(i, carry):
        for u in range(unroll):
            rf = pl.multiple_of((i * unroll + u) * c, c)
            rb = pl.multiple_of((n - 1 - i * unroll - u) * c, c)
            of_s[pl.ds(rf, c), :] = chunk(rf, la_f, 0)
            ob_s[pl.ds(rb, c), :] = chunk(rb, la_b, 1)
        return carry

    lax.fori_loop(0, n // unroll, body, 0)
    if not latent:
        st_ref[...] = st_s[...]
    for r in range(t // QB):
        rows = slice(r * QB, (r + 1) * QB)
        o_ref[rows, :] = _head_rms_gate(of_s[rows, :] + ob_s[rows, :], p_ref[rows, 512:768]).astype(BF16)


def _gla(p, gate_w, gate_b, s0_bd, l, latent):
    t, nseq, row0 = (T_LAT, N_LAT, M_CTX // T_LAT) if latent else (T_CTX, N_CTX, 0)
    st_shape = (2, GLA_H * GLA_DV, GLA_H * GLA_DK)
    in_specs = [pl.BlockSpec((t, W_GLA), lambda b: (row0 + b, 0)),
                _layer_spec(l, 2, GLA_RANK, GLA_H * GLA_DK), _layer_spec(l, 2, 1, GLA_H * GLA_DK)]
    args = [p, gate_w, gate_b]
    o_shape = jax.ShapeDtypeStruct((nseq * t, 256), BF16)
    o_spec = pl.BlockSpec((t, 256), lambda b: (b, 0))
    if latent:
        in_specs.append(pl.BlockSpec((None, None) + st_shape, lambda b: (b, l, 0, 0, 0)))
        args.append(s0_bd)
        out_shape, out_specs = o_shape, o_spec
    else:
        out_shape = [o_shape, jax.ShapeDtypeStruct((nseq,) + st_shape, F32)]
        out_specs = [o_spec, pl.BlockSpec((None,) + st_shape, lambda b: (b, 0, 0, 0))]
    return pl.pallas_call(
        functools.partial(_gla_kernel, t, latent),
        out_shape=out_shape, grid=(nseq,), in_specs=in_specs, out_specs=out_specs,
        scratch_shapes=[pltpu.VMEM((t, 128), F32), pltpu.VMEM((t, 128), F32),
                        pltpu.VMEM((t, 256), F32), pltpu.VMEM((t, 256), F32), pltpu.VMEM(st_shape, F32)],
        compiler_params=_cparams("parallel"),
        name="gla_latent" if latent else "gla_context",
    )(*args)


def _ret_kernel(t, latent, *refs):
    if latent:
        p_ref, dec_ref, cos_ref, slo_ref, shi_ref, s0_ref, o_ref, q_s, k_s = refs
    else:
        p_ref, dec_ref, o_ref, st_ref = refs
    lg = _log_sigmoid(dec_ref[...])
    scale = RET_DK ** -0.5
    if latent:
        q_s[...] = _rope(p_ref[:, 0:256], cos_ref[...], slo_ref[...], shi_ref[...], RET_DK // 2)
        k_s[...] = _rope(p_ref[:, 256:512], cos_ref[...], slo_ref[...], shi_ref[...], RET_DK // 2) * scale
        q_of = lambda rows, sl: q_s[rows, sl]
        k_of = lambda sl: k_s[:, sl]
    else:
        q_of = lambda rows, sl: p_ref[rows, sl]
        k_of = lambda sl: p_ref[:, 256 + sl.start:256 + sl.stop] * scale
    col = lax.broadcasted_iota(jnp.int32, (QB, t), 1)
    for r in range(t // QB):
        rows = slice(r * QB, (r + 1) * QB)
        row = lax.broadcasted_iota(jnp.int32, (QB, t), 0) + r * QB
        diff = (row - col).astype(F32)
        pos = (lax.broadcasted_iota(jnp.int32, (QB, 1), 0) + r * QB).astype(F32)
        outs = []
        for h in range(RET_H):
            sl = slice(h * RET_DK, (h + 1) * RET_DK)
            lg_f, lg_b = lg[0:1, h * 64:h * 64 + 1], lg[1:2, h * 64:h * 64 + 1]
            decay = (jnp.where(diff >= 0, jnp.exp(jnp.maximum(diff, 0.0) * lg_f), 0.0)
                     + jnp.where(diff <= 0, jnp.exp(jnp.maximum(-diff, 0.0) * lg_b), 0.0))
            q = q_of(rows, sl)
            v = p_ref[:, 512 + h * RET_DV:512 + (h + 1) * RET_DV]
            o = _mm(_mm_nt(q, k_of(sl)) * decay, v)
            if latent:
                o = o + _mm(q * jnp.exp((pos + 1.0) * lg_f), s0_ref[0, h])
                o = o + _mm(q * jnp.exp((float(t) - pos) * lg_b), s0_ref[1, h])
            outs.append(o)
        o_ref[rows, :] = _head_rms_gate(jnp.concatenate(outs, axis=-1), p_ref[rows, 768:1024]).astype(BF16)
    if not latent:
        j = lax.broadcasted_iota(jnp.int32, (t, 1), 0).astype(F32)
        for h in range(RET_H):
            sl = slice(h * RET_DK, (h + 1) * RET_DK)
            lg_f, lg_b = lg[0:1, h * 64:h * 64 + 1], lg[1:2, h * 64:h * 64 + 1]
            v = p_ref[:, 512 + h * RET_DV:512 + (h + 1) * RET_DV]
            k = k_of(sl)
            st_ref[0, h] = _mm_tn(k * jnp.exp((float(t - 1) - j) * lg_f), v)
            st_ref[1, h] = _mm_tn(k * jnp.exp(j * lg_b), v)


def _ret(p, dec_lanes, tables, s0, l, latent):
    t, nseq, row0 = (T_LAT, N_LAT, M_CTX // T_LAT) if latent else (T_CTX, N_CTX, 0)
    st_shape = (2, RET_H, RET_DK, RET_DV)
    in_specs = [pl.BlockSpec((t, W_RET), lambda b: (row0 + b, 0)), _layer_spec(l, 2, 256)]
    args = [p, dec_lanes]
    o_shape = jax.ShapeDtypeStruct((nseq * t, 256), BF16)
    o_spec = pl.BlockSpec((t, 256), lambda b: (b, 0))
    scratch = []
    if latent:
        in_specs += [pl.BlockSpec((t, 256), lambda b: (0, 0))] * 3
        in_specs.append(pl.BlockSpec((None, None) + st_shape, lambda b: (b, l, 0, 0, 0, 0)))
        args += list(tables) + [s0]
        out_shape, out_specs = o_shape, o_spec
        scratch = [pltpu.VMEM((t, 256), F32), pltpu.VMEM((t, 256), F32)]
    else:
        out_shape = [o_shape, jax.ShapeDtypeStruct((nseq,) + st_shape, F32)]
        out_specs = [o_spec, pl.BlockSpec((None,) + st_shape, lambda b: (b, 0, 0, 0, 0))]
    return pl.pallas_call(
        functools.partial(_ret_kernel, t, latent),
        out_shape=out_shape, grid=(nseq,), in_specs=in_specs, out_specs=out_specs, scratch_shapes=scratch,
        compiler_params=_cparams("parallel"),
        name="ret_latent" if latent else "ret_context",
    )(*args)


def _mla_kernel(t, latent, *refs):
    if latent:
        (p_ref, gq_ref, gkv_ref, wq_ref, wkv_ref, cq_ref, slq_ref, shq_ref, ck_ref, slk_ref, shk_ref,
         cckv_ref, ckr_ref, o_ref, qn_s, qr_s, kn_s, kr_s, v_s) = refs
    else:
        p_ref, gq_ref, gkv_ref, wq_ref, wkv_ref, o_ref, ckv_ref, qn_s, qr_s, kn_s, kr_s, v_s = refs
    nk = t + (PAST if latent else 0)
    qh = _mm(_rms(p_ref[:, 0:256]) * gq_ref[...], wq_ref[...])
    ckv = _rms(p_ref[:, 256:384]) * gkv_ref[...]
    kv = _mm(ckv, wkv_ref[...])
    qn_s[...] = qh[:, 0:256]
    kn_s[0:t, :] = kv[:, 0:256]
    v_s[0:t, :] = kv[:, 256:512]
    if latent:
        qr_s[...] = _rope(qh[:, 256:384], cq_ref[...], slq_ref[...], shq_ref[...], MLA_ROPE // 2)
        kr_s[0:t, :] = _rope(p_ref[:, 384:512], ck_ref[...], slk_ref[...], shk_ref[...], MLA_ROPE // 2)
        kvc = _mm(cckv_ref[...], wkv_ref[...])
        kn_s[t:nk, :] = kvc[:, 0:256]
        v_s[t:nk, :] = kvc[:, 256:512]
        kr_s[t:nk, :] = ckr_ref[...]
    else:
        qr_s[...] = qh[:, 256:384]
        kr_s[...] = p_ref[:, 384:512]
        ckv_ref[...] = ckv
    scale = (MLA_NOPE + MLA_ROPE) ** -0.5
    for r in range(t // QB):
        rows = slice(r * QB, (r + 1) * QB)
        outs = []
        for h in range(MLA_H):
            s = (_mm_nt(qn_s[rows, h * 64:(h + 1) * 64], kn_s[:, h * 64:(h + 1) * 64])
                 + _mm_nt(qr_s[rows, h * 32:(h + 1) * 32], kr_s[:, 0:MLA_ROPE])) * scale
            e = jnp.exp(s - jnp.max(s, axis=-1, keepdims=True))
            outs.append(_mm(e, v_s[:, h * 64:(h + 1) * 64]) / jnp.sum(e, axis=-1, keepdims=True))
        o_ref[rows, :] = jnp.concatenate(outs, axis=-1).astype(BF16)


def _mla(p, gq, gkv, wq, wkv, tables_q, tables_k, cache_ckv, cache_kr, l, latent):
    t, nseq, row0 = (T_LAT, N_LAT, M_CTX // T_LAT) if latent else (T_CTX, N_CTX, 0)
    nk = t + (PAST if latent else 0)
    const = lambda shape: pl.BlockSpec(shape, lambda b: (0,) * len(shape))
    in_specs = [pl.BlockSpec((t, W_MLA), lambda b: (row0 + b, 0)), _layer_spec(l, 1, MLA_QR),
                _layer_spec(l, 1, MLA_KVR), _layer_spec(l, MLA_QR, 384), _layer_spec(l, MLA_KVR, 512)]
    args = [p, gq, gkv, wq, wkv]
    o_shape = jax.ShapeDtypeStruct((nseq * t, 256), BF16)
    o_spec = pl.BlockSpec((t, 256), lambda b: (b, 0))
    if latent:
        in_specs += [const((t, 128))] * 6
        in_specs += [pl.BlockSpec((None, None, PAST, MLA_KVR), lambda b: (b, l, 0, 0)),
                     pl.BlockSpec((None, None, PAST, 128), lambda b: (b, l, 0, 0))]
        args += list(tables_q) + list(tables_k) + [cache_ckv, cache_kr]
        out_shape, out_specs = o_shape, o_spec
    else:
        out_shape = [o_shape, jax.ShapeDtypeStruct((nseq * t, MLA_KVR), F32)]
        out_specs = [o_spec, pl.BlockSpec((t, MLA_KVR), lambda b: (b, 0))]
    return pl.pallas_call(
        functools.partial(_mla_kernel, t, latent),
        out_shape=out_shape, grid=(nseq,), in_specs=in_specs, out_specs=out_specs,
        scratch_shapes=[pltpu.VMEM((t, 256), F32), pltpu.VMEM((t, 128), F32), pltpu.VMEM((nk, 256), F32),
                        pltpu.VMEM((nk, 128), F32), pltpu.VMEM((nk, 256), F32)],
        compiler_params=_cparams("parallel"),
        name="mla_latent" if latent else "mla_context",
    )(*args)


def _gqa_kernel(t, latent, *refs):
    if latent:
        p_ref, sink_ref, cos_ref, slo_ref, shi_ref, ck_ref, cv_ref, o_ref, q_s, k_s = refs
    else:
        p_ref, sink_ref, o_ref = refs
    scale = GQA_HD ** -0.5
    grp = GQA_H // GQA_KV
    if latent:
        q_s[...] = _rope(p_ref[:, 0:256], cos_ref[...], slo_ref[...], shi_ref[...], GQA_HD // 2)
        k_s[...] = _rope(p_ref[:, 256:384], cos_ref[:, 0:128], slo_ref[:, 0:128], shi_ref[:, 0:128], GQA_HD // 2)
    col = lax.broadcasted_iota(jnp.int32, (QB, t), 1)
    for r in range(t // QB):
        rows = slice(r * QB, (r + 1) * QB)
        if latent:
            row = lax.broadcasted_iota(jnp.int32, (QB, t), 0) + r * QB
            near = jnp.abs(row - col) <= WINDOW
        outs = []
        for h in range(GQA_H):
            kvh = h // grp
            ksl = slice(kvh * GQA_HD, (kvh + 1) * GQA_HD)
            sk = sink_ref[0:1, h * 64:h * 64 + 1]
            v = p_ref[:, 384 + kvh * GQA_HD:384 + (kvh + 1) * GQA_HD]
            if latent:
                q = q_s[rows, h * 64:(h + 1) * 64]
                s_loc = jnp.where(near, _mm_nt(q, k_s[:, ksl]) * scale, NEG)
                s_ctx = _mm_nt(q, ck_ref[kvh]) * scale
                m = jnp.maximum(jnp.maximum(jnp.max(s_loc, axis=-1, keepdims=True),
                                            jnp.max(s_ctx, axis=-1, keepdims=True)), sk)
                e_loc, e_ctx = jnp.exp(s_loc - m), jnp.exp(s_ctx - m)
                den = jnp.sum(e_loc, axis=-1, keepdims=True) + jnp.sum(e_ctx, axis=-1, keepdims=True) + jnp.exp(sk - m)
                outs.append((_mm(e_loc, v) + _mm(e_ctx, cv_ref[kvh])) / den)
            else:
                q = p_ref[rows, h * 64:(h + 1) * 64]
                s = _mm_nt(q, p_ref[:, 256 + ksl.start:256 + ksl.stop]) * scale
                m = jnp.maximum(jnp.max(s, axis=-1, keepdims=True), sk)
                e = jnp.exp(s - m)
                outs.append(_mm(e, v) / (jnp.sum(e, axis=-1, keepdims=True) + jnp.exp(sk - m)))
        o_ref[rows, :] = jnp.concatenate(outs, axis=-1).astype(BF16)


def _gqa(p, sink_lanes, tables, cache_k, cache_v, l, latent):
    t, nseq, row0 = (T_LAT, N_LAT, M_CTX // T_LAT) if latent else (T_CTX, N_CTX, 0)
    in_specs = [pl.BlockSpec((t, W_GQA), lambda b: (row0 + b, 0)), _layer_spec(l, 1, 256)]
    args = [p, sink_lanes]
    scratch = []
    if latent:
        in_specs += [pl.BlockSpec((t, 256), lambda b: (0, 0))] * 3
        in_specs += [pl.BlockSpec((None, None, GQA_KV, PAST, GQA_HD), lambda b: (b, l, 0, 0, 0))] * 2
        args += list(tables) + [cache_k, cache_v]
        scratch = [pltpu.VMEM((t, 256), F32), pltpu.VMEM((t, 128), F32)]
    return pl.pallas_call(
        functools.partial(_gqa_kernel, t, latent),
        out_shape=jax.ShapeDtypeStruct((nseq * t, 256), BF16), grid=(nseq,), in_specs=in_specs,
        out_specs=pl.BlockSpec((t, 256), lambda b: (b, 0)), scratch_shapes=scratch,
        compiler_params=_cparams("parallel"),
        name="gqa_latent" if latent else "gqa_context",
    )(*args)


def _outproj_kernel(route, split, tm, *refs):
    is_lat = pl.program_id(0) >= M_CTX // tm
    if split:
        x_in = jnp.where(is_lat, refs[1][...], refs[0][...])
        refs = refs[2:]
    else:
        x_in = refs[0][...]
        refs = refs[1:]
    ctx_refs, lat_refs = refs[0:4], refs[4:8]
    w_ref, gt_ref, g_ref, sh_ref, sc_ref = refs[8:13]
    refs = refs[13:]
    mix = jnp.zeros((tm, D), F32)
    for m in range(4):
        o = jnp.where(is_lat, lat_refs[m][...], ctx_refs[m][...])
        mix = mix + jnp.dot(o, w_ref[m * 256:(m + 1) * 256, :], preferred_element_type=F32)
    x = x_in + gt_ref[...] * mix
    h = (_rms(x) * g_ref[...] * (1.0 + sc_ref[...]) + sh_ref[...]).astype(BF16)
    if not route:
        xo_ref, h_ref = refs
    else:
        r_ref, xo_ref, h_ref, gate_ref, sel_ref = refs
        lane = lax.broadcasted_iota(jnp.int32, gate_ref.shape, 1)
        logits = jnp.where(lane < N_EXP, jnp.dot(h, r_ref[...], preferred_element_type=F32), NEG)
        m1 = jnp.max(logits, axis=-1, keepdims=True)
        i1 = jnp.min(jnp.where(logits == m1, lane, LANE), axis=-1, keepdims=True)
        rest = jnp.where(lane == i1, NEG, logits)
        m2 = jnp.max(rest, axis=-1, keepdims=True)
        i2 = jnp.min(jnp.where(rest == m2, lane, LANE), axis=-1, keepdims=True)
        e2 = jnp.exp(m2 - m1)
        gate_ref[...] = jnp.where(lane == i1, 1.0 / (1.0 + e2), 0.0) + jnp.where(lane == i2, e2 / (1.0 + e2), 0.0)
        sel_ref[...] = jnp.where((lane == i1) | (lane == i2), 1, 0)
    xo_ref[...] = x
    h_ref[...] = h


def _outproj(xs, o_ctx, o_lat, w_out, g2, mod, router, l):
    tm = 512
    route = router is not None
    split = len(xs) == 2
    rows = lambda w: pl.BlockSpec((tm, w), lambda i: (i, 0))
    ctx_spec, lat_spec = _row_split_specs(tm, 256)
    x_specs = _row_split_specs(tm, D) if split else [rows(D)]
    in_specs = x_specs + [ctx_spec] * 4 + [lat_spec] * 4 + [
        _layer_spec(l, D, D), _mod_spec(l, 2, tm), _layer_spec(l, 1, D), _mod_spec(l, 3, tm), _mod_spec(l, 4, tm)]
    args = [*xs, *o_ctx, *o_lat, w_out, mod, g2, mod, mod]
    out_shape = [jax.ShapeDtypeStruct((M_ALL, D), F32), jax.ShapeDtypeStruct((M_ALL, D), BF16)]
    out_specs = [rows(D), rows(D)]
    if route:
        in_specs.append(_layer_spec(l // 2, D, LANE))
        args.append(router)
        out_shape += [jax.ShapeDtypeStruct((M_ALL, LANE), F32), jax.ShapeDtypeStruct((M_ALL, LANE), jnp.int32)]
        out_specs += [rows(LANE), rows(LANE)]
    return pl.pallas_call(
        functools.partial(_outproj_kernel, route, split, tm),
        out_shape=out_shape, grid=(M_ALL // tm,), in_specs=in_specs, out_specs=out_specs,
        compiler_params=_cparams("parallel"),
        name="outproj_route" if route else "outproj",
    )(*args)


def _ffn_kernel(h_ref, x_ref, gt_ref, wg_ref, wu_ref, wd_ref, o_ref, acc_ref):
    f = pl.program_id(1)

    @pl.when(f == 0)
    def _():
        acc_ref[...] = jnp.zeros_like(acc_ref)

    h = h_ref[...]
    g = jnp.dot(h, wg_ref[...].astype(BF16), preferred_element_type=F32)
    u = jnp.dot(h, wu_ref[...].astype(BF16), preferred_element_type=F32)
    acc_ref[...] += jnp.dot((_silu(g) * u).astype(BF16), wd_ref[...].astype(BF16), preferred_element_type=F32)

    @pl.when(f == pl.num_programs(1) - 1)
    def _():
        o_ref[...] = x_ref[...] + gt_ref[...] * acc_ref[...]


def _ffn(h, x, mod, wg, wu, wd, l):
    tm, tf = 1024, 256
    j = l // 2
    return pl.pallas_call(
        _ffn_kernel,
        out_shape=jax.ShapeDtypeStruct((M_ALL, D), F32),
        grid=(M_ALL // tm, D_FF // tf),
        in_specs=[pl.BlockSpec((tm, D), lambda i, f: (i, 0)), pl.BlockSpec((tm, D), lambda i, f: (i, 0)),
                  _mod_spec(l, 5, tm),
                  pl.BlockSpec((None, D, tf), lambda i, f: (j, 0, f)),
                  pl.BlockSpec((None, D, tf), lambda i, f: (j, 0, f)),
                  pl.BlockSpec((None, tf, D), lambda i, f: (j, f, 0))],
        out_specs=pl.BlockSpec((tm, D), lambda i, f: (i, 0)),
        scratch_shapes=[pltpu.VMEM((tm, D), F32)],
        compiler_params=_cparams("parallel", "arbitrary"),
        name="ffn_dense",
    )(h, x, mod, wg, wu, wd)


SUP, SUB, CHUNK = 2048, 256, 256
N_SUB = SUP // SUB
S_MAX = 2 * M_ALL // SUP + N_EXP
P_SLOT = S_MAX * SUP
TC = 256
WIN = TC + 16


def _moe_expert_kernel(se_ref, nt_ref, sblk_ref, clo_ref, chi_ref, pos_ref, h_ref, wg_ref, wu_ref, wd_ref, o_ref,
                       xs_s, acc_s, wg_s, wu_s, wd_s):
    s = pl.program_id(0)
    f = pl.program_id(1)
    n = nt_ref[s]

    @pl.when((f == 0) & (n == 0))
    def _():
        o_ref[...] = jnp.zeros_like(o_ref)

    @pl.when((f == 0) & (n > 0))
    def _():
        row = lax.broadcasted_iota(jnp.int32, (SUB, CHUNK), 0)

        def gather(j, carry):
            row0 = (s * N_SUB + j) * SUB
            acc_s[j] = jnp.zeros((SUB, D), F32)

            def chunk(c, carry):
                tpos = pos_ref[:, pl.ds(pl.multiple_of(c * CHUNK, CHUNK), CHUNK)]
                onehot = jnp.where(tpos - row0 == row, 1.0, 0.0).astype(BF16)
                rows = h_ref[pl.ds(pl.multiple_of(c * CHUNK, CHUNK), CHUNK), :]
                acc_s[j] += jnp.dot(onehot, rows, preferred_element_type=F32)
                return carry

            g = s * N_SUB + j
            lax.fori_loop(clo_ref[g], chi_ref[g] + 1, chunk, 0)
            xs_s[j] = acc_s[j].astype(BF16)
            acc_s[j] = jnp.zeros((SUB, D), F32)
            return carry

        lax.fori_loop(0, n, gather, 0)

    @pl.when(n > 0)
    def _():
        wg_s[...] = wg_ref[...].astype(BF16)
        wu_s[...] = wu_ref[...].astype(BF16)
        wd_s[...] = wd_ref[...].astype(BF16)

        def sub(j, carry):
            x = xs_s[j]
            g = jnp.dot(x, wg_s[...], preferred_element_type=F32)
            u = jnp.dot(x, wu_s[...], preferred_element_type=F32)
            acc_s[j] += jnp.dot((_silu(g) * u).astype(BF16), wd_s[...], preferred_element_type=F32)
            return carry

        lax.fori_loop(0, n, sub, 0)

    @pl.when((f == pl.num_programs(1) - 1) & (n > 0))
    def _():
        for j in range(N_SUB):
            rows = slice(j * SUB, (j + 1) * SUB)

            @pl.when(j < n)
            def _():
                o_ref[rows, :] = acc_s[j].astype(BF16)

            @pl.when(j >= n)
            def _():
                o_ref[rows, :] = jnp.zeros((SUB, D), BF16)


def _moe_experts(h, pos_t, meta, wg, wu, wd, layer):
    tf = 512
    nf = D_FFE // tf
    se, nt, sblk, clo, chi = meta

    def w_up(s, f, se, nt, sblk, clo, chi):
        return (layer, se[s], 0, jnp.where(nt[s] > 0, f, nf - 1))

    def w_down(s, f, se, nt, sblk, clo, chi):
        return (layer, se[s], jnp.where(nt[s] > 0, f, nf - 1), 0)

    grid_spec = pltpu.PrefetchScalarGridSpec(
        num_scalar_prefetch=5,
        grid=(S_MAX, nf),
        in_specs=[pl.BlockSpec((None, 1, M_ALL), lambda s, f, se, nt, sblk, clo, chi: (se[s], 0, 0)),
                  pl.BlockSpec((M_ALL, D), lambda s, f, *_: (0, 0), pipeline_mode=pl.Buffered(1)),
                  pl.BlockSpec((None, None, D, tf), w_up), pl.BlockSpec((None, None, D, tf), w_up),
                  pl.BlockSpec((None, None, tf, D), w_down)],
        out_specs=pl.BlockSpec((SUP, D), lambda s, f, *_: (s, 0)),
        scratch_shapes=[pltpu.VMEM((N_SUB, SUB, D), BF16), pltpu.VMEM((N_SUB, SUB, D), F32),
                        pltpu.VMEM((D, tf), BF16), pltpu.VMEM((D, tf), BF16), pltpu.VMEM((tf, D), BF16)],
    )
    return pl.pallas_call(
        _moe_expert_kernel,
        out_shape=jax.ShapeDtypeStruct((P_SLOT, D), BF16),
        grid_spec=grid_spec,
        compiler_params=pltpu.CompilerParams(dimension_semantics=("arbitrary", "arbitrary"),
                                             vmem_limit_bytes=60 * 1024 * 1024),
        name="moe_experts",
    )(se, nt, sblk, clo, chi, pos_t, h, wg, wu, wd)


def _moe_combine_kernel(off_ref, x_ref, gt_ref, gate_ref, pos_ref, *refs):
    win_refs, o_ref = refs[:N_EXP], refs[N_EXP]
    i = pl.program_id(0)
    lane = lax.broadcasted_iota(jnp.int32, (TC, WIN), 1)
    y = jnp.zeros((TC, D), F32)
    for e in range(N_EXP):
        rel = pos_ref[:, e:e + 1] - off_ref[i * N_EXP + e] * 16
        onehot = jnp.where(rel == lane, 1.0, 0.0).astype(BF16)
        y = y + gate_ref[:, e:e + 1] * jnp.dot(onehot, win_refs[e][...], preferred_element_type=F32)
    o_ref[...] = x_ref[...] + gt_ref[...] * y


def _moe_combine(x, mod, gates, pos, off, slots, l):
    def win_spec(e):
        return pl.BlockSpec((pl.Element(WIN), pl.Element(D)), lambda i, off: (off[i * N_EXP + e] * 16, 0))

    grid_spec = pltpu.PrefetchScalarGridSpec(
        num_scalar_prefetch=1,
        grid=(M_ALL // TC,),
        in_specs=[pl.BlockSpec((TC, D), lambda i, off: (i, 0)),
                  _mod_spec(l, 5, TC),
                  pl.BlockSpec((TC, LANE), lambda i, off: (i, 0)), pl.BlockSpec((TC, LANE), lambda i, off: (i, 0))]
        + [win_spec(e) for e in range(N_EXP)],
        out_specs=pl.BlockSpec((TC, D), lambda i, off: (i, 0)),
    )
    return pl.pallas_call(
        _moe_combine_kernel,
        out_shape=jax.ShapeDtypeStruct((M_ALL, D), F32),
        grid_spec=grid_spec,
        compiler_params=_cparams("arbitrary"),
        name="moe_combine",
    )(off, x, mod, gates, pos, *([slots] * N_EXP))


def _moe_plan(sel):
    i32 = jnp.int32
    sel8 = sel[:, :N_EXP]
    csum = jnp.cumsum(sel8, axis=0)
    rank = csum - sel8
    n_e = csum[-1]
    ns_e = (n_e + SUP - 1) // SUP
    end_e = jnp.cumsum(ns_e)
    start_e = end_e - ns_e
    n_used = end_e[-1]
    pos = jnp.where(sel8 > 0, start_e[None, :] * SUP + rank, -1)
    s_ids = jnp.arange(S_MAX, dtype=i32)
    sblk = jnp.minimum(s_ids, n_used - 1)
    se = jnp.sum((end_e[None, :] <= sblk[:, None]).astype(i32), axis=1)
    nv = jnp.clip(n_e[se] - (sblk - start_e[se]) * SUP, 0, SUP)
    nt = jnp.where(s_ids < n_used, (nv + SUB - 1) // SUB, 0)
    g_ids = jnp.arange(S_MAX * N_SUB, dtype=i32)
    s_g, e_g = g_ids // N_SUB, se[g_ids // N_SUB]
    r0 = (sblk[s_g] - start_e[e_g]) * SUP + (g_ids % N_SUB) * SUB
    r1 = jnp.minimum(r0 + SUB, n_e[e_g])
    live = (s_g < n_used) & (r0 < n_e[e_g])
    cs_g = jnp.take(csum, e_g, axis=1)
    t_first = jnp.sum((cs_g <= r0[None, :]).astype(i32), axis=0)
    t_last = jnp.sum((cs_g < r1[None, :]).astype(i32), axis=0)
    clo = jnp.where(live, t_first // CHUNK, 0)
    chi = jnp.where(live, t_last // CHUNK, -1)
    pos_t = pos.T.reshape(N_EXP, 1, M_ALL)
    before = jnp.concatenate([jnp.zeros((1, N_EXP), i32), csum[TC - 1::TC][:-1]], axis=0)
    base = start_e[None, :] * SUP + before
    off = jnp.minimum(base // 16, (P_SLOT - WIN) // 16).reshape(-1)
    pos128 = jnp.pad(pos, ((0, 0), (0, LANE - N_EXP)), constant_values=-1)
    meta = (se.astype(i32), nt.astype(i32), sblk.astype(i32), clo.astype(i32), chi.astype(i32))
    return meta, pos_t, pos128, off.astype(i32)


def _moe(h, x, mod, gates, sel, wg, wu, wd, l):
    meta, pos_t, pos, off = _moe_plan(sel)
    slots = _moe_experts(h, pos_t, meta, wg, wu, wd, l // 2)
    return _moe_combine(x, mod, gates, pos, off, slots, l)


def _final_kernel(x_ref, g_ref, o_ref):
    o_ref[...] = _rms(x_ref[...]) * g_ref[...]


def _final_norm(x, g):
    tm = 1024
    return pl.pallas_call(
        _final_kernel,
        out_shape=jax.ShapeDtypeStruct((M_ALL, D), F32),
        grid=(M_ALL // tm,),
        in_specs=[pl.BlockSpec((tm, D), lambda i: (i, 0)), pl.BlockSpec((1, D), lambda i: (0, 0))],
        out_specs=pl.BlockSpec((tm, D), lambda i: (i, 0)),
        compiler_params=_cparams("parallel"),
        name="final_norm",
    )(x, g)


def _pad_cols(w, width):
    return jnp.pad(w, ((0, 0), (0, 0), (0, width - w.shape[-1])))


def kernel(x_prompt, x_sample, state_gla, state_ret, cache_mla_ckv, cache_mla_krope, cache_gqa_k, cache_gqa_v,
           c, c_ctx, norm1_g, norm2_g, final_norm_g, w_mod, b_mod, w_in, w_out, gla_gate_w, gla_gate_b,
           mla_q_norm_g, mla_w_q_up, mla_kv_norm_g, mla_w_kv_up, ret_decay, gqa_sink,
           ffn_w_gate, ffn_w_up, ffn_w_down, moe_router, moe_w_gate, moe_w_up, moe_w_down):
    xs = (x_prompt.reshape(M_CTX, D), x_sample.reshape(M_LAT, D))

    cond = jnp.concatenate([c_ctx[None], c, jnp.zeros((8 - 1 - N_LAT, D), F32)], axis=0)
    mod = _modulation(cond, w_mod, b_mod)
    mod = mod[:, :1 + N_LAT].reshape(DEPTH, (1 + N_LAT) * 6, 1, D)

    w_groups = (_pad_cols(w_in[:, :, 0:800], W_GLA).astype(BF16), _pad_cols(w_in[:, :, 800:1216], W_MLA).astype(BF16),
                w_in[:, :, 1216:2240].astype(BF16), w_in[:, :, 2240:2752].astype(BF16))
    w_out_b = w_out.astype(BF16)
    wq = mla_w_q_up.reshape(DEPTH, MLA_QR, MLA_H, MLA_NOPE + MLA_ROPE)
    wq = jnp.concatenate([wq[..., :MLA_NOPE].reshape(DEPTH, MLA_QR, MLA_H * MLA_NOPE),
                          wq[..., MLA_NOPE:].reshape(DEPTH, MLA_QR, MLA_H * MLA_ROPE)], axis=-1).astype(BF16)
    wkv = mla_w_kv_up.reshape(DEPTH, MLA_KVR, MLA_H, MLA_NOPE + MLA_DV)
    wkv = jnp.concatenate([wkv[..., :MLA_NOPE].reshape(DEPTH, MLA_KVR, MLA_H * MLA_NOPE),
                           wkv[..., MLA_NOPE:].reshape(DEPTH, MLA_KVR, MLA_H * MLA_DV)], axis=-1).astype(BF16)
    router = jnp.pad(moe_router, ((0, 0), (0, 0), (0, LANE - N_EXP))).astype(BF16)
    dec_lanes = jnp.repeat(ret_decay, RET_DV, axis=-1)
    sink_lanes = jnp.repeat(gqa_sink, GQA_HD, axis=-1).reshape(DEPTH, 1, 256)
    gate_b = gla_gate_b.reshape(DEPTH, 2, 1, GLA_H * GLA_DK)
    g1, g2 = norm1_g.reshape(DEPTH, 1, D), norm2_g.reshape(DEPTH, 1, D)
    gq, gkv = mla_q_norm_g.reshape(DEPTH, 1, MLA_QR), mla_kv_norm_g.reshape(DEPTH, 1, MLA_KVR)
    eye = jnp.eye(GLA_H, dtype=F32)
    s0_gla = jnp.einsum('bldhkv,hg->bldhvgk', state_gla, eye).reshape(
        N_LAT, DEPTH, 2, GLA_H * GLA_DV, GLA_H * GLA_DK)
    cache_kr = jnp.pad(cache_mla_krope, ((0, 0), (0, 0), (0, 0), (0, 128 - MLA_ROPE)))

    rope64 = _rope_tables(T_LAT, 64, 256)
    rope32_q = _rope_tables(T_LAT, 32, 128)
    ck, sl, sh = _rope_tables(T_LAT, 32, 128)
    live = jnp.asarray((np.arange(128) < MLA_ROPE).astype(np.float32))[None]
    rope32_k = (ck * live, sl * live, sh * live)

    gla_l, ret_l, ckv_l, kr_l, gk_l, gv_l = [], [], [], [], [], []
    for l in range(DEPTH):
        p_gla, p_mla, p_ret, p_gqa = _inproj(xs, g1, mod, w_groups, l)

        o_gla_c, st_gla = _gla(p_gla, gla_gate_w, gate_b, None, l, False)
        o_gla_s = _gla(p_gla, gla_gate_w, gate_b, s0_gla, l, True)
        o_ret_c, st_ret = _ret(p_ret, dec_lanes, None, None, l, False)
        o_ret_s = _ret(p_ret, dec_lanes, rope64, state_ret, l, True)
        o_mla_c, ckv = _mla(p_mla, gq, gkv, wq, wkv, None, None, None, None, l, False)
        o_mla_s = _mla(p_mla, gq, gkv, wq, wkv, rope32_q, rope32_k, cache_mla_ckv, cache_kr, l, True)
        o_gqa_c = _gqa(p_gqa, sink_lanes, None, None, None, l, False)
        o_gqa_s = _gqa(p_gqa, sink_lanes, rope64, cache_gqa_k, cache_gqa_v, l, True)

        o_ctx = (o_gla_c, o_mla_c, o_ret_c, o_gqa_c)
        o_lat = (o_gla_s, o_mla_s, o_ret_s, o_gqa_s)
        if l % 2 == 0:
            x, h2 = _outproj(xs, o_ctx, o_lat, w_out_b, g2, mod, None, l)
            x = _ffn(h2, x, mod, ffn_w_gate, ffn_w_up, ffn_w_down, l)
        else:
            x, h2, gates, sel = _outproj(xs, o_ctx, o_lat, w_out_b, g2, mod, router, l)
            x = _moe(h2, x, mod, gates, sel, moe_w_gate, moe_w_up, moe_w_down, l)
        xs = (x,)

        st_blocks = st_gla.reshape(N_CTX, 2, GLA_H, GLA_DV, GLA_H, GLA_DK)
        gla_l.append(jnp.stack([st_blocks[:, :, h, :, h, :] for h in range(GLA_H)], axis=2).swapaxes(-1, -2))
        ret_l.append(st_ret)
        ckv_l.append(ckv.reshape(N_CTX, T_CTX, MLA_KVR))
        kr_l.append(p_mla[:M_CTX, 384:384 + MLA_ROPE].reshape(N_CTX, T_CTX, MLA_ROPE))
        kv = p_gqa[:M_CTX, 256:512].reshape(N_CTX, T_CTX, 2, GQA_KV, GQA_HD)
        gk_l.append(kv[:, :, 0].transpose(0, 2, 1, 3))
        gv_l.append(kv[:, :, 1].transpose(0, 2, 1, 3))

    y = _final_norm(x, final_norm_g[None])
    return (y[:M_CTX].reshape(N_CTX, T_CTX, D), y[M_CTX:].reshape(N_LAT, T_LAT, D),
            jnp.stack(gla_l, axis=1), jnp.stack(ret_l, axis=1), jnp.stack(ckv_l, axis=1), jnp.stack(kr_l, axis=1),
            jnp.stack(gk_l, axis=1), jnp.stack(gv_l, axis=1))
```

```python
import functools

import numpy as np
import jax
import jax.numpy as jnp
from jax import lax
from jax.experimental import pallas as pl
from jax.experimental.pallas import tpu as pltpu

F32 = jnp.float32
BF16 = jnp.bfloat16
HIGHEST = lax.Precision.HIGHEST

D = 1024
N_CTX, T_CTX = 16, 256
N_LAT, T_LAT = 2, 1024
PAST = 256
DEPTH = 4
M_CTX = N_CTX * T_CTX
M_LAT = N_LAT * T_LAT
M_ALL = M_CTX + M_LAT
GRID_W = 64
ROPE_BASE = 10000.0
EPS = 1e-6

GLA_H, GLA_DK, GLA_DV, GLA_RANK, GLA_NORM, GLA_C = 4, 32, 64, 16, 16.0, 64
MLA_H, MLA_QR, MLA_KVR, MLA_NOPE, MLA_ROPE, MLA_DV = 4, 256, 128, 64, 32, 64
RET_H, RET_DK, RET_DV = 4, 64, 64
GQA_H, GQA_KV, GQA_HD, WINDOW = 4, 2, 64, 128
D_FF, N_EXP, D_FFE = 2816, 8, 3584

W_GLA, W_MLA, W_RET, W_GQA = 896, 512, 1024, 512
LANE = 128
NEG = -1e30
QB = 256
VMEM_LIMIT = 56 * 1024 * 1024


def _cparams(*sem):
    return pltpu.CompilerParams(dimension_semantics=sem, vmem_limit_bytes=VMEM_LIMIT)


def _mm(a, b):
    return jnp.dot(a.astype(BF16), b.astype(BF16), preferred_element_type=F32)


def _mm_nt(a, b):
    return lax.dot_general(a.astype(BF16), b.astype(BF16), (((1,), (1,)), ((), ())), preferred_element_type=F32)


def _mm_tn(a, b):
    return lax.dot_general(a.astype(BF16), b.astype(BF16), (((0,), (0,)), ((), ())), preferred_element_type=F32)


def _mm_f32(a, b):
    return jnp.dot(a, b, precision=HIGHEST, preferred_element_type=F32)


def _silu(x):
    return x * (1.0 / (1.0 + jnp.exp(-x)))


def _log_sigmoid(x):
    return jnp.minimum(x, 0.0) - jnp.log1p(jnp.exp(-jnp.abs(x)))


def _rms(x):
    return x * lax.rsqrt(jnp.mean(x * x, axis=-1, keepdims=True) + EPS)


def _mod_row(tile, tm):
    return jnp.maximum((tile * tm) // T_LAT - (M_CTX // T_LAT - 1), 0)


def _mod_kernel(c_ref, w_ref, b_ref, o_ref):
    o_ref[...] = _mm(_silu(c_ref[...]), w_ref[...]) + b_ref[...]


def _modulation(cond, w_mod, b_mod):
    tn = 1536
    return pl.pallas_call(
        _mod_kernel,
        out_shape=jax.ShapeDtypeStruct((DEPTH, 8, 6 * D), F32),
        grid=(DEPTH, 6 * D // tn),
        in_specs=[pl.BlockSpec((8, D), lambda l, j: (0, 0)),
                  pl.BlockSpec((None, D, tn), lambda l, j: (l, 0, j)),
                  pl.BlockSpec((None, 1, tn), lambda l, j: (l, 0, j))],
        out_specs=pl.BlockSpec((None, 8, tn), lambda l, j: (l, 0, j)),
        compiler_params=_cparams("parallel", "parallel"),
        name="modulation",
    )(cond, w_mod, b_mod.reshape(DEPTH, 1, 6 * D))


def _inproj_kernel(split, tm, *refs):
    if split:
        xc_ref, xl_ref = refs[0:2]
        x = jnp.where(pl.program_id(0) >= M_CTX // tm, xl_ref[...], xc_ref[...])
        refs = refs[2:]
    else:
        x = refs[0][...]
        refs = refs[1:]
    g_ref, sh_ref, sc_ref, w_ref = refs[0:4]
    h = (_rms(x) * g_ref[...] * (1.0 + sc_ref[...]) + sh_ref[...]).astype(BF16)
    col = 0
    for o_ref in refs[4:]:
        width = o_ref.shape[-1]
        o_ref[...] = jnp.dot(h, w_ref[:, col:col + width], preferred_element_type=F32)
        col += width


def _mod_spec(l, j, tm):
    return pl.BlockSpec((None, None, 1, D), lambda i, *_: (l, _mod_row(i, tm) * 6 + j, 0, 0))


def _layer_spec(l, *shape):
    return pl.BlockSpec((None,) + shape, lambda *_: (l,) + (0,) * len(shape))


def _stacked_output(prev, tail, l, nseq, in_specs, args, out_shape, out_specs, aliases):
    out_shape.append(jax.ShapeDtypeStruct((nseq, DEPTH) + tail, F32))
    out_specs.append(pl.BlockSpec((None, None) + tail, lambda b: (b, l) + (0,) * len(tail)))
    if prev is not None:
        in_specs.append(pl.BlockSpec(memory_space=pl.ANY))
        args.append(prev)
        aliases[len(args) - 1] = len(out_shape) - 1


def _row_split_specs(tm, width):
    n_ctx = M_CTX // tm
    return [pl.BlockSpec((tm, width), lambda i: (jnp.minimum(i, n_ctx - 1), 0)),
            pl.BlockSpec((tm, width), lambda i: (jnp.maximum(i - n_ctx, 0), 0))]


def _inproj(xs, g1, mod, w_all, l):
    tm = 512
    split = len(xs) == 2
    rows = lambda w: pl.BlockSpec((tm, w), lambda i: (i, 0))
    widths = (W_GLA, W_MLA, W_RET, W_GQA)
    x_specs = _row_split_specs(tm, D) if split else [rows(D)]
    return pl.pallas_call(
        functools.partial(_inproj_kernel, split, tm),
        out_shape=[jax.ShapeDtypeStruct((M_ALL, w), F32) for w in widths],
        grid=(M_ALL // tm,),
        in_specs=x_specs + [_layer_spec(l, 1, D), _mod_spec(l, 0, tm), _mod_spec(l, 1, tm)]
        + [_layer_spec(l, D, sum(widths))],
        out_specs=[rows(w) for w in widths],
        compiler_params=_cparams("parallel"),
        name="inproj",
    )(*xs, g1, mod, mod, w_all)


def _head_rms_gate(o, gate):
    r = lax.broadcasted_iota(jnp.int32, (256, 256), 0) // 64
    c = lax.broadcasted_iota(jnp.int32, (256, 256), 1) // 64
    group_mean = jnp.where(r == c, 1.0 / 64.0, 0.0).astype(BF16)
    sq = o * o
    sq_hi = sq.astype(BF16)
    sq_lo = (sq - sq_hi.astype(F32)).astype(BF16)
    ms = (jnp.dot(sq_hi, group_mean, preferred_element_type=F32)
          + jnp.dot(sq_lo, group_mean, preferred_element_type=F32))
    return o * lax.rsqrt(ms + EPS) * _silu(gate)


def _rope(x, cos, sin_lo, sin_hi, half):
    w = x.shape[-1]
    return x * cos + pltpu.roll(x, w - half, 1) * sin_lo + pltpu.roll(x, half, 1) * sin_hi


def _rope_tables(t, head_dim, width):
    half = head_dim // 2
    quarter = head_dim // 4
    pos = np.arange(t)
    rows = (pos // GRID_W).astype(np.float32)
    cols = (pos % GRID_W).astype(np.float32)
    inv = np.power(np.float32(ROPE_BASE), -np.arange(quarter, dtype=np.float32) / np.float32(quarter)).astype(np.float32)
    ang = np.concatenate([rows[:, None] * inv, cols[:, None] * inv], axis=-1).astype(np.float32)
    lane = np.arange(width)
    a = ang[:, lane % half]
    cos, sin = np.cos(a).astype(np.float32), np.sin(a).astype(np.float32)
    low = (lane % head_dim) < half
    return (jnp.asarray(cos), jnp.asarray(np.where(low[None], -sin, 0.0).astype(np.float32)),
            jnp.asarray(np.where(low[None], 0.0, sin).astype(np.float32)))


def _gla_kernel(t, latent, *refs):
    if latent:
        p_ref, gw_ref, gb_ref, s0_ref, o_ref, la_f, la_b, of_s, ob_s, st_s = refs
    else:
        p_ref, gw_ref, gb_ref = refs[0:3]
        o_ref, st_ref, la_f, la_b, of_s, ob_s, st_s = refs[-7:]
    c = GLA_C
    n = t // c
    hd = GLA_H * GLA_DK
    la_f[...] = _log_sigmoid(_mm_f32(p_ref[:, 768:784], gw_ref[0]) + gb_ref[0]) / GLA_NORM
    la_b[...] = _log_sigmoid(_mm_f32(p_ref[:, 784:800], gw_ref[1]) + gb_ref[1]) / GLA_NORM

    ri = lax.broadcasted_iota(jnp.int32, (GLA_H * c, c), 0) % c
    ci = lax.broadcasted_iota(jnp.int32, (GLA_H * c, c), 1)
    tri_r = lax.broadcasted_iota(jnp.int32, (c, c), 0)
    tri_c = lax.broadcasted_iota(jnp.int32, (c, c), 1)
    head_rows = lax.broadcasted_iota(jnp.int32, (GLA_H * c, hd), 0) // c
    own_dk = (head_rows == lax.broadcasted_iota(jnp.int32, (GLA_H * c, hd), 1) // GLA_DK).astype(F32)
    own_dv = (lax.broadcasted_iota(jnp.int32, (GLA_H * c, GLA_H * GLA_DV), 0) // c
              == lax.broadcasted_iota(jnp.int32, (GLA_H * c, GLA_H * GLA_DV), 1) // GLA_DV).astype(F32)
    scale = GLA_DK ** -0.5
    if latent:
        st_s[...] = s0_ref[...]
    else:
        st_s[...] = jnp.zeros_like(st_s)

    def chunk(row0, la_ref, d):
        fwd = d == 0
        keep = (ci <= ri) if fwd else (ci >= ri)
        tri = ((tri_c <= tri_r) if fwd else (tri_c >= tri_r)).astype(BF16)
        q = p_ref[pl.ds(row0, c), 0:128]
        k = p_ref[pl.ds(row0, c), 128:256] * scale
        v = p_ref[pl.ds(row0, c), 256:512]
        la = la_ref[pl.ds(row0, c), :]
        la_hi = la.astype(BF16)
        la_lo = (la - la_hi.astype(F32)).astype(BF16)
        bc = (jnp.dot(tri, la_hi, preferred_element_type=F32) + jnp.dot(tri, la_lo, preferred_element_type=F32))
        tot = bc[c - 1:c, :] if fwd else bc[0:1, :]
        mid = bc[c // 2 - 1:c // 2, :] if fwd else bc[c // 2:c // 2 + 1, :]
        qe, ke = q * jnp.exp(bc - mid), k * jnp.exp(mid - bc)
        q_in, k_out, a = q * jnp.exp(bc), k * jnp.exp(tot - bc), jnp.exp(tot)
        q_rows = jnp.concatenate([qe] * GLA_H, axis=0) * own_dk
        att = jnp.where(keep, _mm_nt(q_rows, ke), 0.0)
        o_all = _mm(att, v) * own_dv
        o = o_all[0:c] + o_all[c:2 * c] + o_all[2 * c:3 * c] + o_all[3 * c:4 * c]
        st = st_s[d]
        o = o + _mm_nt(q_in, st)
        st_s[d] = st * a + _mm_tn(v, k_out) * own_dk
        return o

    unroll = 4

    def body(i, carry):
        for u in range(unroll):
            rf = pl.multiple_of((i * unroll + u) * c, c)
            rb = pl.multiple_of((n - 1 - i * unroll - u) * c, c)
            of_s[pl.ds(rf, c), :] = chunk(rf, la_f, 0)
            ob_s[pl.ds(rb, c), :] = chunk(rb, la_b, 1)
        return carry

    lax.fori_loop(0, n // unroll, body, 0)
    if not latent:
        for d in range(2):
            for h in range(GLA_H):
                st_ref[d, h] = st_s[d, h * GLA_DV:(h + 1) * GLA_DV, h * GLA_DK:(h + 1) * GLA_DK]
    for r in range(t // QB):
        rows = slice(r * QB, (r + 1) * QB)
        o_ref[rows, :] = _head_rms_gate(of_s[rows, :] + ob_s[rows, :], p_ref[rows, 512:768]).astype(BF16)


def _gla(p, gate_w, gate_b, s0_bd, st_prev, l, latent):
    t, nseq, row0 = (T_LAT, N_LAT, M_CTX // T_LAT) if latent else (T_CTX, N_CTX, 0)
    st_shape = (2, GLA_H * GLA_DV, GLA_H * GLA_DK)
    in_specs = [pl.BlockSpec((t, W_GLA), lambda b: (row0 + b, 0)),
                _layer_spec(l, 2, GLA_RANK, GLA_H * GLA_DK), _layer_spec(l, 2, 1, GLA_H * GLA_DK)]
    args = [p, gate_w, gate_b]
    aliases = {}
    o_shape = jax.ShapeDtypeStruct((nseq * t, 256), BF16)
    o_spec = pl.BlockSpec((t, 256), lambda b: (b, 0))
    if latent:
        in_specs.append(pl.BlockSpec((None, None) + st_shape, lambda b: (b, l, 0, 0, 0)))
        args.append(s0_bd)
        out_shape, out_specs = o_shape, o_spec
    else:
        out_shape, out_specs = [o_shape], [o_spec]
        _stacked_output(st_prev, (2, GLA_H, GLA_DV, GLA_DK), l, nseq, in_specs, args, out_shape, out_specs, aliases)
    return pl.pallas_call(
        functools.partial(_gla_kernel, t, latent),
        out_shape=out_shape, grid=(nseq,), in_specs=in_specs, out_specs=out_specs, input_output_aliases=aliases,
        scratch_shapes=[pltpu.VMEM((t, 128), F32), pltpu.VMEM((t, 128), F32),
                        pltpu.VMEM((t, 256), F32), pltpu.VMEM((t, 256), F32), pltpu.VMEM(st_shape, F32)],
        compiler_params=_cparams("parallel"),
        name="gla_latent" if latent else "gla_context",
    )(*args)


def _ret_kernel(t, latent, *refs):
    if latent:
        p_ref, dec_ref, cos_ref, slo_ref, shi_ref, s0_ref, o_ref, q_s, k_s = refs
    else:
        p_ref, dec_ref = refs[0:2]
        o_ref, st_ref = refs[-2:]
    lg = _log_sigmoid(dec_ref[...])
    scale = RET_DK ** -0.5
    if latent:
        q_s[...] = _rope(p_ref[:, 0:256], cos_ref[...], slo_ref[...], shi_ref[...], RET_DK // 2)
        k_s[...] = _rope(p_ref[:, 256:512], cos_ref[...], slo_ref[...], shi_ref[...], RET_DK // 2) * scale
        q_of = lambda rows, sl: q_s[rows, sl]
        k_of = lambda sl: k_s[:, sl]
    else:
        q_of = lambda rows, sl: p_ref[rows, sl]
        k_of = lambda sl: p_ref[:, 256 + sl.start:256 + sl.stop] * scale
    col = lax.broadcasted_iota(jnp.int32, (QB, t), 1)
    for r in range(t // QB):
        rows = slice(r * QB, (r + 1) * QB)
        row = lax.broadcasted_iota(jnp.int32, (QB, t), 0) + r * QB
        diff = (row - col).astype(F32)
        pos = (lax.broadcasted_iota(jnp.int32, (QB, 1), 0) + r * QB).astype(F32)
        outs = []
        for h in range(RET_H):
            sl = slice(h * RET_DK, (h + 1) * RET_DK)
            lg_f, lg_b = lg[0:1, h * 64:h * 64 + 1], lg[1:2, h * 64:h * 64 + 1]
            decay = (jnp.where(diff >= 0, jnp.exp(jnp.maximum(diff, 0.0) * lg_f), 0.0)
                     + jnp.where(diff <= 0, jnp.exp(jnp.maximum(-diff, 0.0) * lg_b), 0.0))
            q = q_of(rows, sl)
            v = p_ref[:, 512 + h * RET_DV:512 + (h + 1) * RET_DV]
            o = _mm(_mm_nt(q, k_of(sl)) * decay, v)
            if latent:
                o = o + _mm(q * jnp.exp((pos + 1.0) * lg_f), s0_ref[0, h])
                o = o + _mm(q * jnp.exp((float(t) - pos) * lg_b), s0_ref[1, h])
            outs.append(o)
        o_ref[rows, :] = _head_rms_gate(jnp.concatenate(outs, axis=-1), p_ref[rows, 768:1024]).astype(BF16)
    if not latent:
        j = lax.broadcasted_iota(jnp.int32, (t, 1), 0).astype(F32)
        for h in range(RET_H):
            sl = slice(h * RET_DK, (h + 1) * RET_DK)
            lg_f, lg_b = lg[0:1, h * 64:h * 64 + 1], lg[1:2, h * 64:h * 64 + 1]
            v = p_ref[:, 512 + h * RET_DV:512 + (h + 1) * RET_DV]
            k = k_of(sl)
            st_ref[0, h] = _mm_tn(k * jnp.exp((float(t - 1) - j) * lg_f), v)
            st_ref[1, h] = _mm_tn(k * jnp.exp(j * lg_b), v)


def _ret(p, dec_lanes, tables, s0, st_prev, l, latent):
    t, nseq, row0 = (T_LAT, N_LAT, M_CTX // T_LAT) if latent else (T_CTX, N_CTX, 0)
    st_shape = (2, RET_H, RET_DK, RET_DV)
    in_specs = [pl.BlockSpec((t, W_RET), lambda b: (row0 + b, 0)), _layer_spec(l, 2, 256)]
    args = [p, dec_lanes]
    aliases = {}
    o_shape = jax.ShapeDtypeStruct((nseq * t, 256), BF16)
    o_spec = pl.BlockSpec((t, 256), lambda b: (b, 0))
    scratch = []
    if latent:
        in_specs += [pl.BlockSpec((t, 256), lambda b: (0, 0))] * 3
        in_specs.append(pl.BlockSpec((None, None) + st_shape, lambda b: (b, l, 0, 0, 0, 0)))
        args += list(tables) + [s0]
        out_shape, out_specs = o_shape, o_spec
        scratch = [pltpu.VMEM((t, 256), F32), pltpu.VMEM((t, 256), F32)]
    else:
        out_shape, out_specs = [o_shape], [o_spec]
        _stacked_output(st_prev, st_shape, l, nseq, in_specs, args, out_shape, out_specs, aliases)
    return pl.pallas_call(
        functools.partial(_ret_kernel, t, latent),
        out_shape=out_shape, grid=(nseq,), in_specs=in_specs, out_specs=out_specs, scratch_shapes=scratch,
        input_output_aliases=aliases,
        compiler_params=_cparams("parallel"),
        name="ret_latent" if latent else "ret_context",
    )(*args)


def _mla_kernel(t, latent, *refs):
    if latent:
        (p_ref, gq_ref, gkv_ref, wq_ref, wkv_ref, cq_ref, slq_ref, shq_ref, ck_ref, slk_ref, shk_ref,
         cckv_ref, ckr_ref, o_ref, qn_s, qr_s, kn_s, kr_s, v_s) = refs
    else:
        p_ref, gq_ref, gkv_ref, wq_ref, wkv_ref = refs[0:5]
        o_ref, ckv_ref, kro_ref, qn_s, qr_s, kn_s, kr_s, v_s = refs[-8:]
    nk = t + (PAST if latent else 0)
    qh = _mm(_rms(p_ref[:, 0:256]) * gq_ref[...], wq_ref[...])
    ckv = _rms(p_ref[:, 256:384]) * gkv_ref[...]
    kv = _mm(ckv, wkv_ref[...])
    qn_s[...] = qh[:, 0:256]
    kn_s[0:t, :] = kv[:, 0:256]
    v_s[0:t, :] = kv[:, 256:512]
    if latent:
        qr_s[...] = _rope(qh[:, 256:384], cq_ref[...], slq_ref[...], shq_ref[...], MLA_ROPE // 2)
        kr_s[0:t, :] = _rope(p_ref[:, 384:512], ck_ref[...], slk_ref[...], shk_ref[...], MLA_ROPE // 2)
        kvc = _mm(cckv_ref[...], wkv_ref[...])
        kn_s[t:nk, :] = kvc[:, 0:256]
        v_s[t:nk, :] = kvc[:, 256:512]
        kr_s[t:nk, :] = ckr_ref[...]
    else:
        qr_s[...] = qh[:, 256:384]
        kr_s[...] = p_ref[:, 384:512]
        ckv_ref[...] = ckv
        kro_ref[...] = p_ref[:, 384:384 + MLA_ROPE]
    scale = (MLA_NOPE + MLA_ROPE) ** -0.5
    for r in range(t // QB):
        rows = slice(r * QB, (r + 1) * QB)
        outs = []
        for h in range(MLA_H):
            s = (_mm_nt(qn_s[rows, h * 64:(h + 1) * 64], kn_s[:, h * 64:(h + 1) * 64])
                 + _mm_nt(qr_s[rows, h * 32:(h + 1) * 32], kr_s[:, 0:MLA_ROPE])) * scale
            e = jnp.exp(s - jnp.max(s, axis=-1, keepdims=True))
            outs.append(_mm(e, v_s[:, h * 64:(h + 1) * 64]) / jnp.sum(e, axis=-1, keepdims=True))
        o_ref[rows, :] = jnp.concatenate(outs, axis=-1).astype(BF16)


def _mla(p, gq, gkv, wq, wkv, tables_q, tables_k, cache_ckv, cache_kr, prev, l, latent):
    t, nseq, row0 = (T_LAT, N_LAT, M_CTX // T_LAT) if latent else (T_CTX, N_CTX, 0)
    nk = t + (PAST if latent else 0)
    const = lambda shape: pl.BlockSpec(shape, lambda b: (0,) * len(shape))
    in_specs = [pl.BlockSpec((t, W_MLA), lambda b: (row0 + b, 0)), _layer_spec(l, 1, MLA_QR),
                _layer_spec(l, 1, MLA_KVR), _layer_spec(l, MLA_QR, 384), _layer_spec(l, MLA_KVR, 512)]
    args = [p, gq, gkv, wq, wkv]
    aliases = {}
    o_shape = jax.ShapeDtypeStruct((nseq * t, 256), BF16)
    o_spec = pl.BlockSpec((t, 256), lambda b: (b, 0))
    if latent:
        in_specs += [const((t, 128))] * 6
        in_specs += [pl.BlockSpec((None, None, PAST, MLA_KVR), lambda b: (b, l, 0, 0)),
                     pl.BlockSpec((None, None, PAST, 128), lambda b: (b, l, 0, 0))]
        args += list(tables_q) + list(tables_k) + [cache_ckv, cache_kr]
        out_shape, out_specs = o_shape, o_spec
    else:
        out_shape, out_specs = [o_shape], [o_spec]
        prev = prev or (None, None)
        _stacked_output(prev[0], (t, MLA_KVR), l, nseq, in_specs, args, out_shape, out_specs, aliases)
        _stacked_output(prev[1], (t, MLA_ROPE), l, nseq, in_specs, args, out_shape, out_specs, aliases)
    return pl.pallas_call(
        functools.partial(_mla_kernel, t, latent),
        out_shape=out_shape, grid=(nseq,), in_specs=in_specs, out_specs=out_specs, input_output_aliases=aliases,
        scratch_shapes=[pltpu.VMEM((t, 256), F32), pltpu.VMEM((t, 128), F32), pltpu.VMEM((nk, 256), F32),
                        pltpu.VMEM((nk, 128), F32), pltpu.VMEM((nk, 256), F32)],
        compiler_params=_cparams("parallel"),
        name="mla_latent" if latent else "mla_context",
    )(*args)


def _gqa_kernel(t, latent, *refs):
    if latent:
        p_ref, sink_ref, cos_ref, slo_ref, shi_ref, ck_ref, cv_ref, o_ref, q_s, k_s = refs
    else:
        p_ref, sink_ref = refs[0:2]
        o_ref, ko_ref, vo_ref = refs[-3:]
        for kvh in range(GQA_KV):
            ko_ref[kvh] = p_ref[:, 256 + kvh * GQA_HD:256 + (kvh + 1) * GQA_HD]
            vo_ref[kvh] = p_ref[:, 384 + kvh * GQA_HD:384 + (kvh + 1) * GQA_HD]
    scale = GQA_HD ** -0.5
    grp = GQA_H // GQA_KV
    if latent:
        q_s[...] = _rope(p_ref[:, 0:256], cos_ref[...], slo_ref[...], shi_ref[...], GQA_HD // 2)
        k_s[...] = _rope(p_ref[:, 256:384], cos_ref[:, 0:128], slo_ref[:, 0:128], shi_ref[:, 0:128], GQA_HD // 2)
    col = lax.broadcasted_iota(jnp.int32, (QB, t), 1)
    for r in range(t // QB):
        rows = slice(r * QB, (r + 1) * QB)
        if latent:
            row = lax.broadcasted_iota(jnp.int32, (QB, t), 0) + r * QB
            near = jnp.abs(row - col) <= WINDOW
        outs = []
        for h in range(GQA_H):
            kvh = h // grp
            ksl = slice(kvh * GQA_HD, (kvh + 1) * GQA_HD)
            sk = sink_ref[0:1, h * 64:h * 64 + 1]
            v = p_ref[:, 384 + kvh * GQA_HD:384 + (kvh + 1) * GQA_HD]
            if latent:
                q = q_s[rows, h * 64:(h + 1) * 64]
                s_loc = jnp.where(near, _mm_nt(q, k_s[:, ksl]) * scale, NEG)
                s_ctx = _mm_nt(q, ck_ref[kvh]) * scale
                m = jnp.maximum(jnp.maximum(jnp.max(s_loc, axis=-1, keepdims=True),
                                            jnp.max(s_ctx, axis=-1, keepdims=True)), sk)
                e_loc, e_ctx = jnp.exp(s_loc - m), jnp.exp(s_ctx - m)
                den = jnp.sum(e_loc, axis=-1, keepdims=True) + jnp.sum(e_ctx, axis=-1, keepdims=True) + jnp.exp(sk - m)
                outs.append((_mm(e_loc, v) + _mm(e_ctx, cv_ref[kvh])) / den)
            else:
                q = p_ref[rows, h * 64:(h + 1) * 64]
                s = _mm_nt(q, p_ref[:, 256 + ksl.start:256 + ksl.stop]) * scale
                m = jnp.maximum(jnp.max(s, axis=-1, keepdims=True), sk)
                e = jnp.exp(s - m)
                outs.append(_mm(e, v) / (jnp.sum(e, axis=-1, keepdims=True) + jnp.exp(sk - m)))
        o_ref[rows, :] = jnp.concatenate(outs, axis=-1).astype(BF16)


def _gqa(p, sink_lanes, tables, cache_k, cache_v, prev, l, latent):
    t, nseq, row0 = (T_LAT, N_LAT, M_CTX // T_LAT) if latent else (T_CTX, N_CTX, 0)
    in_specs = [pl.BlockSpec((t, W_GQA), lambda b: (row0 + b, 0)), _layer_spec(l, 1, 256)]
    args = [p, sink_lanes]
    scratch = []
    aliases = {}
    out_shape = [jax.ShapeDtypeStruct((nseq * t, 256), BF16)]
    out_specs = [pl.BlockSpec((t, 256), lambda b: (b, 0))]
    if latent:
        in_specs += [pl.BlockSpec((t, 256), lambda b: (0, 0))] * 3
        in_specs += [pl.BlockSpec((None, None, GQA_KV, PAST, GQA_HD), lambda b: (b, l, 0, 0, 0))] * 2
        args += list(tables) + [cache_k, cache_v]
        scratch = [pltpu.VMEM((t, 256), F32), pltpu.VMEM((t, 128), F32)]
        out_shape, out_specs = out_shape[0], out_specs[0]
    else:
        prev = prev or (None, None)
        for pv in prev:
            _stacked_output(pv, (GQA_KV, t, GQA_HD), l, nseq, in_specs, args, out_shape, out_specs, aliases)
    return pl.pallas_call(
        functools.partial(_gqa_kernel, t, latent),
        out_shape=out_shape, grid=(nseq,), in_specs=in_specs, out_specs=out_specs, scratch_shapes=scratch,
        input_output_aliases=aliases,
        compiler_params=_cparams("parallel"),
        name="gqa_latent" if latent else "gqa_context",
    )(*args)


def _outproj_kernel(route, split, tm, *refs):
    is_lat = pl.program_id(0) >= M_CTX // tm
    if split:
        x_in = jnp.where(is_lat, refs[1][...], refs[0][...])
        refs = refs[2:]
    else:
        x_in = refs[0][...]
        refs = refs[1:]
    ctx_refs, lat_refs = refs[0:4], refs[4:8]
    w_ref, gt_ref, g_ref, sh_ref, sc_ref = refs[8:13]
    refs = refs[13:]
    mix = jnp.zeros((tm, D), F32)
    for m in range(4):
        o = jnp.where(is_lat, lat_refs[m][...], ctx_refs[m][...])
        mix = mix + jnp.dot(o, w_ref[m * 256:(m + 1) * 256, :], preferred_element_type=F32)
    x = x_in + gt_ref[...] * mix
    h = (_rms(x) * g_ref[...] * (1.0 + sc_ref[...]) + sh_ref[...]).astype(BF16)
    if not route:
        xo_ref, h_ref = refs
    else:
        r_ref, xo_ref, h_ref, gate_ref, sel_ref = refs
        lane = lax.broadcasted_iota(jnp.int32, gate_ref.shape, 1)
        logits = jnp.where(lane < N_EXP, jnp.dot(h, r_ref[...], preferred_element_type=F32), NEG)
        m1 = jnp.max(logits, axis=-1, keepdims=True)
        i1 = jnp.min(jnp.where(logits == m1, lane, LANE), axis=-1, keepdims=True)
        rest = jnp.where(lane == i1, NEG, logits)
        m2 = jnp.max(rest, axis=-1, keepdims=True)
        i2 = jnp.min(jnp.where(rest == m2, lane, LANE), axis=-1, keepdims=True)
        e2 = jnp.exp(m2 - m1)
        gate_ref[...] = jnp.where(lane == i1, 1.0 / (1.0 + e2), 0.0) + jnp.where(lane == i2, e2 / (1.0 + e2), 0.0)
        sel_ref[...] = jnp.where((lane == i1) | (lane == i2), 1, 0)
    xo_ref[...] = x
    h_ref[...] = h


def _outproj(xs, o_ctx, o_lat, w_out, g2, mod, router, l):
    tm = 512
    route = router is not None
    split = len(xs) == 2
    rows = lambda w: pl.BlockSpec((tm, w), lambda i: (i, 0))
    ctx_spec, lat_spec = _row_split_specs(tm, 256)
    x_specs = _row_split_specs(tm, D) if split else [rows(D)]
    in_specs = x_specs + [ctx_spec] * 4 + [lat_spec] * 4 + [
        _layer_spec(l, D, D), _mod_spec(l, 2, tm), _layer_spec(l, 1, D), _mod_spec(l, 3, tm), _mod_spec(l, 4, tm)]
    args = [*xs, *o_ctx, *o_lat, w_out, mod, g2, mod, mod]
    out_shape = [jax.ShapeDtypeStruct((M_ALL, D), F32), jax.ShapeDtypeStruct((M_ALL, D), BF16)]
    out_specs = [rows(D), rows(D)]
    if route:
        in_specs.append(_layer_spec(l // 2, D, LANE))
        args.append(router)
        out_shape += [jax.ShapeDtypeStruct((M_ALL, LANE), F32), jax.ShapeDtypeStruct((M_ALL, LANE), jnp.int32)]
        out_specs += [rows(LANE), rows(LANE)]
    return pl.pallas_call(
        functools.partial(_outproj_kernel, route, split, tm),
        out_shape=out_shape, grid=(M_ALL // tm,), in_specs=in_specs, out_specs=out_specs,
        compiler_params=_cparams("parallel"),
        name="outproj_route" if route else "outproj",
    )(*args)


def _ffn_kernel(h_ref, x_ref, gt_ref, wg_ref, wu_ref, wd_ref, o_ref, acc_ref):
    f = pl.program_id(1)

    @pl.when(f == 0)
    def _():
        acc_ref[...] = jnp.zeros_like(acc_ref)

    h = h_ref[...]
    g = jnp.dot(h, wg_ref[...].astype(BF16), preferred_element_type=F32)
    u = jnp.dot(h, wu_ref[...].astype(BF16), preferred_element_type=F32)
    acc_ref[...] += jnp.dot((_silu(g) * u).astype(BF16), wd_ref[...].astype(BF16), preferred_element_type=F32)

    @pl.when(f == pl.num_programs(1) - 1)
    def _():
        o_ref[...] = x_ref[...] + gt_ref[...] * acc_ref[...]


def _ffn(h, x, mod, wg, wu, wd, l):
    tm, tf = 1024, 256
    j = l // 2
    return pl.pallas_call(
        _ffn_kernel,
        out_shape=jax.ShapeDtypeStruct((M_ALL, D), F32),
        grid=(M_ALL // tm, D_FF // tf),
        in_specs=[pl.BlockSpec((tm, D), lambda i, f: (i, 0)), pl.BlockSpec((tm, D), lambda i, f: (i, 0)),
                  _mod_spec(l, 5, tm),
                  pl.BlockSpec((None, D, tf), lambda i, f: (j, 0, f)),
                  pl.BlockSpec((None, D, tf), lambda i, f: (j, 0, f)),
                  pl.BlockSpec((None, tf, D), lambda i, f: (j, f, 0))],
        out_specs=pl.BlockSpec((tm, D), lambda i, f: (i, 0)),
        scratch_shapes=[pltpu.VMEM((tm, D), F32)],
        compiler_params=_cparams("parallel", "arbitrary"),
        name="ffn_dense",
    )(h, x, mod, wg, wu, wd)


SUP, SUB, CHUNK = 2048, 256, 256
N_SUB = SUP // SUB
S_MAX = 2 * M_ALL // SUP + N_EXP
P_SLOT = S_MAX * SUP
TC = 256
WIN = TC + 16


def _moe_expert_kernel(se_ref, nt_ref, sblk_ref, clo_ref, chi_ref, pos_ref, h_ref, wg_ref, wu_ref, wd_ref, o_ref,
                       xs_s, acc_s, wg_s, wu_s, wd_s):
    s = pl.program_id(0)
    f = pl.program_id(1)
    n = nt_ref[s]

    @pl.when((f == 0) & (n == 0))
    def _():
        o_ref[...] = jnp.zeros_like(o_ref)

    @pl.when((f == 0) & (n > 0))
    def _():
        row = lax.broadcasted_iota(jnp.int32, (SUB, CHUNK), 0)

        def gather(j, carry):
            row0 = (s * N_SUB + j) * SUB
            acc_s[j] = jnp.zeros((SUB, D), F32)

            def chunk(c, carry):
                tpos = pos_ref[:, pl.ds(pl.multiple_of(c * CHUNK, CHUNK), CHUNK)]
                onehot = jnp.where(tpos - row0 == row, 1.0, 0.0).astype(BF16)
                rows = h_ref[pl.ds(pl.multiple_of(c * CHUNK, CHUNK), CHUNK), :]
                acc_s[j] += jnp.dot(onehot, rows, preferred_element_type=F32)
                return carry

            g = s * N_SUB + j
            lax.fori_loop(clo_ref[g], chi_ref[g] + 1, chunk, 0)
            xs_s[j] = acc_s[j].astype(BF16)
            acc_s[j] = jnp.zeros((SUB, D), F32)
            return carry

        lax.fori_loop(0, n, gather, 0)

    @pl.when(n > 0)
    def _():
        wg_s[...] = wg_ref[...].astype(BF16)
        wu_s[...] = wu_ref[...].astype(BF16)
        wd_s[...] = wd_ref[...].astype(BF16)

        def sub(j, carry):
            x = xs_s[j]
            g = jnp.dot(x, wg_s[...], preferred_element_type=F32)
            u = jnp.dot(x, wu_s[...], preferred_element_type=F32)
            acc_s[j] += jnp.dot((_silu(g) * u).astype(BF16), wd_s[...], preferred_element_type=F32)
            return carry

        lax.fori_loop(0, n, sub, 0)

    @pl.when((f == pl.num_programs(1) - 1) & (n > 0))
    def _():
        for j in range(N_SUB):
            rows = slice(j * SUB, (j + 1) * SUB)

            @pl.when(j < n)
            def _():
                o_ref[rows, :] = acc_s[j].astype(BF16)

            @pl.when(j >= n)
            def _():
                o_ref[rows, :] = jnp.zeros((SUB, D), BF16)


def _moe_experts(h, pos_t, meta, wg, wu, wd, layer):
    tf = 512
    nf = D_FFE // tf
    se, nt, sblk, clo, chi = meta

    def w_up(s, f, se, nt, sblk, clo, chi):
        return (layer, se[s], 0, jnp.where(nt[s] > 0, f, nf - 1))

    def w_down(s, f, se, nt, sblk, clo, chi):
        return (layer, se[s], jnp.where(nt[s] > 0, f, nf - 1), 0)

    grid_spec = pltpu.PrefetchScalarGridSpec(
        num_scalar_prefetch=5,
        grid=(S_MAX, nf),
        in_specs=[pl.BlockSpec((None, 1, M_ALL), lambda s, f, se, nt, sblk, clo, chi: (se[s], 0, 0)),
                  pl.BlockSpec((M_ALL, D), lambda s, f, *_: (0, 0), pipeline_mode=pl.Buffered(1)),
                  pl.BlockSpec((None, None, D, tf), w_up), pl.BlockSpec((None, None, D, tf), w_up),
                  pl.BlockSpec((None, None, tf, D), w_down)],
        out_specs=pl.BlockSpec((SUP, D), lambda s, f, *_: (s, 0)),
        scratch_shapes=[pltpu.VMEM((N_SUB, SUB, D), BF16), pltpu.VMEM((N_SUB, SUB, D), F32),
                        pltpu.VMEM((D, tf), BF16), pltpu.VMEM((D, tf), BF16), pltpu.VMEM((tf, D), BF16)],
    )
    return pl.pallas_call(
        _moe_expert_kernel,
        out_shape=jax.ShapeDtypeStruct((P_SLOT, D), BF16),
        grid_spec=grid_spec,
        compiler_params=pltpu.CompilerParams(dimension_semantics=("arbitrary", "arbitrary"),
                                             vmem_limit_bytes=60 * 1024 * 1024),
        name="moe_experts",
    )(se, nt, sblk, clo, chi, pos_t, h, wg, wu, wd)


def _moe_combine_kernel(off_ref, x_ref, gt_ref, gate_ref, pos_ref, *refs):
    win_refs, o_ref = refs[:N_EXP], refs[N_EXP]
    i = pl.program_id(0)
    lane = lax.broadcasted_iota(jnp.int32, (TC, WIN), 1)
    y = jnp.zeros((TC, D), F32)
    for e in range(N_EXP):
        rel = pos_ref[:, e:e + 1] - off_ref[i * N_EXP + e] * 16
        onehot = jnp.where(rel == lane, 1.0, 0.0).astype(BF16)
        y = y + gate_ref[:, e:e + 1] * jnp.dot(onehot, win_refs[e][...], preferred_element_type=F32)
    o_ref[...] = x_ref[...] + gt_ref[...] * y


def _moe_combine(x, mod, gates, pos, off, slots, l):
    def win_spec(e):
        return pl.BlockSpec((pl.Element(WIN), pl.Element(D)), lambda i, off: (off[i * N_EXP + e] * 16, 0))

    grid_spec = pltpu.PrefetchScalarGridSpec(
        num_scalar_prefetch=1,
        grid=(M_ALL // TC,),
        in_specs=[pl.BlockSpec((TC, D), lambda i, off: (i, 0)),
                  _mod_spec(l, 5, TC),
                  pl.BlockSpec((TC, LANE), lambda i, off: (i, 0)), pl.BlockSpec((TC, LANE), lambda i, off: (i, 0))]
        + [win_spec(e) for e in range(N_EXP)],
        out_specs=pl.BlockSpec((TC, D), lambda i, off: (i, 0)),
    )
    return pl.pallas_call(
        _moe_combine_kernel,
        out_shape=jax.ShapeDtypeStruct((M_ALL, D), F32),
        grid_spec=grid_spec,
        compiler_params=_cparams("arbitrary"),
        name="moe_combine",
    )(off, x, mod, gates, pos, *([slots] * N_EXP))


def _moe_plan(sel):
    i32 = jnp.int32
    sel_t = sel[:, :N_EXP].T
    csum = jnp.cumsum(sel_t, axis=1)
    rank = csum - sel_t
    n_e = csum[:, -1]
    ns_e = (n_e + SUP - 1) // SUP
    end_e = jnp.cumsum(ns_e)
    start_e = end_e - ns_e
    n_used = end_e[-1]
    pos_t = jnp.where(sel_t > 0, start_e[:, None] * SUP + rank, -1)
    s_ids = jnp.arange(S_MAX, dtype=i32)
    sblk = jnp.minimum(s_ids, n_used - 1)
    se = jnp.sum((end_e[None, :] <= sblk[:, None]).astype(i32), axis=1)
    nv = jnp.clip(n_e[se] - (sblk - start_e[se]) * SUP, 0, SUP)
    nt = jnp.where(s_ids < n_used, (nv + SUB - 1) // SUB, 0)
    g_ids = jnp.arange(S_MAX * N_SUB, dtype=i32)
    s_g, e_g = g_ids // N_SUB, se[g_ids // N_SUB]
    r0 = (sblk[s_g] - start_e[e_g]) * SUP + (g_ids % N_SUB) * SUB
    r1 = jnp.minimum(r0 + SUB, n_e[e_g])
    live = (s_g < n_used) & (r0 < n_e[e_g])
    cs_g = csum[e_g]
    t_first = jnp.sum((cs_g <= r0[:, None]).astype(i32), axis=1)
    t_last = jnp.sum((cs_g < r1[:, None]).astype(i32), axis=1)
    clo = jnp.where(live, t_first // CHUNK, 0)
    chi = jnp.where(live, t_last // CHUNK, -1)
    before = jnp.concatenate([jnp.zeros((N_EXP, 1), i32), csum[:, TC - 1::TC][:, :-1]], axis=1)
    base = start_e[:, None] * SUP + before
    off = jnp.minimum(base // 16, (P_SLOT - WIN) // 16).T.reshape(-1)
    pos128 = jnp.pad(pos_t.T, ((0, 0), (0, LANE - N_EXP)), constant_values=-1)
    meta = (se.astype(i32), nt.astype(i32), sblk.astype(i32), clo.astype(i32), chi.astype(i32))
    return meta, pos_t.reshape(N_EXP, 1, M_ALL), pos128, off.astype(i32)


def _moe(h, x, mod, gates, sel, wg, wu, wd, l):
    meta, pos_t, pos, off = _moe_plan(sel)
    slots = _moe_experts(h, pos_t, meta, wg, wu, wd, l // 2)
    return _moe_combine(x, mod, gates, pos, off, slots, l)


def _final_kernel(tm, x_ref, g_ref, oc_ref, ol_ref):
    y = _rms(x_ref[...]) * g_ref[...]
    is_lat = pl.program_id(0) >= M_CTX // tm

    @pl.when(jnp.logical_not(is_lat))
    def _():
        oc_ref[...] = y

    @pl.when(is_lat)
    def _():
        ol_ref[...] = y


def _final_norm(x, g):
    tm = 1024
    return pl.pallas_call(
        functools.partial(_final_kernel, tm),
        out_shape=[jax.ShapeDtypeStruct((M_CTX, D), F32), jax.ShapeDtypeStruct((M_LAT, D), F32)],
        grid=(M_ALL // tm,),
        in_specs=[pl.BlockSpec((tm, D), lambda i: (i, 0)), pl.BlockSpec((1, D), lambda i: (0, 0))],
        out_specs=_row_split_specs(tm, D),
        compiler_params=_cparams("arbitrary"),
        name="final_norm",
    )(x, g)


def kernel(x_prompt, x_sample, state_gla, state_ret, cache_mla_ckv, cache_mla_krope, cache_gqa_k, cache_gqa_v,
           c, c_ctx, norm1_g, norm2_g, final_norm_g, w_mod, b_mod, w_in, w_out, gla_gate_w, gla_gate_b,
           mla_q_norm_g, mla_w_q_up, mla_kv_norm_g, mla_w_kv_up, ret_decay, gqa_sink,
           ffn_w_gate, ffn_w_up, ffn_w_down, moe_router, moe_w_gate, moe_w_up, moe_w_down):
    xs = (x_prompt.reshape(M_CTX, D), x_sample.reshape(M_LAT, D))

    cond = jnp.concatenate([c_ctx[None], c, jnp.zeros((8 - 1 - N_LAT, D), F32)], axis=0)
    mod = _modulation(cond, w_mod, b_mod)
    mod = mod[:, :1 + N_LAT].reshape(DEPTH, (1 + N_LAT) * 6, 1, D)

    gap = jnp.zeros((DEPTH, D, W_GLA - 800), F32)
    w_all = jnp.concatenate([w_in[:, :, 0:800], gap, w_in[:, :, 800:1216], gap, w_in[:, :, 1216:2752]],
                            axis=-1).astype(BF16)
    w_out_b = w_out.astype(BF16)
    wq = mla_w_q_up.reshape(DEPTH, MLA_QR, MLA_H, MLA_NOPE + MLA_ROPE)
    wq = jnp.concatenate([wq[..., :MLA_NOPE].reshape(DEPTH, MLA_QR, MLA_H * MLA_NOPE),
                          wq[..., MLA_NOPE:].reshape(DEPTH, MLA_QR, MLA_H * MLA_ROPE)], axis=-1).astype(BF16)
    wkv = mla_w_kv_up.reshape(DEPTH, MLA_KVR, MLA_H, MLA_NOPE + MLA_DV)
    wkv = jnp.concatenate([wkv[..., :MLA_NOPE].reshape(DEPTH, MLA_KVR, MLA_H * MLA_NOPE),
                           wkv[..., MLA_NOPE:].reshape(DEPTH, MLA_KVR, MLA_H * MLA_DV)], axis=-1).astype(BF16)
    router = jnp.pad(moe_router, ((0, 0), (0, 0), (0, LANE - N_EXP))).astype(BF16)
    dec_lanes = jnp.repeat(ret_decay, RET_DV, axis=-1)
    sink_lanes = jnp.repeat(gqa_sink, GQA_HD, axis=-1).reshape(DEPTH, 1, 256)
    gate_b = gla_gate_b.reshape(DEPTH, 2, 1, GLA_H * GLA_DK)
    g1, g2 = norm1_g.reshape(DEPTH, 1, D), norm2_g.reshape(DEPTH, 1, D)
    gq, gkv = mla_q_norm_g.reshape(DEPTH, 1, MLA_QR), mla_kv_norm_g.reshape(DEPTH, 1, MLA_KVR)
    eye = jnp.eye(GLA_H, dtype=F32)
    s0_gla = jnp.einsum('bldhkv,hg->bldhvgk', state_gla, eye).reshape(
        N_LAT, DEPTH, 2, GLA_H * GLA_DV, GLA_H * GLA_DK)
    cache_kr = jnp.pad(cache_mla_krope, ((0, 0), (0, 0), (0, 0), (0, 128 - MLA_ROPE)))

    rope64 = _rope_tables(T_LAT, 64, 256)
    rope32_q = _rope_tables(T_LAT, 32, 128)
    ck, sl, sh = _rope_tables(T_LAT, 32, 128)
    live = jnp.asarray((np.arange(128) < MLA_ROPE).astype(np.float32))[None]
    rope32_k = (ck * live, sl * live, sh * live)

    st_gla = st_ret = caches_mla = caches_gqa = None
    for l in range(DEPTH):
        p_gla, p_mla, p_ret, p_gqa = _inproj(xs, g1, mod, w_all, l)

        o_gla_c, st_gla = _gla(p_gla, gla_gate_w, gate_b, None, st_gla, l, False)
        o_gla_s = _gla(p_gla, gla_gate_w, gate_b, s0_gla, None, l, True)
        o_ret_c, st_ret = _ret(p_ret, dec_lanes, None, None, st_ret, l, False)
        o_ret_s = _ret(p_ret, dec_lanes, rope64, state_ret, None, l, True)
        o_mla_c, *caches_mla = _mla(p_mla, gq, gkv, wq, wkv, None, None, None, None, caches_mla, l, False)
        o_mla_s = _mla(p_mla, gq, gkv, wq, wkv, rope32_q, rope32_k, cache_mla_ckv, cache_kr, None, l, True)
        o_gqa_c, *caches_gqa = _gqa(p_gqa, sink_lanes, None, None, None, caches_gqa, l, False)
        o_gqa_s = _gqa(p_gqa, sink_lanes, rope64, cache_gqa_k, cache_gqa_v, None, l, True)

        o_ctx = (o_gla_c, o_mla_c, o_ret_c, o_gqa_c)
        o_lat = (o_gla_s, o_mla_s, o_ret_s, o_gqa_s)
        if l % 2 == 0:
            x, h2 = _outproj(xs, o_ctx, o_lat, w_out_b, g2, mod, None, l)
            x = _ffn(h2, x, mod, ffn_w_gate, ffn_w_up, ffn_w_down, l)
        else:
            x, h2, gates, sel = _outproj(xs, o_ctx, o_lat, w_out_b, g2, mod, router, l)
            x = _moe(h2, x, mod, gates, sel, moe_w_gate, moe_w_up, moe_w_down, l)
        xs = (x,)

    y_ctx, y_lat = _final_norm(x, final_norm_g[None])
    return (y_ctx.reshape(N_CTX, T_CTX, D), y_lat.reshape(N_LAT, T_LAT, D), jnp.swapaxes(st_gla, -1, -2), st_ret,
            *caches_mla, *caches_gqa)
```

```python
import functools

import numpy as np
import jax
import jax.numpy as jnp
from jax import lax
from jax.experimental import pallas as pl
from jax.experimental.pallas import tpu as pltpu

F32 = jnp.float32
BF16 = jnp.bfloat16
HIGHEST = lax.Precision.HIGHEST

D = 1024
N_CTX, T_CTX = 16, 256
N_LAT, T_LAT = 2, 1024
PAST = 256
DEPTH = 4
M_CTX = N_CTX * T_CTX
M_LAT = N_LAT * T_LAT
M_ALL = M_CTX + M_LAT
GRID_W = 64
ROPE_BASE = 10000.0
EPS = 1e-6

GLA_H, GLA_DK, GLA_DV, GLA_RANK, GLA_NORM, GLA_C = 4, 32, 64, 16, 16.0, 64
MLA_H, MLA_QR, MLA_KVR, MLA_NOPE, MLA_ROPE, MLA_DV = 4, 256, 128, 64, 32, 64
RET_H, RET_DK, RET_DV = 4, 64, 64
GQA_H, GQA_KV, GQA_HD, WINDOW = 4, 2, 64, 128
D_FF, N_EXP, D_FFE = 2816, 8, 3584

W_GLA, W_MLA, W_RET, W_GQA = 896, 512, 1024, 512
LANE = 128
NEG = -1e30
QB = 256
VMEM_LIMIT = 56 * 1024 * 1024


def _cparams(*sem):
    return pltpu.CompilerParams(dimension_semantics=sem, vmem_limit_bytes=VMEM_LIMIT)


def _mm(a, b):
    return jnp.dot(a.astype(BF16), b.astype(BF16), preferred_element_type=F32)


def _mm_nt(a, b):
    return lax.dot_general(a.astype(BF16), b.astype(BF16), (((1,), (1,)), ((), ())), preferred_element_type=F32)


def _mm_tn(a, b):
    return lax.dot_general(a.astype(BF16), b.astype(BF16), (((0,), (0,)), ((), ())), preferred_element_type=F32)


def _mm_f32(a, b):
    return jnp.dot(a, b, precision=HIGHEST, preferred_element_type=F32)


def _silu(x):
    return x * (1.0 / (1.0 + jnp.exp(-x)))


def _log_sigmoid(x):
    return jnp.minimum(x, 0.0) - jnp.log1p(jnp.exp(-jnp.abs(x)))


def _rms(x):
    return x * lax.rsqrt(jnp.mean(x * x, axis=-1, keepdims=True) + EPS)


def _mod_row(tile, tm):
    return jnp.maximum((tile * tm) // T_LAT - (M_CTX // T_LAT - 1), 0)


def _mod_kernel(c_ref, w_ref, b_ref, o_ref):
    o_ref[...] = _mm(_silu(c_ref[...]), w_ref[...]) + b_ref[...]


def _modulation(cond, w_mod, b_mod):
    tn = 1536
    return pl.pallas_call(
        _mod_kernel,
        out_shape=jax.ShapeDtypeStruct((DEPTH, 8, 6 * D), F32),
        grid=(DEPTH, 6 * D // tn),
        in_specs=[pl.BlockSpec((8, D), lambda l, j: (0, 0)),
                  pl.BlockSpec((None, D, tn), lambda l, j: (l, 0, j)),
                  pl.BlockSpec((None, 1, tn), lambda l, j: (l, 0, j))],
        out_specs=pl.BlockSpec((None, 8, tn), lambda l, j: (l, 0, j)),
        compiler_params=_cparams("parallel", "parallel"),
        name="modulation",
    )(cond, w_mod, b_mod.reshape(DEPTH, 1, 6 * D))


def _inproj_kernel(split, tm, *refs):
    if split:
        xc_ref, xl_ref = refs[0:2]
        x = jnp.where(pl.program_id(0) >= M_CTX // tm, xl_ref[...], xc_ref[...])
        refs = refs[2:]
    else:
        x = refs[0][...]
        refs = refs[1:]
    g_ref, sh_ref, sc_ref, w_ref = refs[0:4]
    h = (_rms(x) * g_ref[...] * (1.0 + sc_ref[...]) + sh_ref[...]).astype(BF16)
    col = 0
    for o_ref in refs[4:]:
        width = o_ref.shape[-1]
        o_ref[...] = jnp.dot(h, w_ref[:, col:col + width], preferred_element_type=F32)
        col += width


def _mod_spec(l, j, tm):
    return pl.BlockSpec((None, None, 1, D), lambda i, *_: (l, _mod_row(i, tm) * 6 + j, 0, 0))


def _layer_spec(l, *shape):
    return pl.BlockSpec((None,) + shape, lambda *_: (l,) + (0,) * len(shape))


def _stacked_output(prev, tail, l, nseq, in_specs, args, out_shape, out_specs, aliases):
    out_shape.append(jax.ShapeDtypeStruct((nseq, DEPTH) + tail, F32))
    out_specs.append(pl.BlockSpec((None, None) + tail, lambda b: (b, l) + (0,) * len(tail)))
    if prev is not None:
        in_specs.append(pl.BlockSpec(memory_space=pl.ANY))
        args.append(prev)
        aliases[len(args) - 1] = len(out_shape) - 1


def _row_split_specs(tm, width):
    n_ctx = M_CTX // tm
    return [pl.BlockSpec((tm, width), lambda i: (jnp.minimum(i, n_ctx - 1), 0)),
            pl.BlockSpec((tm, width), lambda i: (jnp.maximum(i - n_ctx, 0), 0))]


def _inproj(xs, g1, mod, w_all, l):
    tm = 512
    split = len(xs) == 2
    rows = lambda w: pl.BlockSpec((tm, w), lambda i: (i, 0))
    widths = (W_GLA, W_MLA, W_RET, W_GQA)
    x_specs = _row_split_specs(tm, D) if split else [rows(D)]
    return pl.pallas_call(
        functools.partial(_inproj_kernel, split, tm),
        out_shape=[jax.ShapeDtypeStruct((M_ALL, w), F32) for w in widths],
        grid=(M_ALL // tm,),
        in_specs=x_specs + [_layer_spec(l, 1, D), _mod_spec(l, 0, tm), _mod_spec(l, 1, tm)]
        + [_layer_spec(l, D, sum(widths))],
        out_specs=[rows(w) for w in widths],
        compiler_params=_cparams("parallel"),
        name="inproj",
    )(*xs, g1, mod, mod, w_all)


def _head_rms_gate(o, gate):
    r = lax.broadcasted_iota(jnp.int32, (256, 256), 0) // 64
    c = lax.broadcasted_iota(jnp.int32, (256, 256), 1) // 64
    group_mean = jnp.where(r == c, 1.0 / 64.0, 0.0).astype(BF16)
    sq = o * o
    sq_hi = sq.astype(BF16)
    sq_lo = (sq - sq_hi.astype(F32)).astype(BF16)
    ms = (jnp.dot(sq_hi, group_mean, preferred_element_type=F32)
          + jnp.dot(sq_lo, group_mean, preferred_element_type=F32))
    return o * lax.rsqrt(ms + EPS) * _silu(gate)


def _rope(x, cos, sin_lo, sin_hi, half):
    w = x.shape[-1]
    return x * cos + pltpu.roll(x, w - half, 1) * sin_lo + pltpu.roll(x, half, 1) * sin_hi


def _rope_tables(t, head_dim, width):
    half = head_dim // 2
    quarter = head_dim // 4
    pos = np.arange(t)
    rows = (pos // GRID_W).astype(np.float32)
    cols = (pos % GRID_W).astype(np.float32)
    inv = np.power(np.float32(ROPE_BASE), -np.arange(quarter, dtype=np.float32) / np.float32(quarter)).astype(np.float32)
    ang = np.concatenate([rows[:, None] * inv, cols[:, None] * inv], axis=-1).astype(np.float32)
    lane = np.arange(width)
    a = ang[:, lane % half]
    cos, sin = np.cos(a).astype(np.float32), np.sin(a).astype(np.float32)
    low = (lane % head_dim) < half
    return (jnp.asarray(cos), jnp.asarray(np.where(low[None], -sin, 0.0).astype(np.float32)),
            jnp.asarray(np.where(low[None], 0.0, sin).astype(np.float32)))


def _seq_spec(t, width, row0, latent):
    return pl.BlockSpec((t, width), lambda b: (row0 + b, 0), pipeline_mode=pl.Buffered(1) if latent else None)


def _table_spec(t, width):
    return pl.BlockSpec((t, width), lambda b: (0, 0), pipeline_mode=pl.Buffered(1))


def _fused_call(parts, nseq, name):
    in_specs, args, out_shape, out_specs, scratch, aliases, layout = [], [], [], [], [], {}, []
    for fn, p_in, p_args, p_shape, p_out, p_scratch, p_alias in parts:
        layout.append((fn, len(p_args), len(p_shape), len(p_scratch)))
        for k, v in p_alias.items():
            aliases[len(args) + k] = len(out_shape) + v
        in_specs += p_in
        args += p_args
        out_shape += p_shape
        out_specs += p_out
        scratch += p_scratch
    n_in, n_out = len(args), len(out_shape)

    def kernel(*refs):
        i, o, s = 0, n_in, n_in + n_out
        for fn, ni, no, ns in layout:
            fn(*refs[i:i + ni], *refs[o:o + no], *refs[s:s + ns])
            i, o, s = i + ni, o + no, s + ns

    return pl.pallas_call(
        kernel, out_shape=out_shape, grid=(nseq,), in_specs=in_specs, out_specs=out_specs, scratch_shapes=scratch,
        input_output_aliases=aliases, name=name,
        compiler_params=pltpu.CompilerParams(dimension_semantics=("parallel",), vmem_limit_bytes=60 * 1024 * 1024),
    )(*args)


def _gla_kernel(t, latent, *refs):
    if latent:
        p_ref, gw_ref, gb_ref, s0_ref, o_ref, la_f, la_b, of_s, ob_s, st_s = refs
    else:
        p_ref, gw_ref, gb_ref = refs[0:3]
        o_ref, st_ref, la_f, la_b, of_s, ob_s, st_s = refs[-7:]
    c = GLA_C
    n = t // c
    hd = GLA_H * GLA_DK
    la_f[...] = _log_sigmoid(_mm_f32(p_ref[:, 768:784], gw_ref[0]) + gb_ref[0]) / GLA_NORM
    la_b[...] = _log_sigmoid(_mm_f32(p_ref[:, 784:800], gw_ref[1]) + gb_ref[1]) / GLA_NORM

    ri = lax.broadcasted_iota(jnp.int32, (GLA_H * c, c), 0) % c
    ci = lax.broadcasted_iota(jnp.int32, (GLA_H * c, c), 1)
    tri_r = lax.broadcasted_iota(jnp.int32, (c, c), 0)
    tri_c = lax.broadcasted_iota(jnp.int32, (c, c), 1)
    head_rows = lax.broadcasted_iota(jnp.int32, (GLA_H * c, hd), 0) // c
    own_dk = (head_rows == lax.broadcasted_iota(jnp.int32, (GLA_H * c, hd), 1) // GLA_DK).astype(F32)
    own_dv = (lax.broadcasted_iota(jnp.int32, (GLA_H * c, GLA_H * GLA_DV), 0) // c
              == lax.broadcasted_iota(jnp.int32, (GLA_H * c, GLA_H * GLA_DV), 1) // GLA_DV).astype(F32)
    scale = GLA_DK ** -0.5
    if latent:
        st_s[...] = s0_ref[...]
    else:
        st_s[...] = jnp.zeros_like(st_s)

    def chunk(row0, la_ref, d):
        fwd = d == 0
        keep = (ci <= ri) if fwd else (ci >= ri)
        tri = ((tri_c <= tri_r) if fwd else (tri_c >= tri_r)).astype(BF16)
        q = p_ref[pl.ds(row0, c), 0:128]
        k = p_ref[pl.ds(row0, c), 128:256] * scale
        v = p_ref[pl.ds(row0, c), 256:512]
        la = la_ref[pl.ds(row0, c), :]
        la_hi = la.astype(BF16)
        la_lo = (la - la_hi.astype(F32)).astype(BF16)
        bc = (jnp.dot(tri, la_hi, preferred_element_type=F32) + jnp.dot(tri, la_lo, preferred_element_type=F32))
        tot = bc[c - 1:c, :] if fwd else bc[0:1, :]
        mid = bc[c // 2 - 1:c // 2, :] if fwd else bc[c // 2:c // 2 + 1, :]
        qe, ke = q * jnp.exp(bc - mid), k * jnp.exp(mid - bc)
        q_in, k_out, a = q * jnp.exp(bc), k * jnp.exp(tot - bc), jnp.exp(tot)
        q_rows = jnp.concatenate([qe] * GLA_H, axis=0) * own_dk
        att = jnp.where(keep, _mm_nt(q_rows, ke), 0.0)
        o_all = _mm(att, v) * own_dv
        o = o_all[0:c] + o_all[c:2 * c] + o_all[2 * c:3 * c] + o_all[3 * c:4 * c]
        st = st_s[d]
        o = o + _mm_nt(q_in, st)
        st_s[d] = st * a + _mm_tn(v, k_out) * own_dk
        return o

    unroll = 4

    def body(i, carry):
        for u in range(unroll):
            rf = pl.multiple_of((i * unroll + u) * c, c)
            rb = pl.multiple_of((n - 1 - i * unroll - u) * c, c)
            of_s[pl.ds(rf, c), :] = chunk(rf, la_f, 0)
            ob_s[pl.ds(rb, c), :] = chunk(rb, la_b, 1)
        return carry

    lax.fori_loop(0, n // unroll, body, 0)
    if not latent:
        for d in range(2):
            for h in range(GLA_H):
                st_ref[d, h] = st_s[d, h * GLA_DV:(h + 1) * GLA_DV, h * GLA_DK:(h + 1) * GLA_DK]
    for r in range(t // QB):
        rows = slice(r * QB, (r + 1) * QB)
        o_ref[rows, :] = _head_rms_gate(of_s[rows, :] + ob_s[rows, :], p_ref[rows, 512:768]).astype(BF16)


def _gla(p, gate_w, gate_b, s0_bd, st_prev, l, latent):
    t, nseq, row0 = (T_LAT, N_LAT, M_CTX // T_LAT) if latent else (T_CTX, N_CTX, 0)
    st_shape = (2, GLA_H * GLA_DV, GLA_H * GLA_DK)
    in_specs = [_seq_spec(t, W_GLA, row0, latent),
                _layer_spec(l, 2, GLA_RANK, GLA_H * GLA_DK), _layer_spec(l, 2, 1, GLA_H * GLA_DK)]
    args = [p, gate_w, gate_b]
    aliases = {}
    o_shape = jax.ShapeDtypeStruct((nseq * t, 256), BF16)
    o_spec = pl.BlockSpec((t, 256), lambda b: (b, 0))
    if latent:
        in_specs.append(pl.BlockSpec((None, None) + st_shape, lambda b: (b, l, 0, 0, 0)))
        args.append(s0_bd)
    out_shape, out_specs = [o_shape], [o_spec]
    if not latent:
        _stacked_output(st_prev, (2, GLA_H, GLA_DV, GLA_DK), l, nseq, in_specs, args, out_shape, out_specs, aliases)
    scratch = [pltpu.VMEM((t, 128), F32), pltpu.VMEM((t, 128), F32),
               pltpu.VMEM((t, 256), F32), pltpu.VMEM((t, 256), F32), pltpu.VMEM(st_shape, F32)]
    return functools.partial(_gla_kernel, t, latent), in_specs, args, out_shape, out_specs, scratch, aliases


def _ret_kernel(t, latent, *refs):
    if latent:
        p_ref, dec_ref, cos_ref, slo_ref, shi_ref, s0_ref, o_ref, q_s, k_s = refs
    else:
        p_ref, dec_ref = refs[0:2]
        o_ref, st_ref = refs[-2:]
    lg = _log_sigmoid(dec_ref[...])
    scale = RET_DK ** -0.5
    if latent:
        q_s[...] = _rope(p_ref[:, 0:256], cos_ref[...], slo_ref[...], shi_ref[...], RET_DK // 2)
        k_s[...] = _rope(p_ref[:, 256:512], cos_ref[...], slo_ref[...], shi_ref[...], RET_DK // 2) * scale
        q_of = lambda rows, sl: q_s[rows, sl]
        k_of = lambda sl: k_s[:, sl]
    else:
        q_of = lambda rows, sl: p_ref[rows, sl]
        k_of = lambda sl: p_ref[:, 256 + sl.start:256 + sl.stop] * scale
    col = lax.broadcasted_iota(jnp.int32, (QB, t), 1)
    for r in range(t // QB):
        rows = slice(r * QB, (r + 1) * QB)
        row = lax.broadcasted_iota(jnp.int32, (QB, t), 0) + r * QB
        diff = (row - col).astype(F32)
        pos = (lax.broadcasted_iota(jnp.int32, (QB, 1), 0) + r * QB).astype(F32)
        outs = []
        for h in range(RET_H):
            sl = slice(h * RET_DK, (h + 1) * RET_DK)
            lg_f, lg_b = lg[0:1, h * 64:h * 64 + 1], lg[1:2, h * 64:h * 64 + 1]
            decay = (jnp.where(diff >= 0, jnp.exp(jnp.maximum(diff, 0.0) * lg_f), 0.0)
                     + jnp.where(diff <= 0, jnp.exp(jnp.maximum(-diff, 0.0) * lg_b), 0.0))
            q = q_of(rows, sl)
            v = p_ref[:, 512 + h * RET_DV:512 + (h + 1) * RET_DV]
            o = _mm(_mm_nt(q, k_of(sl)) * decay, v)
            if latent:
                o = o + _mm(q * jnp.exp((pos + 1.0) * lg_f), s0_ref[0, h])
                o = o + _mm(q * jnp.exp((float(t) - pos) * lg_b), s0_ref[1, h])
            outs.append(o)
        o_ref[rows, :] = _head_rms_gate(jnp.concatenate(outs, axis=-1), p_ref[rows, 768:1024]).astype(BF16)
    if not latent:
        j = lax.broadcasted_iota(jnp.int32, (t, 1), 0).astype(F32)
        for h in range(RET_H):
            sl = slice(h * RET_DK, (h + 1) * RET_DK)
            lg_f, lg_b = lg[0:1, h * 64:h * 64 + 1], lg[1:2, h * 64:h * 64 + 1]
            v = p_ref[:, 512 + h * RET_DV:512 + (h + 1) * RET_DV]
            k = k_of(sl)
            st_ref[0, h] = _mm_tn(k * jnp.exp((float(t - 1) - j) * lg_f), v)
            st_ref[1, h] = _mm_tn(k * jnp.exp(j * lg_b), v)


def _ret(p, dec_lanes, tables, s0, st_prev, l, latent):
    t, nseq, row0 = (T_LAT, N_LAT, M_CTX // T_LAT) if latent else (T_CTX, N_CTX, 0)
    st_shape = (2, RET_H, RET_DK, RET_DV)
    in_specs = [_seq_spec(t, W_RET, row0, latent), _layer_spec(l, 2, 256)]
    args = [p, dec_lanes]
    aliases = {}
    o_shape = jax.ShapeDtypeStruct((nseq * t, 256), BF16)
    o_spec = pl.BlockSpec((t, 256), lambda b: (b, 0))
    scratch = []
    if latent:
        in_specs += [_table_spec(t, 256)] * 3
        in_specs.append(pl.BlockSpec((None, None) + st_shape, lambda b: (b, l, 0, 0, 0, 0)))
        args += list(tables) + [s0]
        scratch = [pltpu.VMEM((t, 256), F32), pltpu.VMEM((t, 256), F32)]
    out_shape, out_specs = [o_shape], [o_spec]
    if not latent:
        _stacked_output(st_prev, st_shape, l, nseq, in_specs, args, out_shape, out_specs, aliases)
    return functools.partial(_ret_kernel, t, latent), in_specs, args, out_shape, out_specs, scratch, aliases


def _mla_kernel(t, latent, *refs):
    if latent:
        (p_ref, gq_ref, gkv_ref, wq_ref, wkv_ref, cq_ref, slq_ref, shq_ref, ck_ref, slk_ref, shk_ref,
         cckv_ref, ckr_ref, o_ref, qn_s, qr_s, kn_s, kr_s, v_s) = refs
    else:
        p_ref, gq_ref, gkv_ref, wq_ref, wkv_ref = refs[0:5]
        o_ref, ckv_ref, kro_ref, qn_s, qr_s, kn_s, kr_s, v_s = refs[-8:]
    nk = t + (PAST if latent else 0)
    qh = _mm(_rms(p_ref[:, 0:256]) * gq_ref[...], wq_ref[...])
    ckv = _rms(p_ref[:, 256:384]) * gkv_ref[...]
    kv = _mm(ckv, wkv_ref[...])
    qn_s[...] = qh[:, 0:256]
    kn_s[0:t, :] = kv[:, 0:256]
    v_s[0:t, :] = kv[:, 256:512]
    if latent:
        qr_s[...] = _rope(qh[:, 256:384], cq_ref[...], slq_ref[...], shq_ref[...], MLA_ROPE // 2)
        kr_s[0:t, :] = _rope(p_ref[:, 384:512], ck_ref[...], slk_ref[...], shk_ref[...], MLA_ROPE // 2)
        kvc = _mm(cckv_ref[...], wkv_ref[...])
        kn_s[t:nk, :] = kvc[:, 0:256]
        v_s[t:nk, :] = kvc[:, 256:512]
        kr_s[t:nk, :] = ckr_ref[...]
    else:
        qr_s[...] = qh[:, 256:384]
        kr_s[...] = p_ref[:, 384:512]
        ckv_ref[...] = ckv
        kro_ref[...] = p_ref[:, 384:384 + MLA_ROPE]
    scale = (MLA_NOPE + MLA_ROPE) ** -0.5
    for r in range(t // QB):
        rows = slice(r * QB, (r + 1) * QB)
        outs = []
        for h in range(MLA_H):
            s = (_mm_nt(qn_s[rows, h * 64:(h + 1) * 64], kn_s[:, h * 64:(h + 1) * 64])
                 + _mm_nt(qr_s[rows, h * 32:(h + 1) * 32], kr_s[:, 0:MLA_ROPE])) * scale
            e = jnp.exp(s - jnp.max(s, axis=-1, keepdims=True))
            outs.append(_mm(e, v_s[:, h * 64:(h + 1) * 64]) / jnp.sum(e, axis=-1, keepdims=True))
        o_ref[rows, :] = jnp.concatenate(outs, axis=-1).astype(BF16)


def _mla(p, gq, gkv, wq, wkv, tables_q, tables_k, cache_ckv, cache_kr, prev, l, latent):
    t, nseq, row0 = (T_LAT, N_LAT, M_CTX // T_LAT) if latent else (T_CTX, N_CTX, 0)
    nk = t + (PAST if latent else 0)
    in_specs = [_seq_spec(t, W_MLA, row0, latent), _layer_spec(l, 1, MLA_QR),
                _layer_spec(l, 1, MLA_KVR), _layer_spec(l, MLA_QR, 384), _layer_spec(l, MLA_KVR, 512)]
    args = [p, gq, gkv, wq, wkv]
    aliases = {}
    o_shape = jax.ShapeDtypeStruct((nseq * t, 256), BF16)
    o_spec = pl.BlockSpec((t, 256), lambda b: (b, 0))
    if latent:
        in_specs += [_table_spec(t, 128)] * 6
        in_specs += [pl.BlockSpec((None, None, PAST, MLA_KVR), lambda b: (b, l, 0, 0)),
                     pl.BlockSpec((None, None, PAST, 128), lambda b: (b, l, 0, 0))]
        args += list(tables_q) + list(tables_k) + [cache_ckv, cache_kr]
    out_shape, out_specs = [o_shape], [o_spec]
    if not latent:
        prev = prev or (None, None)
        _stacked_output(prev[0], (t, MLA_KVR), l, nseq, in_specs, args, out_shape, out_specs, aliases)
        _stacked_output(prev[1], (t, MLA_ROPE), l, nseq, in_specs, args, out_shape, out_specs, aliases)
    scratch = [pltpu.VMEM((t, 256), F32), pltpu.VMEM((t, 128), F32), pltpu.VMEM((nk, 256), F32),
               pltpu.VMEM((nk, 128), F32), pltpu.VMEM((nk, 256), F32)]
    return functools.partial(_mla_kernel, t, latent), in_specs, args, out_shape, out_specs, scratch, aliases


def _gqa_kernel(t, latent, *refs):
    if latent:
        p_ref, sink_ref, cos_ref, slo_ref, shi_ref, ck_ref, cv_ref, o_ref, q_s, k_s = refs
    else:
        p_ref, sink_ref = refs[0:2]
        o_ref, ko_ref, vo_ref = refs[-3:]
        for kvh in range(GQA_KV):
            ko_ref[kvh] = p_ref[:, 256 + kvh * GQA_HD:256 + (kvh + 1) * GQA_HD]
            vo_ref[kvh] = p_ref[:, 384 + kvh * GQA_HD:384 + (kvh + 1) * GQA_HD]
    scale = GQA_HD ** -0.5
    grp = GQA_H // GQA_KV
    if latent:
        q_s[...] = _rope(p_ref[:, 0:256], cos_ref[...], slo_ref[...], shi_ref[...], GQA_HD // 2)
        k_s[...] = _rope(p_ref[:, 256:384], cos_ref[:, 0:128], slo_ref[:, 0:128], shi_ref[:, 0:128], GQA_HD // 2)
    col = lax.broadcasted_iota(jnp.int32, (QB, t), 1)
    for r in range(t // QB):
        rows = slice(r * QB, (r + 1) * QB)
        if latent:
            row = lax.broadcasted_iota(jnp.int32, (QB, t), 0) + r * QB
            near = jnp.abs(row - col) <= WINDOW
        outs = []
        for h in range(GQA_H):
            kvh = h // grp
            ksl = slice(kvh * GQA_HD, (kvh + 1) * GQA_HD)
            sk = sink_ref[0:1, h * 64:h * 64 + 1]
            v = p_ref[:, 384 + kvh * GQA_HD:384 + (kvh + 1) * GQA_HD]
            if latent:
                q = q_s[rows, h * 64:(h + 1) * 64]
                s_loc = jnp.where(near, _mm_nt(q, k_s[:, ksl]) * scale, NEG)
                s_ctx = _mm_nt(q, ck_ref[kvh]) * scale
                m = jnp.maximum(jnp.maximum(jnp.max(s_loc, axis=-1, keepdims=True),
                                            jnp.max(s_ctx, axis=-1, keepdims=True)), sk)
                e_loc, e_ctx = jnp.exp(s_loc - m), jnp.exp(s_ctx - m)
                den = jnp.sum(e_loc, axis=-1, keepdims=True) + jnp.sum(e_ctx, axis=-1, keepdims=True) + jnp.exp(sk - m)
                outs.append((_mm(e_loc, v) + _mm(e_ctx, cv_ref[kvh])) / den)
            else:
                q = p_ref[rows, h * 64:(h + 1) * 64]
                s = _mm_nt(q, p_ref[:, 256 + ksl.start:256 + ksl.stop]) * scale
                m = jnp.maximum(jnp.max(s, axis=-1, keepdims=True), sk)
                e = jnp.exp(s - m)
                outs.append(_mm(e, v) / (jnp.sum(e, axis=-1, keepdims=True) + jnp.exp(sk - m)))
        o_ref[rows, :] = jnp.concatenate(outs, axis=-1).astype(BF16)


def _gqa(p, sink_lanes, tables, cache_k, cache_v, prev, l, latent):
    t, nseq, row0 = (T_LAT, N_LAT, M_CTX // T_LAT) if latent else (T_CTX, N_CTX, 0)
    in_specs = [_seq_spec(t, W_GQA, row0, latent), _layer_spec(l, 1, 256)]
    args = [p, sink_lanes]
    scratch = []
    aliases = {}
    out_shape = [jax.ShapeDtypeStruct((nseq * t, 256), BF16)]
    out_specs = [pl.BlockSpec((t, 256), lambda b: (b, 0))]
    if latent:
        in_specs += [_table_spec(t, 256)] * 3
        in_specs += [pl.BlockSpec((None, None, GQA_KV, PAST, GQA_HD), lambda b: (b, l, 0, 0, 0))] * 2
        args += list(tables) + [cache_k, cache_v]
        scratch = [pltpu.VMEM((t, 256), F32), pltpu.VMEM((t, 128), F32)]
    else:
        prev = prev or (None, None)
        for pv in prev:
            _stacked_output(pv, (GQA_KV, t, GQA_HD), l, nseq, in_specs, args, out_shape, out_specs, aliases)
    return functools.partial(_gqa_kernel, t, latent), in_specs, args, out_shape, out_specs, scratch, aliases


def _outproj_kernel(route, split, tm, *refs):
    is_lat = pl.program_id(0) >= M_CTX // tm
    if split:
        x_in = jnp.where(is_lat, refs[1][...], refs[0][...])
        refs = refs[2:]
    else:
        x_in = refs[0][...]
        refs = refs[1:]
    ctx_refs, lat_refs = refs[0:4], refs[4:8]
    w_ref, gt_ref, g_ref, sh_ref, sc_ref = refs[8:13]
    refs = refs[13:]
    mix = jnp.zeros((tm, D), F32)
    for m in range(4):
        o = jnp.where(is_lat, lat_refs[m][...], ctx_refs[m][...])
        mix = mix + jnp.dot(o, w_ref[m * 256:(m + 1) * 256, :], preferred_element_type=F32)
    x = x_in + gt_ref[...] * mix
    h = (_rms(x) * g_ref[...] * (1.0 + sc_ref[...]) + sh_ref[...]).astype(BF16)
    if not route:
        xo_ref, h_ref = refs
    else:
        r_ref, xo_ref, h_ref, gate_ref, sel_ref = refs
        lane = lax.broadcasted_iota(jnp.int32, gate_ref.shape, 1)
        logits = jnp.where(lane < N_EXP, jnp.dot(h, r_ref[...], preferred_element_type=F32), NEG)
        m1 = jnp.max(logits, axis=-1, keepdims=True)
        i1 = jnp.min(jnp.where(logits == m1, lane, LANE), axis=-1, keepdims=True)
        rest = jnp.where(lane == i1, NEG, logits)
        m2 = jnp.max(rest, axis=-1, keepdims=True)
        i2 = jnp.min(jnp.where(rest == m2, lane, LANE), axis=-1, keepdims=True)
        e2 = jnp.exp(m2 - m1)
        gate_ref[...] = jnp.where(lane == i1, 1.0 / (1.0 + e2), 0.0) + jnp.where(lane == i2, e2 / (1.0 + e2), 0.0)
        sel_ref[...] = jnp.where((lane == i1) | (lane == i2), 1, 0)
    xo_ref[...] = x
    h_ref[...] = h


def _outproj(xs, o_ctx, o_lat, w_out, g2, mod, router, l):
    tm = 512
    route = router is not None
    split = len(xs) == 2
    rows = lambda w: pl.BlockSpec((tm, w), lambda i: (i, 0))
    ctx_spec, lat_spec = _row_split_specs(tm, 256)
    x_specs = _row_split_specs(tm, D) if split else [rows(D)]
    in_specs = x_specs + [ctx_spec] * 4 + [lat_spec] * 4 + [
        _layer_spec(l, D, D), _mod_spec(l, 2, tm), _layer_spec(l, 1, D), _mod_spec(l, 3, tm), _mod_spec(l, 4, tm)]
    args = [*xs, *o_ctx, *o_lat, w_out, mod, g2, mod, mod]
    out_shape = [jax.ShapeDtypeStruct((M_ALL, D), F32), jax.ShapeDtypeStruct((M_ALL, D), BF16)]
    out_specs = [rows(D), rows(D)]
    if route:
        in_specs.append(_layer_spec(l // 2, D, LANE))
        args.append(router)
        out_shape += [jax.ShapeDtypeStruct((M_ALL, LANE), F32), jax.ShapeDtypeStruct((M_ALL, LANE), jnp.int32)]
        out_specs += [rows(LANE), rows(LANE)]
    return pl.pallas_call(
        functools.partial(_outproj_kernel, route, split, tm),
        out_shape=out_shape, grid=(M_ALL // tm,), in_specs=in_specs, out_specs=out_specs,
        compiler_params=_cparams("parallel"),
        name="outproj_route" if route else "outproj",
    )(*args)


def _ffn_kernel(h_ref, x_ref, gt_ref, wg_ref, wu_ref, wd_ref, o_ref, acc_ref):
    f = pl.program_id(1)

    @pl.when(f == 0)
    def _():
        acc_ref[...] = jnp.zeros_like(acc_ref)

    h = h_ref[...]
    g = jnp.dot(h, wg_ref[...].astype(BF16), preferred_element_type=F32)
    u = jnp.dot(h, wu_ref[...].astype(BF16), preferred_element_type=F32)
    acc_ref[...] += jnp.dot((_silu(g) * u).astype(BF16), wd_ref[...].astype(BF16), preferred_element_type=F32)

    @pl.when(f == pl.num_programs(1) - 1)
    def _():
        o_ref[...] = x_ref[...] + gt_ref[...] * acc_ref[...]


def _ffn(h, x, mod, wg, wu, wd, l):
    tm, tf = 1024, 256
    j = l // 2
    return pl.pallas_call(
        _ffn_kernel,
        out_shape=jax.ShapeDtypeStruct((M_ALL, D), F32),
        grid=(M_ALL // tm, D_FF // tf),
        in_specs=[pl.BlockSpec((tm, D), lambda i, f: (i, 0)), pl.BlockSpec((tm, D), lambda i, f: (i, 0)),
                  _mod_spec(l, 5, tm),
                  pl.BlockSpec((None, D, tf), lambda i, f: (j, 0, f)),
                  pl.BlockSpec((None, D, tf), lambda i, f: (j, 0, f)),
                  pl.BlockSpec((None, tf, D), lambda i, f: (j, f, 0))],
        out_specs=pl.BlockSpec((tm, D), lambda i, f: (i, 0)),
        scratch_shapes=[pltpu.VMEM((tm, D), F32)],
        compiler_params=_cparams("parallel", "arbitrary"),
        name="ffn_dense",
    )(h, x, mod, wg, wu, wd)


SUP, SUB, CHUNK = 2048, 256, 256
N_SUB = SUP // SUB
S_MAX = 2 * M_ALL // SUP + N_EXP
P_SLOT = S_MAX * SUP
TC = 256
WIN = TC + 16


def _moe_expert_kernel(se_ref, nt_ref, sblk_ref, clo_ref, chi_ref, pos_ref, h_ref, wg_ref, wu_ref, wd_ref, o_ref,
                       xs_s, acc_s, wg_s, wu_s, wd_s):
    s = pl.program_id(0)
    f = pl.program_id(1)
    n = nt_ref[s]

    @pl.when((f == 0) & (n == 0))
    def _():
        o_ref[...] = jnp.zeros_like(o_ref)

    @pl.when((f == 0) & (n > 0))
    def _():
        row = lax.broadcasted_iota(jnp.int32, (SUB, CHUNK), 0)

        def gather(j, carry):
            row0 = (s * N_SUB + j) * SUB
            acc_s[j] = jnp.zeros((SUB, D), F32)

            def chunk(c, carry):
                tpos = pos_ref[:, pl.ds(pl.multiple_of(c * CHUNK, CHUNK), CHUNK)]
                onehot = jnp.where(tpos - row0 == row, 1.0, 0.0).astype(BF16)
                rows = h_ref[pl.ds(pl.multiple_of(c * CHUNK, CHUNK), CHUNK), :]
                acc_s[j] += jnp.dot(onehot, rows, preferred_element_type=F32)
                return carry

            g = s * N_SUB + j
            lax.fori_loop(clo_ref[g], chi_ref[g] + 1, chunk, 0)
            xs_s[j] = acc_s[j].astype(BF16)
            acc_s[j] = jnp.zeros((SUB, D), F32)
            return carry

        lax.fori_loop(0, n, gather, 0)

    @pl.when(n > 0)
    def _():
        wg_s[...] = wg_ref[...].astype(BF16)
        wu_s[...] = wu_ref[...].astype(BF16)
        wd_s[...] = wd_ref[...].astype(BF16)

        def sub(j, carry):
            x = xs_s[j]
            g = jnp.dot(x, wg_s[...], preferred_element_type=F32)
            u = jnp.dot(x, wu_s[...], preferred_element_type=F32)
            acc_s[j] += jnp.dot((_silu(g) * u).astype(BF16), wd_s[...], preferred_element_type=F32)
            return carry

        lax.fori_loop(0, n, sub, 0)

    @pl.when((f == pl.num_programs(1) - 1) & (n > 0))
    def _():
        for j in range(N_SUB):
            rows = slice(j * SUB, (j + 1) * SUB)

            @pl.when(j < n)
            def _():
                o_ref[rows, :] = acc_s[j].astype(BF16)

            @pl.when(j >= n)
            def _():
                o_ref[rows, :] = jnp.zeros((SUB, D), BF16)


def _moe_experts(h, pos_t, meta, wg, wu, wd, layer):
    tf = 512
    nf = D_FFE // tf
    se, nt, sblk, clo, chi = meta

    def w_up(s, f, se, nt, sblk, clo, chi):
        return (layer, se[s], 0, jnp.where(nt[s] > 0, f, nf - 1))

    def w_down(s, f, se, nt, sblk, clo, chi):
        return (layer, se[s], jnp.where(nt[s] > 0, f, nf - 1), 0)

    grid_spec = pltpu.PrefetchScalarGridSpec(
        num_scalar_prefetch=5,
        grid=(S_MAX, nf),
        in_specs=[pl.BlockSpec((None, 1, M_ALL), lambda s, f, se, nt, sblk, clo, chi: (se[s], 0, 0)),
                  pl.BlockSpec((M_ALL, D), lambda s, f, *_: (0, 0), pipeline_mode=pl.Buffered(1)),
                  pl.BlockSpec((None, None, D, tf), w_up), pl.BlockSpec((None, None, D, tf), w_up),
                  pl.BlockSpec((None, None, tf, D), w_down)],
        out_specs=pl.BlockSpec((SUP, D), lambda s, f, *_: (s, 0)),
        scratch_shapes=[pltpu.VMEM((N_SUB, SUB, D), BF16), pltpu.VMEM((N_SUB, SUB, D), F32),
                        pltpu.VMEM((D, tf), BF16), pltpu.VMEM((D, tf), BF16), pltpu.VMEM((tf, D), BF16)],
    )
    return pl.pallas_call(
        _moe_expert_kernel,
        out_shape=jax.ShapeDtypeStruct((P_SLOT, D), BF16),
        grid_spec=grid_spec,
        compiler_params=pltpu.CompilerParams(dimension_semantics=("arbitrary", "arbitrary"),
                                             vmem_limit_bytes=60 * 1024 * 1024),
        name="moe_experts",
    )(se, nt, sblk, clo, chi, pos_t, h, wg, wu, wd)


def _moe_combine_kernel(off_ref, x_ref, gt_ref, gate_ref, pos_ref, *refs):
    win_refs, o_ref = refs[:N_EXP], refs[N_EXP]
    i = pl.program_id(0)
    lane = lax.broadcasted_iota(jnp.int32, (TC, WIN), 1)
    y = jnp.zeros((TC, D), F32)
    for e in range(N_EXP):
        rel = pos_ref[:, e:e + 1] - off_ref[i * N_EXP + e] * 16
        onehot = jnp.where(rel == lane, 1.0, 0.0).astype(BF16)
        y = y + gate_ref[:, e:e + 1] * jnp.dot(onehot, win_refs[e][...], preferred_element_type=F32)
    o_ref[...] = x_ref[...] + gt_ref[...] * y


def _moe_combine(x, mod, gates, pos, off, slots, l):
    def win_spec(e):
        return pl.BlockSpec((pl.Element(WIN), pl.Element(D)), lambda i, off: (off[i * N_EXP + e] * 16, 0))

    grid_spec = pltpu.PrefetchScalarGridSpec(
        num_scalar_prefetch=1,
        grid=(M_ALL // TC,),
        in_specs=[pl.BlockSpec((TC, D), lambda i, off: (i, 0)),
                  _mod_spec(l, 5, TC),
                  pl.BlockSpec((TC, LANE), lambda i, off: (i, 0)), pl.BlockSpec((TC, LANE), lambda i, off: (i, 0))]
        + [win_spec(e) for e in range(N_EXP)],
        out_specs=pl.BlockSpec((TC, D), lambda i, off: (i, 0)),
    )
    return pl.pallas_call(
        _moe_combine_kernel,
        out_shape=jax.ShapeDtypeStruct((M_ALL, D), F32),
        grid_spec=grid_spec,
        compiler_params=_cparams("arbitrary"),
        name="moe_combine",
    )(off, x, mod, gates, pos, *([slots] * N_EXP))


def _moe_plan(sel):
    i32 = jnp.int32
    sel_t = sel[:, :N_EXP].T
    csum = jnp.cumsum(sel_t, axis=1)
    rank = csum - sel_t
    n_e = csum[:, -1]
    ns_e = (n_e + SUP - 1) // SUP
    end_e = jnp.cumsum(ns_e)
    start_e = end_e - ns_e
    n_used = end_e[-1]
    pos_t = jnp.where(sel_t > 0, start_e[:, None] * SUP + rank, -1)
    s_ids = jnp.arange(S_MAX, dtype=i32)
    sblk = jnp.minimum(s_ids, n_used - 1)
    se = jnp.sum((end_e[None, :] <= sblk[:, None]).astype(i32), axis=1)
    nv = jnp.clip(n_e[se] - (sblk - start_e[se]) * SUP, 0, SUP)
    nt = jnp.where(s_ids < n_used, (nv + SUB - 1) // SUB, 0)
    g_ids = jnp.arange(S_MAX * N_SUB, dtype=i32)
    s_g, e_g = g_ids // N_SUB, se[g_ids // N_SUB]
    r0 = (sblk[s_g] - start_e[e_g]) * SUP + (g_ids % N_SUB) * SUB
    r1 = jnp.minimum(r0 + SUB, n_e[e_g])
    live = (s_g < n_used) & (r0 < n_e[e_g])
    cs_g = csum[e_g]
    t_first = jnp.sum((cs_g <= r0[:, None]).astype(i32), axis=1)
    t_last = jnp.sum((cs_g < r1[:, None]).astype(i32), axis=1)
    clo = jnp.where(live, t_first // CHUNK, 0)
    chi = jnp.where(live, t_last // CHUNK, -1)
    before = jnp.concatenate([jnp.zeros((N_EXP, 1), i32), csum[:, TC - 1::TC][:, :-1]], axis=1)
    base = start_e[:, None] * SUP + before
    off = jnp.minimum(base // 16, (P_SLOT - WIN) // 16).T.reshape(-1)
    pos128 = jnp.pad(pos_t.T, ((0, 0), (0, LANE - N_EXP)), constant_values=-1)
    meta = (se.astype(i32), nt.astype(i32), sblk.astype(i32), clo.astype(i32), chi.astype(i32))
    return meta, pos_t.reshape(N_EXP, 1, M_ALL), pos128, off.astype(i32)


def _moe(h, x, mod, gates, sel, wg, wu, wd, l):
    meta, pos_t, pos, off = _moe_plan(sel)
    slots = _moe_experts(h, pos_t, meta, wg, wu, wd, l // 2)
    return _moe_combine(x, mod, gates, pos, off, slots, l)


def _final_kernel(tm, x_ref, g_ref, oc_ref, ol_ref):
    y = _rms(x_ref[...]) * g_ref[...]
    is_lat = pl.program_id(0) >= M_CTX // tm

    @pl.when(jnp.logical_not(is_lat))
    def _():
        oc_ref[...] = y

    @pl.when(is_lat)
    def _():
        ol_ref[...] = y


def _final_norm(x, g):
    tm = 1024
    return pl.pallas_call(
        functools.partial(_final_kernel, tm),
        out_shape=[jax.ShapeDtypeStruct((M_CTX, D), F32), jax.ShapeDtypeStruct((M_LAT, D), F32)],
        grid=(M_ALL // tm,),
        in_specs=[pl.BlockSpec((tm, D), lambda i: (i, 0)), pl.BlockSpec((1, D), lambda i: (0, 0))],
        out_specs=_row_split_specs(tm, D),
        compiler_params=_cparams("arbitrary"),
        name="final_norm",
    )(x, g)


def kernel(x_prompt, x_sample, state_gla, state_ret, cache_mla_ckv, cache_mla_krope, cache_gqa_k, cache_gqa_v,
           c, c_ctx, norm1_g, norm2_g, final_norm_g, w_mod, b_mod, w_in, w_out, gla_gate_w, gla_gate_b,
           mla_q_norm_g, mla_w_q_up, mla_kv_norm_g, mla_w_kv_up, ret_decay, gqa_sink,
           ffn_w_gate, ffn_w_up, ffn_w_down, moe_router, moe_w_gate, moe_w_up, moe_w_down):
    xs = (x_prompt.reshape(M_CTX, D), x_sample.reshape(M_LAT, D))

    cond = jnp.concatenate([c_ctx[None], c, jnp.zeros((8 - 1 - N_LAT, D), F32)], axis=0)
    mod = _modulation(cond, w_mod, b_mod)
    mod = mod[:, :1 + N_LAT].reshape(DEPTH, (1 + N_LAT) * 6, 1, D)

    gap = jnp.zeros((DEPTH, D, W_GLA - 800), F32)
    w_all = jnp.concatenate([w_in[:, :, 0:800], gap, w_in[:, :, 800:1216], gap, w_in[:, :, 1216:2752]],
                            axis=-1).astype(BF16)
    w_out_b = w_out.astype(BF16)
    wq = mla_w_q_up.reshape(DEPTH, MLA_QR, MLA_H, MLA_NOPE + MLA_ROPE)
    wq = jnp.concatenate([wq[..., :MLA_NOPE].reshape(DEPTH, MLA_QR, MLA_H * MLA_NOPE),
                          wq[..., MLA_NOPE:].reshape(DEPTH, MLA_QR, MLA_H * MLA_ROPE)], axis=-1).astype(BF16)
    wkv = mla_w_kv_up.reshape(DEPTH, MLA_KVR, MLA_H, MLA_NOPE + MLA_DV)
    wkv = jnp.concatenate([wkv[..., :MLA_NOPE].reshape(DEPTH, MLA_KVR, MLA_H * MLA_NOPE),
                           wkv[..., MLA_NOPE:].reshape(DEPTH, MLA_KVR, MLA_H * MLA_DV)], axis=-1).astype(BF16)
    router = jnp.pad(moe_router, ((0, 0), (0, 0), (0, LANE - N_EXP))).astype(BF16)
    dec_lanes = jnp.repeat(ret_decay, RET_DV, axis=-1)
    sink_lanes = jnp.repeat(gqa_sink, GQA_HD, axis=-1).reshape(DEPTH, 1, 256)
    gate_b = gla_gate_b.reshape(DEPTH, 2, 1, GLA_H * GLA_DK)
    g1, g2 = norm1_g.reshape(DEPTH, 1, D), norm2_g.reshape(DEPTH, 1, D)
    gq, gkv = mla_q_norm_g.reshape(DEPTH, 1, MLA_QR), mla_kv_norm_g.reshape(DEPTH, 1, MLA_KVR)
    eye = jnp.eye(GLA_H, dtype=F32)
    s0_gla = jnp.einsum('bldhkv,hg->bldhvgk', state_gla, eye).reshape(
        N_LAT, DEPTH, 2, GLA_H * GLA_DV, GLA_H * GLA_DK)
    cache_kr = jnp.pad(cache_mla_krope, ((0, 0), (0, 0), (0, 0), (0, 128 - MLA_ROPE)))

    rope64 = _rope_tables(T_LAT, 64, 256)
    rope32_q = _rope_tables(T_LAT, 32, 128)
    ck, sl, sh = _rope_tables(T_LAT, 32, 128)
    live = jnp.asarray((np.arange(128) < MLA_ROPE).astype(np.float32))[None]
    rope32_k = (ck * live, sl * live, sh * live)

    st_gla = st_ret = caches_mla = caches_gqa = None
    for l in range(DEPTH):
        p_gla, p_mla, p_ret, p_gqa = _inproj(xs, g1, mod, w_all, l)

        o_gla_c, st_gla, o_mla_c, ckv_all, kr_all, o_ret_c, st_ret, o_gqa_c, gk_all, gv_all = _fused_call([
            _gla(p_gla, gla_gate_w, gate_b, None, st_gla, l, False),
            _mla(p_mla, gq, gkv, wq, wkv, None, None, None, None, caches_mla, l, False),
            _ret(p_ret, dec_lanes, None, None, st_ret, l, False),
            _gqa(p_gqa, sink_lanes, None, None, None, caches_gqa, l, False)], N_CTX, "mixers_context")
        caches_mla, caches_gqa = (ckv_all, kr_all), (gk_all, gv_all)
        o_gla_s, o_ret_s = _fused_call([
            _gla(p_gla, gla_gate_w, gate_b, s0_gla, None, l, True),
            _ret(p_ret, dec_lanes, rope64, state_ret, None, l, True)], N_LAT, "mixers_latent_scan")
        o_mla_s, o_gqa_s = _fused_call([
            _mla(p_mla, gq, gkv, wq, wkv, rope32_q, rope32_k, cache_mla_ckv, cache_kr, None, l, True),
            _gqa(p_gqa, sink_lanes, rope64, cache_gqa_k, cache_gqa_v, None, l, True)], N_LAT, "mixers_latent_attn")
        o_ctx = (o_gla_c, o_mla_c, o_ret_c, o_gqa_c)
        o_lat = (o_gla_s, o_mla_s, o_ret_s, o_gqa_s)
        if l % 2 == 0:
            x, h2 = _outproj(xs, o_ctx, o_lat, w_out_b, g2, mod, None, l)
            x = _ffn(h2, x, mod, ffn_w_gate, ffn_w_up, ffn_w_down, l)
        else:
            x, h2, gates, sel = _outproj(xs, o_ctx, o_lat, w_out_b, g2, mod, router, l)
            x = _moe(h2, x, mod, gates, sel, moe_w_gate, moe_w_up, moe_w_down, l)
        xs = (x,)

    y_ctx, y_lat = _final_norm(x, final_norm_g[None])
    return (y_ctx.reshape(N_CTX, T_CTX, D), y_lat.reshape(N_LAT, T_LAT, D), jnp.swapaxes(st_gla, -1, -2), st_ret,
            *caches_mla, *caches_gqa)
```

```python
import functools

import numpy as np
import jax
import jax.numpy as jnp
from jax import lax
from jax.experimental import pallas as pl
from jax.experimental.pallas import tpu as pltpu

F32 = jnp.float32
BF16 = jnp.bfloat16
HIGHEST = lax.Precision.HIGHEST

D = 1024
N_CTX, T_CTX = 16, 256
N_LAT, T_LAT = 2, 1024
PAST = 256
DEPTH = 4
M_CTX = N_CTX * T_CTX
M_LAT = N_LAT * T_LAT
M_ALL = M_CTX + M_LAT
GRID_W = 64
ROPE_BASE = 10000.0
EPS = 1e-6

GLA_H, GLA_DK, GLA_DV, GLA_RANK, GLA_NORM, GLA_C = 4, 32, 64, 16, 16.0, 64
MLA_H, MLA_QR, MLA_KVR, MLA_NOPE, MLA_ROPE, MLA_DV = 4, 256, 128, 64, 32, 64
RET_H, RET_DK, RET_DV = 4, 64, 64
GQA_H, GQA_KV, GQA_HD, WINDOW = 4, 2, 64, 128
D_FF, N_EXP, D_FFE = 2816, 8, 3584

W_GLA, W_MLA, W_RET, W_GQA = 896, 512, 1024, 512
LANE = 128
NEG = -1e30
QB = 256
VMEM_LIMIT = 56 * 1024 * 1024


def _cparams(*sem):
    return pltpu.CompilerParams(dimension_semantics=sem, vmem_limit_bytes=VMEM_LIMIT)


def _mm(a, b):
    return jnp.dot(a.astype(BF16), b.astype(BF16), preferred_element_type=F32)


def _mm_nt(a, b):
    return lax.dot_general(a.astype(BF16), b.astype(BF16), (((1,), (1,)), ((), ())), preferred_element_type=F32)


def _mm_tn(a, b):
    return lax.dot_general(a.astype(BF16), b.astype(BF16), (((0,), (0,)), ((), ())), preferred_element_type=F32)


def _mm_f32(a, b):
    return jnp.dot(a, b, precision=HIGHEST, preferred_element_type=F32)


def _silu(x):
    return x * (1.0 / (1.0 + jnp.exp(-x)))


def _log_sigmoid(x):
    return jnp.minimum(x, 0.0) - jnp.log1p(jnp.exp(-jnp.abs(x)))


def _rms(x):
    return x * lax.rsqrt(jnp.mean(x * x, axis=-1, keepdims=True) + EPS)


def _mod_row(tile, tm):
    return jnp.maximum((tile * tm) // T_LAT - (M_CTX // T_LAT - 1), 0)


def _mod_kernel(c_ref, w_ref, b_ref, o_ref):
    o_ref[...] = _mm(_silu(c_ref[...]), w_ref[...]) + b_ref[...]


def _modulation(cond, w_mod, b_mod):
    tn = 1536
    return pl.pallas_call(
        _mod_kernel,
        out_shape=jax.ShapeDtypeStruct((DEPTH, 8, 6 * D), F32),
        grid=(DEPTH, 6 * D // tn),
        in_specs=[pl.BlockSpec((8, D), lambda l, j: (0, 0)),
                  pl.BlockSpec((None, D, tn), lambda l, j: (l, 0, j)),
                  pl.BlockSpec((None, 1, tn), lambda l, j: (l, 0, j))],
        out_specs=pl.BlockSpec((None, 8, tn), lambda l, j: (l, 0, j)),
        compiler_params=_cparams("parallel", "parallel"),
        name="modulation",
    )(cond, w_mod, b_mod.reshape(DEPTH, 1, 6 * D))


def _inproj_kernel(split, tm, *refs):
    if split:
        xc_ref, xl_ref = refs[0:2]
        x = jnp.where(pl.program_id(0) >= M_CTX // tm, xl_ref[...], xc_ref[...])
        refs = refs[2:]
    else:
        x = refs[0][...]
        refs = refs[1:]
    g_ref, sh_ref, sc_ref, w_ref = refs[0:4]
    h = (_rms(x) * g_ref[...] * (1.0 + sc_ref[...]) + sh_ref[...]).astype(BF16)
    col = 0
    for o_ref in refs[4:]:
        width = o_ref.shape[-1]
        o_ref[...] = jnp.dot(h, w_ref[:, col:col + width], preferred_element_type=F32)
        col += width


def _mod_spec(l, j, tm):
    return pl.BlockSpec((None, None, 1, D), lambda i, *_: (l, _mod_row(i, tm) * 6 + j, 0, 0))


def _layer_spec(l, *shape):
    return pl.BlockSpec((None,) + shape, lambda *_: (l,) + (0,) * len(shape))


def _stacked_output(prev, tail, l, nseq, in_specs, args, out_shape, out_specs, aliases):
    out_shape.append(jax.ShapeDtypeStruct((nseq, DEPTH) + tail, F32))
    out_specs.append(pl.BlockSpec((None, None) + tail, lambda b: (b, l) + (0,) * len(tail)))
    if prev is not None:
        in_specs.append(pl.BlockSpec(memory_space=pl.ANY))
        args.append(prev)
        aliases[len(args) - 1] = len(out_shape) - 1


def _row_split_specs(tm, width):
    n_ctx = M_CTX // tm
    return [pl.BlockSpec((tm, width), lambda i: (jnp.minimum(i, n_ctx - 1), 0)),
            pl.BlockSpec((tm, width), lambda i: (jnp.maximum(i - n_ctx, 0), 0))]


def _inproj(xs, g1, mod, w_all, l):
    tm = 512
    split = len(xs) == 2
    rows = lambda w: pl.BlockSpec((tm, w), lambda i: (i, 0))
    widths = (W_GLA, W_MLA, W_RET, W_GQA)
    x_specs = _row_split_specs(tm, D) if split else [rows(D)]
    return pl.pallas_call(
        functools.partial(_inproj_kernel, split, tm),
        out_shape=[jax.ShapeDtypeStruct((M_ALL, w), F32) for w in widths],
        grid=(M_ALL // tm,),
        in_specs=x_specs + [_layer_spec(l, 1, D), _mod_spec(l, 0, tm), _mod_spec(l, 1, tm)]
        + [_layer_spec(l, D, sum(widths))],
        out_specs=[rows(w) for w in widths],
        compiler_params=_cparams("parallel"),
        name="inproj",
    )(*xs, g1, mod, mod, w_all)


def _head_rms_gate(o, gate):
    r = lax.broadcasted_iota(jnp.int32, (256, 256), 0) // 64
    c = lax.broadcasted_iota(jnp.int32, (256, 256), 1) // 64
    group_mean = jnp.where(r == c, 1.0 / 64.0, 0.0).astype(BF16)
    sq = o * o
    sq_hi = sq.astype(BF16)
    sq_lo = (sq - sq_hi.astype(F32)).astype(BF16)
    ms = (jnp.dot(sq_hi, group_mean, preferred_element_type=F32)
          + jnp.dot(sq_lo, group_mean, preferred_element_type=F32))
    return o * lax.rsqrt(ms + EPS) * _silu(gate)


def _rope(x, cos, sin_lo, sin_hi, half):
    w = x.shape[-1]
    return x * cos + pltpu.roll(x, w - half, 1) * sin_lo + pltpu.roll(x, half, 1) * sin_hi


def _rope_tables(t, head_dim, width):
    half = head_dim // 2
    quarter = head_dim // 4
    pos = np.arange(t)
    rows = (pos // GRID_W).astype(np.float32)
    cols = (pos % GRID_W).astype(np.float32)
    inv = np.power(np.float32(ROPE_BASE), -np.arange(quarter, dtype=np.float32) / np.float32(quarter)).astype(np.float32)
    ang = np.concatenate([rows[:, None] * inv, cols[:, None] * inv], axis=-1).astype(np.float32)
    lane = np.arange(width)
    a = ang[:, lane % half]
    cos, sin = np.cos(a).astype(np.float32), np.sin(a).astype(np.float32)
    low = (lane % head_dim) < half
    return (jnp.asarray(cos), jnp.asarray(np.where(low[None], -sin, 0.0).astype(np.float32)),
            jnp.asarray(np.where(low[None], 0.0, sin).astype(np.float32)))


def _seq_spec(t, width, row0, latent):
    return pl.BlockSpec((t, width), lambda b: (row0 + b, 0), pipeline_mode=pl.Buffered(1) if latent else None)


def _table_spec(t, width):
    return pl.BlockSpec((t, width), lambda b: (0, 0), pipeline_mode=pl.Buffered(1))


def _fused_call(parts, nseq, name):
    in_specs, args, out_shape, out_specs, scratch, aliases, layout = [], [], [], [], [], {}, []
    for fn, p_in, p_args, p_shape, p_out, p_scratch, p_alias in parts:
        layout.append((fn, len(p_args), len(p_shape), len(p_scratch)))
        for k, v in p_alias.items():
            aliases[len(args) + k] = len(out_shape) + v
        in_specs += p_in
        args += p_args
        out_shape += p_shape
        out_specs += p_out
        scratch += p_scratch
    n_in, n_out = len(args), len(out_shape)

    def kernel(*refs):
        i, o, s = 0, n_in, n_in + n_out
        for fn, ni, no, ns in layout:
            fn(*refs[i:i + ni], *refs[o:o + no], *refs[s:s + ns])
            i, o, s = i + ni, o + no, s + ns

    return pl.pallas_call(
        kernel, out_shape=out_shape, grid=(nseq,), in_specs=in_specs, out_specs=out_specs, scratch_shapes=scratch,
        input_output_aliases=aliases, name=name,
        compiler_params=pltpu.CompilerParams(dimension_semantics=("arbitrary",), vmem_limit_bytes=60 * 1024 * 1024),
    )(*args)


def _gla_kernel(t, latent, *refs):
    if latent:
        p_ref, gw_ref, gb_ref, s0_ref, o_ref, la_f, la_b, of_s, ob_s, st_s = refs
    else:
        p_ref, gw_ref, gb_ref = refs[0:3]
        o_ref, st_ref, la_f, la_b, of_s, ob_s, st_s = refs[-7:]
    c = GLA_C
    n = t // c
    hd = GLA_H * GLA_DK
    la_f[...] = _log_sigmoid(_mm_f32(p_ref[:, 768:784], gw_ref[0]) + gb_ref[0]) / GLA_NORM
    la_b[...] = _log_sigmoid(_mm_f32(p_ref[:, 784:800], gw_ref[1]) + gb_ref[1]) / GLA_NORM

    ri = lax.broadcasted_iota(jnp.int32, (GLA_H * c, c), 0) % c
    ci = lax.broadcasted_iota(jnp.int32, (GLA_H * c, c), 1)
    tri_r = lax.broadcasted_iota(jnp.int32, (c, c), 0)
    tri_c = lax.broadcasted_iota(jnp.int32, (c, c), 1)
    head_rows = lax.broadcasted_iota(jnp.int32, (GLA_H * c, hd), 0) // c
    own_dk = (head_rows == lax.broadcasted_iota(jnp.int32, (GLA_H * c, hd), 1) // GLA_DK).astype(F32)
    own_dv = (lax.broadcasted_iota(jnp.int32, (GLA_H * c, GLA_H * GLA_DV), 0) // c
              == lax.broadcasted_iota(jnp.int32, (GLA_H * c, GLA_H * GLA_DV), 1) // GLA_DV).astype(F32)
    scale = GLA_DK ** -0.5
    if latent:
        st_s[...] = s0_ref[...]
    else:
        st_s[...] = jnp.zeros_like(st_s)

    def chunk(row0, la_ref, d):
        fwd = d == 0
        keep = (ci <= ri) if fwd else (ci >= ri)
        tri = ((tri_c <= tri_r) if fwd else (tri_c >= tri_r)).astype(BF16)
        q = p_ref[pl.ds(row0, c), 0:128]
        k = p_ref[pl.ds(row0, c), 128:256] * scale
        v = p_ref[pl.ds(row0, c), 256:512]
        la = la_ref[pl.ds(row0, c), :]
        la_hi = la.astype(BF16)
        la_lo = (la - la_hi.astype(F32)).astype(BF16)
        bc = (jnp.dot(tri, la_hi, preferred_element_type=F32) + jnp.dot(tri, la_lo, preferred_element_type=F32))
        tot = bc[c - 1:c, :] if fwd else bc[0:1, :]
        mid = bc[c // 2 - 1:c // 2, :] if fwd else bc[c // 2:c // 2 + 1, :]
        qe, ke = q * jnp.exp(bc - mid), k * jnp.exp(mid - bc)
        q_in, k_out, a = q * jnp.exp(bc), k * jnp.exp(tot - bc), jnp.exp(tot)
        q_rows = jnp.concatenate([qe] * GLA_H, axis=0) * own_dk
        att = jnp.where(keep, _mm_nt(q_rows, ke), 0.0)
        o_all = _mm(att, v) * own_dv
        o = o_all[0:c] + o_all[c:2 * c] + o_all[2 * c:3 * c] + o_all[3 * c:4 * c]
        st = st_s[d]
        o = o + _mm_nt(q_in, st)
        st_s[d] = st * a + _mm_tn(v, k_out) * own_dk
        return o

    unroll = 4

    def body(i, carry):
        for u in range(unroll):
            rf = pl.multiple_of((i * unroll + u) * c, c)
            rb = pl.multiple_of((n - 1 - i * unroll - u) * c, c)
            of_s[pl.ds(rf, c), :] = chunk(rf, la_f, 0)
            ob_s[pl.ds(rb, c), :] = chunk(rb, la_b, 1)
        return carry

    lax.fori_loop(0, n // unroll, body, 0)
    if not latent:
        for d in range(2):
            for h in range(GLA_H):
                st_ref[d, h] = st_s[d, h * GLA_DV:(h + 1) * GLA_DV, h * GLA_DK:(h + 1) * GLA_DK]
    for r in range(t // QB):
        rows = slice(r * QB, (r + 1) * QB)
        o_ref[rows, :] = _head_rms_gate(of_s[rows, :] + ob_s[rows, :], p_ref[rows, 512:768]).astype(BF16)


def _gla(p, gate_w, gate_b, s0_bd, st_prev, l, latent):
    t, nseq, row0 = (T_LAT, N_LAT, M_CTX // T_LAT) if latent else (T_CTX, N_CTX, 0)
    st_shape = (2, GLA_H * GLA_DV, GLA_H * GLA_DK)
    in_specs = [_seq_spec(t, W_GLA, row0, latent),
                _layer_spec(l, 2, GLA_RANK, GLA_H * GLA_DK), _layer_spec(l, 2, 1, GLA_H * GLA_DK)]
    args = [p, gate_w, gate_b]
    aliases = {}
    o_shape = jax.ShapeDtypeStruct((nseq * t, 256), BF16)
    o_spec = pl.BlockSpec((t, 256), lambda b: (b, 0))
    if latent:
        in_specs.append(pl.BlockSpec((None, None) + st_shape, lambda b: (b, l, 0, 0, 0)))
        args.append(s0_bd)
    out_shape, out_specs = [o_shape], [o_spec]
    if not latent:
        _stacked_output(st_prev, (2, GLA_H, GLA_DV, GLA_DK), l, nseq, in_specs, args, out_shape, out_specs, aliases)
    scratch = [pltpu.VMEM((t, 128), F32), pltpu.VMEM((t, 128), F32),
               pltpu.VMEM((t, 256), F32), pltpu.VMEM((t, 256), F32), pltpu.VMEM(st_shape, F32)]
    return functools.partial(_gla_kernel, t, latent), in_specs, args, out_shape, out_specs, scratch, aliases


def _ret_kernel(t, latent, *refs):
    if latent:
        p_ref, dec_ref, cos_ref, slo_ref, shi_ref, s0_ref, o_ref, q_s, k_s, decay_s = refs
    else:
        p_ref, dec_ref = refs[0:2]
        o_ref, st_ref, decay_s = refs[-3:]
    lg = _log_sigmoid(dec_ref[...])
    scale = RET_DK ** -0.5
    nblk = t // QB

    @pl.when(pl.program_id(0) == 0)
    def _():
        wide = decay_s.shape[-1]
        dist = (lax.broadcasted_iota(jnp.int32, (QB, wide), 0) - lax.broadcasted_iota(jnp.int32, (QB, wide), 1)
                + (nblk - 1) * QB).astype(F32)
        for h in range(RET_H):
            lg_f, lg_b = lg[0:1, h * 64:h * 64 + 1], lg[1:2, h * 64:h * 64 + 1]
            decay_s[h] = (jnp.where(dist >= 0, jnp.exp(jnp.maximum(dist, 0.0) * lg_f), 0.0)
                          + jnp.where(dist <= 0, jnp.exp(jnp.maximum(-dist, 0.0) * lg_b), 0.0))

    if latent:
        q_s[...] = _rope(p_ref[:, 0:256], cos_ref[...], slo_ref[...], shi_ref[...], RET_DK // 2)
        k_s[...] = _rope(p_ref[:, 256:512], cos_ref[...], slo_ref[...], shi_ref[...], RET_DK // 2) * scale
        q_of = lambda rows, sl: q_s[rows, sl]
        k_of = lambda sl: k_s[:, sl]
    else:
        q_of = lambda rows, sl: p_ref[rows, sl]
        k_of = lambda sl: p_ref[:, 256 + sl.start:256 + sl.stop] * scale
    for r in range(nblk):
        rows = slice(r * QB, (r + 1) * QB)
        pos = (lax.broadcasted_iota(jnp.int32, (QB, 1), 0) + r * QB).astype(F32)
        outs = []
        for h in range(RET_H):
            sl = slice(h * RET_DK, (h + 1) * RET_DK)
            lg_f, lg_b = lg[0:1, h * 64:h * 64 + 1], lg[1:2, h * 64:h * 64 + 1]
            decay = decay_s[h, :, (nblk - 1 - r) * QB:(nblk - 1 - r) * QB + t]
            q = q_of(rows, sl)
            v = p_ref[:, 512 + h * RET_DV:512 + (h + 1) * RET_DV]
            o = _mm(_mm_nt(q, k_of(sl)) * decay, v)
            if latent:
                o = o + _mm(q * jnp.exp((pos + 1.0) * lg_f), s0_ref[0, h])
                o = o + _mm(q * jnp.exp((float(t) - pos) * lg_b), s0_ref[1, h])
            outs.append(o)
        o_ref[rows, :] = _head_rms_gate(jnp.concatenate(outs, axis=-1), p_ref[rows, 768:1024]).astype(BF16)
    if not latent:
        j = lax.broadcasted_iota(jnp.int32, (t, 1), 0).astype(F32)
        for h in range(RET_H):
            sl = slice(h * RET_DK, (h + 1) * RET_DK)
            lg_f, lg_b = lg[0:1, h * 64:h * 64 + 1], lg[1:2, h * 64:h * 64 + 1]
            v = p_ref[:, 512 + h * RET_DV:512 + (h + 1) * RET_DV]
            k = k_of(sl)
            st_ref[0, h] = _mm_tn(k * jnp.exp((float(t - 1) - j) * lg_f), v)
            st_ref[1, h] = _mm_tn(k * jnp.exp(j * lg_b), v)


def _ret(p, dec_lanes, tables, s0, st_prev, l, latent):
    t, nseq, row0 = (T_LAT, N_LAT, M_CTX // T_LAT) if latent else (T_CTX, N_CTX, 0)
    st_shape = (2, RET_H, RET_DK, RET_DV)
    in_specs = [_seq_spec(t, W_RET, row0, latent), _layer_spec(l, 2, 256)]
    args = [p, dec_lanes]
    aliases = {}
    o_shape = jax.ShapeDtypeStruct((nseq * t, 256), BF16)
    o_spec = pl.BlockSpec((t, 256), lambda b: (b, 0))
    scratch = []
    if latent:
        in_specs += [_table_spec(t, 256)] * 3
        in_specs.append(pl.BlockSpec((None, None) + st_shape, lambda b: (b, l, 0, 0, 0, 0)))
        args += list(tables) + [s0]
        scratch = [pltpu.VMEM((t, 256), F32), pltpu.VMEM((t, 256), F32)]
    scratch.append(pltpu.VMEM((RET_H, QB, 2 * t - QB), F32))
    out_shape, out_specs = [o_shape], [o_spec]
    if not latent:
        _stacked_output(st_prev, st_shape, l, nseq, in_specs, args, out_shape, out_specs, aliases)
    return functools.partial(_ret_kernel, t, latent), in_specs, args, out_shape, out_specs, scratch, aliases


def _mla_kernel(t, latent, *refs):
    if latent:
        (p_ref, gq_ref, gkv_ref, wq_ref, wkv_ref, cq_ref, slq_ref, shq_ref, ck_ref, slk_ref, shk_ref,
         cckv_ref, ckr_ref, o_ref, qn_s, qr_s, kn_s, kr_s, v_s) = refs
    else:
        p_ref, gq_ref, gkv_ref, wq_ref, wkv_ref = refs[0:5]
        o_ref, ckv_ref, kro_ref, qn_s, qr_s, kn_s, kr_s, v_s = refs[-8:]
    nk = t + (PAST if latent else 0)
    qh = _mm(_rms(p_ref[:, 0:256]) * gq_ref[...], wq_ref[...])
    ckv = _rms(p_ref[:, 256:384]) * gkv_ref[...]
    kv = _mm(ckv, wkv_ref[...])
    qn_s[...] = qh[:, 0:256]
    kn_s[0:t, :] = kv[:, 0:256]
    v_s[0:t, :] = kv[:, 256:512]
    if latent:
        qr_s[...] = _rope(qh[:, 256:384], cq_ref[...], slq_ref[...], shq_ref[...], MLA_ROPE // 2)
        kr_s[0:t, :] = _rope(p_ref[:, 384:512], ck_ref[...], slk_ref[...], shk_ref[...], MLA_ROPE // 2)
        kvc = _mm(cckv_ref[...], wkv_ref[...])
        kn_s[t:nk, :] = kvc[:, 0:256]
        v_s[t:nk, :] = kvc[:, 256:512]
        kr_s[t:nk, :] = ckr_ref[...]
    else:
        qr_s[...] = qh[:, 256:384]
        kr_s[...] = p_ref[:, 384:512]
        ckv_ref[...] = ckv
        kro_ref[...] = p_ref[:, 384:384 + MLA_ROPE]
    scale = (MLA_NOPE + MLA_ROPE) ** -0.5
    for r in range(t // QB):
        rows = slice(r * QB, (r + 1) * QB)
        outs = []
        for h in range(MLA_H):
            s = (_mm_nt(qn_s[rows, h * 64:(h + 1) * 64], kn_s[:, h * 64:(h + 1) * 64])
                 + _mm_nt(qr_s[rows, h * 32:(h + 1) * 32], kr_s[:, 0:MLA_ROPE])) * scale
            e = jnp.exp(s - jnp.max(s, axis=-1, keepdims=True))
            outs.append(_mm(e, v_s[:, h * 64:(h + 1) * 64]) / jnp.sum(e, axis=-1, keepdims=True))
        o_ref[rows, :] = jnp.concatenate(outs, axis=-1).astype(BF16)


def _mla(p, gq, gkv, wq, wkv, tables_q, tables_k, cache_ckv, cache_kr, prev, l, latent):
    t, nseq, row0 = (T_LAT, N_LAT, M_CTX // T_LAT) if latent else (T_CTX, N_CTX, 0)
    nk = t + (PAST if latent else 0)
    in_specs = [_seq_spec(t, W_MLA, row0, latent), _layer_spec(l, 1, MLA_QR),
                _layer_spec(l, 1, MLA_KVR), _layer_spec(l, MLA_QR, 384), _layer_spec(l, MLA_KVR, 512)]
    args = [p, gq, gkv, wq, wkv]
    aliases = {}
    o_shape = jax.ShapeDtypeStruct((nseq * t, 256), BF16)
    o_spec = pl.BlockSpec((t, 256), lambda b: (b, 0))
    if latent:
        in_specs += [_table_spec(t, 128)] * 6
        in_specs += [pl.BlockSpec((None, None, PAST, MLA_KVR), lambda b: (b, l, 0, 0)),
                     pl.BlockSpec((None, None, PAST, 128), lambda b: (b, l, 0, 0))]
        args += list(tables_q) + list(tables_k) + [cache_ckv, cache_kr]
    out_shape, out_specs = [o_shape], [o_spec]
    if not latent:
        prev = prev or (None, None)
        _stacked_output(prev[0], (t, MLA_KVR), l, nseq, in_specs, args, out_shape, out_specs, aliases)
        _stacked_output(prev[1], (t, MLA_ROPE), l, nseq, in_specs, args, out_shape, out_specs, aliases)
    scratch = [pltpu.VMEM((t, 256), F32), pltpu.VMEM((t, 128), F32), pltpu.VMEM((nk, 256), F32),
               pltpu.VMEM((nk, 128), F32), pltpu.VMEM((nk, 256), F32)]
    return functools.partial(_mla_kernel, t, latent), in_specs, args, out_shape, out_specs, scratch, aliases


def _gqa_kernel(t, latent, *refs):
    if latent:
        p_ref, sink_ref, cos_ref, slo_ref, shi_ref, ck_ref, cv_ref, o_ref, q_s, k_s = refs
    else:
        p_ref, sink_ref = refs[0:2]
        o_ref, ko_ref, vo_ref = refs[-3:]
        for kvh in range(GQA_KV):
            ko_ref[kvh] = p_ref[:, 256 + kvh * GQA_HD:256 + (kvh + 1) * GQA_HD]
            vo_ref[kvh] = p_ref[:, 384 + kvh * GQA_HD:384 + (kvh + 1) * GQA_HD]
    scale = GQA_HD ** -0.5
    grp = GQA_H // GQA_KV
    if latent:
        q_s[...] = _rope(p_ref[:, 0:256], cos_ref[...], slo_ref[...], shi_ref[...], GQA_HD // 2)
        k_s[...] = _rope(p_ref[:, 256:384], cos_ref[:, 0:128], slo_ref[:, 0:128], shi_ref[:, 0:128], GQA_HD // 2)
    for r in range(t // QB):
        rows = slice(r * QB, (r + 1) * QB)
        keys = slice(max(0, r * QB - WINDOW), min(t, (r + 1) * QB + WINDOW)) if latent else slice(0, t)
        if latent:
            nkeys = keys.stop - keys.start
            row = lax.broadcasted_iota(jnp.int32, (QB, nkeys), 0) + r * QB
            col = lax.broadcasted_iota(jnp.int32, (QB, nkeys), 1) + keys.start
            near = jnp.abs(row - col) <= WINDOW
        outs = []
        for h in range(GQA_H):
            kvh = h // grp
            ksl = slice(kvh * GQA_HD, (kvh + 1) * GQA_HD)
            sk = sink_ref[0:1, h * 64:h * 64 + 1]
            v = p_ref[keys, 384 + kvh * GQA_HD:384 + (kvh + 1) * GQA_HD]
            if latent:
                q = q_s[rows, h * 64:(h + 1) * 64]
                s_loc = jnp.where(near, _mm_nt(q, k_s[keys, ksl]) * scale, NEG)
                s_ctx = _mm_nt(q, ck_ref[kvh]) * scale
                m = jnp.maximum(jnp.maximum(jnp.max(s_loc, axis=-1, keepdims=True),
                                            jnp.max(s_ctx, axis=-1, keepdims=True)), sk)
                e_loc, e_ctx = jnp.exp(s_loc - m), jnp.exp(s_ctx - m)
                den = jnp.sum(e_loc, axis=-1, keepdims=True) + jnp.sum(e_ctx, axis=-1, keepdims=True) + jnp.exp(sk - m)
                outs.append((_mm(e_loc, v) + _mm(e_ctx, cv_ref[kvh])) / den)
            else:
                q = p_ref[rows, h * 64:(h + 1) * 64]
                s = _mm_nt(q, p_ref[:, 256 + ksl.start:256 + ksl.stop]) * scale
                m = jnp.maximum(jnp.max(s, axis=-1, keepdims=True), sk)
                e = jnp.exp(s - m)
                outs.append(_mm(e, v) / (jnp.sum(e, axis=-1, keepdims=True) + jnp.exp(sk - m)))
        o_ref[rows, :] = jnp.concatenate(outs, axis=-1).astype(BF16)


def _gqa(p, sink_lanes, tables, cache_k, cache_v, prev, l, latent):
    t, nseq, row0 = (T_LAT, N_LAT, M_CTX // T_LAT) if latent else (T_CTX, N_CTX, 0)
    in_specs = [_seq_spec(t, W_GQA, row0, latent), _layer_spec(l, 1, 256)]
    args = [p, sink_lanes]
    scratch = []
    aliases = {}
    out_shape = [jax.ShapeDtypeStruct((nseq * t, 256), BF16)]
    out_specs = [pl.BlockSpec((t, 256), lambda b: (b, 0))]
    if latent:
        in_specs += [_table_spec(t, 256)] * 3
        in_specs += [pl.BlockSpec((None, None, GQA_KV, PAST, GQA_HD), lambda b: (b, l, 0, 0, 0))] * 2
        args += list(tables) + [cache_k, cache_v]
        scratch = [pltpu.VMEM((t, 256), F32), pltpu.VMEM((t, 128), F32)]
    else:
        prev = prev or (None, None)
        for pv in prev:
            _stacked_output(pv, (GQA_KV, t, GQA_HD), l, nseq, in_specs, args, out_shape, out_specs, aliases)
    return functools.partial(_gqa_kernel, t, latent), in_specs, args, out_shape, out_specs, scratch, aliases


def _outproj_kernel(route, split, tm, *refs):
    is_lat = pl.program_id(0) >= M_CTX // tm
    if split:
        x_in = jnp.where(is_lat, refs[1][...], refs[0][...])
        refs = refs[2:]
    else:
        x_in = refs[0][...]
        refs = refs[1:]
    ctx_refs, lat_refs = refs[0:4], refs[4:8]
    w_ref, gt_ref, g_ref, sh_ref, sc_ref = refs[8:13]
    refs = refs[13:]
    mix = jnp.zeros((tm, D), F32)
    for m in range(4):
        o = jnp.where(is_lat, lat_refs[m][...], ctx_refs[m][...])
        mix = mix + jnp.dot(o, w_ref[m * 256:(m + 1) * 256, :], preferred_element_type=F32)
    x = x_in + gt_ref[...] * mix
    h = (_rms(x) * g_ref[...] * (1.0 + sc_ref[...]) + sh_ref[...]).astype(BF16)
    if not route:
        xo_ref, h_ref = refs
    else:
        r_ref, xo_ref, h_ref, gate_ref, sel_ref = refs
        lane = lax.broadcasted_iota(jnp.int32, gate_ref.shape, 1)
        logits = jnp.where(lane < N_EXP, jnp.dot(h, r_ref[...], preferred_element_type=F32), NEG)
        m1 = jnp.max(logits, axis=-1, keepdims=True)
        i1 = jnp.min(jnp.where(logits == m1, lane, LANE), axis=-1, keepdims=True)
        rest = jnp.where(lane == i1, NEG, logits)
        m2 = jnp.max(rest, axis=-1, keepdims=True)
        i2 = jnp.min(jnp.where(rest == m2, lane, LANE), axis=-1, keepdims=True)
        e2 = jnp.exp(m2 - m1)
        gate_ref[...] = jnp.where(lane == i1, 1.0 / (1.0 + e2), 0.0) + jnp.where(lane == i2, e2 / (1.0 + e2), 0.0)
        sel_ref[...] = jnp.where((lane == i1) | (lane == i2), 1, 0)
    xo_ref[...] = x
    h_ref[...] = h


def _outproj(xs, o_ctx, o_lat, w_out, g2, mod, router, l):
    tm = 512
    route = router is not None
    split = len(xs) == 2
    rows = lambda w: pl.BlockSpec((tm, w), lambda i: (i, 0))
    ctx_spec, lat_spec = _row_split_specs(tm, 256)
    x_specs = _row_split_specs(tm, D) if split else [rows(D)]
    in_specs = x_specs + [ctx_spec] * 4 + [lat_spec] * 4 + [
        _layer_spec(l, D, D), _mod_spec(l, 2, tm), _layer_spec(l, 1, D), _mod_spec(l, 3, tm), _mod_spec(l, 4, tm)]
    args = [*xs, *o_ctx, *o_lat, w_out, mod, g2, mod, mod]
    out_shape = [jax.ShapeDtypeStruct((M_ALL, D), F32), jax.ShapeDtypeStruct((M_ALL, D), BF16)]
    out_specs = [rows(D), rows(D)]
    if route:
        in_specs.append(_layer_spec(l // 2, D, LANE))
        args.append(router)
        out_shape += [jax.ShapeDtypeStruct((M_ALL, LANE), F32), jax.ShapeDtypeStruct((M_ALL, LANE), jnp.int32)]
        out_specs += [rows(LANE), rows(LANE)]
    return pl.pallas_call(
        functools.partial(_outproj_kernel, route, split, tm),
        out_shape=out_shape, grid=(M_ALL // tm,), in_specs=in_specs, out_specs=out_specs,
        compiler_params=_cparams("parallel"),
        name="outproj_route" if route else "outproj",
    )(*args)


def _ffn_kernel(h_ref, x_ref, gt_ref, wg_ref, wu_ref, wd_ref, o_ref, acc_ref):
    f = pl.program_id(1)

    @pl.when(f == 0)
    def _():
        acc_ref[...] = jnp.zeros_like(acc_ref)

    h = h_ref[...]
    g = jnp.dot(h, wg_ref[...].astype(BF16), preferred_element_type=F32)
    u = jnp.dot(h, wu_ref[...].astype(BF16), preferred_element_type=F32)
    acc_ref[...] += jnp.dot((_silu(g) * u).astype(BF16), wd_ref[...].astype(BF16), preferred_element_type=F32)

    @pl.when(f == pl.num_programs(1) - 1)
    def _():
        o_ref[...] = x_ref[...] + gt_ref[...] * acc_ref[...]


def _ffn(h, x, mod, wg, wu, wd, l):
    tm, tf = 1024, 256
    j = l // 2
    return pl.pallas_call(
        _ffn_kernel,
        out_shape=jax.ShapeDtypeStruct((M_ALL, D), F32),
        grid=(M_ALL // tm, D_FF // tf),
        in_specs=[pl.BlockSpec((tm, D), lambda i, f: (i, 0)), pl.BlockSpec((tm, D), lambda i, f: (i, 0)),
                  _mod_spec(l, 5, tm),
                  pl.BlockSpec((None, D, tf), lambda i, f: (j, 0, f)),
                  pl.BlockSpec((None, D, tf), lambda i, f: (j, 0, f)),
                  pl.BlockSpec((None, tf, D), lambda i, f: (j, f, 0))],
        out_specs=pl.BlockSpec((tm, D), lambda i, f: (i, 0)),
        scratch_shapes=[pltpu.VMEM((tm, D), F32)],
        compiler_params=_cparams("parallel", "arbitrary"),
        name="ffn_dense",
    )(h, x, mod, wg, wu, wd)


SUP, SUB, CHUNK = 2048, 256, 256
N_SUB = SUP // SUB
S_MAX = 2 * M_ALL // SUP + N_EXP
P_SLOT = S_MAX * SUP
TC = 128
WIN = TC + 16


def _moe_expert_kernel(se_ref, nt_ref, sblk_ref, clo_ref, chi_ref, pos_ref, h_ref, wg_ref, wu_ref, wd_ref, o_ref,
                       xs_s, acc_s, wg_s, wu_s, wd_s):
    s = pl.program_id(0)
    f = pl.program_id(1)
    n = nt_ref[s]

    @pl.when((f == 0) & (n == 0))
    def _():
        o_ref[...] = jnp.zeros_like(o_ref)

    @pl.when((f == 0) & (n > 0))
    def _():
        row = lax.broadcasted_iota(jnp.int32, (SUB, CHUNK), 0)

        def gather(j, carry):
            row0 = (s * N_SUB + j) * SUB
            acc_s[j] = jnp.zeros((SUB, D), F32)

            def chunk(c, carry):
                tpos = pos_ref[:, pl.ds(pl.multiple_of(c * CHUNK, CHUNK), CHUNK)]
                onehot = jnp.where(tpos - row0 == row, 1.0, 0.0).astype(BF16)
                rows = h_ref[pl.ds(pl.multiple_of(c * CHUNK, CHUNK), CHUNK), :]
                acc_s[j] += jnp.dot(onehot, rows, preferred_element_type=F32)
                return carry

            g = s * N_SUB + j
            lax.fori_loop(clo_ref[g], chi_ref[g] + 1, chunk, 0)
            xs_s[j] = acc_s[j].astype(BF16)
            acc_s[j] = jnp.zeros((SUB, D), F32)
            return carry

        lax.fori_loop(0, n, gather, 0)

    @pl.when(n > 0)
    def _():
        wg_s[...] = wg_ref[...].astype(BF16)
        wu_s[...] = wu_ref[...].astype(BF16)
        wd_s[...] = wd_ref[...].astype(BF16)

        def sub(j):
            x = xs_s[j]
            g = jnp.dot(x, wg_s[...], preferred_element_type=F32)
            u = jnp.dot(x, wu_s[...], preferred_element_type=F32)
            acc_s[j] += jnp.dot((_silu(g) * u).astype(BF16), wd_s[...], preferred_element_type=F32)

        def pair(jj, carry):
            sub(2 * jj)
            sub(2 * jj + 1)
            return carry

        lax.fori_loop(0, n // 2, pair, 0)

        @pl.when(n % 2 == 1)
        def _():
            sub(n - 1)

    @pl.when((f == pl.num_programs(1) - 1) & (n > 0))
    def _():
        for j in range(N_SUB):
            rows = slice(j * SUB, (j + 1) * SUB)

            @pl.when(j < n)
            def _():
                o_ref[rows, :] = acc_s[j].astype(BF16)

            @pl.when(j >= n)
            def _():
                o_ref[rows, :] = jnp.zeros((SUB, D), BF16)


def _moe_experts(h, pos_t, meta, wg, wu, wd, layer):
    tf = 512
    nf = D_FFE // tf
    se, nt, sblk, clo, chi = meta

    def w_up(s, f, se, nt, sblk, clo, chi):
        return (layer, se[s], 0, jnp.where(nt[s] > 0, f, nf - 1))

    def w_down(s, f, se, nt, sblk, clo, chi):
        return (layer, se[s], jnp.where(nt[s] > 0, f, nf - 1), 0)

    grid_spec = pltpu.PrefetchScalarGridSpec(
        num_scalar_prefetch=5,
        grid=(S_MAX, nf),
        in_specs=[pl.BlockSpec((None, 1, M_ALL), lambda s, f, se, nt, sblk, clo, chi: (se[s], 0, 0)),
                  pl.BlockSpec((M_ALL, D), lambda s, f, *_: (0, 0), pipeline_mode=pl.Buffered(1)),
                  pl.BlockSpec((None, None, D, tf), w_up), pl.BlockSpec((None, None, D, tf), w_up),
                  pl.BlockSpec((None, None, tf, D), w_down)],
        out_specs=pl.BlockSpec((SUP, D), lambda s, f, *_: (s, 0)),
        scratch_shapes=[pltpu.VMEM((N_SUB, SUB, D), BF16), pltpu.VMEM((N_SUB, SUB, D), F32),
                        pltpu.VMEM((D, tf), BF16), pltpu.VMEM((D, tf), BF16), pltpu.VMEM((tf, D), BF16)],
    )
    return pl.pallas_call(
        _moe_expert_kernel,
        out_shape=jax.ShapeDtypeStruct((P_SLOT, D), BF16),
        grid_spec=grid_spec,
        compiler_params=pltpu.CompilerParams(dimension_semantics=("arbitrary", "arbitrary"),
                                             vmem_limit_bytes=60 * 1024 * 1024),
        name="moe_experts",
    )(se, nt, sblk, clo, chi, pos_t, h, wg, wu, wd)


def _moe_combine_kernel(off_ref, x_ref, gt_ref, gate_ref, pos_ref, *refs):
    win_refs, o_ref = refs[:N_EXP], refs[N_EXP]
    i = pl.program_id(0)
    lane = lax.broadcasted_iota(jnp.int32, (TC, WIN), 1)
    y = jnp.zeros((TC, D), F32)
    for e in range(N_EXP):
        rel = pos_ref[:, e:e + 1] - off_ref[i * N_EXP + e] * 16
        onehot = jnp.where(rel == lane, 1.0, 0.0).astype(BF16)
        y = y + gate_ref[:, e:e + 1] * jnp.dot(onehot, win_refs[e][...], preferred_element_type=F32)
    o_ref[...] = x_ref[...] + gt_ref[...] * y


def _moe_combine(x, mod, gates, pos, off, slots, l):
    def win_spec(e):
        return pl.BlockSpec((pl.Element(WIN), pl.Element(D)), lambda i, off: (off[i * N_EXP + e] * 16, 0))

    grid_spec = pltpu.PrefetchScalarGridSpec(
        num_scalar_prefetch=1,
        grid=(M_ALL // TC,),
        in_specs=[pl.BlockSpec((TC, D), lambda i, off: (i, 0)),
                  _mod_spec(l, 5, TC),
                  pl.BlockSpec((TC, LANE), lambda i, off: (i, 0)), pl.BlockSpec((TC, LANE), lambda i, off: (i, 0))]
        + [win_spec(e) for e in range(N_EXP)],
        out_specs=pl.BlockSpec((TC, D), lambda i, off: (i, 0)),
    )
    return pl.pallas_call(
        _moe_combine_kernel,
        out_shape=jax.ShapeDtypeStruct((M_ALL, D), F32),
        grid_spec=grid_spec,
        compiler_params=_cparams("arbitrary"),
        name="moe_combine",
    )(off, x, mod, gates, pos, *([slots] * N_EXP))


def _moe_plan(sel):
    i32 = jnp.int32
    sel_t = sel[:, :N_EXP].T
    csum = jnp.cumsum(sel_t, axis=1)
    rank = csum - sel_t
    n_e = csum[:, -1]
    ns_e = (n_e + SUP - 1) // SUP
    end_e = jnp.cumsum(ns_e)
    start_e = end_e - ns_e
    n_used = end_e[-1]
    pos_t = jnp.where(sel_t > 0, start_e[:, None] * SUP + rank, -1)
    s_ids = jnp.arange(S_MAX, dtype=i32)
    sblk = jnp.minimum(s_ids, n_used - 1)
    se = jnp.sum((end_e[None, :] <= sblk[:, None]).astype(i32), axis=1)
    nv = jnp.clip(n_e[se] - (sblk - start_e[se]) * SUP, 0, SUP)
    nt = jnp.where(s_ids < n_used, (nv + SUB - 1) // SUB, 0)
    g_ids = jnp.arange(S_MAX * N_SUB, dtype=i32)
    s_g, e_g = g_ids // N_SUB, se[g_ids // N_SUB]
    r0 = (sblk[s_g] - start_e[e_g]) * SUP + (g_ids % N_SUB) * SUB
    r1 = jnp.minimum(r0 + SUB, n_e[e_g])
    live = (s_g < n_used) & (r0 < n_e[e_g])
    cs_g = csum[e_g]
    t_first = jnp.sum((cs_g <= r0[:, None]).astype(i32), axis=1)
    t_last = jnp.sum((cs_g < r1[:, None]).astype(i32), axis=1)
    clo = jnp.where(live, t_first // CHUNK, 0)
    chi = jnp.where(live, t_last // CHUNK, -1)
    before = jnp.concatenate([jnp.zeros((N_EXP, 1), i32), csum[:, TC - 1::TC][:, :-1]], axis=1)
    base = start_e[:, None] * SUP + before
    off = jnp.minimum(base // 16, (P_SLOT - WIN) // 16).T.reshape(-1)
    pos128 = jnp.pad(pos_t.T, ((0, 0), (0, LANE - N_EXP)), constant_values=-1)
    meta = (se.astype(i32), nt.astype(i32), sblk.astype(i32), clo.astype(i32), chi.astype(i32))
    return meta, pos_t.reshape(N_EXP, 1, M_ALL), pos128, off.astype(i32)


def _moe(h, x, mod, gates, sel, wg, wu, wd, l):
    meta, pos_t, pos, off = _moe_plan(sel)
    slots = _moe_experts(h, pos_t, meta, wg, wu, wd, l // 2)
    return _moe_combine(x, mod, gates, pos, off, slots, l)


def _final_kernel(tm, x_ref, g_ref, oc_ref, ol_ref):
    y = _rms(x_ref[...]) * g_ref[...]
    is_lat = pl.program_id(0) >= M_CTX // tm

    @pl.when(jnp.logical_not(is_lat))
    def _():
        oc_ref[...] = y

    @pl.when(is_lat)
    def _():
        ol_ref[...] = y


def _final_norm(x, g):
    tm = 1024
    return pl.pallas_call(
        functools.partial(_final_kernel, tm),
        out_shape=[jax.ShapeDtypeStruct((M_CTX, D), F32), jax.ShapeDtypeStruct((M_LAT, D), F32)],
        grid=(M_ALL // tm,),
        in_specs=[pl.BlockSpec((tm, D), lambda i: (i, 0)), pl.BlockSpec((1, D), lambda i: (0, 0))],
        out_specs=_row_split_specs(tm, D),
        compiler_params=_cparams("arbitrary"),
        name="final_norm",
    )(x, g)


def kernel(x_prompt, x_sample, state_gla, state_ret, cache_mla_ckv, cache_mla_krope, cache_gqa_k, cache_gqa_v,
           c, c_ctx, norm1_g, norm2_g, final_norm_g, w_mod, b_mod, w_in, w_out, gla_gate_w, gla_gate_b,
           mla_q_norm_g, mla_w_q_up, mla_kv_norm_g, mla_w_kv_up, ret_decay, gqa_sink,
           ffn_w_gate, ffn_w_up, ffn_w_down, moe_router, moe_w_gate, moe_w_up, moe_w_down):
    xs = (x_prompt.reshape(M_CTX, D), x_sample.reshape(M_LAT, D))

    cond = jnp.concatenate([c_ctx[None], c, jnp.zeros((8 - 1 - N_LAT, D), F32)], axis=0)
    mod = _modulation(cond, w_mod, b_mod)
    mod = mod[:, :1 + N_LAT].reshape(DEPTH, (1 + N_LAT) * 6, 1, D)

    gap = jnp.zeros((DEPTH, D, W_GLA - 800), F32)
    w_all = jnp.concatenate([w_in[:, :, 0:800], gap, w_in[:, :, 800:1216], gap, w_in[:, :, 1216:2752]],
                            axis=-1).astype(BF16)
    w_out_b = w_out.astype(BF16)
    wq = mla_w_q_up.reshape(DEPTH, MLA_QR, MLA_H, MLA_NOPE + MLA_ROPE)
    wq = jnp.concatenate([wq[..., :MLA_NOPE].reshape(DEPTH, MLA_QR, MLA_H * MLA_NOPE),
                          wq[..., MLA_NOPE:].reshape(DEPTH, MLA_QR, MLA_H * MLA_ROPE)], axis=-1).astype(BF16)
    wkv = mla_w_kv_up.reshape(DEPTH, MLA_KVR, MLA_H, MLA_NOPE + MLA_DV)
    wkv = jnp.concatenate([wkv[..., :MLA_NOPE].reshape(DEPTH, MLA_KVR, MLA_H * MLA_NOPE),
                           wkv[..., MLA_NOPE:].reshape(DEPTH, MLA_KVR, MLA_H * MLA_DV)], axis=-1).astype(BF16)
    router = jnp.pad(moe_router, ((0, 0), (0, 0), (0, LANE - N_EXP))).astype(BF16)
    dec_lanes = jnp.repeat(ret_decay, RET_DV, axis=-1)
    sink_lanes = jnp.repeat(gqa_sink, GQA_HD, axis=-1).reshape(DEPTH, 1, 256)
    gate_b = gla_gate_b.reshape(DEPTH, 2, 1, GLA_H * GLA_DK)
    g1, g2 = norm1_g.reshape(DEPTH, 1, D), norm2_g.reshape(DEPTH, 1, D)
    gq, gkv = mla_q_norm_g.reshape(DEPTH, 1, MLA_QR), mla_kv_norm_g.reshape(DEPTH, 1, MLA_KVR)
    eye = jnp.eye(GLA_H, dtype=F32)
    s0_gla = jnp.einsum('bldhkv,hg->bldhvgk', state_gla, eye).reshape(
        N_LAT, DEPTH, 2, GLA_H * GLA_DV, GLA_H * GLA_DK)
    cache_kr = jnp.pad(cache_mla_krope, ((0, 0), (0, 0), (0, 0), (0, 128 - MLA_ROPE)))

    rope64 = _rope_tables(T_LAT, 64, 256)
    rope32_q = _rope_tables(T_LAT, 32, 128)
    ck, sl, sh = _rope_tables(T_LAT, 32, 128)
    live = jnp.asarray((np.arange(128) < MLA_ROPE).astype(np.float32))[None]
    rope32_k = (ck * live, sl * live, sh * live)

    st_gla = st_ret = caches_mla = caches_gqa = None
    for l in range(DEPTH):
        p_gla, p_mla, p_ret, p_gqa = _inproj(xs, g1, mod, w_all, l)

        o_gla_c, st_gla, o_mla_c, ckv_all, kr_all, o_ret_c, st_ret, o_gqa_c, gk_all, gv_all = _fused_call([
            _gla(p_gla, gla_gate_w, gate_b, None, st_gla, l, False),
            _mla(p_mla, gq, gkv, wq, wkv, None, None, None, None, caches_mla, l, False),
            _ret(p_ret, dec_lanes, None, None, st_ret, l, False),
            _gqa(p_gqa, sink_lanes, None, None, None, caches_gqa, l, False)], N_CTX, "mixers_context")
        caches_mla, caches_gqa = (ckv_all, kr_all), (gk_all, gv_all)
        o_gla_s, o_ret_s = _fused_call([
            _gla(p_gla, gla_gate_w, gate_b, s0_gla, None, l, True),
            _ret(p_ret, dec_lanes, rope64, state_ret, None, l, True)], N_LAT, "mixers_latent_scan")
        o_mla_s, o_gqa_s = _fused_call([
            _mla(p_mla, gq, gkv, wq, wkv, rope32_q, rope32_k, cache_mla_ckv, cache_kr, None, l, True),
            _gqa(p_gqa, sink_lanes, rope64, cache_gqa_k, cache_gqa_v, None, l, True)], N_LAT, "mixers_latent_attn")
        o_ctx = (o_gla_c, o_mla_c, o_ret_c, o_gqa_c)
        o_lat = (o_gla_s, o_mla_s, o_ret_s, o_gqa_s)
        if l % 2 == 0:
            x, h2 = _outproj(xs, o_ctx, o_lat, w_out_b, g2, mod, None, l)
            x = _ffn(h2, x, mod, ffn_w_gate, ffn_w_up, ffn_w_down, l)
        else:
            x, h2, gates, sel = _outproj(xs, o_ctx, o_lat, w_out_b, g2, mod, router, l)
            x = _moe(h2, x, mod, gates, sel, moe_w_gate, moe_w_up, moe_w_down, l)
        xs = (x,)

    y_ctx, y_lat = _final_norm(x, final_norm_g[None])
    return (y_ctx.reshape(N_CTX, T_CTX, D), y_lat.reshape(N_LAT, T_LAT, D), jnp.swapaxes(st_gla, -1, -2), st_ret,
            *caches_mla, *caches_gqa)
```

```python
import functools

import numpy as np
import jax
import jax.numpy as jnp
from jax import lax
from jax.experimental import pallas as pl
from jax.experimental.pallas import tpu as pltpu

F32 = jnp.float32
BF16 = jnp.bfloat16
HIGHEST = lax.Precision.HIGHEST

D = 1024
N_CTX, T_CTX = 16, 256
N_LAT, T_LAT = 2, 1024
PAST = 256
DEPTH = 4
M_CTX = N_CTX * T_CTX
M_LAT = N_LAT * T_LAT
M_ALL = M_CTX + M_LAT
GRID_W = 64
ROPE_BASE = 10000.0
EPS = 1e-6

GLA_H, GLA_DK, GLA_DV, GLA_RANK, GLA_NORM, GLA_C = 4, 32, 64, 16, 16.0, 64
MLA_H, MLA_QR, MLA_KVR, MLA_NOPE, MLA_ROPE, MLA_DV = 4, 256, 128, 64, 32, 64
RET_H, RET_DK, RET_DV = 4, 64, 64
GQA_H, GQA_KV, GQA_HD, WINDOW = 4, 2, 64, 128
D_FF, N_EXP, D_FFE = 2816, 8, 3584

W_GLA, W_MLA, W_RET, W_GQA = 896, 512, 1024, 512
IN_WIDTH = 2752
IN_GROUPS = ((0, 800), (800, 1216), (1216, 2240), (2240, 2752))
LANE = 128
NEG = -1e30
QB = 256
VMEM_LIMIT = 56 * 1024 * 1024


def _cparams(*sem):
    return pltpu.CompilerParams(dimension_semantics=sem, vmem_limit_bytes=VMEM_LIMIT)


def _mm(a, b):
    return jnp.dot(a.astype(BF16), b.astype(BF16), preferred_element_type=F32)


def _mm_nt(a, b):
    return lax.dot_general(a.astype(BF16), b.astype(BF16), (((1,), (1,)), ((), ())), preferred_element_type=F32)


def _mm_tn(a, b):
    return lax.dot_general(a.astype(BF16), b.astype(BF16), (((0,), (0,)), ((), ())), preferred_element_type=F32)


def _mm_f32(a, b):
    return jnp.dot(a, b, precision=HIGHEST, preferred_element_type=F32)


def _silu(x):
    return x * (1.0 / (1.0 + jnp.exp(-x)))


def _log_sigmoid(x):
    return jnp.minimum(x, 0.0) - jnp.log1p(jnp.exp(-jnp.abs(x)))


def _rms(x):
    return x * lax.rsqrt(jnp.mean(x * x, axis=-1, keepdims=True) + EPS)


def _mod_row(tile, tm):
    return jnp.maximum((tile * tm) // T_LAT - (M_CTX // T_LAT - 1), 0)


def _mod_kernel(c_ref, w_ref, b_ref, o_ref):
    o_ref[...] = _mm(_silu(c_ref[...]), w_ref[...]) + b_ref[...]


def _modulation(cond, w_mod, b_mod):
    tn = 1536
    return pl.pallas_call(
        _mod_kernel,
        out_shape=jax.ShapeDtypeStruct((DEPTH, 8, 6 * D), F32),
        grid=(DEPTH, 6 * D // tn),
        in_specs=[pl.BlockSpec((8, D), lambda l, j: (0, 0)),
                  pl.BlockSpec((None, D, tn), lambda l, j: (l, 0, j)),
                  pl.BlockSpec((None, 1, tn), lambda l, j: (l, 0, j))],
        out_specs=pl.BlockSpec((None, 8, tn), lambda l, j: (l, 0, j)),
        compiler_params=_cparams("parallel", "parallel"),
        name="modulation",
    )(cond, w_mod, b_mod.reshape(DEPTH, 1, 6 * D))


def _inproj_kernel(split, tm, *refs):
    if split:
        xc_ref, xl_ref = refs[0:2]
        x = jnp.where(pl.program_id(0) >= M_CTX // tm, xl_ref[...], xc_ref[...])
        refs = refs[2:]
    else:
        x = refs[0][...]
        refs = refs[1:]
    g_ref, sh_ref, sc_ref, w_ref = refs[0:4]
    out_refs, w_s = refs[4:-1], refs[-1]

    @pl.when(pl.program_id(0) == 0)
    def _():
        dst = 0
        for (lo, hi), o_ref in zip(IN_GROUPS, out_refs):
            width = o_ref.shape[-1]
            w_s[:, dst:dst + hi - lo] = w_ref[:, lo:hi].astype(BF16)
            if width > hi - lo:
                w_s[:, dst + hi - lo:dst + width] = jnp.zeros((D, width - (hi - lo)), BF16)
            dst += width

    h = (_rms(x) * g_ref[...] * (1.0 + sc_ref[...]) + sh_ref[...]).astype(BF16)
    col = 0
    for o_ref in out_refs:
        width = o_ref.shape[-1]
        o_ref[...] = jnp.dot(h, w_s[:, col:col + width], preferred_element_type=F32)
        col += width


def _mod_spec(l, j, tm):
    return pl.BlockSpec((None, None, 1, D), lambda i, *_: (l, _mod_row(i, tm) * 6 + j, 0, 0))


def _layer_spec(l, *shape):
    return pl.BlockSpec((None,) + shape, lambda *_: (l,) + (0,) * len(shape))


def _stacked_output(prev, tail, l, nseq, in_specs, args, out_shape, out_specs, aliases):
    out_shape.append(jax.ShapeDtypeStruct((nseq, DEPTH) + tail, F32))
    out_specs.append(pl.BlockSpec((None, None) + tail, lambda b: (b, l) + (0,) * len(tail)))
    if prev is not None:
        in_specs.append(pl.BlockSpec(memory_space=pl.ANY))
        args.append(prev)
        aliases[len(args) - 1] = len(out_shape) - 1


def _row_split_specs(tm, width):
    n_ctx = M_CTX // tm
    return [pl.BlockSpec((tm, width), lambda i: (jnp.minimum(i, n_ctx - 1), 0)),
            pl.BlockSpec((tm, width), lambda i: (jnp.maximum(i - n_ctx, 0), 0))]


def _inproj(xs, g1, mod, w_in, l):
    tm = 512
    split = len(xs) == 2
    rows = lambda w: pl.BlockSpec((tm, w), lambda i: (i, 0))
    widths = (W_GLA, W_MLA, W_RET, W_GQA)
    x_specs = _row_split_specs(tm, D) if split else [rows(D)]
    w_spec = pl.BlockSpec((None, D, IN_WIDTH), lambda i: (l, 0, 0), pipeline_mode=pl.Buffered(1))
    return pl.pallas_call(
        functools.partial(_inproj_kernel, split, tm),
        out_shape=[jax.ShapeDtypeStruct((M_ALL, w), F32) for w in widths],
        grid=(M_ALL // tm,),
        in_specs=x_specs + [_layer_spec(l, 1, D), _mod_spec(l, 0, tm), _mod_spec(l, 1, tm), w_spec],
        out_specs=[rows(w) for w in widths],
        scratch_shapes=[pltpu.VMEM((D, sum(widths)), BF16)],
        compiler_params=_cparams("arbitrary"),
        name="inproj",
    )(*xs, g1, mod, mod, w_in)


def _head_rms_gate(o, gate):
    r = lax.broadcasted_iota(jnp.int32, (256, 256), 0) // 64
    c = lax.broadcasted_iota(jnp.int32, (256, 256), 1) // 64
    group_mean = jnp.where(r == c, 1.0 / 64.0, 0.0).astype(BF16)
    sq = o * o
    sq_hi = sq.astype(BF16)
    sq_lo = (sq - sq_hi.astype(F32)).astype(BF16)
    ms = (jnp.dot(sq_hi, group_mean, preferred_element_type=F32)
          + jnp.dot(sq_lo, group_mean, preferred_element_type=F32))
    return o * lax.rsqrt(ms + EPS) * _silu(gate)


def _rope(x, cos, sin_lo, sin_hi, half):
    w = x.shape[-1]
    return x * cos + pltpu.roll(x, w - half, 1) * sin_lo + pltpu.roll(x, half, 1) * sin_hi


def _rope_tables(t, head_dim, width):
    half = head_dim // 2
    quarter = head_dim // 4
    pos = np.arange(t)
    rows = (pos // GRID_W).astype(np.float32)
    cols = (pos % GRID_W).astype(np.float32)
    inv = np.power(np.float32(ROPE_BASE), -np.arange(quarter, dtype=np.float32) / np.float32(quarter)).astype(np.float32)
    ang = np.concatenate([rows[:, None] * inv, cols[:, None] * inv], axis=-1).astype(np.float32)
    lane = np.arange(width)
    a = ang[:, lane % half]
    cos, sin = np.cos(a).astype(np.float32), np.sin(a).astype(np.float32)
    low = (lane % head_dim) < half
    return (jnp.asarray(cos), jnp.asarray(np.where(low[None], -sin, 0.0).astype(np.float32)),
            jnp.asarray(np.where(low[None], 0.0, sin).astype(np.float32)))


def _seq_spec(t, width, row0, latent):
    return pl.BlockSpec((t, width), lambda b: (row0 + b, 0), pipeline_mode=pl.Buffered(1) if latent else None)


def _table_spec(t, width):
    return pl.BlockSpec((t, width), lambda b: (0, 0), pipeline_mode=pl.Buffered(1))


def _fused_call(parts, nseq, name):
    in_specs, args, out_shape, out_specs, scratch, aliases, layout = [], [], [], [], [], {}, []
    for fn, p_in, p_args, p_shape, p_out, p_scratch, p_alias in parts:
        layout.append((fn, len(p_args), len(p_shape), len(p_scratch)))
        for k, v in p_alias.items():
            aliases[len(args) + k] = len(out_shape) + v
        in_specs += p_in
        args += p_args
        out_shape += p_shape
        out_specs += p_out
        scratch += p_scratch
    n_in, n_out = len(args), len(out_shape)

    def kernel(*refs):
        i, o, s = 0, n_in, n_in + n_out
        for fn, ni, no, ns in layout:
            fn(*refs[i:i + ni], *refs[o:o + no], *refs[s:s + ns])
            i, o, s = i + ni, o + no, s + ns

    return pl.pallas_call(
        kernel, out_shape=out_shape, grid=(nseq,), in_specs=in_specs, out_specs=out_specs, scratch_shapes=scratch,
        input_output_aliases=aliases, name=name,
        compiler_params=pltpu.CompilerParams(dimension_semantics=("arbitrary",), vmem_limit_bytes=60 * 1024 * 1024),
    )(*args)


def _gla_kernel(t, latent, *refs):
    if latent:
        p_ref, gw_ref, gb_ref, s0_ref, o_ref, la_f, la_b, of_s, ob_s, st_s = refs
    else:
        p_ref, gw_ref, gb_ref = refs[0:3]
        o_ref, st_ref, la_f, la_b, of_s, ob_s, st_s = refs[-7:]
    c = GLA_C
    n = t // c
    hd = GLA_H * GLA_DK
    la_f[...] = _log_sigmoid(_mm_f32(p_ref[:, 768:784], gw_ref[0]) + gb_ref[0]) / GLA_NORM
    la_b[...] = _log_sigmoid(_mm_f32(p_ref[:, 784:800], gw_ref[1]) + gb_ref[1]) / GLA_NORM

    ri = lax.broadcasted_iota(jnp.int32, (GLA_H * c, c), 0) % c
    ci = lax.broadcasted_iota(jnp.int32, (GLA_H * c, c), 1)
    tri_r = lax.broadcasted_iota(jnp.int32, (c, c), 0)
    tri_c = lax.broadcasted_iota(jnp.int32, (c, c), 1)
    head_rows = lax.broadcasted_iota(jnp.int32, (GLA_H * c, hd), 0) // c
    own_dk = (head_rows == lax.broadcasted_iota(jnp.int32, (GLA_H * c, hd), 1) // GLA_DK).astype(F32)
    own_dv = (lax.broadcasted_iota(jnp.int32, (GLA_H * c, GLA_H * GLA_DV), 0) // c
              == lax.broadcasted_iota(jnp.int32, (GLA_H * c, GLA_H * GLA_DV), 1) // GLA_DV).astype(F32)
    scale = GLA_DK ** -0.5
    if latent:
        st_s[...] = s0_ref[...]
    else:
        st_s[...] = jnp.zeros_like(st_s)

    def chunk(row0, la_ref, d):
        fwd = d == 0
        keep = (ci <= ri) if fwd else (ci >= ri)
        tri = ((tri_c <= tri_r) if fwd else (tri_c >= tri_r)).astype(BF16)
        q = p_ref[pl.ds(row0, c), 0:128]
        k = p_ref[pl.ds(row0, c), 128:256] * scale
        v = p_ref[pl.ds(row0, c), 256:512]
        la = la_ref[pl.ds(row0, c), :]
        la_hi = la.astype(BF16)
        la_lo = (la - la_hi.astype(F32)).astype(BF16)
        bc = (jnp.dot(tri, la_hi, preferred_element_type=F32) + jnp.dot(tri, la_lo, preferred_element_type=F32))
        tot = bc[c - 1:c, :] if fwd else bc[0:1, :]
        mid = bc[c // 2 - 1:c // 2, :] if fwd else bc[c // 2:c // 2 + 1, :]
        qe, ke = q * jnp.exp(bc - mid), k * jnp.exp(mid - bc)
        q_in, k_out, a = q * jnp.exp(bc), k * jnp.exp(tot - bc), jnp.exp(tot)
        q_rows = jnp.concatenate([qe] * GLA_H, axis=0) * own_dk
        att = jnp.where(keep, _mm_nt(q_rows, ke), 0.0)
        o_all = _mm(att, v) * own_dv
        o = o_all[0:c] + o_all[c:2 * c] + o_all[2 * c:3 * c] + o_all[3 * c:4 * c]
        st = st_s[d]
        o = o + _mm_nt(q_in, st)
        st_s[d] = st * a + _mm_tn(v, k_out) * own_dk
        return o

    unroll = 4

    def body(i, carry):
        for u in range(unroll):
            rf = pl.multiple_of((i * unroll + u) * c, c)
            rb = pl.multiple_of((n - 1 - i * unroll - u) * c, c)
            of_s[pl.ds(rf, c), :] = chunk(rf, la_f, 0)
            ob_s[pl.ds(rb, c), :] = chunk(rb, la_b, 1)
        return carry

    lax.fori_loop(0, n // unroll, body, 0)
    if not latent:
        for d in range(2):
            for h in range(GLA_H):
                st_ref[d, h] = st_s[d, h * GLA_DV:(h + 1) * GLA_DV, h * GLA_DK:(h + 1) * GLA_DK]
    for r in range(t // QB):
        rows = slice(r * QB, (r + 1) * QB)
        o_ref[rows, :] = _head_rms_gate(of_s[rows, :] + ob_s[rows, :], p_ref[rows, 512:768]).astype(BF16)


def _gla(p, gate_w, gate_b, s0_bd, st_prev, l, latent):
    t, nseq, row0 = (T_LAT, N_LAT, M_CTX // T_LAT) if latent else (T_CTX, N_CTX, 0)
    st_shape = (2, GLA_H * GLA_DV, GLA_H * GLA_DK)
    in_specs = [_seq_spec(t, W_GLA, row0, latent),
                _layer_spec(l, 2, GLA_RANK, GLA_H * GLA_DK), _layer_spec(l, 2, 1, GLA_H * GLA_DK)]
    args = [p, gate_w, gate_b]
    aliases = {}
    o_shape = jax.ShapeDtypeStruct((nseq * t, 256), BF16)
    o_spec = pl.BlockSpec((t, 256), lambda b: (b, 0))
    if latent:
        in_specs.append(pl.BlockSpec((None, None) + st_shape, lambda b: (b, l, 0, 0, 0)))
        args.append(s0_bd)
    out_shape, out_specs = [o_shape], [o_spec]
    if not latent:
        _stacked_output(st_prev, (2, GLA_H, GLA_DV, GLA_DK), l, nseq, in_specs, args, out_shape, out_specs, aliases)
    scratch = [pltpu.VMEM((t, 128), F32), pltpu.VMEM((t, 128), F32),
               pltpu.VMEM((t, 256), F32), pltpu.VMEM((t, 256), F32), pltpu.VMEM(st_shape, F32)]
    return functools.partial(_gla_kernel, t, latent), in_specs, args, out_shape, out_specs, scratch, aliases


def _ret_kernel(t, latent, *refs):
    if latent:
        p_ref, dec_ref, cos_ref, slo_ref, shi_ref, s0_ref, o_ref, q_s, k_s, decay_s = refs
    else:
        p_ref, dec_ref = refs[0:2]
        o_ref, st_ref, decay_s = refs[-3:]
    lg = _log_sigmoid(dec_ref[...])
    scale = RET_DK ** -0.5
    nblk = t // QB

    @pl.when(pl.program_id(0) == 0)
    def _():
        wide = decay_s.shape[-1]
        dist = (lax.broadcasted_iota(jnp.int32, (QB, wide), 0) - lax.broadcasted_iota(jnp.int32, (QB, wide), 1)
                + (nblk - 1) * QB).astype(F32)
        for h in range(RET_H):
            lg_f, lg_b = lg[0:1, h * 64:h * 64 + 1], lg[1:2, h * 64:h * 64 + 1]
            decay_s[h] = (jnp.where(dist >= 0, jnp.exp(jnp.maximum(dist, 0.0) * lg_f), 0.0)
                          + jnp.where(dist <= 0, jnp.exp(jnp.maximum(-dist, 0.0) * lg_b), 0.0))

    if latent:
        q_s[...] = _rope(p_ref[:, 0:256], cos_ref[...], slo_ref[...], shi_ref[...], RET_DK // 2)
        k_s[...] = _rope(p_ref[:, 256:512], cos_ref[...], slo_ref[...], shi_ref[...], RET_DK // 2) * scale
        q_of = lambda rows, sl: q_s[rows, sl]
        k_of = lambda sl: k_s[:, sl]
    else:
        q_of = lambda rows, sl: p_ref[rows, sl]
        k_of = lambda sl: p_ref[:, 256 + sl.start:256 + sl.stop] * scale
    for r in range(nblk):
        rows = slice(r * QB, (r + 1) * QB)
        pos = (lax.broadcasted_iota(jnp.int32, (QB, 1), 0) + r * QB).astype(F32)
        outs = []
        for h in range(RET_H):
            sl = slice(h * RET_DK, (h + 1) * RET_DK)
            lg_f, lg_b = lg[0:1, h * 64:h * 64 + 1], lg[1:2, h * 64:h * 64 + 1]
            decay = decay_s[h, :, (nblk - 1 - r) * QB:(nblk - 1 - r) * QB + t]
            q = q_of(rows, sl)
            v = p_ref[:, 512 + h * RET_DV:512 + (h + 1) * RET_DV]
            o = _mm(_mm_nt(q, k_of(sl)) * decay, v)
            if latent:
                o = o + _mm(q * jnp.exp((pos + 1.0) * lg_f), s0_ref[0, h])
                o = o + _mm(q * jnp.exp((float(t) - pos) * lg_b), s0_ref[1, h])
            outs.append(o)
        o_ref[rows, :] = _head_rms_gate(jnp.concatenate(outs, axis=-1), p_ref[rows, 768:1024]).astype(BF16)
    if not latent:
        j = lax.broadcasted_iota(jnp.int32, (t, 1), 0).astype(F32)
        for h in range(RET_H):
            sl = slice(h * RET_DK, (h + 1) * RET_DK)
            lg_f, lg_b = lg[0:1, h * 64:h * 64 + 1], lg[1:2, h * 64:h * 64 + 1]
            v = p_ref[:, 512 + h * RET_DV:512 + (h + 1) * RET_DV]
            k = k_of(sl)
            st_ref[0, h] = _mm_tn(k * jnp.exp((float(t - 1) - j) * lg_f), v)
            st_ref[1, h] = _mm_tn(k * jnp.exp(j * lg_b), v)


def _ret(p, dec_lanes, tables, s0, st_prev, l, latent):
    t, nseq, row0 = (T_LAT, N_LAT, M_CTX // T_LAT) if latent else (T_CTX, N_CTX, 0)
    st_shape = (2, RET_H, RET_DK, RET_DV)
    in_specs = [_seq_spec(t, W_RET, row0, latent), _layer_spec(l, 2, 256)]
    args = [p, dec_lanes]
    aliases = {}
    o_shape = jax.ShapeDtypeStruct((nseq * t, 256), BF16)
    o_spec = pl.BlockSpec((t, 256), lambda b: (b, 0))
    scratch = []
    if latent:
        in_specs += [_table_spec(t, 256)] * 3
        in_specs.append(pl.BlockSpec((None, None) + st_shape, lambda b: (b, l, 0, 0, 0, 0)))
        args += list(tables) + [s0]
        scratch = [pltpu.VMEM((t, 256), F32), pltpu.VMEM((t, 256), F32)]
    scratch.append(pltpu.VMEM((RET_H, QB, 2 * t - QB), F32))
    out_shape, out_specs = [o_shape], [o_spec]
    if not latent:
        _stacked_output(st_prev, st_shape, l, nseq, in_specs, args, out_shape, out_specs, aliases)
    return functools.partial(_ret_kernel, t, latent), in_specs, args, out_shape, out_specs, scratch, aliases


def _mla_kernel(t, latent, *refs):
    if latent:
        (p_ref, gq_ref, gkv_ref, wq_ref, wkv_ref, cq_ref, slq_ref, shq_ref, ck_ref, slk_ref, shk_ref,
         cckv_ref, ckr_ref, o_ref, qn_s, qr_s, kn_s, kr_s, v_s) = refs
    else:
        p_ref, gq_ref, gkv_ref, wq_ref, wkv_ref = refs[0:5]
        o_ref, ckv_ref, kro_ref, qn_s, qr_s, kn_s, kr_s, v_s = refs[-8:]
    nk = t + (PAST if latent else 0)
    qh = _mm(_rms(p_ref[:, 0:256]) * gq_ref[...], wq_ref[...])
    ckv = _rms(p_ref[:, 256:384]) * gkv_ref[...]
    kv = _mm(ckv, wkv_ref[...])
    qn_s[...] = qh[:, 0:256]
    kn_s[0:t, :] = kv[:, 0:256]
    v_s[0:t, :] = kv[:, 256:512]
    if latent:
        qr_s[...] = _rope(qh[:, 256:384], cq_ref[...], slq_ref[...], shq_ref[...], MLA_ROPE // 2)
        kr_s[0:t, :] = _rope(p_ref[:, 384:512], ck_ref[...], slk_ref[...], shk_ref[...], MLA_ROPE // 2)
        kvc = _mm(cckv_ref[...], wkv_ref[...])
        kn_s[t:nk, :] = kvc[:, 0:256]
        v_s[t:nk, :] = kvc[:, 256:512]
        kr_s[t:nk, :] = ckr_ref[...]
    else:
        qr_s[...] = qh[:, 256:384]
        kr_s[...] = p_ref[:, 384:512]
        ckv_ref[...] = ckv
        kro_ref[...] = p_ref[:, 384:384 + MLA_ROPE]
    scale = (MLA_NOPE + MLA_ROPE) ** -0.5
    for r in range(t // QB):
        rows = slice(r * QB, (r + 1) * QB)
        outs = []
        for h in range(MLA_H):
            s = (_mm_nt(qn_s[rows, h * 64:(h + 1) * 64], kn_s[:, h * 64:(h + 1) * 64])
                 + _mm_nt(qr_s[rows, h * 32:(h + 1) * 32], kr_s[:, 0:MLA_ROPE])) * scale
            e = jnp.exp(s - jnp.max(s, axis=-1, keepdims=True))
            outs.append(_mm(e, v_s[:, h * 64:(h + 1) * 64]) / jnp.sum(e, axis=-1, keepdims=True))
        o_ref[rows, :] = jnp.concatenate(outs, axis=-1).astype(BF16)


def _mla(p, gq, gkv, wq, wkv, tables_q, tables_k, cache_ckv, cache_kr, prev, l, latent):
    t, nseq, row0 = (T_LAT, N_LAT, M_CTX // T_LAT) if latent else (T_CTX, N_CTX, 0)
    nk = t + (PAST if latent else 0)
    in_specs = [_seq_spec(t, W_MLA, row0, latent), _layer_spec(l, 1, MLA_QR),
                _layer_spec(l, 1, MLA_KVR), _layer_spec(l, MLA_QR, 384), _layer_spec(l, MLA_KVR, 512)]
    args = [p, gq, gkv, wq, wkv]
    aliases = {}
    o_shape = jax.ShapeDtypeStruct((nseq * t, 256), BF16)
    o_spec = pl.BlockSpec((t, 256), lambda b: (b, 0))
    if latent:
        in_specs += [_table_spec(t, 128)] * 6
        in_specs += [pl.BlockSpec((None, None, PAST, MLA_KVR), lambda b: (b, l, 0, 0)),
                     pl.BlockSpec((None, None, PAST, 128), lambda b: (b, l, 0, 0))]
        args += list(tables_q) + list(tables_k) + [cache_ckv, cache_kr]
    out_shape, out_specs = [o_shape], [o_spec]
    if not latent:
        prev = prev or (None, None)
        _stacked_output(prev[0], (t, MLA_KVR), l, nseq, in_specs, args, out_shape, out_specs, aliases)
        _stacked_output(prev[1], (t, MLA_ROPE), l, nseq, in_specs, args, out_shape, out_specs, aliases)
    scratch = [pltpu.VMEM((t, 256), F32), pltpu.VMEM((t, 128), F32), pltpu.VMEM((nk, 256), F32),
               pltpu.VMEM((nk, 128), F32), pltpu.VMEM((nk, 256), F32)]
    return functools.partial(_mla_kernel, t, latent), in_specs, args, out_shape, out_specs, scratch, aliases


def _gqa_kernel(t, latent, *refs):
    if latent:
        p_ref, sink_ref, cos_ref, slo_ref, shi_ref, ck_ref, cv_ref, o_ref, q_s, k_s = refs
    else:
        p_ref, sink_ref = refs[0:2]
        o_ref, ko_ref, vo_ref = refs[-3:]
        for kvh in range(GQA_KV):
            ko_ref[kvh] = p_ref[:, 256 + kvh * GQA_HD:256 + (kvh + 1) * GQA_HD]
            vo_ref[kvh] = p_ref[:, 384 + kvh * GQA_HD:384 + (kvh + 1) * GQA_HD]
    scale = GQA_HD ** -0.5
    grp = GQA_H // GQA_KV
    if latent:
        q_s[...] = _rope(p_ref[:, 0:256], cos_ref[...], slo_ref[...], shi_ref[...], GQA_HD // 2)
        k_s[...] = _rope(p_ref[:, 256:384], cos_ref[:, 0:128], slo_ref[:, 0:128], shi_ref[:, 0:128], GQA_HD // 2)
    for r in range(t // QB):
        rows = slice(r * QB, (r + 1) * QB)
        keys = slice(max(0, r * QB - WINDOW), min(t, (r + 1) * QB + WINDOW)) if latent else slice(0, t)
        if latent:
            nkeys = keys.stop - keys.start
            row = lax.broadcasted_iota(jnp.int32, (QB, nkeys), 0) + r * QB
            col = lax.broadcasted_iota(jnp.int32, (QB, nkeys), 1) + keys.start
            near = jnp.abs(row - col) <= WINDOW
        outs = []
        for h in range(GQA_H):
            kvh = h // grp
            ksl = slice(kvh * GQA_HD, (kvh + 1) * GQA_HD)
            sk = sink_ref[0:1, h * 64:h * 64 + 1]
            v = p_ref[keys, 384 + kvh * GQA_HD:384 + (kvh + 1) * GQA_HD]
            if latent:
                q = q_s[rows, h * 64:(h + 1) * 64]
                s_loc = jnp.where(near, _mm_nt(q, k_s[keys, ksl]) * scale, NEG)
                s_ctx = _mm_nt(q, ck_ref[kvh]) * scale
                m = jnp.maximum(jnp.maximum(jnp.max(s_loc, axis=-1, keepdims=True),
                                            jnp.max(s_ctx, axis=-1, keepdims=True)), sk)
                e_loc, e_ctx = jnp.exp(s_loc - m), jnp.exp(s_ctx - m)
                den = jnp.sum(e_loc, axis=-1, keepdims=True) + jnp.sum(e_ctx, axis=-1, keepdims=True) + jnp.exp(sk - m)
                outs.append((_mm(e_loc, v) + _mm(e_ctx, cv_ref[kvh])) / den)
            else:
                q = p_ref[rows, h * 64:(h + 1) * 64]
                s = _mm_nt(q, p_ref[:, 256 + ksl.start:256 + ksl.stop]) * scale
                m = jnp.maximum(jnp.max(s, axis=-1, keepdims=True), sk)
                e = jnp.exp(s - m)
                outs.append(_mm(e, v) / (jnp.sum(e, axis=-1, keepdims=True) + jnp.exp(sk - m)))
        o_ref[rows, :] = jnp.concatenate(outs, axis=-1).astype(BF16)


def _gqa(p, sink_lanes, tables, cache_k, cache_v, prev, l, latent):
    t, nseq, row0 = (T_LAT, N_LAT, M_CTX // T_LAT) if latent else (T_CTX, N_CTX, 0)
    in_specs = [_seq_spec(t, W_GQA, row0, latent), _layer_spec(l, 1, 256)]
    args = [p, sink_lanes]
    scratch = []
    aliases = {}
    out_shape = [jax.ShapeDtypeStruct((nseq * t, 256), BF16)]
    out_specs = [pl.BlockSpec((t, 256), lambda b: (b, 0))]
    if latent:
        in_specs += [_table_spec(t, 256)] * 3
        in_specs += [pl.BlockSpec((None, None, GQA_KV, PAST, GQA_HD), lambda b: (b, l, 0, 0, 0))] * 2
        args += list(tables) + [cache_k, cache_v]
        scratch = [pltpu.VMEM((t, 256), F32), pltpu.VMEM((t, 128), F32)]
    else:
        prev = prev or (None, None)
        for pv in prev:
            _stacked_output(pv, (GQA_KV, t, GQA_HD), l, nseq, in_specs, args, out_shape, out_specs, aliases)
    return functools.partial(_gqa_kernel, t, latent), in_specs, args, out_shape, out_specs, scratch, aliases


def _outproj_kernel(route, split, tm, *refs):
    is_lat = pl.program_id(0) >= M_CTX // tm
    if split:
        x_in = jnp.where(is_lat, refs[1][...], refs[0][...])
        refs = refs[2:]
    else:
        x_in = refs[0][...]
        refs = refs[1:]
    ctx_refs, lat_refs = refs[0:4], refs[4:8]
    w_ref, gt_ref, g_ref, sh_ref, sc_ref = refs[8:13]
    refs = refs[13:]
    mix = jnp.zeros((tm, D), F32)
    for m in range(4):
        o = jnp.where(is_lat, lat_refs[m][...], ctx_refs[m][...])
        mix = mix + jnp.dot(o, w_ref[m * 256:(m + 1) * 256, :], preferred_element_type=F32)
    x = x_in + gt_ref[...] * mix
    h = (_rms(x) * g_ref[...] * (1.0 + sc_ref[...]) + sh_ref[...]).astype(BF16)
    if not route:
        xo_ref, h_ref = refs
    else:
        r_ref, xo_ref, h_ref, gate_ref, sel_ref = refs
        lane = lax.broadcasted_iota(jnp.int32, gate_ref.shape, 1)
        logits = jnp.where(lane < N_EXP, jnp.dot(h, r_ref[...], preferred_element_type=F32), NEG)
        m1 = jnp.max(logits, axis=-1, keepdims=True)
        i1 = jnp.min(jnp.where(logits == m1, lane, LANE), axis=-1, keepdims=True)
        rest = jnp.where(lane == i1, NEG, logits)
        m2 = jnp.max(rest, axis=-1, keepdims=True)
        i2 = jnp.min(jnp.where(rest == m2, lane, LANE), axis=-1, keepdims=True)
        e2 = jnp.exp(m2 - m1)
        gate_ref[...] = jnp.where(lane == i1, 1.0 / (1.0 + e2), 0.0) + jnp.where(lane == i2, e2 / (1.0 + e2), 0.0)
        sel_ref[...] = jnp.where((lane == i1) | (lane == i2), 1, 0)
    xo_ref[...] = x
    h_ref[...] = h


def _outproj(xs, o_ctx, o_lat, w_out, g2, mod, router, l):
    tm = 512
    route = router is not None
    split = len(xs) == 2
    rows = lambda w: pl.BlockSpec((tm, w), lambda i: (i, 0))
    ctx_spec, lat_spec = _row_split_specs(tm, 256)
    x_specs = _row_split_specs(tm, D) if split else [rows(D)]
    in_specs = x_specs + [ctx_spec] * 4 + [lat_spec] * 4 + [
        _layer_spec(l, D, D), _mod_spec(l, 2, tm), _layer_spec(l, 1, D), _mod_spec(l, 3, tm), _mod_spec(l, 4, tm)]
    args = [*xs, *o_ctx, *o_lat, w_out, mod, g2, mod, mod]
    out_shape = [jax.ShapeDtypeStruct((M_ALL, D), F32), jax.ShapeDtypeStruct((M_ALL, D), BF16)]
    out_specs = [rows(D), rows(D)]
    if route:
        in_specs.append(_layer_spec(l // 2, D, LANE))
        args.append(router)
        out_shape += [jax.ShapeDtypeStruct((M_ALL, LANE), F32), jax.ShapeDtypeStruct((M_ALL, LANE), jnp.int32)]
        out_specs += [rows(LANE), rows(LANE)]
    return pl.pallas_call(
        functools.partial(_outproj_kernel, route, split, tm),
        out_shape=out_shape, grid=(M_ALL // tm,), in_specs=in_specs, out_specs=out_specs,
        compiler_params=_cparams("parallel"),
        name="outproj_route" if route else "outproj",
    )(*args)


def _ffn_kernel(h_ref, x_ref, gt_ref, wg_ref, wu_ref, wd_ref, o_ref, acc_ref):
    f = pl.program_id(1)

    @pl.when(f == 0)
    def _():
        acc_ref[...] = jnp.zeros_like(acc_ref)

    wg, wu, wd = wg_ref[...].astype(BF16), wu_ref[...].astype(BF16), wd_ref[...].astype(BF16)
    half = h_ref.shape[0] // 2
    for rows in (slice(0, half), slice(half, 2 * half)):
        h = h_ref[rows, :]
        g = jnp.dot(h, wg, preferred_element_type=F32)
        u = jnp.dot(h, wu, preferred_element_type=F32)
        acc_ref[rows, :] += jnp.dot((_silu(g) * u).astype(BF16), wd, preferred_element_type=F32)

    @pl.when(f == pl.num_programs(1) - 1)
    def _():
        o_ref[...] = x_ref[...] + gt_ref[...] * acc_ref[...]


def _ffn(h, x, mod, wg, wu, wd, l):
    tm, tf = 1024, 256
    j = l // 2
    return pl.pallas_call(
        _ffn_kernel,
        out_shape=jax.ShapeDtypeStruct((M_ALL, D), F32),
        grid=(M_ALL // tm, D_FF // tf),
        in_specs=[pl.BlockSpec((tm, D), lambda i, f: (i, 0)), pl.BlockSpec((tm, D), lambda i, f: (i, 0)),
                  _mod_spec(l, 5, tm),
                  pl.BlockSpec((None, D, tf), lambda i, f: (j, 0, f)),
                  pl.BlockSpec((None, D, tf), lambda i, f: (j, 0, f)),
                  pl.BlockSpec((None, tf, D), lambda i, f: (j, f, 0))],
        out_specs=pl.BlockSpec((tm, D), lambda i, f: (i, 0)),
        scratch_shapes=[pltpu.VMEM((tm, D), F32)],
        compiler_params=_cparams("parallel", "arbitrary"),
        name="ffn_dense",
    )(h, x, mod, wg, wu, wd)


SUP, SUB, CHUNK = 2048, 256, 256
N_SUB = SUP // SUB
S_MAX = 2 * M_ALL // SUP + N_EXP
P_SLOT = S_MAX * SUP
TC = 128
WIN = TC + 16


def _moe_expert_kernel(se_ref, nt_ref, sblk_ref, clo_ref, chi_ref, pos_ref, h_ref, wg_ref, wu_ref, wd_ref, o_ref,
                       xs_s, acc_s, wg_s, wu_s, wd_s):
    s = pl.program_id(0)
    f = pl.program_id(1)
    n = nt_ref[s]

    @pl.when((f == 0) & (n == 0))
    def _():
        o_ref[...] = jnp.zeros_like(o_ref)

    @pl.when((f == 0) & (n > 0))
    def _():
        row = lax.broadcasted_iota(jnp.int32, (SUB, CHUNK), 0)

        def gather(j, carry):
            row0 = (s * N_SUB + j) * SUB
            acc_s[j] = jnp.zeros((SUB, D), F32)

            def chunk(c, carry):
                tpos = pos_ref[:, pl.ds(pl.multiple_of(c * CHUNK, CHUNK), CHUNK)]
                onehot = jnp.where(tpos - row0 == row, 1.0, 0.0).astype(BF16)
                rows = h_ref[pl.ds(pl.multiple_of(c * CHUNK, CHUNK), CHUNK), :]
                acc_s[j] += jnp.dot(onehot, rows, preferred_element_type=F32)
                return carry

            g = s * N_SUB + j
            lax.fori_loop(clo_ref[g], chi_ref[g] + 1, chunk, 0)
            xs_s[j] = acc_s[j].astype(BF16)
            acc_s[j] = jnp.zeros((SUB, D), F32)
            return carry

        lax.fori_loop(0, n, gather, 0)

    @pl.when(n > 0)
    def _():
        wg_s[...] = wg_ref[...].astype(BF16)
        wu_s[...] = wu_ref[...].astype(BF16)
        wd_s[...] = wd_ref[...].astype(BF16)

        def sub(j):
            x = xs_s[j]
            g = jnp.dot(x, wg_s[...], preferred_element_type=F32)
            u = jnp.dot(x, wu_s[...], preferred_element_type=F32)
            acc_s[j] += jnp.dot((_silu(g) * u).astype(BF16), wd_s[...], preferred_element_type=F32)

        def pair(jj, carry):
            sub(2 * jj)
            sub(2 * jj + 1)
            return carry

        lax.fori_loop(0, n // 2, pair, 0)

        @pl.when(n % 2 == 1)
        def _():
            sub(n - 1)

    @pl.when((f == pl.num_programs(1) - 1) & (n > 0))
    def _():
        for j in range(N_SUB):
            rows = slice(j * SUB, (j + 1) * SUB)

            @pl.when(j < n)
            def _():
                o_ref[rows, :] = acc_s[j].astype(BF16)

            @pl.when(j >= n)
            def _():
                o_ref[rows, :] = jnp.zeros((SUB, D), BF16)


def _moe_experts(h, pos_t, meta, wg, wu, wd, layer):
    tf = 512
    nf = D_FFE // tf
    se, nt, sblk, clo, chi = meta

    def w_up(s, f, se, nt, sblk, clo, chi):
        return (layer, se[s], 0, jnp.where(nt[s] > 0, f, nf - 1))

    def w_down(s, f, se, nt, sblk, clo, chi):
        return (layer, se[s], jnp.where(nt[s] > 0, f, nf - 1), 0)

    grid_spec = pltpu.PrefetchScalarGridSpec(
        num_scalar_prefetch=5,
        grid=(S_MAX, nf),
        in_specs=[pl.BlockSpec((None, 1, M_ALL), lambda s, f, se, nt, sblk, clo, chi: (se[s], 0, 0)),
                  pl.BlockSpec((M_ALL, D), lambda s, f, *_: (0, 0), pipeline_mode=pl.Buffered(1)),
                  pl.BlockSpec((None, None, D, tf), w_up), pl.BlockSpec((None, None, D, tf), w_up),
                  pl.BlockSpec((None, None, tf, D), w_down)],
        out_specs=pl.BlockSpec((SUP, D), lambda s, f, *_: (s, 0)),
        scratch_shapes=[pltpu.VMEM((N_SUB, SUB, D), BF16), pltpu.VMEM((N_SUB, SUB, D), F32),
                        pltpu.VMEM((D, tf), BF16), pltpu.VMEM((D, tf), BF16), pltpu.VMEM((tf, D), BF16)],
    )
    return pl.pallas_call(
        _moe_expert_kernel,
        out_shape=jax.ShapeDtypeStruct((P_SLOT, D), BF16),
        grid_spec=grid_spec,
        compiler_params=pltpu.CompilerParams(dimension_semantics=("arbitrary", "arbitrary"),
                                             vmem_limit_bytes=60 * 1024 * 1024),
        name="moe_experts",
    )(se, nt, sblk, clo, chi, pos_t, h, wg, wu, wd)


def _moe_combine_kernel(off_ref, x_ref, gt_ref, gate_ref, pos_ref, *refs):
    win_refs, o_ref = refs[:N_EXP], refs[N_EXP]
    i = pl.program_id(0)
    lane = lax.broadcasted_iota(jnp.int32, (TC, WIN), 1)
    y = jnp.zeros((TC, D), F32)
    for e in range(N_EXP):
        rel = pos_ref[:, e:e + 1] - off_ref[i * N_EXP + e] * 16
        onehot = jnp.where(rel == lane, 1.0, 0.0).astype(BF16)
        y = y + gate_ref[:, e:e + 1] * jnp.dot(onehot, win_refs[e][...], preferred_element_type=F32)
    o_ref[...] = x_ref[...] + gt_ref[...] * y


def _moe_combine(x, mod, gates, pos, off, slots, l):
    def win_spec(e):
        return pl.BlockSpec((pl.Element(WIN), pl.Element(D)), lambda i, off: (off[i * N_EXP + e] * 16, 0))

    grid_spec = pltpu.PrefetchScalarGridSpec(
        num_scalar_prefetch=1,
        grid=(M_ALL // TC,),
        in_specs=[pl.BlockSpec((TC, D), lambda i, off: (i, 0)),
                  _mod_spec(l, 5, TC),
                  pl.BlockSpec((TC, LANE), lambda i, off: (i, 0)), pl.BlockSpec((TC, LANE), lambda i, off: (i, 0))]
        + [win_spec(e) for e in range(N_EXP)],
        out_specs=pl.BlockSpec((TC, D), lambda i, off: (i, 0)),
    )
    return pl.pallas_call(
        _moe_combine_kernel,
        out_shape=jax.ShapeDtypeStruct((M_ALL, D), F32),
        grid_spec=grid_spec,
        compiler_params=_cparams("arbitrary"),
        name="moe_combine",
    )(off, x, mod, gates, pos, *([slots] * N_EXP))


def _moe_plan(sel):
    i32 = jnp.int32
    sel_t = sel[:, :N_EXP].T
    csum = jnp.cumsum(sel_t, axis=1)
    rank = csum - sel_t
    n_e = csum[:, -1]
    ns_e = (n_e + SUP - 1) // SUP
    end_e = jnp.cumsum(ns_e)
    start_e = end_e - ns_e
    n_used = end_e[-1]
    pos_t = jnp.where(sel_t > 0, start_e[:, None] * SUP + rank, -1)
    s_ids = jnp.arange(S_MAX, dtype=i32)
    sblk = jnp.minimum(s_ids, n_used - 1)
    se = jnp.sum((end_e[None, :] <= sblk[:, None]).astype(i32), axis=1)
    nv = jnp.clip(n_e[se] - (sblk - start_e[se]) * SUP, 0, SUP)
    nt = jnp.where(s_ids < n_used, (nv + SUB - 1) // SUB, 0)
    g_ids = jnp.arange(S_MAX * N_SUB, dtype=i32)
    s_g, e_g = g_ids // N_SUB, se[g_ids // N_SUB]
    r0 = (sblk[s_g] - start_e[e_g]) * SUP + (g_ids % N_SUB) * SUB
    r1 = jnp.minimum(r0 + SUB, n_e[e_g])
    live = (s_g < n_used) & (r0 < n_e[e_g])
    cs_g = csum[e_g]
    t_first = jnp.sum((cs_g <= r0[:, None]).astype(i32), axis=1)
    t_last = jnp.sum((cs_g < r1[:, None]).astype(i32), axis=1)
    clo = jnp.where(live, t_first // CHUNK, 0)
    chi = jnp.where(live, t_last // CHUNK, -1)
    before = jnp.concatenate([jnp.zeros((N_EXP, 1), i32), csum[:, TC - 1::TC][:, :-1]], axis=1)
    base = start_e[:, None] * SUP + before
    off = jnp.minimum(base // 16, (P_SLOT - WIN) // 16).T.reshape(-1)
    pos128 = jnp.pad(pos_t.T, ((0, 0), (0, LANE - N_EXP)), constant_values=-1)
    meta = (se.astype(i32), nt.astype(i32), sblk.astype(i32), clo.astype(i32), chi.astype(i32))
    return meta, pos_t.reshape(N_EXP, 1, M_ALL), pos128, off.astype(i32)


def _moe(h, x, mod, gates, sel, wg, wu, wd, l):
    meta, pos_t, pos, off = _moe_plan(sel)
    slots = _moe_experts(h, pos_t, meta, wg, wu, wd, l // 2)
    return _moe_combine(x, mod, gates, pos, off, slots, l)


def _final_kernel(tm, x_ref, g_ref, oc_ref, ol_ref):
    y = _rms(x_ref[...]) * g_ref[...]
    is_lat = pl.program_id(0) >= M_CTX // tm

    @pl.when(jnp.logical_not(is_lat))
    def _():
        oc_ref[...] = y

    @pl.when(is_lat)
    def _():
        ol_ref[...] = y


def _final_norm(x, g):
    tm = 1024
    return pl.pallas_call(
        functools.partial(_final_kernel, tm),
        out_shape=[jax.ShapeDtypeStruct((M_CTX, D), F32), jax.ShapeDtypeStruct((M_LAT, D), F32)],
        grid=(M_ALL // tm,),
        in_specs=[pl.BlockSpec((tm, D), lambda i: (i, 0)), pl.BlockSpec((1, D), lambda i: (0, 0))],
        out_specs=_row_split_specs(tm, D),
        compiler_params=_cparams("arbitrary"),
        name="final_norm",
    )(x, g)


def kernel(x_prompt, x_sample, state_gla, state_ret, cache_mla_ckv, cache_mla_krope, cache_gqa_k, cache_gqa_v,
           c, c_ctx, norm1_g, norm2_g, final_norm_g, w_mod, b_mod, w_in, w_out, gla_gate_w, gla_gate_b,
           mla_q_norm_g, mla_w_q_up, mla_kv_norm_g, mla_w_kv_up, ret_decay, gqa_sink,
           ffn_w_gate, ffn_w_up, ffn_w_down, moe_router, moe_w_gate, moe_w_up, moe_w_down):
    xs = (x_prompt.reshape(M_CTX, D), x_sample.reshape(M_LAT, D))

    cond = jnp.concatenate([c_ctx[None], c, jnp.zeros((8 - 1 - N_LAT, D), F32)], axis=0)
    mod = _modulation(cond, w_mod, b_mod)
    mod = mod[:, :1 + N_LAT].reshape(DEPTH, (1 + N_LAT) * 6, 1, D)

    w_out_b = w_out.astype(BF16)
    wq = mla_w_q_up.reshape(DEPTH, MLA_QR, MLA_H, MLA_NOPE + MLA_ROPE)
    wq = jnp.concatenate([wq[..., :MLA_NOPE].reshape(DEPTH, MLA_QR, MLA_H * MLA_NOPE),
                          wq[..., MLA_NOPE:].reshape(DEPTH, MLA_QR, MLA_H * MLA_ROPE)], axis=-1).astype(BF16)
    wkv = mla_w_kv_up.reshape(DEPTH, MLA_KVR, MLA_H, MLA_NOPE + MLA_DV)
    wkv = jnp.concatenate([wkv[..., :MLA_NOPE].reshape(DEPTH, MLA_KVR, MLA_H * MLA_NOPE),
                           wkv[..., MLA_NOPE:].reshape(DEPTH, MLA_KVR, MLA_H * MLA_DV)], axis=-1).astype(BF16)
    router = jnp.pad(moe_router, ((0, 0), (0, 0), (0, LANE - N_EXP))).astype(BF16)
    dec_lanes = jnp.repeat(ret_decay, RET_DV, axis=-1)
    sink_lanes = jnp.repeat(gqa_sink, GQA_HD, axis=-1).reshape(DEPTH, 1, 256)
    gate_b = gla_gate_b.reshape(DEPTH, 2, 1, GLA_H * GLA_DK)
    g1, g2 = norm1_g.reshape(DEPTH, 1, D), norm2_g.reshape(DEPTH, 1, D)
    gq, gkv = mla_q_norm_g.reshape(DEPTH, 1, MLA_QR), mla_kv_norm_g.reshape(DEPTH, 1, MLA_KVR)
    eye = jnp.eye(GLA_H, dtype=F32)
    s0_gla = jnp.einsum('bldhkv,hg->bldhvgk', state_gla, eye).reshape(
        N_LAT, DEPTH, 2, GLA_H * GLA_DV, GLA_H * GLA_DK)
    cache_kr = jnp.pad(cache_mla_krope, ((0, 0), (0, 0), (0, 0), (0, 128 - MLA_ROPE)))

    rope64 = _rope_tables(T_LAT, 64, 256)
    rope32_q = _rope_tables(T_LAT, 32, 128)
    ck, sl, sh = _rope_tables(T_LAT, 32, 128)
    live = jnp.asarray((np.arange(128) < MLA_ROPE).astype(np.float32))[None]
    rope32_k = (ck * live, sl * live, sh * live)

    st_gla = st_ret = caches_mla = caches_gqa = None
    for l in range(DEPTH):
        p_gla, p_mla, p_ret, p_gqa = _inproj(xs, g1, mod, w_in, l)

        o_gla_c, st_gla, o_mla_c, ckv_all, kr_all, o_ret_c, st_ret, o_gqa_c, gk_all, gv_all = _fused_call([
            _gla(p_gla, gla_gate_w, gate_b, None, st_gla, l, False),
            _mla(p_mla, gq, gkv, wq, wkv, None, None, None, None, caches_mla, l, False),
            _ret(p_ret, dec_lanes, None, None, st_ret, l, False),
            _gqa(p_gqa, sink_lanes, None, None, None, caches_gqa, l, False)], N_CTX, "mixers_context")
        caches_mla, caches_gqa = (ckv_all, kr_all), (gk_all, gv_all)
        o_gla_s, o_ret_s = _fused_call([
            _gla(p_gla, gla_gate_w, gate_b, s0_gla, None, l, True),
            _ret(p_ret, dec_lanes, rope64, state_ret, None, l, True)], N_LAT, "mixers_latent_scan")
        o_mla_s, o_gqa_s = _fused_call([
            _mla(p_mla, gq, gkv, wq, wkv, rope32_q, rope32_k, cache_mla_ckv, cache_kr, None, l, True),
            _gqa(p_gqa, sink_lanes, rope64, cache_gqa_k, cache_gqa_v, None, l, True)], N_LAT, "mixers_latent_attn")
        o_ctx = (o_gla_c, o_mla_c, o_ret_c, o_gqa_c)
        o_lat = (o_gla_s, o_mla_s, o_ret_s, o_gqa_s)
        if l % 2 == 0:
            x, h2 = _outproj(xs, o_ctx, o_lat, w_out_b, g2, mod, None, l)
            x = _ffn(h2, x, mod, ffn_w_gate, ffn_w_up, ffn_w_down, l)
        else:
            x, h2, gates, sel = _outproj(xs, o_ctx, o_lat, w_out_b, g2, mod, router, l)
            x = _moe(h2, x, mod, gates, sel, moe_w_gate, moe_w_up, moe_w_down, l)
        xs = (x,)

    y_ctx, y_lat = _final_norm(x, final_norm_g[None])
    return (y_ctx.reshape(N_CTX, T_CTX, D), y_lat.reshape(N_LAT, T_LAT, D), jnp.swapaxes(st_gla, -1, -2), st_ret,
            *caches_mla, *caches_gqa)
```

```python
import functools

import numpy as np
import jax
import jax.numpy as jnp
from jax import lax
from jax.experimental import pallas as pl
from jax.experimental.pallas import tpu as pltpu

F32 = jnp.float32
BF16 = jnp.bfloat16
HIGHEST = lax.Precision.HIGHEST

D = 1024
N_CTX, T_CTX = 16, 256
N_LAT, T_LAT = 2, 1024
PAST = 256
DEPTH = 4
M_CTX = N_CTX * T_CTX
M_LAT = N_LAT * T_LAT
M_ALL = M_CTX + M_LAT
GRID_W = 64
ROPE_BASE = 10000.0
EPS = 1e-6

GLA_H, GLA_DK, GLA_DV, GLA_RANK, GLA_NORM, GLA_C = 4, 32, 64, 16, 16.0, 64
MLA_H, MLA_QR, MLA_KVR, MLA_NOPE, MLA_ROPE, MLA_DV = 4, 256, 128, 64, 32, 64
RET_H, RET_DK, RET_DV = 4, 64, 64
GQA_H, GQA_KV, GQA_HD, WINDOW = 4, 2, 64, 128
D_FF, N_EXP, D_FFE = 2816, 8, 3584

W_GLA, W_MLA, W_RET, W_GQA = 896, 512, 1024, 512
IN_WIDTH = 2752
IN_GROUPS = ((0, 800), (800, 1216), (1216, 2240), (2240, 2752))
LANE = 128
NEG = -1e30
QB = 256
VMEM_LIMIT = 56 * 1024 * 1024


def _cparams(*sem):
    return pltpu.CompilerParams(dimension_semantics=sem, vmem_limit_bytes=VMEM_LIMIT)


def _mm(a, b):
    return jnp.dot(a.astype(BF16), b.astype(BF16), preferred_element_type=F32)


def _mm_nt(a, b):
    return lax.dot_general(a.astype(BF16), b.astype(BF16), (((1,), (1,)), ((), ())), preferred_element_type=F32)


def _mm_tn(a, b):
    return lax.dot_general(a.astype(BF16), b.astype(BF16), (((0,), (0,)), ((), ())), preferred_element_type=F32)


def _mm_f32(a, b):
    return jnp.dot(a, b, precision=HIGHEST, preferred_element_type=F32)


def _silu(x):
    return x * (1.0 / (1.0 + jnp.exp(-x)))


def _log_sigmoid(x):
    return jnp.minimum(x, 0.0) - jnp.log1p(jnp.exp(-jnp.abs(x)))


def _rms(x):
    return x * lax.rsqrt(jnp.mean(x * x, axis=-1, keepdims=True) + EPS)


def _mod_row(tile, tm):
    return jnp.maximum((tile * tm) // T_LAT - (M_CTX // T_LAT - 1), 0)


def _mod_kernel(c_ref, w_ref, b_ref, o_ref):
    o_ref[...] = _mm(_silu(c_ref[...]), w_ref[...]) + b_ref[...]


def _modulation(cond, w_mod, b_mod):
    tn = 1536
    return pl.pallas_call(
        _mod_kernel,
        out_shape=jax.ShapeDtypeStruct((DEPTH, 8, 6 * D), F32),
        grid=(DEPTH, 6 * D // tn),
        in_specs=[pl.BlockSpec((8, D), lambda l, j: (0, 0)),
                  pl.BlockSpec((None, D, tn), lambda l, j: (l, 0, j)),
                  pl.BlockSpec((None, 1, tn), lambda l, j: (l, 0, j))],
        out_specs=pl.BlockSpec((None, 8, tn), lambda l, j: (l, 0, j)),
        compiler_params=_cparams("parallel", "parallel"),
        name="modulation",
    )(cond, w_mod, b_mod.reshape(DEPTH, 1, 6 * D))


def _inproj_kernel(split, tm, *refs):
    if split:
        xc_ref, xl_ref = refs[0:2]
        x = jnp.where(pl.program_id(0) >= M_CTX // tm, xl_ref[...], xc_ref[...])
        refs = refs[2:]
    else:
        x = refs[0][...]
        refs = refs[1:]
    g_ref, sh_ref, sc_ref, w_ref = refs[0:4]
    out_refs, w_s = refs[4:-1], refs[-1]

    @pl.when(pl.program_id(0) == 0)
    def _():
        dst = 0
        for (lo, hi), o_ref in zip(IN_GROUPS, out_refs):
            width = o_ref.shape[-1]
            w_s[:, dst:dst + hi - lo] = w_ref[:, lo:hi].astype(BF16)
            if width > hi - lo:
                w_s[:, dst + hi - lo:dst + width] = jnp.zeros((D, width - (hi - lo)), BF16)
            dst += width

    h = (_rms(x) * g_ref[...] * (1.0 + sc_ref[...]) + sh_ref[...]).astype(BF16)
    col = 0
    for o_ref in out_refs:
        width = o_ref.shape[-1]
        o_ref[...] = jnp.dot(h, w_s[:, col:col + width], preferred_element_type=F32)
        col += width


def _mod_spec(l, j, tm):
    return pl.BlockSpec((None, None, 1, D), lambda i, *_: (l, _mod_row(i, tm) * 6 + j, 0, 0))


def _layer_spec(l, *shape):
    return pl.BlockSpec((None,) + shape, lambda *_: (l,) + (0,) * len(shape))


def _stacked_output(prev, tail, l, nseq, in_specs, args, out_shape, out_specs, aliases):
    out_shape.append(jax.ShapeDtypeStruct((nseq, DEPTH) + tail, F32))
    out_specs.append(pl.BlockSpec((None, None) + tail, lambda b: (b, l) + (0,) * len(tail)))
    if prev is not None:
        in_specs.append(pl.BlockSpec(memory_space=pl.ANY))
        args.append(prev)
        aliases[len(args) - 1] = len(out_shape) - 1


def _row_split_specs(tm, width):
    n_ctx = M_CTX // tm
    return [pl.BlockSpec((tm, width), lambda i: (jnp.minimum(i, n_ctx - 1), 0)),
            pl.BlockSpec((tm, width), lambda i: (jnp.maximum(i - n_ctx, 0), 0))]


def _inproj(xs, g1, mod, w_in, l):
    tm = 512
    split = len(xs) == 2
    rows = lambda w: pl.BlockSpec((tm, w), lambda i: (i, 0))
    widths = (W_GLA, W_MLA, W_RET, W_GQA)
    x_specs = _row_split_specs(tm, D) if split else [rows(D)]
    w_spec = pl.BlockSpec((None, D, IN_WIDTH), lambda i: (l, 0, 0), pipeline_mode=pl.Buffered(1))
    return pl.pallas_call(
        functools.partial(_inproj_kernel, split, tm),
        out_shape=[jax.ShapeDtypeStruct((M_ALL, w), F32) for w in widths],
        grid=(M_ALL // tm,),
        in_specs=x_specs + [_layer_spec(l, 1, D), _mod_spec(l, 0, tm), _mod_spec(l, 1, tm), w_spec],
        out_specs=[rows(w) for w in widths],
        scratch_shapes=[pltpu.VMEM((D, sum(widths)), BF16)],
        compiler_params=_cparams("arbitrary"),
        name="inproj",
    )(*xs, g1, mod, mod, w_in)


def _head_rms_gate(o, gate):
    r = lax.broadcasted_iota(jnp.int32, (256, 256), 0) // 64
    c = lax.broadcasted_iota(jnp.int32, (256, 256), 1) // 64
    group_mean = jnp.where(r == c, 1.0 / 64.0, 0.0).astype(BF16)
    sq = o * o
    sq_hi = sq.astype(BF16)
    sq_lo = (sq - sq_hi.astype(F32)).astype(BF16)
    ms = (jnp.dot(sq_hi, group_mean, preferred_element_type=F32)
          + jnp.dot(sq_lo, group_mean, preferred_element_type=F32))
    return o * lax.rsqrt(ms + EPS) * _silu(gate)


def _own_lanes(n_heads, rows_per_head, lanes_per_head):
    shape = (n_heads * rows_per_head, n_heads * lanes_per_head)
    return (lax.broadcasted_iota(jnp.int32, shape, 0) // rows_per_head
            == lax.broadcasted_iota(jnp.int32, shape, 1) // lanes_per_head).astype(F32)


def _stack_heads(x, n_heads, lanes_per_head):
    return jnp.concatenate([x] * n_heads, axis=0) * _own_lanes(n_heads, x.shape[0], lanes_per_head)


def _unstack_heads(y, n_heads, lanes_per_head):
    t = y.shape[0] // n_heads
    y = y * _own_lanes(n_heads, t, lanes_per_head)
    out = y[0:t]
    for h in range(1, n_heads):
        out = out + y[h * t:(h + 1) * t]
    return out


def _rope(x, cos, sin_lo, sin_hi, half):
    w = x.shape[-1]
    return x * cos + pltpu.roll(x, w - half, 1) * sin_lo + pltpu.roll(x, half, 1) * sin_hi


def _rope_tables(t, head_dim, width):
    half = head_dim // 2
    quarter = head_dim // 4
    pos = np.arange(t)
    rows = (pos // GRID_W).astype(np.float32)
    cols = (pos % GRID_W).astype(np.float32)
    inv = np.power(np.float32(ROPE_BASE), -np.arange(quarter, dtype=np.float32) / np.float32(quarter)).astype(np.float32)
    ang = np.concatenate([rows[:, None] * inv, cols[:, None] * inv], axis=-1).astype(np.float32)
    lane = np.arange(width)
    a = ang[:, lane % half]
    cos, sin = np.cos(a).astype(np.float32), np.sin(a).astype(np.float32)
    low = (lane % head_dim) < half
    return (jnp.asarray(cos), jnp.asarray(np.where(low[None], -sin, 0.0).astype(np.float32)),
            jnp.asarray(np.where(low[None], 0.0, sin).astype(np.float32)))


def _seq_spec(t, width, row0, latent):
    return pl.BlockSpec((t, width), lambda b: (row0 + b, 0), pipeline_mode=pl.Buffered(1) if latent else None)


def _table_spec(t, width):
    return pl.BlockSpec((t, width), lambda b: (0, 0), pipeline_mode=pl.Buffered(1))


def _fused_call(parts, nseq, name):
    in_specs, args, out_shape, out_specs, scratch, aliases, layout = [], [], [], [], [], {}, []
    for fn, p_in, p_args, p_shape, p_out, p_scratch, p_alias in parts:
        layout.append((fn, len(p_args), len(p_shape), len(p_scratch)))
        for k, v in p_alias.items():
            aliases[len(args) + k] = len(out_shape) + v
        in_specs += p_in
        args += p_args
        out_shape += p_shape
        out_specs += p_out
        scratch += p_scratch
    n_in, n_out = len(args), len(out_shape)

    def kernel(*refs):
        i, o, s = 0, n_in, n_in + n_out
        for fn, ni, no, ns in layout:
            fn(*refs[i:i + ni], *refs[o:o + no], *refs[s:s + ns])
            i, o, s = i + ni, o + no, s + ns

    return pl.pallas_call(
        kernel, out_shape=out_shape, grid=(nseq,), in_specs=in_specs, out_specs=out_specs, scratch_shapes=scratch,
        input_output_aliases=aliases, name=name,
        compiler_params=pltpu.CompilerParams(dimension_semantics=("arbitrary",), vmem_limit_bytes=60 * 1024 * 1024),
    )(*args)


def _gla_kernel(t, latent, *refs):
    if latent:
        p_ref, gw_ref, gb_ref, s0_ref, o_ref, la_f, la_b, of_s, ob_s, st_s = refs
    else:
        p_ref, gw_ref, gb_ref = refs[0:3]
        o_ref, st_ref, la_f, la_b, of_s, ob_s, st_s = refs[-7:]
    c = GLA_C
    n = t // c
    hd = GLA_H * GLA_DK
    la_f[...] = _log_sigmoid(_mm_f32(p_ref[:, 768:784], gw_ref[0]) + gb_ref[0]) / GLA_NORM
    la_b[...] = _log_sigmoid(_mm_f32(p_ref[:, 784:800], gw_ref[1]) + gb_ref[1]) / GLA_NORM

    ri = lax.broadcasted_iota(jnp.int32, (GLA_H * c, c), 0) % c
    ci = lax.broadcasted_iota(jnp.int32, (GLA_H * c, c), 1)
    tri_r = lax.broadcasted_iota(jnp.int32, (c, c), 0)
    tri_c = lax.broadcasted_iota(jnp.int32, (c, c), 1)
    head_rows = lax.broadcasted_iota(jnp.int32, (GLA_H * c, hd), 0) // c
    own_dk = (head_rows == lax.broadcasted_iota(jnp.int32, (GLA_H * c, hd), 1) // GLA_DK).astype(F32)
    own_dv = (lax.broadcasted_iota(jnp.int32, (GLA_H * c, GLA_H * GLA_DV), 0) // c
              == lax.broadcasted_iota(jnp.int32, (GLA_H * c, GLA_H * GLA_DV), 1) // GLA_DV).astype(F32)
    scale = GLA_DK ** -0.5
    if latent:
        st_s[...] = s0_ref[...]
    else:
        st_s[...] = jnp.zeros_like(st_s)

    def chunk(row0, la_ref, d):
        fwd = d == 0
        keep = (ci <= ri) if fwd else (ci >= ri)
        tri = ((tri_c <= tri_r) if fwd else (tri_c >= tri_r)).astype(BF16)
        q = p_ref[pl.ds(row0, c), 0:128]
        k = p_ref[pl.ds(row0, c), 128:256] * scale
        v = p_ref[pl.ds(row0, c), 256:512]
        la = la_ref[pl.ds(row0, c), :]
        la_hi = la.astype(BF16)
        la_lo = (la - la_hi.astype(F32)).astype(BF16)
        bc = (jnp.dot(tri, la_hi, preferred_element_type=F32) + jnp.dot(tri, la_lo, preferred_element_type=F32))
        tot = bc[c - 1:c, :] if fwd else bc[0:1, :]
        mid = bc[c // 2 - 1:c // 2, :] if fwd else bc[c // 2:c // 2 + 1, :]
        qe, ke = q * jnp.exp(bc - mid), k * jnp.exp(mid - bc)
        q_in, k_out, a = q * jnp.exp(bc), k * jnp.exp(tot - bc), jnp.exp(tot)
        q_rows = jnp.concatenate([qe] * GLA_H, axis=0) * own_dk
        att = jnp.where(keep, _mm_nt(q_rows, ke), 0.0)
        o_all = _mm(att, v) * own_dv
        o = o_all[0:c] + o_all[c:2 * c] + o_all[2 * c:3 * c] + o_all[3 * c:4 * c]
        st = st_s[d]
        o = o + _mm_nt(q_in, st)
        st_s[d] = st * a + _mm_tn(v, k_out) * own_dk
        return o

    unroll = 4

    def body(i, carry):
        for u in range(unroll):
            rf = pl.multiple_of((i * unroll + u) * c, c)
            rb = pl.multiple_of((n - 1 - i * unroll - u) * c, c)
            of_s[pl.ds(rf, c), :] = chunk(rf, la_f, 0)
            ob_s[pl.ds(rb, c), :] = chunk(rb, la_b, 1)
        return carry

    lax.fori_loop(0, n // unroll, body, 0)
    if not latent:
        for d in range(2):
            for h in range(GLA_H):
                st_ref[d, h] = st_s[d, h * GLA_DV:(h + 1) * GLA_DV, h * GLA_DK:(h + 1) * GLA_DK]
    for r in range(t // QB):
        rows = slice(r * QB, (r + 1) * QB)
        o_ref[rows, :] = _head_rms_gate(of_s[rows, :] + ob_s[rows, :], p_ref[rows, 512:768]).astype(BF16)


def _gla(p, gate_w, gate_b, s0_bd, st_prev, l, latent):
    t, nseq, row0 = (T_LAT, N_LAT, M_CTX // T_LAT) if latent else (T_CTX, N_CTX, 0)
    st_shape = (2, GLA_H * GLA_DV, GLA_H * GLA_DK)
    in_specs = [_seq_spec(t, W_GLA, row0, latent),
                _layer_spec(l, 2, GLA_RANK, GLA_H * GLA_DK), _layer_spec(l, 2, 1, GLA_H * GLA_DK)]
    args = [p, gate_w, gate_b]
    aliases = {}
    o_shape = jax.ShapeDtypeStruct((nseq * t, 256), BF16)
    o_spec = pl.BlockSpec((t, 256), lambda b: (b, 0))
    if latent:
        in_specs.append(pl.BlockSpec((None, None) + st_shape, lambda b: (b, l, 0, 0, 0)))
        args.append(s0_bd)
    out_shape, out_specs = [o_shape], [o_spec]
    if not latent:
        _stacked_output(st_prev, (2, GLA_H, GLA_DV, GLA_DK), l, nseq, in_specs, args, out_shape, out_specs, aliases)
    scratch = [pltpu.VMEM((t, 128), F32), pltpu.VMEM((t, 128), F32),
               pltpu.VMEM((t, 256), F32), pltpu.VMEM((t, 256), F32), pltpu.VMEM(st_shape, F32)]
    return functools.partial(_gla_kernel, t, latent), in_specs, args, out_shape, out_specs, scratch, aliases


def _ret_kernel(t, latent, *refs):
    if latent:
        p_ref, dec_ref, cos_ref, slo_ref, shi_ref, s0_ref, o_ref, q_s, k_s, decay_s = refs
    else:
        p_ref, dec_ref = refs[0:2]
        o_ref, st_ref, decay_s = refs[-3:]
    lg = _log_sigmoid(dec_ref[...])
    scale = RET_DK ** -0.5
    nblk = t // QB

    @pl.when(pl.program_id(0) == 0)
    def _():
        wide = decay_s.shape[-1]
        dist = (lax.broadcasted_iota(jnp.int32, (QB, wide), 0) - lax.broadcasted_iota(jnp.int32, (QB, wide), 1)
                + (nblk - 1) * QB).astype(F32)
        for h in range(RET_H):
            lg_f, lg_b = lg[0:1, h * 64:h * 64 + 1], lg[1:2, h * 64:h * 64 + 1]
            decay_s[h] = (jnp.where(dist >= 0, jnp.exp(jnp.maximum(dist, 0.0) * lg_f), 0.0)
                          + jnp.where(dist <= 0, jnp.exp(jnp.maximum(-dist, 0.0) * lg_b), 0.0))

    if not latent:
        k = p_ref[:, 256:512] * scale
        v = p_ref[:, 512:768]
        decay = jnp.concatenate([decay_s[h] for h in range(RET_H)], axis=0)
        o = _unstack_heads(_mm(_mm_nt(_stack_heads(p_ref[:, 0:256], RET_H, RET_DK), k) * decay, v), RET_H, RET_DV)
        o_ref[...] = _head_rms_gate(o, p_ref[:, 768:1024]).astype(BF16)
        j = lax.broadcasted_iota(jnp.int32, (t, 1), 0).astype(F32)
        kv_f = _mm_tn(k * jnp.exp((float(t - 1) - j) * lg[0:1, :]), v)
        kv_b = _mm_tn(k * jnp.exp(j * lg[1:2, :]), v)
        for h in range(RET_H):
            st_ref[0, h] = kv_f[h * RET_DK:(h + 1) * RET_DK, h * RET_DV:(h + 1) * RET_DV]
            st_ref[1, h] = kv_b[h * RET_DK:(h + 1) * RET_DK, h * RET_DV:(h + 1) * RET_DV]
        return
    q_s[...] = _rope(p_ref[:, 0:256], cos_ref[...], slo_ref[...], shi_ref[...], RET_DK // 2)
    k_s[...] = _rope(p_ref[:, 256:512], cos_ref[...], slo_ref[...], shi_ref[...], RET_DK // 2) * scale
    for r in range(nblk):
        rows = slice(r * QB, (r + 1) * QB)
        pos = (lax.broadcasted_iota(jnp.int32, (QB, 1), 0) + r * QB).astype(F32)
        outs = []
        for h in range(RET_H):
            sl = slice(h * RET_DK, (h + 1) * RET_DK)
            lg_f, lg_b = lg[0:1, h * 64:h * 64 + 1], lg[1:2, h * 64:h * 64 + 1]
            decay = decay_s[h, :, (nblk - 1 - r) * QB:(nblk - 1 - r) * QB + t]
            q = q_s[rows, sl]
            v = p_ref[:, 512 + h * RET_DV:512 + (h + 1) * RET_DV]
            o = _mm(_mm_nt(q, k_s[:, sl]) * decay, v)
            o = o + _mm(q * jnp.exp((pos + 1.0) * lg_f), s0_ref[0, h])
            o = o + _mm(q * jnp.exp((float(t) - pos) * lg_b), s0_ref[1, h])
            outs.append(o)
        o_ref[rows, :] = _head_rms_gate(jnp.concatenate(outs, axis=-1), p_ref[rows, 768:1024]).astype(BF16)


def _ret(p, dec_lanes, tables, s0, st_prev, l, latent):
    t, nseq, row0 = (T_LAT, N_LAT, M_CTX // T_LAT) if latent else (T_CTX, N_CTX, 0)
    st_shape = (2, RET_H, RET_DK, RET_DV)
    in_specs = [_seq_spec(t, W_RET, row0, latent), _layer_spec(l, 2, 256)]
    args = [p, dec_lanes]
    aliases = {}
    o_shape = jax.ShapeDtypeStruct((nseq * t, 256), BF16)
    o_spec = pl.BlockSpec((t, 256), lambda b: (b, 0))
    scratch = []
    if latent:
        in_specs += [_table_spec(t, 256)] * 3
        in_specs.append(pl.BlockSpec((None, None) + st_shape, lambda b: (b, l, 0, 0, 0, 0)))
        args += list(tables) + [s0]
        scratch = [pltpu.VMEM((t, 256), F32), pltpu.VMEM((t, 256), F32)]
    scratch.append(pltpu.VMEM((RET_H, QB, 2 * t - QB), F32))
    out_shape, out_specs = [o_shape], [o_spec]
    if not latent:
        _stacked_output(st_prev, st_shape, l, nseq, in_specs, args, out_shape, out_specs, aliases)
    return functools.partial(_ret_kernel, t, latent), in_specs, args, out_shape, out_specs, scratch, aliases


def _mla_kernel(t, latent, *refs):
    if latent:
        (p_ref, gq_ref, gkv_ref, wq_ref, wkv_ref, cq_ref, slq_ref, shq_ref, ck_ref, slk_ref, shk_ref,
         cckv_ref, ckr_ref, o_ref, qn_s, qr_s, kn_s, kr_s, v_s) = refs
    else:
        p_ref, gq_ref, gkv_ref, wq_ref, wkv_ref = refs[0:5]
        o_ref, ckv_ref, kro_ref, qn_s, qr_s, kn_s, kr_s, v_s = refs[-8:]
    nk = t + (PAST if latent else 0)
    qh = _mm(_rms(p_ref[:, 0:256]) * gq_ref[...], wq_ref[...])
    ckv = _rms(p_ref[:, 256:384]) * gkv_ref[...]
    kv = _mm(ckv, wkv_ref[...])
    qn_s[...] = qh[:, 0:256]
    kn_s[0:t, :] = kv[:, 0:256]
    v_s[0:t, :] = kv[:, 256:512]
    if latent:
        qr_s[...] = _rope(qh[:, 256:384], cq_ref[...], slq_ref[...], shq_ref[...], MLA_ROPE // 2)
        kr_s[0:t, :] = _rope(p_ref[:, 384:512], ck_ref[...], slk_ref[...], shk_ref[...], MLA_ROPE // 2)
        kvc = _mm(cckv_ref[...], wkv_ref[...])
        kn_s[t:nk, :] = kvc[:, 0:256]
        v_s[t:nk, :] = kvc[:, 256:512]
        kr_s[t:nk, :] = ckr_ref[...]
    scale = (MLA_NOPE + MLA_ROPE) ** -0.5
    if not latent:
        ckv_ref[...] = ckv
        kro_ref[...] = p_ref[:, 384:384 + MLA_ROPE]
        kr = p_ref[:, 384:512]
        kr4 = kr + pltpu.roll(kr, MLA_ROPE, 1) + pltpu.roll(kr, 2 * MLA_ROPE, 1) + pltpu.roll(kr, 3 * MLA_ROPE, 1)
        s = (_mm_nt(_stack_heads(qh[:, 0:256], MLA_H, MLA_NOPE), kv[:, 0:256])
             + _mm_nt(_stack_heads(qh[:, 256:384], MLA_H, MLA_ROPE), kr4)) * scale
        e = jnp.exp(s - jnp.max(s, axis=-1, keepdims=True))
        o = _mm(e, kv[:, 256:512]) / jnp.sum(e, axis=-1, keepdims=True)
        o_ref[...] = _unstack_heads(o, MLA_H, MLA_DV).astype(BF16)
        return
    for r in range(t // QB):
        rows = slice(r * QB, (r + 1) * QB)
        outs = []
        for h in range(MLA_H):
            s = (_mm_nt(qn_s[rows, h * 64:(h + 1) * 64], kn_s[:, h * 64:(h + 1) * 64])
                 + _mm_nt(qr_s[rows, h * 32:(h + 1) * 32], kr_s[:, 0:MLA_ROPE])) * scale
            e = jnp.exp(s - jnp.max(s, axis=-1, keepdims=True))
            outs.append(_mm(e, v_s[:, h * 64:(h + 1) * 64]) / jnp.sum(e, axis=-1, keepdims=True))
        o_ref[rows, :] = jnp.concatenate(outs, axis=-1).astype(BF16)


def _mla(p, gq, gkv, wq, wkv, tables_q, tables_k, cache_ckv, cache_kr, prev, l, latent):
    t, nseq, row0 = (T_LAT, N_LAT, M_CTX // T_LAT) if latent else (T_CTX, N_CTX, 0)
    nk = t + (PAST if latent else 0)
    in_specs = [_seq_spec(t, W_MLA, row0, latent), _layer_spec(l, 1, MLA_QR),
                _layer_spec(l, 1, MLA_KVR), _layer_spec(l, MLA_QR, 384), _layer_spec(l, MLA_KVR, 512)]
    args = [p, gq, gkv, wq, wkv]
    aliases = {}
    o_shape = jax.ShapeDtypeStruct((nseq * t, 256), BF16)
    o_spec = pl.BlockSpec((t, 256), lambda b: (b, 0))
    if latent:
        in_specs += [_table_spec(t, 128)] * 6
        in_specs += [pl.BlockSpec((None, None, PAST, MLA_KVR), lambda b: (b, l, 0, 0)),
                     pl.BlockSpec((None, None, PAST, 128), lambda b: (b, l, 0, 0))]
        args += list(tables_q) + list(tables_k) + [cache_ckv, cache_kr]
    out_shape, out_specs = [o_shape], [o_spec]
    if not latent:
        prev = prev or (None, None)
        _stacked_output(prev[0], (t, MLA_KVR), l, nseq, in_specs, args, out_shape, out_specs, aliases)
        _stacked_output(prev[1], (t, MLA_ROPE), l, nseq, in_specs, args, out_shape, out_specs, aliases)
    scratch = [pltpu.VMEM((t, 256), F32), pltpu.VMEM((t, 128), F32), pltpu.VMEM((nk, 256), F32),
               pltpu.VMEM((nk, 128), F32), pltpu.VMEM((nk, 256), F32)]
    return functools.partial(_mla_kernel, t, latent), in_specs, args, out_shape, out_specs, scratch, aliases


def _gqa_kernel(t, latent, *refs):
    if latent:
        p_ref, sink_ref, cos_ref, slo_ref, shi_ref, ck_ref, cv_ref, o_ref, q_s, k_s = refs
    else:
        p_ref, sink_ref = refs[0:2]
        o_ref, ko_ref, vo_ref = refs[-3:]
        for kvh in range(GQA_KV):
            ko_ref[kvh] = p_ref[:, 256 + kvh * GQA_HD:256 + (kvh + 1) * GQA_HD]
            vo_ref[kvh] = p_ref[:, 384 + kvh * GQA_HD:384 + (kvh + 1) * GQA_HD]
    scale = GQA_HD ** -0.5
    grp = GQA_H // GQA_KV
    if not latent:
        lane = lax.broadcasted_iota(jnp.int32, (t, GQA_KV * GQA_HD), 1)

        def per_query_head(x):
            swapped = pltpu.roll(x, GQA_HD, 1)
            return jnp.concatenate([jnp.where(lane < GQA_HD, x, swapped), jnp.where(lane < GQA_HD, swapped, x)],
                                   axis=1)

        own = _own_lanes(GQA_H, t, GQA_HD)
        sink = jnp.max(jnp.where(own > 0, sink_ref[...], NEG), axis=-1, keepdims=True)
        s = _mm_nt(_stack_heads(p_ref[:, 0:256], GQA_H, GQA_HD), per_query_head(p_ref[:, 256:384])) * scale
        m = jnp.maximum(jnp.max(s, axis=-1, keepdims=True), sink)
        e = jnp.exp(s - m)
        o = _mm(e, per_query_head(p_ref[:, 384:512])) / (jnp.sum(e, axis=-1, keepdims=True) + jnp.exp(sink - m))
        o_ref[...] = _unstack_heads(o, GQA_H, GQA_HD).astype(BF16)
        return
    if latent:
        q_s[...] = _rope(p_ref[:, 0:256], cos_ref[...], slo_ref[...], shi_ref[...], GQA_HD // 2)
        k_s[...] = _rope(p_ref[:, 256:384], cos_ref[:, 0:128], slo_ref[:, 0:128], shi_ref[:, 0:128], GQA_HD // 2)
    for r in range(t // QB):
        rows = slice(r * QB, (r + 1) * QB)
        keys = slice(max(0, r * QB - WINDOW), min(t, (r + 1) * QB + WINDOW)) if latent else slice(0, t)
        if latent:
            nkeys = keys.stop - keys.start
            row = lax.broadcasted_iota(jnp.int32, (QB, nkeys), 0) + r * QB
            col = lax.broadcasted_iota(jnp.int32, (QB, nkeys), 1) + keys.start
            near = jnp.abs(row - col) <= WINDOW
        outs = []
        for h in range(GQA_H):
            kvh = h // grp
            ksl = slice(kvh * GQA_HD, (kvh + 1) * GQA_HD)
            sk = sink_ref[0:1, h * 64:h * 64 + 1]
            v = p_ref[keys, 384 + kvh * GQA_HD:384 + (kvh + 1) * GQA_HD]
            if latent:
                q = q_s[rows, h * 64:(h + 1) * 64]
                s_loc = jnp.where(near, _mm_nt(q, k_s[keys, ksl]) * scale, NEG)
                s_ctx = _mm_nt(q, ck_ref[kvh]) * scale
                m = jnp.maximum(jnp.maximum(jnp.max(s_loc, axis=-1, keepdims=True),
                                            jnp.max(s_ctx, axis=-1, keepdims=True)), sk)
                e_loc, e_ctx = jnp.exp(s_loc - m), jnp.exp(s_ctx - m)
                den = jnp.sum(e_loc, axis=-1, keepdims=True) + jnp.sum(e_ctx, axis=-1, keepdims=True) + jnp.exp(sk - m)
                outs.append((_mm(e_loc, v) + _mm(e_ctx, cv_ref[kvh])) / den)
            else:
                q = p_ref[rows, h * 64:(h + 1) * 64]
                s = _mm_nt(q, p_ref[:, 256 + ksl.start:256 + ksl.stop]) * scale
                m = jnp.maximum(jnp.max(s, axis=-1, keepdims=True), sk)
                e = jnp.exp(s - m)
                outs.append(_mm(e, v) / (jnp.sum(e, axis=-1, keepdims=True) + jnp.exp(sk - m)))
        o_ref[rows, :] = jnp.concatenate(outs, axis=-1).astype(BF16)


def _gqa(p, sink_lanes, tables, cache_k, cache_v, prev, l, latent):
    t, nseq, row0 = (T_LAT, N_LAT, M_CTX // T_LAT) if latent else (T_CTX, N_CTX, 0)
    in_specs = [_seq_spec(t, W_GQA, row0, latent), _layer_spec(l, 1, 256)]
    args = [p, sink_lanes]
    scratch = []
    aliases = {}
    out_shape = [jax.ShapeDtypeStruct((nseq * t, 256), BF16)]
    out_specs = [pl.BlockSpec((t, 256), lambda b: (b, 0))]
    if latent:
        in_specs += [_table_spec(t, 256)] * 3
        in_specs += [pl.BlockSpec((None, None, GQA_KV, PAST, GQA_HD), lambda b: (b, l, 0, 0, 0))] * 2
        args += list(tables) + [cache_k, cache_v]
        scratch = [pltpu.VMEM((t, 256), F32), pltpu.VMEM((t, 128), F32)]
    else:
        prev = prev or (None, None)
        for pv in prev:
            _stacked_output(pv, (GQA_KV, t, GQA_HD), l, nseq, in_specs, args, out_shape, out_specs, aliases)
    return functools.partial(_gqa_kernel, t, latent), in_specs, args, out_shape, out_specs, scratch, aliases


def _outproj_kernel(route, split, tm, *refs):
    is_lat = pl.program_id(0) >= M_CTX // tm
    if split:
        x_in = jnp.where(is_lat, refs[1][...], refs[0][...])
        refs = refs[2:]
    else:
        x_in = refs[0][...]
        refs = refs[1:]
    ctx_refs, lat_refs = refs[0:4], refs[4:8]
    w_ref, gt_ref, g_ref, sh_ref, sc_ref = refs[8:13]
    refs = refs[13:]
    mix = jnp.zeros((tm, D), F32)
    for m in range(4):
        o = jnp.where(is_lat, lat_refs[m][...], ctx_refs[m][...])
        mix = mix + jnp.dot(o, w_ref[m * 256:(m + 1) * 256, :], preferred_element_type=F32)
    x = x_in + gt_ref[...] * mix
    h = (_rms(x) * g_ref[...] * (1.0 + sc_ref[...]) + sh_ref[...]).astype(BF16)
    if not route:
        xo_ref, h_ref = refs
    else:
        r_ref, xo_ref, h_ref, gate_ref, sel_ref = refs
        lane = lax.broadcasted_iota(jnp.int32, gate_ref.shape, 1)
        logits = jnp.where(lane < N_EXP, jnp.dot(h, r_ref[...], preferred_element_type=F32), NEG)
        m1 = jnp.max(logits, axis=-1, keepdims=True)
        i1 = jnp.min(jnp.where(logits == m1, lane, LANE), axis=-1, keepdims=True)
        rest = jnp.where(lane == i1, NEG, logits)
        m2 = jnp.max(rest, axis=-1, keepdims=True)
        i2 = jnp.min(jnp.where(rest == m2, lane, LANE), axis=-1, keepdims=True)
        e2 = jnp.exp(m2 - m1)
        gate_ref[...] = jnp.where(lane == i1, 1.0 / (1.0 + e2), 0.0) + jnp.where(lane == i2, e2 / (1.0 + e2), 0.0)
        sel_ref[...] = jnp.where((lane == i1) | (lane == i2), 1, 0)
    xo_ref[...] = x
    h_ref[...] = h


def _outproj(xs, o_ctx, o_lat, w_out, g2, mod, router, l):
    tm = 512
    route = router is not None
    split = len(xs) == 2
    rows = lambda w: pl.BlockSpec((tm, w), lambda i: (i, 0))
    ctx_spec, lat_spec = _row_split_specs(tm, 256)
    x_specs = _row_split_specs(tm, D) if split else [rows(D)]
    in_specs = x_specs + [ctx_spec] * 4 + [lat_spec] * 4 + [
        _layer_spec(l, D, D), _mod_spec(l, 2, tm), _layer_spec(l, 1, D), _mod_spec(l, 3, tm), _mod_spec(l, 4, tm)]
    args = [*xs, *o_ctx, *o_lat, w_out, mod, g2, mod, mod]
    out_shape = [jax.ShapeDtypeStruct((M_ALL, D), F32), jax.ShapeDtypeStruct((M_ALL, D), BF16)]
    out_specs = [rows(D), rows(D)]
    if route:
        in_specs.append(_layer_spec(l // 2, D, LANE))
        args.append(router)
        out_shape += [jax.ShapeDtypeStruct((M_ALL, LANE), F32), jax.ShapeDtypeStruct((M_ALL, LANE), jnp.int32)]
        out_specs += [rows(LANE), rows(LANE)]
    return pl.pallas_call(
        functools.partial(_outproj_kernel, route, split, tm),
        out_shape=out_shape, grid=(M_ALL // tm,), in_specs=in_specs, out_specs=out_specs,
        compiler_params=_cparams("parallel"),
        name="outproj_route" if route else "outproj",
    )(*args)


def _ffn_kernel(h_ref, x_ref, gt_ref, wg_ref, wu_ref, wd_ref, o_ref, acc_ref):
    f = pl.program_id(1)

    @pl.when(f == 0)
    def _():
        acc_ref[...] = jnp.zeros_like(acc_ref)

    wg, wu, wd = wg_ref[...].astype(BF16), wu_ref[...].astype(BF16), wd_ref[...].astype(BF16)
    half = h_ref.shape[0] // 2
    for rows in (slice(0, half), slice(half, 2 * half)):
        h = h_ref[rows, :]
        g = jnp.dot(h, wg, preferred_element_type=F32)
        u = jnp.dot(h, wu, preferred_element_type=F32)
        acc_ref[rows, :] += jnp.dot((_silu(g) * u).astype(BF16), wd, preferred_element_type=F32)

    @pl.when(f == pl.num_programs(1) - 1)
    def _():
        o_ref[...] = x_ref[...] + gt_ref[...] * acc_ref[...]


def _ffn(h, x, mod, wg, wu, wd, l):
    tm, tf = 1024, 256
    j = l // 2
    return pl.pallas_call(
        _ffn_kernel,
        out_shape=jax.ShapeDtypeStruct((M_ALL, D), F32),
        grid=(M_ALL // tm, D_FF // tf),
        in_specs=[pl.BlockSpec((tm, D), lambda i, f: (i, 0)), pl.BlockSpec((tm, D), lambda i, f: (i, 0)),
                  _mod_spec(l, 5, tm),
                  pl.BlockSpec((None, D, tf), lambda i, f: (j, 0, f)),
                  pl.BlockSpec((None, D, tf), lambda i, f: (j, 0, f)),
                  pl.BlockSpec((None, tf, D), lambda i, f: (j, f, 0))],
        out_specs=pl.BlockSpec((tm, D), lambda i, f: (i, 0)),
        scratch_shapes=[pltpu.VMEM((tm, D), F32)],
        compiler_params=_cparams("parallel", "arbitrary"),
        name="ffn_dense",
    )(h, x, mod, wg, wu, wd)


SUP, SUB, CHUNK = 2048, 256, 256
N_SUB = SUP // SUB
S_MAX = 2 * M_ALL // SUP + N_EXP
P_SLOT = S_MAX * SUP
TC = 128
WIN = TC + 16


def _moe_expert_kernel(se_ref, nt_ref, sblk_ref, clo_ref, chi_ref, pos_ref, h_ref, wg_ref, wu_ref, wd_ref, o_ref,
                       xs_s, acc_s, wg_s, wu_s, wd_s):
    s = pl.program_id(0)
    f = pl.program_id(1)
    n = nt_ref[s]

    @pl.when((f == 0) & (n == 0))
    def _():
        o_ref[...] = jnp.zeros_like(o_ref)

    @pl.when((f == 0) & (n > 0))
    def _():
        row = lax.broadcasted_iota(jnp.int32, (SUB, CHUNK), 0)

        def gather(j, carry):
            row0 = (s * N_SUB + j) * SUB
            acc_s[j] = jnp.zeros((SUB, D), F32)

            def chunk(c, carry):
                tpos = pos_ref[:, pl.ds(pl.multiple_of(c * CHUNK, CHUNK), CHUNK)]
                onehot = jnp.where(tpos - row0 == row, 1.0, 0.0).astype(BF16)
                rows = h_ref[pl.ds(pl.multiple_of(c * CHUNK, CHUNK), CHUNK), :]
                acc_s[j] += jnp.dot(onehot, rows, preferred_element_type=F32)
                return carry

            g = s * N_SUB + j
            lax.fori_loop(clo_ref[g], chi_ref[g] + 1, chunk, 0)
            xs_s[j] = acc_s[j].astype(BF16)
            acc_s[j] = jnp.zeros((SUB, D), F32)
            return carry

        lax.fori_loop(0, n, gather, 0)

    @pl.when(n > 0)
    def _():
        wg_s[...] = wg_ref[...].astype(BF16)
        wu_s[...] = wu_ref[...].astype(BF16)
        wd_s[...] = wd_ref[...].astype(BF16)

        def sub(j):
            x = xs_s[j]
            g = jnp.dot(x, wg_s[...], preferred_element_type=F32)
            u = jnp.dot(x, wu_s[...], preferred_element_type=F32)
            acc_s[j] += jnp.dot((_silu(g) * u).astype(BF16), wd_s[...], preferred_element_type=F32)

        def pair(jj, carry):
            sub(2 * jj)
            sub(2 * jj + 1)
            return carry

        lax.fori_loop(0, n // 2, pair, 0)

        @pl.when(n % 2 == 1)
        def _():
            sub(n - 1)

    @pl.when((f == pl.num_programs(1) - 1) & (n > 0))
    def _():
        for j in range(N_SUB):
            rows = slice(j * SUB, (j + 1) * SUB)

            @pl.when(j < n)
            def _():
                o_ref[rows, :] = acc_s[j].astype(BF16)

            @pl.when(j >= n)
            def _():
                o_ref[rows, :] = jnp.zeros((SUB, D), BF16)


def _moe_experts(h, pos_t, meta, wg, wu, wd, layer):
    tf = 512
    nf = D_FFE // tf
    se, nt, sblk, clo, chi = meta

    def w_up(s, f, se, nt, sblk, clo, chi):
        return (layer, se[s], 0, jnp.where(nt[s] > 0, f, nf - 1))

    def w_down(s, f, se, nt, sblk, clo, chi):
        return (layer, se[s], jnp.where(nt[s] > 0, f, nf - 1), 0)

    grid_spec = pltpu.PrefetchScalarGridSpec(
        num_scalar_prefetch=5,
        grid=(S_MAX, nf),
        in_specs=[pl.BlockSpec((None, 1, M_ALL), lambda s, f, se, nt, sblk, clo, chi: (se[s], 0, 0)),
                  pl.BlockSpec((M_ALL, D), lambda s, f, *_: (0, 0), pipeline_mode=pl.Buffered(1)),
                  pl.BlockSpec((None, None, D, tf), w_up), pl.BlockSpec((None, None, D, tf), w_up),
                  pl.BlockSpec((None, None, tf, D), w_down)],
        out_specs=pl.BlockSpec((SUP, D), lambda s, f, *_: (s, 0)),
        scratch_shapes=[pltpu.VMEM((N_SUB, SUB, D), BF16), pltpu.VMEM((N_SUB, SUB, D), F32),
                        pltpu.VMEM((D, tf), BF16), pltpu.VMEM((D, tf), BF16), pltpu.VMEM((tf, D), BF16)],
    )
    return pl.pallas_call(
        _moe_expert_kernel,
        out_shape=jax.ShapeDtypeStruct((P_SLOT, D), BF16),
        grid_spec=grid_spec,
        compiler_params=pltpu.CompilerParams(dimension_semantics=("arbitrary", "arbitrary"),
                                             vmem_limit_bytes=60 * 1024 * 1024),
        name="moe_experts",
    )(se, nt, sblk, clo, chi, pos_t, h, wg, wu, wd)


def _moe_combine_kernel(off_ref, x_ref, gt_ref, gate_ref, pos_ref, *refs):
    win_refs, o_ref = refs[:N_EXP], refs[N_EXP]
    i = pl.program_id(0)
    lane = lax.broadcasted_iota(jnp.int32, (TC, WIN), 1)
    y = jnp.zeros((TC, D), F32)
    for e in range(N_EXP):
        rel = pos_ref[:, e:e + 1] - off_ref[i * N_EXP + e] * 16
        onehot = jnp.where(rel == lane, 1.0, 0.0).astype(BF16)
        y = y + gate_ref[:, e:e + 1] * jnp.dot(onehot, win_refs[e][...], preferred_element_type=F32)
    o_ref[...] = x_ref[...] + gt_ref[...] * y


def _moe_combine(x, mod, gates, pos, off, slots, l):
    def win_spec(e):
        return pl.BlockSpec((pl.Element(WIN), pl.Element(D)), lambda i, off: (off[i * N_EXP + e] * 16, 0))

    grid_spec = pltpu.PrefetchScalarGridSpec(
        num_scalar_prefetch=1,
        grid=(M_ALL // TC,),
        in_specs=[pl.BlockSpec((TC, D), lambda i, off: (i, 0)),
                  _mod_spec(l, 5, TC),
                  pl.BlockSpec((TC, LANE), lambda i, off: (i, 0)), pl.BlockSpec((TC, LANE), lambda i, off: (i, 0))]
        + [win_spec(e) for e in range(N_EXP)],
        out_specs=pl.BlockSpec((TC, D), lambda i, off: (i, 0)),
    )
    return pl.pallas_call(
        _moe_combine_kernel,
        out_shape=jax.ShapeDtypeStruct((M_ALL, D), F32),
        grid_spec=grid_spec,
        compiler_params=_cparams("arbitrary"),
        name="moe_combine",
    )(off, x, mod, gates, pos, *([slots] * N_EXP))


def _moe_plan(sel):
    i32 = jnp.int32
    sel_t = sel[:, :N_EXP].T
    csum = jnp.cumsum(sel_t, axis=1)
    rank = csum - sel_t
    n_e = csum[:, -1]
    ns_e = (n_e + SUP - 1) // SUP
    end_e = jnp.cumsum(ns_e)
    start_e = end_e - ns_e
    n_used = end_e[-1]
    pos_t = jnp.where(sel_t > 0, start_e[:, None] * SUP + rank, -1)
    s_ids = jnp.arange(S_MAX, dtype=i32)
    sblk = jnp.minimum(s_ids, n_used - 1)
    se = jnp.sum((end_e[None, :] <= sblk[:, None]).astype(i32), axis=1)
    nv = jnp.clip(n_e[se] - (sblk - start_e[se]) * SUP, 0, SUP)
    nt = jnp.where(s_ids < n_used, (nv + SUB - 1) // SUB, 0)
    g_ids = jnp.arange(S_MAX * N_SUB, dtype=i32)
    s_g, e_g = g_ids // N_SUB, se[g_ids // N_SUB]
    r0 = (sblk[s_g] - start_e[e_g]) * SUP + (g_ids % N_SUB) * SUB
    r1 = jnp.minimum(r0 + SUB, n_e[e_g])
    live = (s_g < n_used) & (r0 < n_e[e_g])
    cs_g = csum[e_g]
    t_first = jnp.sum((cs_g <= r0[:, None]).astype(i32), axis=1)
    t_last = jnp.sum((cs_g < r1[:, None]).astype(i32), axis=1)
    clo = jnp.where(live, t_first // CHUNK, 0)
    chi = jnp.where(live, t_last // CHUNK, -1)
    before = jnp.concatenate([jnp.zeros((N_EXP, 1), i32), csum[:, TC - 1::TC][:, :-1]], axis=1)
    base = start_e[:, None] * SUP + before
    off = jnp.minimum(base // 16, (P_SLOT - WIN) // 16).T.reshape(-1)
    pos128 = jnp.pad(pos_t.T, ((0, 0), (0, LANE - N_EXP)), constant_values=-1)
    meta = (se.astype(i32), nt.astype(i32), sblk.astype(i32), clo.astype(i32), chi.astype(i32))
    return meta, pos_t.reshape(N_EXP, 1, M_ALL), pos128, off.astype(i32)


def _moe(h, x, mod, gates, sel, wg, wu, wd, l):
    meta, pos_t, pos, off = _moe_plan(sel)
    slots = _moe_experts(h, pos_t, meta, wg, wu, wd, l // 2)
    return _moe_combine(x, mod, gates, pos, off, slots, l)


def _final_kernel(tm, x_ref, g_ref, oc_ref, ol_ref):
    y = _rms(x_ref[...]) * g_ref[...]
    is_lat = pl.program_id(0) >= M_CTX // tm

    @pl.when(jnp.logical_not(is_lat))
    def _():
        oc_ref[...] = y

    @pl.when(is_lat)
    def _():
        ol_ref[...] = y


def _final_norm(x, g):
    tm = 1024
    return pl.pallas_call(
        functools.partial(_final_kernel, tm),
        out_shape=[jax.ShapeDtypeStruct((M_CTX, D), F32), jax.ShapeDtypeStruct((M_LAT, D), F32)],
        grid=(M_ALL // tm,),
        in_specs=[pl.BlockSpec((tm, D), lambda i: (i, 0)), pl.BlockSpec((1, D), lambda i: (0, 0))],
        out_specs=_row_split_specs(tm, D),
        compiler_params=_cparams("arbitrary"),
        name="final_norm",
    )(x, g)


def kernel(x_prompt, x_sample, state_gla, state_ret, cache_mla_ckv, cache_mla_krope, cache_gqa_k, cache_gqa_v,
           c, c_ctx, norm1_g, norm2_g, final_norm_g, w_mod, b_mod, w_in, w_out, gla_gate_w, gla_gate_b,
           mla_q_norm_g, mla_w_q_up, mla_kv_norm_g, mla_w_kv_up, ret_decay, gqa_sink,
           ffn_w_gate, ffn_w_up, ffn_w_down, moe_router, moe_w_gate, moe_w_up, moe_w_down):
    xs = (x_prompt.reshape(M_CTX, D), x_sample.reshape(M_LAT, D))

    cond = jnp.concatenate([c_ctx[None], c, jnp.zeros((8 - 1 - N_LAT, D), F32)], axis=0)
    mod = _modulation(cond, w_mod, b_mod)
    mod = mod[:, :1 + N_LAT].reshape(DEPTH, (1 + N_LAT) * 6, 1, D)

    w_out_b = w_out.astype(BF16)
    wq = mla_w_q_up.reshape(DEPTH, MLA_QR, MLA_H, MLA_NOPE + MLA_ROPE)
    wq = jnp.concatenate([wq[..., :MLA_NOPE].reshape(DEPTH, MLA_QR, MLA_H * MLA_NOPE),
                          wq[..., MLA_NOPE:].reshape(DEPTH, MLA_QR, MLA_H * MLA_ROPE)], axis=-1).astype(BF16)
    wkv = mla_w_kv_up.reshape(DEPTH, MLA_KVR, MLA_H, MLA_NOPE + MLA_DV)
    wkv = jnp.concatenate([wkv[..., :MLA_NOPE].reshape(DEPTH, MLA_KVR, MLA_H * MLA_NOPE),
                           wkv[..., MLA_NOPE:].reshape(DEPTH, MLA_KVR, MLA_H * MLA_DV)], axis=-1).astype(BF16)
    router = jnp.pad(moe_router, ((0, 0), (0, 0), (0, LANE - N_EXP))).astype(BF16)
    dec_lanes = jnp.repeat(ret_decay, RET_DV, axis=-1)
    sink_lanes = jnp.repeat(gqa_sink, GQA_HD, axis=-1).reshape(DEPTH, 1, 256)
    gate_b = gla_gate_b.reshape(DEPTH, 2, 1, GLA_H * GLA_DK)
    g1, g2 = norm1_g.reshape(DEPTH, 1, D), norm2_g.reshape(DEPTH, 1, D)
    gq, gkv = mla_q_norm_g.reshape(DEPTH, 1, MLA_QR), mla_kv_norm_g.reshape(DEPTH, 1, MLA_KVR)
    eye = jnp.eye(GLA_H, dtype=F32)
    s0_gla = jnp.einsum('bldhkv,hg->bldhvgk', state_gla, eye).reshape(
        N_LAT, DEPTH, 2, GLA_H * GLA_DV, GLA_H * GLA_DK)
    cache_kr = jnp.pad(cache_mla_krope, ((0, 0), (0, 0), (0, 0), (0, 128 - MLA_ROPE)))

    rope64 = _rope_tables(T_LAT, 64, 256)
    rope32_q = _rope_tables(T_LAT, 32, 128)
    ck, sl, sh = _rope_tables(T_LAT, 32, 128)
    live = jnp.asarray((np.arange(128) < MLA_ROPE).astype(np.float32))[None]
    rope32_k = (ck * live, sl * live, sh * live)

    st_gla = st_ret = caches_mla = caches_gqa = None
    for l in range(DEPTH):
        p_gla, p_mla, p_ret, p_gqa = _inproj(xs, g1, mod, w_in, l)

        o_gla_c, st_gla, o_mla_c, ckv_all, kr_all, o_ret_c, st_ret, o_gqa_c, gk_all, gv_all = _fused_call([
            _gla(p_gla, gla_gate_w, gate_b, None, st_gla, l, False),
            _mla(p_mla, gq, gkv, wq, wkv, None, None, None, None, caches_mla, l, False),
            _ret(p_ret, dec_lanes, None, None, st_ret, l, False),
            _gqa(p_gqa, sink_lanes, None, None, None, caches_gqa, l, False)], N_CTX, "mixers_context")
        caches_mla, caches_gqa = (ckv_all, kr_all), (gk_all, gv_all)
        o_gla_s, o_ret_s = _fused_call([
            _gla(p_gla, gla_gate_w, gate_b, s0_gla, None, l, True),
            _ret(p_ret, dec_lanes, rope64, state_ret, None, l, True)], N_LAT, "mixers_latent_scan")
        o_mla_s, o_gqa_s = _fused_call([
            _mla(p_mla, gq, gkv, wq, wkv, rope32_q, rope32_k, cache_mla_ckv, cache_kr, None, l, True),
            _gqa(p_gqa, sink_lanes, rope64, cache_gqa_k, cache_gqa_v, None, l, True)], N_LAT, "mixers_latent_attn")
        o_ctx = (o_gla_c, o_mla_c, o_ret_c, o_gqa_c)
        o_lat = (o_gla_s, o_mla_s, o_ret_s, o_gqa_s)
        if l % 2 == 0:
            x, h2 = _outproj(xs, o_ctx, o_lat, w_out_b, g2, mod, None, l)
            x = _ffn(h2, x, mod, ffn_w_gate, ffn_w_up, ffn_w_down, l)
        else:
            x, h2, gates, sel = _outproj(xs, o_ctx, o_lat, w_out_b, g2, mod, router, l)
            x = _moe(h2, x, mod, gates, sel, moe_w_gate, moe_w_up, moe_w_down, l)
        xs = (x,)

    y_ctx, y_lat = _final_norm(x, final_norm_g[None])
    return (y_ctx.reshape(N_CTX, T_CTX, D), y_lat.reshape(N_LAT, T_LAT, D), jnp.swapaxes(st_gla, -1, -2), st_ret,
            *caches_mla, *caches_gqa)
```

```python
import functools

import numpy as np
import jax
import jax.numpy as jnp
from jax import lax
from jax.experimental import pallas as pl
from jax.experimental.pallas import tpu as pltpu

F32 = jnp.float32
BF16 = jnp.bfloat16
HIGHEST = lax.Precision.HIGHEST

D = 1024
N_CTX, T_CTX = 16, 256
N_LAT, T_LAT = 2, 1024
PAST = 256
DEPTH = 4
M_CTX = N_CTX * T_CTX
M_LAT = N_LAT * T_LAT
M_ALL = M_CTX + M_LAT
GRID_W = 64
ROPE_BASE = 10000.0
EPS = 1e-6

GLA_H, GLA_DK, GLA_DV, GLA_RANK, GLA_NORM, GLA_C = 4, 32, 64, 16, 16.0, 64
MLA_H, MLA_QR, MLA_KVR, MLA_NOPE, MLA_ROPE, MLA_DV = 4, 256, 128, 64, 32, 64
RET_H, RET_DK, RET_DV = 4, 64, 64
GQA_H, GQA_KV, GQA_HD, WINDOW = 4, 2, 64, 128
D_FF, N_EXP, D_FFE = 2816, 8, 3584

W_GLA, W_MLA, W_RET, W_GQA = 896, 512, 1024, 512
IN_WIDTH = 2752
IN_GROUPS = ((0, 800), (800, 1216), (1216, 2240), (2240, 2752))
LANE = 128
NEG = -1e30
QB = 256
QL = 128
VMEM_LIMIT = 56 * 1024 * 1024


def _cparams(*sem):
    return pltpu.CompilerParams(dimension_semantics=sem, vmem_limit_bytes=VMEM_LIMIT)


def _mm(a, b):
    return jnp.dot(a.astype(BF16), b.astype(BF16), preferred_element_type=F32)


def _mm_nt(a, b):
    return lax.dot_general(a.astype(BF16), b.astype(BF16), (((1,), (1,)), ((), ())), preferred_element_type=F32)


def _mm_tn(a, b):
    return lax.dot_general(a.astype(BF16), b.astype(BF16), (((0,), (0,)), ((), ())), preferred_element_type=F32)


def _mm_f32(a, b):
    return jnp.dot(a, b, precision=HIGHEST, preferred_element_type=F32)


def _silu(x):
    return x * (1.0 / (1.0 + jnp.exp(-x)))


def _log_sigmoid(x):
    return jnp.minimum(x, 0.0) - jnp.log1p(jnp.exp(-jnp.abs(x)))


def _rms(x):
    return x * lax.rsqrt(jnp.mean(x * x, axis=-1, keepdims=True) + EPS)


def _mod_row(tile, tm):
    return jnp.maximum((tile * tm) // T_LAT - (M_CTX // T_LAT - 1), 0)


def _mod_kernel(c_ref, w_ref, b_ref, o_ref):
    o_ref[...] = _mm(_silu(c_ref[...]), w_ref[...]) + b_ref[...]


def _modulation(cond, w_mod, b_mod):
    tn = 1536
    return pl.pallas_call(
        _mod_kernel,
        out_shape=jax.ShapeDtypeStruct((DEPTH, 8, 6 * D), F32),
        grid=(DEPTH, 6 * D // tn),
        in_specs=[pl.BlockSpec((8, D), lambda l, j: (0, 0)),
                  pl.BlockSpec((None, D, tn), lambda l, j: (l, 0, j)),
                  pl.BlockSpec((None, 1, tn), lambda l, j: (l, 0, j))],
        out_specs=pl.BlockSpec((None, 8, tn), lambda l, j: (l, 0, j)),
        compiler_params=_cparams("parallel", "parallel"),
        name="modulation",
    )(cond, w_mod, b_mod.reshape(DEPTH, 1, 6 * D))


def _inproj_kernel(split, tm, *refs):
    if split:
        xc_ref, xl_ref = refs[0:2]
        x = jnp.where(pl.program_id(0) >= M_CTX // tm, xl_ref[...], xc_ref[...])
        refs = refs[2:]
    else:
        x = refs[0][...]
        refs = refs[1:]
    g_ref, sh_ref, sc_ref, w_ref = refs[0:4]
    out_refs, w_s = refs[4:-1], refs[-1]

    @pl.when(pl.program_id(0) == 0)
    def _():
        dst = 0
        for (lo, hi), o_ref in zip(IN_GROUPS, out_refs):
            width = o_ref.shape[-1]
            w_s[:, dst:dst + hi - lo] = w_ref[:, lo:hi].astype(BF16)
            if width > hi - lo:
                w_s[:, dst + hi - lo:dst + width] = jnp.zeros((D, width - (hi - lo)), BF16)
            dst += width

    h = (_rms(x) * g_ref[...] * (1.0 + sc_ref[...]) + sh_ref[...]).astype(BF16)
    col = 0
    for o_ref in out_refs:
        width = o_ref.shape[-1]
        o_ref[...] = jnp.dot(h, w_s[:, col:col + width], preferred_element_type=F32)
        col += width


def _mod_spec(l, j, tm):
    return pl.BlockSpec((None, None, 1, D), lambda i, *_: (l, _mod_row(i, tm) * 6 + j, 0, 0))


def _layer_spec(l, *shape):
    return pl.BlockSpec((None,) + shape, lambda *_: (l,) + (0,) * len(shape))


def _stacked_output(prev, tail, l, nseq, in_specs, args, out_shape, out_specs, aliases):
    out_shape.append(jax.ShapeDtypeStruct((nseq, DEPTH) + tail, F32))
    out_specs.append(pl.BlockSpec((None, None) + tail, lambda b: (b, l) + (0,) * len(tail)))
    if prev is not None:
        in_specs.append(pl.BlockSpec(memory_space=pl.ANY))
        args.append(prev)
        aliases[len(args) - 1] = len(out_shape) - 1


def _row_split_specs(tm, width):
    n_ctx = M_CTX // tm
    return [pl.BlockSpec((tm, width), lambda i: (jnp.minimum(i, n_ctx - 1), 0)),
            pl.BlockSpec((tm, width), lambda i: (jnp.maximum(i - n_ctx, 0), 0))]


def _inproj(xs, g1, mod, w_in, l):
    tm = 512
    split = len(xs) == 2
    rows = lambda w: pl.BlockSpec((tm, w), lambda i: (i, 0))
    widths = (W_GLA, W_MLA, W_RET, W_GQA)
    x_specs = _row_split_specs(tm, D) if split else [rows(D)]
    w_spec = pl.BlockSpec((None, D, IN_WIDTH), lambda i: (l, 0, 0), pipeline_mode=pl.Buffered(1))
    return pl.pallas_call(
        functools.partial(_inproj_kernel, split, tm),
        out_shape=[jax.ShapeDtypeStruct((M_ALL, w), F32) for w in widths],
        grid=(M_ALL // tm,),
        in_specs=x_specs + [_layer_spec(l, 1, D), _mod_spec(l, 0, tm), _mod_spec(l, 1, tm), w_spec],
        out_specs=[rows(w) for w in widths],
        scratch_shapes=[pltpu.VMEM((D, sum(widths)), BF16)],
        compiler_params=_cparams("arbitrary"),
        name="inproj",
    )(*xs, g1, mod, mod, w_in)


def _head_rms_gate(o, gate):
    r = lax.broadcasted_iota(jnp.int32, (256, 256), 0) // 64
    c = lax.broadcasted_iota(jnp.int32, (256, 256), 1) // 64
    group_mean = jnp.where(r == c, 1.0 / 64.0, 0.0).astype(BF16)
    sq = o * o
    sq_hi = sq.astype(BF16)
    sq_lo = (sq - sq_hi.astype(F32)).astype(BF16)
    ms = (jnp.dot(sq_hi, group_mean, preferred_element_type=F32)
          + jnp.dot(sq_lo, group_mean, preferred_element_type=F32))
    return o * lax.rsqrt(ms + EPS) * _silu(gate)


def _own_lanes(n_heads, rows_per_head, lanes_per_head):
    shape = (n_heads * rows_per_head, n_heads * lanes_per_head)
    return (lax.broadcasted_iota(jnp.int32, shape, 0) // rows_per_head
            == lax.broadcasted_iota(jnp.int32, shape, 1) // lanes_per_head).astype(F32)


def _stack_heads(x, n_heads, lanes_per_head):
    return jnp.concatenate([x] * n_heads, axis=0) * _own_lanes(n_heads, x.shape[0], lanes_per_head)


def _unstack_heads(y, n_heads, lanes_per_head):
    t = y.shape[0] // n_heads
    y = y * _own_lanes(n_heads, t, lanes_per_head)
    out = y[0:t]
    for h in range(1, n_heads):
        out = out + y[h * t:(h + 1) * t]
    return out


def _per_query_head(x):
    lane = lax.broadcasted_iota(jnp.int32, x.shape, 1)
    swapped = pltpu.roll(x, GQA_HD, 1)
    return jnp.concatenate([jnp.where(lane < GQA_HD, x, swapped), jnp.where(lane < GQA_HD, swapped, x)], axis=1)


def _rope(x, cos, sin_lo, sin_hi, half):
    w = x.shape[-1]
    return x * cos + pltpu.roll(x, w - half, 1) * sin_lo + pltpu.roll(x, half, 1) * sin_hi


def _rope_tables(t, head_dim, width):
    half = head_dim // 2
    quarter = head_dim // 4
    pos = np.arange(t)
    rows = (pos // GRID_W).astype(np.float32)
    cols = (pos % GRID_W).astype(np.float32)
    inv = np.power(np.float32(ROPE_BASE), -np.arange(quarter, dtype=np.float32) / np.float32(quarter)).astype(np.float32)
    ang = np.concatenate([rows[:, None] * inv, cols[:, None] * inv], axis=-1).astype(np.float32)
    lane = np.arange(width)
    a = ang[:, lane % half]
    cos, sin = np.cos(a).astype(np.float32), np.sin(a).astype(np.float32)
    low = (lane % head_dim) < half
    return (jnp.asarray(cos), jnp.asarray(np.where(low[None], -sin, 0.0).astype(np.float32)),
            jnp.asarray(np.where(low[None], 0.0, sin).astype(np.float32)))


def _seq_spec(t, width, row0, latent):
    return pl.BlockSpec((t, width), lambda b: (row0 + b, 0), pipeline_mode=pl.Buffered(1) if latent else None)


def _table_spec(t, width):
    return pl.BlockSpec((t, width), lambda b: (0, 0), pipeline_mode=pl.Buffered(1))


def _fused_call(parts, nseq, name):
    in_specs, args, out_shape, out_specs, scratch, aliases, layout = [], [], [], [], [], {}, []
    for fn, p_in, p_args, p_shape, p_out, p_scratch, p_alias in parts:
        layout.append((fn, len(p_args), len(p_shape), len(p_scratch)))
        for k, v in p_alias.items():
            aliases[len(args) + k] = len(out_shape) + v
        in_specs += p_in
        args += p_args
        out_shape += p_shape
        out_specs += p_out
        scratch += p_scratch
    n_in, n_out = len(args), len(out_shape)

    def kernel(*refs):
        i, o, s = 0, n_in, n_in + n_out
        for fn, ni, no, ns in layout:
            fn(*refs[i:i + ni], *refs[o:o + no], *refs[s:s + ns])
            i, o, s = i + ni, o + no, s + ns

    return pl.pallas_call(
        kernel, out_shape=out_shape, grid=(nseq,), in_specs=in_specs, out_specs=out_specs, scratch_shapes=scratch,
        input_output_aliases=aliases, name=name,
        compiler_params=pltpu.CompilerParams(dimension_semantics=("arbitrary",), vmem_limit_bytes=60 * 1024 * 1024),
    )(*args)


def _gla_kernel(t, latent, *refs):
    if latent:
        p_ref, gw_ref, gb_ref, s0_ref, o_ref, la_f, la_b, of_s, ob_s, st_s = refs
    else:
        p_ref, gw_ref, gb_ref = refs[0:3]
        o_ref, st_ref, la_f, la_b, of_s, ob_s, st_s = refs[-7:]
    c = GLA_C
    n = t // c
    hd = GLA_H * GLA_DK
    la_f[...] = _log_sigmoid(_mm_f32(p_ref[:, 768:784], gw_ref[0]) + gb_ref[0]) / GLA_NORM
    la_b[...] = _log_sigmoid(_mm_f32(p_ref[:, 784:800], gw_ref[1]) + gb_ref[1]) / GLA_NORM

    ri = lax.broadcasted_iota(jnp.int32, (GLA_H * c, c), 0) % c
    ci = lax.broadcasted_iota(jnp.int32, (GLA_H * c, c), 1)
    tri_r = lax.broadcasted_iota(jnp.int32, (c, c), 0)
    tri_c = lax.broadcasted_iota(jnp.int32, (c, c), 1)
    head_rows = lax.broadcasted_iota(jnp.int32, (GLA_H * c, hd), 0) // c
    own_dk = (head_rows == lax.broadcasted_iota(jnp.int32, (GLA_H * c, hd), 1) // GLA_DK).astype(F32)
    own_dv = (lax.broadcasted_iota(jnp.int32, (GLA_H * c, GLA_H * GLA_DV), 0) // c
              == lax.broadcasted_iota(jnp.int32, (GLA_H * c, GLA_H * GLA_DV), 1) // GLA_DV).astype(F32)
    scale = GLA_DK ** -0.5
    if latent:
        st_s[...] = s0_ref[...]
    else:
        st_s[...] = jnp.zeros_like(st_s)

    def chunk(row0, la_ref, d):
        fwd = d == 0
        keep = (ci <= ri) if fwd else (ci >= ri)
        tri = ((tri_c <= tri_r) if fwd else (tri_c >= tri_r)).astype(BF16)
        q = p_ref[pl.ds(row0, c), 0:128]
        k = p_ref[pl.ds(row0, c), 128:256] * scale
        v = p_ref[pl.ds(row0, c), 256:512]
        la = la_ref[pl.ds(row0, c), :]
        la_hi = la.astype(BF16)
        la_lo = (la - la_hi.astype(F32)).astype(BF16)
        bc = (jnp.dot(tri, la_hi, preferred_element_type=F32) + jnp.dot(tri, la_lo, preferred_element_type=F32))
        tot = bc[c - 1:c, :] if fwd else bc[0:1, :]
        mid = bc[c // 2 - 1:c // 2, :] if fwd else bc[c // 2:c // 2 + 1, :]
        qe, ke = q * jnp.exp(bc - mid), k * jnp.exp(mid - bc)
        q_in, k_out, a = q * jnp.exp(bc), k * jnp.exp(tot - bc), jnp.exp(tot)
        q_rows = jnp.concatenate([qe] * GLA_H, axis=0) * own_dk
        att = jnp.where(keep, _mm_nt(q_rows, ke), 0.0)
        o_all = _mm(att, v) * own_dv
        o = o_all[0:c] + o_all[c:2 * c] + o_all[2 * c:3 * c] + o_all[3 * c:4 * c]
        st = st_s[d]
        o = o + _mm_nt(q_in, st)
        st_s[d] = st * a + _mm_tn(v, k_out) * own_dk
        return o

    unroll = 4

    def body(i, carry):
        for u in range(unroll):
            rf = pl.multiple_of((i * unroll + u) * c, c)
            rb = pl.multiple_of((n - 1 - i * unroll - u) * c, c)
            of_s[pl.ds(rf, c), :] = chunk(rf, la_f, 0)
            ob_s[pl.ds(rb, c), :] = chunk(rb, la_b, 1)
        return carry

    lax.fori_loop(0, n // unroll, body, 0)
    if not latent:
        for d in range(2):
            for h in range(GLA_H):
                st_ref[d, h] = st_s[d, h * GLA_DV:(h + 1) * GLA_DV, h * GLA_DK:(h + 1) * GLA_DK]
    for r in range(t // QB):
        rows = slice(r * QB, (r + 1) * QB)
        o_ref[rows, :] = _head_rms_gate(of_s[rows, :] + ob_s[rows, :], p_ref[rows, 512:768]).astype(BF16)


def _gla(p, gate_w, gate_b, s0_bd, st_prev, l, latent):
    t, nseq, row0 = (T_LAT, N_LAT, M_CTX // T_LAT) if latent else (T_CTX, N_CTX, 0)
    st_shape = (2, GLA_H * GLA_DV, GLA_H * GLA_DK)
    in_specs = [_seq_spec(t, W_GLA, row0, latent),
                _layer_spec(l, 2, GLA_RANK, GLA_H * GLA_DK), _layer_spec(l, 2, 1, GLA_H * GLA_DK)]
    args = [p, gate_w, gate_b]
    aliases = {}
    o_shape = jax.ShapeDtypeStruct((nseq * t, 256), BF16)
    o_spec = pl.BlockSpec((t, 256), lambda b: (b, 0))
    if latent:
        in_specs.append(pl.BlockSpec((None, None) + st_shape, lambda b: (b, l, 0, 0, 0)))
        args.append(s0_bd)
    out_shape, out_specs = [o_shape], [o_spec]
    if not latent:
        _stacked_output(st_prev, (2, GLA_H, GLA_DV, GLA_DK), l, nseq, in_specs, args, out_shape, out_specs, aliases)
    scratch = [pltpu.VMEM((t, 128), F32), pltpu.VMEM((t, 128), F32),
               pltpu.VMEM((t, 256), F32), pltpu.VMEM((t, 256), F32), pltpu.VMEM(st_shape, F32)]
    return functools.partial(_gla_kernel, t, latent), in_specs, args, out_shape, out_specs, scratch, aliases


def _ret_kernel(t, latent, *refs):
    if latent:
        p_ref, dec_ref, cos_ref, slo_ref, shi_ref, s0_ref, o_ref, q_s, k_s, decay_s = refs
    else:
        p_ref, dec_ref = refs[0:2]
        o_ref, st_ref, decay_s = refs[-3:]
    lg = _log_sigmoid(dec_ref[...])
    scale = RET_DK ** -0.5
    nblk = t // QB

    @pl.when(pl.program_id(0) == 0)
    def _():
        wide = decay_s.shape[-1]
        dist = (lax.broadcasted_iota(jnp.int32, (QB, wide), 0) - lax.broadcasted_iota(jnp.int32, (QB, wide), 1)
                + (nblk - 1) * QB).astype(F32)
        for h in range(RET_H):
            lg_f, lg_b = lg[0:1, h * 64:h * 64 + 1], lg[1:2, h * 64:h * 64 + 1]
            decay_s[h] = (jnp.where(dist >= 0, jnp.exp(jnp.maximum(dist, 0.0) * lg_f), 0.0)
                          + jnp.where(dist <= 0, jnp.exp(jnp.maximum(-dist, 0.0) * lg_b), 0.0))

    if not latent:
        k = p_ref[:, 256:512] * scale
        v = p_ref[:, 512:768]
        decay = jnp.concatenate([decay_s[h] for h in range(RET_H)], axis=0)
        o = _unstack_heads(_mm(_mm_nt(_stack_heads(p_ref[:, 0:256], RET_H, RET_DK), k) * decay, v), RET_H, RET_DV)
        o_ref[...] = _head_rms_gate(o, p_ref[:, 768:1024]).astype(BF16)
        j = lax.broadcasted_iota(jnp.int32, (t, 1), 0).astype(F32)
        kv_f = _mm_tn(k * jnp.exp((float(t - 1) - j) * lg[0:1, :]), v)
        kv_b = _mm_tn(k * jnp.exp(j * lg[1:2, :]), v)
        for h in range(RET_H):
            st_ref[0, h] = kv_f[h * RET_DK:(h + 1) * RET_DK, h * RET_DV:(h + 1) * RET_DV]
            st_ref[1, h] = kv_b[h * RET_DK:(h + 1) * RET_DK, h * RET_DV:(h + 1) * RET_DV]
        return
    q_s[...] = _rope(p_ref[:, 0:256], cos_ref[...], slo_ref[...], shi_ref[...], RET_DK // 2)
    k_s[...] = _rope(p_ref[:, 256:512], cos_ref[...], slo_ref[...], shi_ref[...], RET_DK // 2) * scale
    for r in range(t // QL):
        rows = slice(r * QL, (r + 1) * QL)
        pos = (lax.broadcasted_iota(jnp.int32, (QL, 1), 0) + r * QL).astype(F32)
        strip_rows = slice(r * QL % QB, r * QL % QB + QL)
        off = (nblk - 1 - r * QL // QB) * QB
        decay = jnp.concatenate([decay_s[h, strip_rows, off:off + t] for h in range(RET_H)], axis=0)
        q = q_s[rows, :]
        o = _unstack_heads(_mm(_mm_nt(_stack_heads(q, RET_H, RET_DK), k_s[...]) * decay, p_ref[:, 512:768]),
                           RET_H, RET_DV)
        o = o + _mm(q * jnp.exp((pos + 1.0) * lg[0:1, :]), s0_ref[0])
        o = o + _mm(q * jnp.exp((float(t) - pos) * lg[1:2, :]), s0_ref[1])
        o_ref[rows, :] = _head_rms_gate(o, p_ref[rows, 768:1024]).astype(BF16)


def _ret(p, dec_lanes, tables, s0, st_prev, l, latent):
    t, nseq, row0 = (T_LAT, N_LAT, M_CTX // T_LAT) if latent else (T_CTX, N_CTX, 0)
    st_shape = (2, RET_H, RET_DK, RET_DV)
    in_specs = [_seq_spec(t, W_RET, row0, latent), _layer_spec(l, 2, 256)]
    args = [p, dec_lanes]
    aliases = {}
    o_shape = jax.ShapeDtypeStruct((nseq * t, 256), BF16)
    o_spec = pl.BlockSpec((t, 256), lambda b: (b, 0))
    scratch = []
    if latent:
        in_specs += [_table_spec(t, 256)] * 3
        in_specs.append(pl.BlockSpec((None, None, 2, RET_H * RET_DK, RET_H * RET_DV), lambda b: (b, l, 0, 0, 0)))
        args += list(tables) + [s0]
        scratch = [pltpu.VMEM((t, 256), F32), pltpu.VMEM((t, 256), F32)]
    scratch.append(pltpu.VMEM((RET_H, QB, 2 * t - QB), F32))
    out_shape, out_specs = [o_shape], [o_spec]
    if not latent:
        _stacked_output(st_prev, st_shape, l, nseq, in_specs, args, out_shape, out_specs, aliases)
    return functools.partial(_ret_kernel, t, latent), in_specs, args, out_shape, out_specs, scratch, aliases


def _mla_kernel(t, latent, *refs):
    if latent:
        (p_ref, gq_ref, gkv_ref, wq_ref, wkv_ref, cq_ref, slq_ref, shq_ref, ck_ref, slk_ref, shk_ref,
         cckv_ref, ckr_ref, o_ref, qn_s, qr_s, kn_s, kr_s, v_s) = refs
    else:
        p_ref, gq_ref, gkv_ref, wq_ref, wkv_ref = refs[0:5]
        o_ref, ckv_ref, kro_ref, qn_s, qr_s, kn_s, kr_s, v_s = refs[-8:]
    nk = t + (PAST if latent else 0)
    qh = _mm(_rms(p_ref[:, 0:256]) * gq_ref[...], wq_ref[...])
    ckv = _rms(p_ref[:, 256:384]) * gkv_ref[...]
    kv = _mm(ckv, wkv_ref[...])
    qn_s[...] = qh[:, 0:256]
    kn_s[0:t, :] = kv[:, 0:256]
    v_s[0:t, :] = kv[:, 256:512]
    if latent:
        qr_s[...] = _rope(qh[:, 256:384], cq_ref[...], slq_ref[...], shq_ref[...], MLA_ROPE // 2)
        kr_s[0:t, :] = _rope(p_ref[:, 384:512], ck_ref[...], slk_ref[...], shk_ref[...], MLA_ROPE // 2)
        kvc = _mm(cckv_ref[...], wkv_ref[...])
        kn_s[t:nk, :] = kvc[:, 0:256]
        v_s[t:nk, :] = kvc[:, 256:512]
        kr_s[t:nk, :] = ckr_ref[...]
    scale = (MLA_NOPE + MLA_ROPE) ** -0.5
    if not latent:
        ckv_ref[...] = ckv
        kro_ref[...] = p_ref[:, 384:384 + MLA_ROPE]
        kr = p_ref[:, 384:512]
        kr4 = kr + pltpu.roll(kr, MLA_ROPE, 1) + pltpu.roll(kr, 2 * MLA_ROPE, 1) + pltpu.roll(kr, 3 * MLA_ROPE, 1)
        s = (_mm_nt(_stack_heads(qh[:, 0:256], MLA_H, MLA_NOPE), kv[:, 0:256])
             + _mm_nt(_stack_heads(qh[:, 256:384], MLA_H, MLA_ROPE), kr4)) * scale
        e = jnp.exp(s - jnp.max(s, axis=-1, keepdims=True))
        o = _mm(e, kv[:, 256:512]) / jnp.sum(e, axis=-1, keepdims=True)
        o_ref[...] = _unstack_heads(o, MLA_H, MLA_DV).astype(BF16)
        return
    kr = kr_s[...]
    kr4 = kr + pltpu.roll(kr, MLA_ROPE, 1) + pltpu.roll(kr, 2 * MLA_ROPE, 1) + pltpu.roll(kr, 3 * MLA_ROPE, 1)
    for r in range(t // QL):
        rows = slice(r * QL, (r + 1) * QL)
        s = (_mm_nt(_stack_heads(qn_s[rows, :], MLA_H, MLA_NOPE), kn_s[...])
             + _mm_nt(_stack_heads(qr_s[rows, :], MLA_H, MLA_ROPE), kr4)) * scale
        e = jnp.exp(s - jnp.max(s, axis=-1, keepdims=True))
        o = _mm(e, v_s[...]) / jnp.sum(e, axis=-1, keepdims=True)
        o_ref[rows, :] = _unstack_heads(o, MLA_H, MLA_DV).astype(BF16)


def _mla(p, gq, gkv, wq, wkv, tables_q, tables_k, cache_ckv, cache_kr, prev, l, latent):
    t, nseq, row0 = (T_LAT, N_LAT, M_CTX // T_LAT) if latent else (T_CTX, N_CTX, 0)
    nk = t + (PAST if latent else 0)
    in_specs = [_seq_spec(t, W_MLA, row0, latent), _layer_spec(l, 1, MLA_QR),
                _layer_spec(l, 1, MLA_KVR), _layer_spec(l, MLA_QR, 384), _layer_spec(l, MLA_KVR, 512)]
    args = [p, gq, gkv, wq, wkv]
    aliases = {}
    o_shape = jax.ShapeDtypeStruct((nseq * t, 256), BF16)
    o_spec = pl.BlockSpec((t, 256), lambda b: (b, 0))
    if latent:
        in_specs += [_table_spec(t, 128)] * 6
        in_specs += [pl.BlockSpec((None, None, PAST, MLA_KVR), lambda b: (b, l, 0, 0)),
                     pl.BlockSpec((None, None, PAST, 128), lambda b: (b, l, 0, 0))]
        args += list(tables_q) + list(tables_k) + [cache_ckv, cache_kr]
    out_shape, out_specs = [o_shape], [o_spec]
    if not latent:
        prev = prev or (None, None)
        _stacked_output(prev[0], (t, MLA_KVR), l, nseq, in_specs, args, out_shape, out_specs, aliases)
        _stacked_output(prev[1], (t, MLA_ROPE), l, nseq, in_specs, args, out_shape, out_specs, aliases)
    scratch = [pltpu.VMEM((t, 256), F32), pltpu.VMEM((t, 128), F32), pltpu.VMEM((nk, 256), F32),
               pltpu.VMEM((nk, 128), F32), pltpu.VMEM((nk, 256), F32)]
    return functools.partial(_mla_kernel, t, latent), in_specs, args, out_shape, out_specs, scratch, aliases


def _gqa_kernel(t, latent, *refs):
    if latent:
        p_ref, sink_ref, cos_ref, slo_ref, shi_ref, ck_ref, cv_ref, o_ref, q_s, k_s = refs
    else:
        p_ref, sink_ref = refs[0:2]
        o_ref, ko_ref, vo_ref = refs[-3:]
        for kvh in range(GQA_KV):
            ko_ref[kvh] = p_ref[:, 256 + kvh * GQA_HD:256 + (kvh + 1) * GQA_HD]
            vo_ref[kvh] = p_ref[:, 384 + kvh * GQA_HD:384 + (kvh + 1) * GQA_HD]
    scale = GQA_HD ** -0.5
    if not latent:
        own = _own_lanes(GQA_H, t, GQA_HD)
        sink = jnp.max(jnp.where(own > 0, sink_ref[...], NEG), axis=-1, keepdims=True)
        s = _mm_nt(_stack_heads(p_ref[:, 0:256], GQA_H, GQA_HD), _per_query_head(p_ref[:, 256:384])) * scale
        m = jnp.maximum(jnp.max(s, axis=-1, keepdims=True), sink)
        e = jnp.exp(s - m)
        o = _mm(e, _per_query_head(p_ref[:, 384:512])) / (jnp.sum(e, axis=-1, keepdims=True) + jnp.exp(sink - m))
        o_ref[...] = _unstack_heads(o, GQA_H, GQA_HD).astype(BF16)
        return
    q_s[...] = _rope(p_ref[:, 0:256], cos_ref[...], slo_ref[...], shi_ref[...], GQA_HD // 2)
    k_s[...] = _rope(p_ref[:, 256:384], cos_ref[:, 0:128], slo_ref[:, 0:128], shi_ref[:, 0:128], GQA_HD // 2)
    kc = _per_query_head(jnp.concatenate([ck_ref[0], ck_ref[1]], axis=1))
    vc = _per_query_head(jnp.concatenate([cv_ref[0], cv_ref[1]], axis=1))
    sink = jnp.max(jnp.where(_own_lanes(GQA_H, QL, GQA_HD) > 0, sink_ref[...], NEG), axis=-1, keepdims=True)
    for r in range(t // QL):
        rows = slice(r * QL, (r + 1) * QL)
        keys = slice(max(0, r * QL - WINDOW), min(t, (r + 1) * QL + WINDOW))
        nkeys = keys.stop - keys.start
        row = lax.broadcasted_iota(jnp.int32, (GQA_H * QL, nkeys), 0) % QL + r * QL
        col = lax.broadcasted_iota(jnp.int32, (GQA_H * QL, nkeys), 1) + keys.start
        near = jnp.abs(row - col) <= WINDOW
        q = _stack_heads(q_s[rows, :], GQA_H, GQA_HD)
        s_loc = jnp.where(near, _mm_nt(q, _per_query_head(k_s[keys, :])) * scale, NEG)
        s_ctx = _mm_nt(q, kc) * scale
        m = jnp.maximum(jnp.maximum(jnp.max(s_loc, axis=-1, keepdims=True), jnp.max(s_ctx, axis=-1, keepdims=True)),
                        sink)
        e_loc, e_ctx = jnp.exp(s_loc - m), jnp.exp(s_ctx - m)
        den = jnp.sum(e_loc, axis=-1, keepdims=True) + jnp.sum(e_ctx, axis=-1, keepdims=True) + jnp.exp(sink - m)
        o = (_mm(e_loc, _per_query_head(p_ref[keys, 384:512])) + _mm(e_ctx, vc)) / den
        o_ref[rows, :] = _unstack_heads(o, GQA_H, GQA_HD).astype(BF16)


def _gqa(p, sink_lanes, tables, cache_k, cache_v, prev, l, latent):
    t, nseq, row0 = (T_LAT, N_LAT, M_CTX // T_LAT) if latent else (T_CTX, N_CTX, 0)
    in_specs = [_seq_spec(t, W_GQA, row0, latent), _layer_spec(l, 1, 256)]
    args = [p, sink_lanes]
    scratch = []
    aliases = {}
    out_shape = [jax.ShapeDtypeStruct((nseq * t, 256), BF16)]
    out_specs = [pl.BlockSpec((t, 256), lambda b: (b, 0))]
    if latent:
        in_specs += [_table_spec(t, 256)] * 3
        in_specs += [pl.BlockSpec((None, None, GQA_KV, PAST, GQA_HD), lambda b: (b, l, 0, 0, 0))] * 2
        args += list(tables) + [cache_k, cache_v]
        scratch = [pltpu.VMEM((t, 256), F32), pltpu.VMEM((t, 128), F32)]
    else:
        prev = prev or (None, None)
        for pv in prev:
            _stacked_output(pv, (GQA_KV, t, GQA_HD), l, nseq, in_specs, args, out_shape, out_specs, aliases)
    return functools.partial(_gqa_kernel, t, latent), in_specs, args, out_shape, out_specs, scratch, aliases


def _outproj_kernel(route, split, tm, *refs):
    is_lat = pl.program_id(0) >= M_CTX // tm
    if split:
        x_in = jnp.where(is_lat, refs[1][...], refs[0][...])
        refs = refs[2:]
    else:
        x_in = refs[0][...]
        refs = refs[1:]
    ctx_refs, lat_refs = refs[0:4], refs[4:8]
    w_ref, gt_ref, g_ref, sh_ref, sc_ref = refs[8:13]
    refs = refs[13:]
    mix = jnp.zeros((tm, D), F32)
    for m in range(4):
        o = jnp.where(is_lat, lat_refs[m][...], ctx_refs[m][...])
        mix = mix + jnp.dot(o, w_ref[m * 256:(m + 1) * 256, :], preferred_element_type=F32)
    x = x_in + gt_ref[...] * mix
    h = (_rms(x) * g_ref[...] * (1.0 + sc_ref[...]) + sh_ref[...]).astype(BF16)
    if not route:
        xo_ref, h_ref = refs
    else:
        r_ref, xo_ref, h_ref, gate_ref, sel_ref = refs
        lane = lax.broadcasted_iota(jnp.int32, gate_ref.shape, 1)
        logits = jnp.where(lane < N_EXP, jnp.dot(h, r_ref[...], preferred_element_type=F32), NEG)
        m1 = jnp.max(logits, axis=-1, keepdims=True)
        i1 = jnp.min(jnp.where(logits == m1, lane, LANE), axis=-1, keepdims=True)
        rest = jnp.where(lane == i1, NEG, logits)
        m2 = jnp.max(rest, axis=-1, keepdims=True)
        i2 = jnp.min(jnp.where(rest == m2, lane, LANE), axis=-1, keepdims=True)
        e2 = jnp.exp(m2 - m1)
        gate_ref[...] = jnp.where(lane == i1, 1.0 / (1.0 + e2), 0.0) + jnp.where(lane == i2, e2 / (1.0 + e2), 0.0)
        sel_ref[...] = jnp.where((lane == i1) | (lane == i2), 1, 0)
    xo_ref[...] = x
    h_ref[...] = h


def _outproj(xs, o_ctx, o_lat, w_out, g2, mod, router, l):
    tm = 512
    route = router is not None
    split = len(xs) == 2
    rows = lambda w: pl.BlockSpec((tm, w), lambda i: (i, 0))
    ctx_spec, lat_spec = _row_split_specs(tm, 256)
    x_specs = _row_split_specs(tm, D) if split else [rows(D)]
    in_specs = x_specs + [ctx_spec] * 4 + [lat_spec] * 4 + [
        _layer_spec(l, D, D), _mod_spec(l, 2, tm), _layer_spec(l, 1, D), _mod_spec(l, 3, tm), _mod_spec(l, 4, tm)]
    args = [*xs, *o_ctx, *o_lat, w_out, mod, g2, mod, mod]
    out_shape = [jax.ShapeDtypeStruct((M_ALL, D), F32), jax.ShapeDtypeStruct((M_ALL, D), BF16)]
    out_specs = [rows(D), rows(D)]
    if route:
        in_specs.append(_layer_spec(l // 2, D, LANE))
        args.append(router)
        out_shape += [jax.ShapeDtypeStruct((M_ALL, LANE), F32), jax.ShapeDtypeStruct((M_ALL, LANE), jnp.int32)]
        out_specs += [rows(LANE), rows(LANE)]
    return pl.pallas_call(
        functools.partial(_outproj_kernel, route, split, tm),
        out_shape=out_shape, grid=(M_ALL // tm,), in_specs=in_specs, out_specs=out_specs,
        compiler_params=_cparams("parallel"),
        name="outproj_route" if route else "outproj",
    )(*args)


def _ffn_kernel(h_ref, x_ref, gt_ref, wg_ref, wu_ref, wd_ref, o_ref, acc_ref):
    f = pl.program_id(1)

    @pl.when(f == 0)
    def _():
        acc_ref[...] = jnp.zeros_like(acc_ref)

    wg, wu, wd = wg_ref[...].astype(BF16), wu_ref[...].astype(BF16), wd_ref[...].astype(BF16)
    half = h_ref.shape[0] // 2
    for rows in (slice(0, half), slice(half, 2 * half)):
        h = h_ref[rows, :]
        g = jnp.dot(h, wg, preferred_element_type=F32)
        u = jnp.dot(h, wu, preferred_element_type=F32)
        acc_ref[rows, :] += jnp.dot((_silu(g) * u).astype(BF16), wd, preferred_element_type=F32)

    @pl.when(f == pl.num_programs(1) - 1)
    def _():
        o_ref[...] = x_ref[...] + gt_ref[...] * acc_ref[...]


def _ffn(h, x, mod, wg, wu, wd, l):
    tm, tf = 1024, 256
    j = l // 2
    return pl.pallas_call(
        _ffn_kernel,
        out_shape=jax.ShapeDtypeStruct((M_ALL, D), F32),
        grid=(M_ALL // tm, D_FF // tf),
        in_specs=[pl.BlockSpec((tm, D), lambda i, f: (i, 0)), pl.BlockSpec((tm, D), lambda i, f: (i, 0)),
                  _mod_spec(l, 5, tm),
                  pl.BlockSpec((None, D, tf), lambda i, f: (j, 0, f)),
                  pl.BlockSpec((None, D, tf), lambda i, f: (j, 0, f)),
                  pl.BlockSpec((None, tf, D), lambda i, f: (j, f, 0))],
        out_specs=pl.BlockSpec((tm, D), lambda i, f: (i, 0)),
        scratch_shapes=[pltpu.VMEM((tm, D), F32)],
        compiler_params=_cparams("parallel", "arbitrary"),
        name="ffn_dense",
    )(h, x, mod, wg, wu, wd)


SUP, SUB, CHUNK = 2048, 256, 256
N_SUB = SUP // SUB
S_MAX = 2 * M_ALL // SUP + N_EXP
P_SLOT = S_MAX * SUP
TC = 128
WIN = TC + 16


def _moe_expert_kernel(se_ref, nt_ref, sblk_ref, clo_ref, chi_ref, pos_ref, h_ref, wg_ref, wu_ref, wd_ref, o_ref,
                       xs_s, acc_s, wg_s, wu_s, wd_s):
    s = pl.program_id(0)
    f = pl.program_id(1)
    n = nt_ref[s]

    @pl.when((f == 0) & (n == 0))
    def _():
        o_ref[...] = jnp.zeros_like(o_ref)

    @pl.when((f == 0) & (n > 0))
    def _():
        row = lax.broadcasted_iota(jnp.int32, (SUB, CHUNK), 0)

        def gather(j, carry):
            row0 = (s * N_SUB + j) * SUB
            acc_s[j] = jnp.zeros((SUB, D), F32)

            def chunk(c, carry):
                tpos = pos_ref[:, pl.ds(pl.multiple_of(c * CHUNK, CHUNK), CHUNK)]
                onehot = jnp.where(tpos - row0 == row, 1.0, 0.0).astype(BF16)
                rows = h_ref[pl.ds(pl.multiple_of(c * CHUNK, CHUNK), CHUNK), :]
                acc_s[j] += jnp.dot(onehot, rows, preferred_element_type=F32)
                return carry

            g = s * N_SUB + j
            lax.fori_loop(clo_ref[g], chi_ref[g] + 1, chunk, 0)
            xs_s[j] = acc_s[j].astype(BF16)
            acc_s[j] = jnp.zeros((SUB, D), F32)
            return carry

        lax.fori_loop(0, n, gather, 0)

    @pl.when(n > 0)
    def _():
        wg_s[...] = wg_ref[...].astype(BF16)
        wu_s[...] = wu_ref[...].astype(BF16)
        wd_s[...] = wd_ref[...].astype(BF16)

        def sub(j):
            x = xs_s[j]
            g = jnp.dot(x, wg_s[...], preferred_element_type=F32)
            u = jnp.dot(x, wu_s[...], preferred_element_type=F32)
            acc_s[j] += jnp.dot((_silu(g) * u).astype(BF16), wd_s[...], preferred_element_type=F32)

        def pair(jj, carry):
            sub(2 * jj)
            sub(2 * jj + 1)
            return carry

        lax.fori_loop(0, n // 2, pair, 0)

        @pl.when(n % 2 == 1)
        def _():
            sub(n - 1)

    @pl.when((f == pl.num_programs(1) - 1) & (n > 0))
    def _():
        for j in range(N_SUB):
            rows = slice(j * SUB, (j + 1) * SUB)

            @pl.when(j < n)
            def _():
                o_ref[rows, :] = acc_s[j].astype(BF16)

            @pl.when(j >= n)
            def _():
                o_ref[rows, :] = jnp.zeros((SUB, D), BF16)


def _moe_experts(h, pos_t, meta, wg, wu, wd, layer):
    tf = 512
    nf = D_FFE // tf
    se, nt, sblk, clo, chi = meta

    def w_up(s, f, se, nt, sblk, clo, chi):
        return (layer, se[s], 0, jnp.where(nt[s] > 0, f, nf - 1))

    def w_down(s, f, se, nt, sblk, clo, chi):
        return (layer, se[s], jnp.where(nt[s] > 0, f, nf - 1), 0)

    grid_spec = pltpu.PrefetchScalarGridSpec(
        num_scalar_prefetch=5,
        grid=(S_MAX, nf),
        in_specs=[pl.BlockSpec((None, 1, M_ALL), lambda s, f, se, nt, sblk, clo, chi: (se[s], 0, 0)),
                  pl.BlockSpec((M_ALL, D), lambda s, f, *_: (0, 0), pipeline_mode=pl.Buffered(1)),
                  pl.BlockSpec((None, None, D, tf), w_up), pl.BlockSpec((None, None, D, tf), w_up),
                  pl.BlockSpec((None, None, tf, D), w_down)],
        out_specs=pl.BlockSpec((SUP, D), lambda s, f, *_: (s, 0)),
        scratch_shapes=[pltpu.VMEM((N_SUB, SUB, D), BF16), pltpu.VMEM((N_SUB, SUB, D), F32),
                        pltpu.VMEM((D, tf), BF16), pltpu.VMEM((D, tf), BF16), pltpu.VMEM((tf, D), BF16)],
    )
    return pl.pallas_call(
        _moe_expert_kernel,
        out_shape=jax.ShapeDtypeStruct((P_SLOT, D), BF16),
        grid_spec=grid_spec,
        compiler_params=pltpu.CompilerParams(dimension_semantics=("arbitrary", "arbitrary"),
                                             vmem_limit_bytes=60 * 1024 * 1024),
        name="moe_experts",
    )(se, nt, sblk, clo, chi, pos_t, h, wg, wu, wd)


def _moe_combine_kernel(off_ref, x_ref, gt_ref, gate_ref, pos_ref, *refs):
    win_refs, o_ref = refs[:N_EXP], refs[N_EXP]
    i = pl.program_id(0)
    lane = lax.broadcasted_iota(jnp.int32, (TC, WIN), 1)
    y = jnp.zeros((TC, D), F32)
    for e in range(N_EXP):
        rel = pos_ref[:, e:e + 1] - off_ref[i * N_EXP + e] * 16
        onehot = jnp.where(rel == lane, 1.0, 0.0).astype(BF16)
        y = y + gate_ref[:, e:e + 1] * jnp.dot(onehot, win_refs[e][...], preferred_element_type=F32)
    o_ref[...] = x_ref[...] + gt_ref[...] * y


def _moe_combine(x, mod, gates, pos, off, slots, l):
    def win_spec(e):
        return pl.BlockSpec((pl.Element(WIN), pl.Element(D)), lambda i, off: (off[i * N_EXP + e] * 16, 0))

    grid_spec = pltpu.PrefetchScalarGridSpec(
        num_scalar_prefetch=1,
        grid=(M_ALL // TC,),
        in_specs=[pl.BlockSpec((TC, D), lambda i, off: (i, 0)),
                  _mod_spec(l, 5, TC),
                  pl.BlockSpec((TC, LANE), lambda i, off: (i, 0)), pl.BlockSpec((TC, LANE), lambda i, off: (i, 0))]
        + [win_spec(e) for e in range(N_EXP)],
        out_specs=pl.BlockSpec((TC, D), lambda i, off: (i, 0)),
    )
    return pl.pallas_call(
        _moe_combine_kernel,
        out_shape=jax.ShapeDtypeStruct((M_ALL, D), F32),
        grid_spec=grid_spec,
        compiler_params=_cparams("arbitrary"),
        name="moe_combine",
    )(off, x, mod, gates, pos, *([slots] * N_EXP))


def _moe_plan(sel):
    i32 = jnp.int32
    sel_t = sel[:, :N_EXP].T
    csum = jnp.cumsum(sel_t, axis=1)
    rank = csum - sel_t
    n_e = csum[:, -1]
    ns_e = (n_e + SUP - 1) // SUP
    end_e = jnp.cumsum(ns_e)
    start_e = end_e - ns_e
    n_used = end_e[-1]
    pos_t = jnp.where(sel_t > 0, start_e[:, None] * SUP + rank, -1)
    s_ids = jnp.arange(S_MAX, dtype=i32)
    sblk = jnp.minimum(s_ids, n_used - 1)
    se = jnp.sum((end_e[None, :] <= sblk[:, None]).astype(i32), axis=1)
    nv = jnp.clip(n_e[se] - (sblk - start_e[se]) * SUP, 0, SUP)
    nt = jnp.where(s_ids < n_used, (nv + SUB - 1) // SUB, 0)
    g_ids = jnp.arange(S_MAX * N_SUB, dtype=i32)
    s_g, e_g = g_ids // N_SUB, se[g_ids // N_SUB]
    r0 = (sblk[s_g] - start_e[e_g]) * SUP + (g_ids % N_SUB) * SUB
    r1 = jnp.minimum(r0 + SUB, n_e[e_g])
    live = (s_g < n_used) & (r0 < n_e[e_g])
    cs_g = csum[e_g]
    t_first = jnp.sum((cs_g <= r0[:, None]).astype(i32), axis=1)
    t_last = jnp.sum((cs_g < r1[:, None]).astype(i32), axis=1)
    clo = jnp.where(live, t_first // CHUNK, 0)
    chi = jnp.where(live, t_last // CHUNK, -1)
    before = jnp.concatenate([jnp.zeros((N_EXP, 1), i32), csum[:, TC - 1::TC][:, :-1]], axis=1)
    base = start_e[:, None] * SUP + before
    off = jnp.minimum(base // 16, (P_SLOT - WIN) // 16).T.reshape(-1)
    pos128 = jnp.pad(pos_t.T, ((0, 0), (0, LANE - N_EXP)), constant_values=-1)
    meta = (se.astype(i32), nt.astype(i32), sblk.astype(i32), clo.astype(i32), chi.astype(i32))
    return meta, pos_t.reshape(N_EXP, 1, M_ALL), pos128, off.astype(i32)


def _moe(h, x, mod, gates, sel, wg, wu, wd, l):
    meta, pos_t, pos, off = _moe_plan(sel)
    slots = _moe_experts(h, pos_t, meta, wg, wu, wd, l // 2)
    return _moe_combine(x, mod, gates, pos, off, slots, l)


def _final_kernel(tm, x_ref, g_ref, oc_ref, ol_ref):
    y = _rms(x_ref[...]) * g_ref[...]
    is_lat = pl.program_id(0) >= M_CTX // tm

    @pl.when(jnp.logical_not(is_lat))
    def _():
        oc_ref[...] = y

    @pl.when(is_lat)
    def _():
        ol_ref[...] = y


def _final_norm(x, g):
    tm = 1024
    return pl.pallas_call(
        functools.partial(_final_kernel, tm),
        out_shape=[jax.ShapeDtypeStruct((M_CTX, D), F32), jax.ShapeDtypeStruct((M_LAT, D), F32)],
        grid=(M_ALL // tm,),
        in_specs=[pl.BlockSpec((tm, D), lambda i: (i, 0)), pl.BlockSpec((1, D), lambda i: (0, 0))],
        out_specs=_row_split_specs(tm, D),
        compiler_params=_cparams("arbitrary"),
        name="final_norm",
    )(x, g)


def kernel(x_prompt, x_sample, state_gla, state_ret, cache_mla_ckv, cache_mla_krope, cache_gqa_k, cache_gqa_v,
           c, c_ctx, norm1_g, norm2_g, final_norm_g, w_mod, b_mod, w_in, w_out, gla_gate_w, gla_gate_b,
           mla_q_norm_g, mla_w_q_up, mla_kv_norm_g, mla_w_kv_up, ret_decay, gqa_sink,
           ffn_w_gate, ffn_w_up, ffn_w_down, moe_router, moe_w_gate, moe_w_up, moe_w_down):
    xs = (x_prompt.reshape(M_CTX, D), x_sample.reshape(M_LAT, D))

    cond = jnp.concatenate([c_ctx[None], c, jnp.zeros((8 - 1 - N_LAT, D), F32)], axis=0)
    mod = _modulation(cond, w_mod, b_mod)
    mod = mod[:, :1 + N_LAT].reshape(DEPTH, (1 + N_LAT) * 6, 1, D)

    w_out_b = w_out.astype(BF16)
    wq = mla_w_q_up.reshape(DEPTH, MLA_QR, MLA_H, MLA_NOPE + MLA_ROPE)
    wq = jnp.concatenate([wq[..., :MLA_NOPE].reshape(DEPTH, MLA_QR, MLA_H * MLA_NOPE),
                          wq[..., MLA_NOPE:].reshape(DEPTH, MLA_QR, MLA_H * MLA_ROPE)], axis=-1).astype(BF16)
    wkv = mla_w_kv_up.reshape(DEPTH, MLA_KVR, MLA_H, MLA_NOPE + MLA_DV)
    wkv = jnp.concatenate([wkv[..., :MLA_NOPE].reshape(DEPTH, MLA_KVR, MLA_H * MLA_NOPE),
                           wkv[..., MLA_NOPE:].reshape(DEPTH, MLA_KVR, MLA_H * MLA_DV)], axis=-1).astype(BF16)
    router = jnp.pad(moe_router, ((0, 0), (0, 0), (0, LANE - N_EXP))).astype(BF16)
    dec_lanes = jnp.repeat(ret_decay, RET_DV, axis=-1)
    sink_lanes = jnp.repeat(gqa_sink, GQA_HD, axis=-1).reshape(DEPTH, 1, 256)
    gate_b = gla_gate_b.reshape(DEPTH, 2, 1, GLA_H * GLA_DK)
    g1, g2 = norm1_g.reshape(DEPTH, 1, D), norm2_g.reshape(DEPTH, 1, D)
    gq, gkv = mla_q_norm_g.reshape(DEPTH, 1, MLA_QR), mla_kv_norm_g.reshape(DEPTH, 1, MLA_KVR)
    eye = jnp.eye(GLA_H, dtype=F32)
    s0_gla = jnp.einsum('bldhkv,hg->bldhvgk', state_gla, eye).reshape(
        N_LAT, DEPTH, 2, GLA_H * GLA_DV, GLA_H * GLA_DK)
    s0_ret = jnp.einsum('bldhkv,hg->bldhkgv', state_ret, jnp.eye(RET_H, dtype=F32)).reshape(
        N_LAT, DEPTH, 2, RET_H * RET_DK, RET_H * RET_DV)
    cache_kr = jnp.pad(cache_mla_krope, ((0, 0), (0, 0), (0, 0), (0, 128 - MLA_ROPE)))

    rope64 = _rope_tables(T_LAT, 64, 256)
    rope32_q = _rope_tables(T_LAT, 32, 128)
    ck, sl, sh = _rope_tables(T_LAT, 32, 128)
    live = jnp.asarray((np.arange(128) < MLA_ROPE).astype(np.float32))[None]
    rope32_k = (ck * live, sl * live, sh * live)

    st_gla = st_ret = caches_mla = caches_gqa = None
    for l in range(DEPTH):
        p_gla, p_mla, p_ret, p_gqa = _inproj(xs, g1, mod, w_in, l)

        o_gla_c, st_gla, o_mla_c, ckv_all, kr_all, o_ret_c, st_ret, o_gqa_c, gk_all, gv_all = _fused_call([
            _gla(p_gla, gla_gate_w, gate_b, None, st_gla, l, False),
            _mla(p_mla, gq, gkv, wq, wkv, None, None, None, None, caches_mla, l, False),
            _ret(p_ret, dec_lanes, None, None, st_ret, l, False),
            _gqa(p_gqa, sink_lanes, None, None, None, caches_gqa, l, False)], N_CTX, "mixers_context")
        caches_mla, caches_gqa = (ckv_all, kr_all), (gk_all, gv_all)
        o_gla_s, o_ret_s = _fused_call([
            _gla(p_gla, gla_gate_w, gate_b, s0_gla, None, l, True),
            _ret(p_ret, dec_lanes, rope64, s0_ret, None, l, True)], N_LAT, "mixers_latent_scan")
        o_mla_s, o_gqa_s = _fused_call([
            _mla(p_mla, gq, gkv, wq, wkv, rope32_q, rope32_k, cache_mla_ckv, cache_kr, None, l, True),
            _gqa(p_gqa, sink_lanes, rope64, cache_gqa_k, cache_gqa_v, None, l, True)], N_LAT, "mixers_latent_attn")
        o_ctx = (o_gla_c, o_mla_c, o_ret_c, o_gqa_c)
        o_lat = (o_gla_s, o_mla_s, o_ret_s, o_gqa_s)
        if l % 2 == 0:
            x, h2 = _outproj(xs, o_ctx, o_lat, w_out_b, g2, mod, None, l)
            x = _ffn(h2, x, mod, ffn_w_gate, ffn_w_up, ffn_w_down, l)
        else:
            x, h2, gates, sel = _outproj(xs, o_ctx, o_lat, w_out_b, g2, mod, router, l)
            x = _moe(h2, x, mod, gates, sel, moe_w_gate, moe_w_up, moe_w_down, l)
        xs = (x,)

    y_ctx, y_lat = _final_norm(x, final_norm_g[None])
    return (y_ctx.reshape(N_CTX, T_CTX, D), y_lat.reshape(N_LAT, T_LAT, D), jnp.swapaxes(st_gla, -1, -2), st_ret,
            *caches_mla, *caches_gqa)
```

```python
import functools

import numpy as np
import jax
import jax.numpy as jnp
from jax import lax
from jax.experimental import pallas as pl
from jax.experimental.pallas import tpu as pltpu

F32 = jnp.float32
BF16 = jnp.bfloat16
HIGHEST = lax.Precision.HIGHEST

D = 1024
N_CTX, T_CTX = 16, 256
N_LAT, T_LAT = 2, 1024
PAST = 256
DEPTH = 4
M_CTX = N_CTX * T_CTX
M_LAT = N_LAT * T_LAT
M_ALL = M_CTX + M_LAT
GRID_W = 64
ROPE_BASE = 10000.0
EPS = 1e-6

GLA_H, GLA_DK, GLA_DV, GLA_RANK, GLA_NORM, GLA_C = 4, 32, 64, 16, 16.0, 64
MLA_H, MLA_QR, MLA_KVR, MLA_NOPE, MLA_ROPE, MLA_DV = 4, 256, 128, 64, 32, 64
RET_H, RET_DK, RET_DV = 4, 64, 64
GQA_H, GQA_KV, GQA_HD, WINDOW = 4, 2, 64, 128
D_FF, N_EXP, D_FFE = 2816, 8, 3584

W_GLA, W_MLA, W_RET, W_GQA = 896, 512, 1024, 512
IN_WIDTH = 2752
IN_GROUPS = ((0, 800), (800, 1216), (1216, 2240), (2240, 2752))
LANE = 128
NEG = -1e30
QB = 256
QL = 128
VMEM_LIMIT = 56 * 1024 * 1024


def _cparams(*sem):
    return pltpu.CompilerParams(dimension_semantics=sem, vmem_limit_bytes=VMEM_LIMIT)


def _mm(a, b):
    return jnp.dot(a.astype(BF16), b.astype(BF16), preferred_element_type=F32)


def _mm_nt(a, b):
    return lax.dot_general(a.astype(BF16), b.astype(BF16), (((1,), (1,)), ((), ())), preferred_element_type=F32)


def _mm_tn(a, b):
    return lax.dot_general(a.astype(BF16), b.astype(BF16), (((0,), (0,)), ((), ())), preferred_element_type=F32)


def _mm_f32(a, b):
    return jnp.dot(a, b, precision=HIGHEST, preferred_element_type=F32)


def _silu(x):
    return x * (1.0 / (1.0 + jnp.exp(-x)))


def _log_sigmoid(x):
    return jnp.minimum(x, 0.0) - jnp.log1p(jnp.exp(-jnp.abs(x)))


def _rms(x):
    return x * lax.rsqrt(jnp.mean(x * x, axis=-1, keepdims=True) + EPS)


def _mod_row(tile, tm):
    return jnp.maximum((tile * tm) // T_LAT - (M_CTX // T_LAT - 1), 0)


def _mod_kernel(c_ref, w_ref, b_ref, o_ref):
    o_ref[...] = _mm(_silu(c_ref[...]), w_ref[...]) + b_ref[...]


def _modulation(cond, w_mod, b_mod):
    tn = 1536
    return pl.pallas_call(
        _mod_kernel,
        out_shape=jax.ShapeDtypeStruct((DEPTH, 8, 6 * D), F32),
        grid=(DEPTH, 6 * D // tn),
        in_specs=[pl.BlockSpec((8, D), lambda l, j: (0, 0)),
                  pl.BlockSpec((None, D, tn), lambda l, j: (l, 0, j)),
                  pl.BlockSpec((None, 1, tn), lambda l, j: (l, 0, j))],
        out_specs=pl.BlockSpec((None, 8, tn), lambda l, j: (l, 0, j)),
        compiler_params=_cparams("parallel", "parallel"),
        name="modulation",
    )(cond, w_mod, b_mod.reshape(DEPTH, 1, 6 * D))


def _inproj_kernel(split, tm, *refs):
    if split:
        xc_ref, xl_ref = refs[0:2]
        x = jnp.where(pl.program_id(0) >= M_CTX // tm, xl_ref[...], xc_ref[...])
        refs = refs[2:]
    else:
        x = refs[0][...]
        refs = refs[1:]
    g_ref, sh_ref, sc_ref, w_ref = refs[0:4]
    out_refs, w_s = refs[4:-1], refs[-1]

    @pl.when(pl.program_id(0) == 0)
    def _():
        dst = 0
        for (lo, hi), o_ref in zip(IN_GROUPS, out_refs):
            width = o_ref.shape[-1]
            w_s[:, dst:dst + hi - lo] = w_ref[:, lo:hi].astype(BF16)
            if width > hi - lo:
                w_s[:, dst + hi - lo:dst + width] = jnp.zeros((D, width - (hi - lo)), BF16)
            dst += width

    h = (_rms(x) * g_ref[...] * (1.0 + sc_ref[...]) + sh_ref[...]).astype(BF16)
    col = 0
    for o_ref in out_refs:
        width = o_ref.shape[-1]
        o_ref[...] = jnp.dot(h, w_s[:, col:col + width], preferred_element_type=F32)
        col += width


def _mod_spec(l, j, tm):
    return pl.BlockSpec((None, None, 1, D), lambda i, *_: (l, _mod_row(i, tm) * 6 + j, 0, 0))


def _layer_spec(l, *shape):
    return pl.BlockSpec((None,) + shape, lambda *_: (l,) + (0,) * len(shape))


def _stacked_output(prev, tail, l, nseq, in_specs, args, out_shape, out_specs, aliases):
    out_shape.append(jax.ShapeDtypeStruct((nseq, DEPTH) + tail, F32))
    out_specs.append(pl.BlockSpec((None, None) + tail, lambda b: (b, l) + (0,) * len(tail)))
    if prev is not None:
        in_specs.append(pl.BlockSpec(memory_space=pl.ANY))
        args.append(prev)
        aliases[len(args) - 1] = len(out_shape) - 1


def _row_split_specs(tm, width):
    n_ctx = M_CTX // tm
    return [pl.BlockSpec((tm, width), lambda i: (jnp.minimum(i, n_ctx - 1), 0)),
            pl.BlockSpec((tm, width), lambda i: (jnp.maximum(i - n_ctx, 0), 0))]


def _inproj(xs, g1, mod, w_in, l):
    tm = 512
    split = len(xs) == 2
    rows = lambda w: pl.BlockSpec((tm, w), lambda i: (i, 0))
    widths = (W_GLA, W_MLA, W_RET, W_GQA)
    x_specs = _row_split_specs(tm, D) if split else [rows(D)]
    w_spec = pl.BlockSpec((None, D, IN_WIDTH), lambda i: (l, 0, 0), pipeline_mode=pl.Buffered(1))
    return pl.pallas_call(
        functools.partial(_inproj_kernel, split, tm),
        out_shape=[jax.ShapeDtypeStruct((M_ALL, w), F32) for w in widths],
        grid=(M_ALL // tm,),
        in_specs=x_specs + [_layer_spec(l, 1, D), _mod_spec(l, 0, tm), _mod_spec(l, 1, tm), w_spec],
        out_specs=[rows(w) for w in widths],
        scratch_shapes=[pltpu.VMEM((D, sum(widths)), BF16)],
        compiler_params=_cparams("arbitrary"),
        name="inproj",
    )(*xs, g1, mod, mod, w_in)


def _head_rms_gate(o, gate):
    r = lax.broadcasted_iota(jnp.int32, (256, 256), 0) // 64
    c = lax.broadcasted_iota(jnp.int32, (256, 256), 1) // 64
    group_mean = jnp.where(r == c, 1.0 / 64.0, 0.0).astype(BF16)
    sq = o * o
    sq_hi = sq.astype(BF16)
    sq_lo = (sq - sq_hi.astype(F32)).astype(BF16)
    ms = (jnp.dot(sq_hi, group_mean, preferred_element_type=F32)
          + jnp.dot(sq_lo, group_mean, preferred_element_type=F32))
    return o * lax.rsqrt(ms + EPS) * _silu(gate)


def _own_lanes(n_heads, rows_per_head, lanes_per_head):
    shape = (n_heads * rows_per_head, n_heads * lanes_per_head)
    return (lax.broadcasted_iota(jnp.int32, shape, 0) // rows_per_head
            == lax.broadcasted_iota(jnp.int32, shape, 1) // lanes_per_head).astype(F32)


def _stack_heads(x, n_heads, lanes_per_head):
    return jnp.concatenate([x] * n_heads, axis=0) * _own_lanes(n_heads, x.shape[0], lanes_per_head)


def _unstack_heads(y, n_heads, lanes_per_head):
    t = y.shape[0] // n_heads
    y = y * _own_lanes(n_heads, t, lanes_per_head)
    out = y[0:t]
    for h in range(1, n_heads):
        out = out + y[h * t:(h + 1) * t]
    return out


def _per_query_head(x):
    lane = lax.broadcasted_iota(jnp.int32, x.shape, 1)
    swapped = pltpu.roll(x, GQA_HD, 1)
    return jnp.concatenate([jnp.where(lane < GQA_HD, x, swapped), jnp.where(lane < GQA_HD, swapped, x)], axis=1)


def _rope(x, cos, sin_lo, sin_hi, half):
    w = x.shape[-1]
    return x * cos + pltpu.roll(x, w - half, 1) * sin_lo + pltpu.roll(x, half, 1) * sin_hi


def _rope_tables(t, head_dim, width):
    half = head_dim // 2
    quarter = head_dim // 4
    pos = np.arange(t)
    rows = (pos // GRID_W).astype(np.float32)
    cols = (pos % GRID_W).astype(np.float32)
    inv = np.power(np.float32(ROPE_BASE), -np.arange(quarter, dtype=np.float32) / np.float32(quarter)).astype(np.float32)
    ang = np.concatenate([rows[:, None] * inv, cols[:, None] * inv], axis=-1).astype(np.float32)
    lane = np.arange(width)
    a = ang[:, lane % half]
    cos, sin = np.cos(a).astype(np.float32), np.sin(a).astype(np.float32)
    low = (lane % head_dim) < half
    return (jnp.asarray(cos), jnp.asarray(np.where(low[None], -sin, 0.0).astype(np.float32)),
            jnp.asarray(np.where(low[None], 0.0, sin).astype(np.float32)))


def _seq_spec(t, width, row0, latent):
    return pl.BlockSpec((t, width), lambda b: (row0 + b, 0), pipeline_mode=pl.Buffered(1) if latent else None)


def _table_spec(t, width):
    return pl.BlockSpec((t, width), lambda b: (0, 0), pipeline_mode=pl.Buffered(1))


def _fused_call(parts, nseq, name):
    in_specs, args, out_shape, out_specs, scratch, aliases, layout = [], [], [], [], [], {}, []
    for fn, p_in, p_args, p_shape, p_out, p_scratch, p_alias in parts:
        layout.append((fn, len(p_args), len(p_shape), len(p_scratch)))
        for k, v in p_alias.items():
            aliases[len(args) + k] = len(out_shape) + v
        in_specs += p_in
        args += p_args
        out_shape += p_shape
        out_specs += p_out
        scratch += p_scratch
    n_in, n_out = len(args), len(out_shape)

    def kernel(*refs):
        i, o, s = 0, n_in, n_in + n_out
        for fn, ni, no, ns in layout:
            fn(*refs[i:i + ni], *refs[o:o + no], *refs[s:s + ns])
            i, o, s = i + ni, o + no, s + ns

    return pl.pallas_call(
        kernel, out_shape=out_shape, grid=(nseq,), in_specs=in_specs, out_specs=out_specs, scratch_shapes=scratch,
        input_output_aliases=aliases, name=name,
        compiler_params=pltpu.CompilerParams(dimension_semantics=("arbitrary",), vmem_limit_bytes=60 * 1024 * 1024),
    )(*args)


def _gla_kernel(t, latent, *refs):
    if latent:
        p_ref, gw_ref, gb_ref, s0_ref, o_ref, la_f, la_b, of_s, ob_s, st_s = refs
    else:
        p_ref, gw_ref, gb_ref = refs[0:3]
        o_ref, st_ref, la_f, la_b, of_s, ob_s, st_s = refs[-7:]
    c = GLA_C
    gr = 256
    nc = gr // c
    n = t // gr
    hd = GLA_H * GLA_DK
    la_f[...] = _log_sigmoid(_mm_f32(p_ref[:, 768:784], gw_ref[0]) + gb_ref[0]) / GLA_NORM
    la_b[...] = _log_sigmoid(_mm_f32(p_ref[:, 784:800], gw_ref[1]) + gb_ref[1]) / GLA_NORM

    ri = lax.broadcasted_iota(jnp.int32, (gr, gr), 0)
    ci = lax.broadcasted_iota(jnp.int32, (gr, gr), 1)
    same = (ri // c) == (ci // c)
    rs = lax.broadcasted_iota(jnp.int32, (GLA_H * gr, gr), 0) % gr
    cs = lax.broadcasted_iota(jnp.int32, (GLA_H * gr, gr), 1)
    same_s = (rs // c) == (cs // c)
    own_chunk = (lax.broadcasted_iota(jnp.int32, (gr, nc * hd), 0) // c
                 == lax.broadcasted_iota(jnp.int32, (gr, nc * hd), 1) // hd).astype(F32)
    own_dk = _own_lanes(GLA_H, GLA_DV, GLA_DK)
    own_dk = jnp.concatenate([own_dk] * nc, axis=1)
    scale = GLA_DK ** -0.5
    if latent:
        st_s[...] = s0_ref[...]
    else:
        st_s[...] = jnp.zeros_like(st_s)

    def group(row0, la_ref, d):
        fwd = d == 0
        ordered = (ci <= ri) if fwd else (ci >= ri)
        first_half = (ci % c < c // 2) if fwd else (ci % c >= c // 2)
        keep = same_s & ((cs <= rs) if fwd else (cs >= rs))
        sums_of = jnp.concatenate([(same & ordered).astype(BF16), same.astype(BF16), (same & first_half).astype(BF16)],
                                  axis=0)
        q = p_ref[pl.ds(row0, gr), 0:128]
        k = p_ref[pl.ds(row0, gr), 128:256] * scale
        v = p_ref[pl.ds(row0, gr), 256:512]
        la = la_ref[pl.ds(row0, gr), :]
        la_hi = la.astype(BF16)
        la_lo = (la - la_hi.astype(F32)).astype(BF16)
        sums = jnp.dot(sums_of, la_hi, preferred_element_type=F32) + jnp.dot(sums_of, la_lo, preferred_element_type=F32)
        bc, tot, mid = sums[0:gr], sums[gr:2 * gr], sums[2 * gr:3 * gr]
        qe, ke = q * jnp.exp(bc - mid), k * jnp.exp(mid - bc)
        q_in, k_out, a = q * jnp.exp(bc), k * jnp.exp(tot - bc), jnp.exp(tot)
        att = jnp.where(keep, _mm_nt(_stack_heads(qe, GLA_H, GLA_DK), ke), 0.0)
        o = _unstack_heads(_mm(att, v), GLA_H, GLA_DV)
        kv = _mm_tn(v, jnp.concatenate([k_out] * nc, axis=1) * own_chunk) * own_dk
        st, entering = st_s[d], [None] * nc
        for cc in (range(nc) if fwd else range(nc - 1, -1, -1)):
            entering[cc] = st
            st = st * a[cc * c:cc * c + 1, :] + kv[:, cc * hd:(cc + 1) * hd]
        st_s[d] = st
        return o + _mm_nt(jnp.concatenate([q_in] * nc, axis=1) * own_chunk, jnp.concatenate(entering, axis=1))

    def body(i, carry):
        rf = pl.multiple_of(i * gr, gr)
        rb = pl.multiple_of((n - 1 - i) * gr, gr)
        of_s[pl.ds(rf, gr), :] = group(rf, la_f, 0)
        ob_s[pl.ds(rb, gr), :] = group(rb, la_b, 1)
        return carry

    lax.fori_loop(0, n, body, 0)
    if not latent:
        for d in range(2):
            for h in range(GLA_H):
                st_ref[d, h] = st_s[d, h * GLA_DV:(h + 1) * GLA_DV, h * GLA_DK:(h + 1) * GLA_DK]
    for r in range(t // QB):
        rows = slice(r * QB, (r + 1) * QB)
        o_ref[rows, :] = _head_rms_gate(of_s[rows, :] + ob_s[rows, :], p_ref[rows, 512:768]).astype(BF16)


def _gla(p, gate_w, gate_b, s0_bd, st_prev, l, latent):
    t, nseq, row0 = (T_LAT, N_LAT, M_CTX // T_LAT) if latent else (T_CTX, N_CTX, 0)
    st_shape = (2, GLA_H * GLA_DV, GLA_H * GLA_DK)
    in_specs = [_seq_spec(t, W_GLA, row0, latent),
                _layer_spec(l, 2, GLA_RANK, GLA_H * GLA_DK), _layer_spec(l, 2, 1, GLA_H * GLA_DK)]
    args = [p, gate_w, gate_b]
    aliases = {}
    o_shape = jax.ShapeDtypeStruct((nseq * t, 256), BF16)
    o_spec = pl.BlockSpec((t, 256), lambda b: (b, 0))
    if latent:
        in_specs.append(pl.BlockSpec((None, None) + st_shape, lambda b: (b, l, 0, 0, 0)))
        args.append(s0_bd)
    out_shape, out_specs = [o_shape], [o_spec]
    if not latent:
        _stacked_output(st_prev, (2, GLA_H, GLA_DV, GLA_DK), l, nseq, in_specs, args, out_shape, out_specs, aliases)
    scratch = [pltpu.VMEM((t, 128), F32), pltpu.VMEM((t, 128), F32),
               pltpu.VMEM((t, 256), F32), pltpu.VMEM((t, 256), F32), pltpu.VMEM(st_shape, F32)]
    return functools.partial(_gla_kernel, t, latent), in_specs, args, out_shape, out_specs, scratch, aliases


def _ret_kernel(t, latent, *refs):
    if latent:
        p_ref, dec_ref, cos_ref, slo_ref, shi_ref, s0_ref, o_ref, q_s, k_s, decay_s = refs
    else:
        p_ref, dec_ref = refs[0:2]
        o_ref, st_ref, decay_s = refs[-3:]
    lg = _log_sigmoid(dec_ref[...])
    scale = RET_DK ** -0.5
    nblk = t // QB

    @pl.when(pl.program_id(0) == 0)
    def _():
        wide = decay_s.shape[-1]
        dist = (lax.broadcasted_iota(jnp.int32, (QB, wide), 0) - lax.broadcasted_iota(jnp.int32, (QB, wide), 1)
                + (nblk - 1) * QB).astype(F32)
        for h in range(RET_H):
            lg_f, lg_b = lg[0:1, h * 64:h * 64 + 1], lg[1:2, h * 64:h * 64 + 1]
            decay_s[h] = (jnp.where(dist >= 0, jnp.exp(jnp.maximum(dist, 0.0) * lg_f), 0.0)
                          + jnp.where(dist <= 0, jnp.exp(jnp.maximum(-dist, 0.0) * lg_b), 0.0))

    if not latent:
        k = p_ref[:, 256:512] * scale
        v = p_ref[:, 512:768]
        decay = jnp.concatenate([decay_s[h] for h in range(RET_H)], axis=0)
        o = _unstack_heads(_mm(_mm_nt(_stack_heads(p_ref[:, 0:256], RET_H, RET_DK), k) * decay, v), RET_H, RET_DV)
        o_ref[...] = _head_rms_gate(o, p_ref[:, 768:1024]).astype(BF16)
        j = lax.broadcasted_iota(jnp.int32, (t, 1), 0).astype(F32)
        kv_f = _mm_tn(k * jnp.exp((float(t - 1) - j) * lg[0:1, :]), v)
        kv_b = _mm_tn(k * jnp.exp(j * lg[1:2, :]), v)
        for h in range(RET_H):
            st_ref[0, h] = kv_f[h * RET_DK:(h + 1) * RET_DK, h * RET_DV:(h + 1) * RET_DV]
            st_ref[1, h] = kv_b[h * RET_DK:(h + 1) * RET_DK, h * RET_DV:(h + 1) * RET_DV]
        return
    q_s[...] = _rope(p_ref[:, 0:256], cos_ref[...], slo_ref[...], shi_ref[...], RET_DK // 2)
    k_s[...] = _rope(p_ref[:, 256:512], cos_ref[...], slo_ref[...], shi_ref[...], RET_DK // 2) * scale
    for r in range(t // QL):
        rows = slice(r * QL, (r + 1) * QL)
        pos = (lax.broadcasted_iota(jnp.int32, (QL, 1), 0) + r * QL).astype(F32)
        strip_rows = slice(r * QL % QB, r * QL % QB + QL)
        off = (nblk - 1 - r * QL // QB) * QB
        decay = jnp.concatenate([decay_s[h, strip_rows, off:off + t] for h in range(RET_H)], axis=0)
        q = q_s[rows, :]
        o = _unstack_heads(_mm(_mm_nt(_stack_heads(q, RET_H, RET_DK), k_s[...]) * decay, p_ref[:, 512:768]),
                           RET_H, RET_DV)
        o = o + _mm(q * jnp.exp((pos + 1.0) * lg[0:1, :]), s0_ref[0])
        o = o + _mm(q * jnp.exp((float(t) - pos) * lg[1:2, :]), s0_ref[1])
        o_ref[rows, :] = _head_rms_gate(o, p_ref[rows, 768:1024]).astype(BF16)


def _ret(p, dec_lanes, tables, s0, st_prev, l, latent):
    t, nseq, row0 = (T_LAT, N_LAT, M_CTX // T_LAT) if latent else (T_CTX, N_CTX, 0)
    st_shape = (2, RET_H, RET_DK, RET_DV)
    in_specs = [_seq_spec(t, W_RET, row0, latent), _layer_spec(l, 2, 256)]
    args = [p, dec_lanes]
    aliases = {}
    o_shape = jax.ShapeDtypeStruct((nseq * t, 256), BF16)
    o_spec = pl.BlockSpec((t, 256), lambda b: (b, 0))
    scratch = []
    if latent:
        in_specs += [_table_spec(t, 256)] * 3
        in_specs.append(pl.BlockSpec((None, None, 2, RET_H * RET_DK, RET_H * RET_DV), lambda b: (b, l, 0, 0, 0)))
        args += list(tables) + [s0]
        scratch = [pltpu.VMEM((t, 256), F32), pltpu.VMEM((t, 256), F32)]
    scratch.append(pltpu.VMEM((RET_H, QB, 2 * t - QB), F32))
    out_shape, out_specs = [o_shape], [o_spec]
    if not latent:
        _stacked_output(st_prev, st_shape, l, nseq, in_specs, args, out_shape, out_specs, aliases)
    return functools.partial(_ret_kernel, t, latent), in_specs, args, out_shape, out_specs, scratch, aliases


def _mla_kernel(t, latent, *refs):
    if latent:
        (p_ref, gq_ref, gkv_ref, wq_ref, wkv_ref, cq_ref, slq_ref, shq_ref, ck_ref, slk_ref, shk_ref,
         cckv_ref, ckr_ref, o_ref, qn_s, qr_s, kn_s, kr_s, v_s) = refs
    else:
        p_ref, gq_ref, gkv_ref, wq_ref, wkv_ref = refs[0:5]
        o_ref, ckv_ref, kro_ref, qn_s, qr_s, kn_s, kr_s, v_s = refs[-8:]
    nk = t + (PAST if latent else 0)
    qh = _mm(_rms(p_ref[:, 0:256]) * gq_ref[...], wq_ref[...])
    ckv = _rms(p_ref[:, 256:384]) * gkv_ref[...]
    kv = _mm(ckv, wkv_ref[...])
    qn_s[...] = qh[:, 0:256]
    kn_s[0:t, :] = kv[:, 0:256]
    v_s[0:t, :] = kv[:, 256:512]
    if latent:
        qr_s[...] = _rope(qh[:, 256:384], cq_ref[...], slq_ref[...], shq_ref[...], MLA_ROPE // 2)
        kr_s[0:t, :] = _rope(p_ref[:, 384:512], ck_ref[...], slk_ref[...], shk_ref[...], MLA_ROPE // 2)
        kvc = _mm(cckv_ref[...], wkv_ref[...])
        kn_s[t:nk, :] = kvc[:, 0:256]
        v_s[t:nk, :] = kvc[:, 256:512]
        kr_s[t:nk, :] = ckr_ref[...]
    scale = (MLA_NOPE + MLA_ROPE) ** -0.5
    if not latent:
        ckv_ref[...] = ckv
        kro_ref[...] = p_ref[:, 384:384 + MLA_ROPE]
        kr = p_ref[:, 384:512]
        kr4 = kr + pltpu.roll(kr, MLA_ROPE, 1) + pltpu.roll(kr, 2 * MLA_ROPE, 1) + pltpu.roll(kr, 3 * MLA_ROPE, 1)
        s = (_mm_nt(_stack_heads(qh[:, 0:256], MLA_H, MLA_NOPE), kv[:, 0:256])
             + _mm_nt(_stack_heads(qh[:, 256:384], MLA_H, MLA_ROPE), kr4)) * scale
        e = jnp.exp(s - jnp.max(s, axis=-1, keepdims=True))
        o = _mm(e, kv[:, 256:512]) / jnp.sum(e, axis=-1, keepdims=True)
        o_ref[...] = _unstack_heads(o, MLA_H, MLA_DV).astype(BF16)
        return
    kr = kr_s[...]
    kr4 = kr + pltpu.roll(kr, MLA_ROPE, 1) + pltpu.roll(kr, 2 * MLA_ROPE, 1) + pltpu.roll(kr, 3 * MLA_ROPE, 1)
    for r in range(t // QL):
        rows = slice(r * QL, (r + 1) * QL)
        s = (_mm_nt(_stack_heads(qn_s[rows, :], MLA_H, MLA_NOPE), kn_s[...])
             + _mm_nt(_stack_heads(qr_s[rows, :], MLA_H, MLA_ROPE), kr4)) * scale
        e = jnp.exp(s - jnp.max(s, axis=-1, keepdims=True))
        o = _mm(e, v_s[...]) / jnp.sum(e, axis=-1, keepdims=True)
        o_ref[rows, :] = _unstack_heads(o, MLA_H, MLA_DV).astype(BF16)


def _mla(p, gq, gkv, wq, wkv, tables_q, tables_k, cache_ckv, cache_kr, prev, l, latent):
    t, nseq, row0 = (T_LAT, N_LAT, M_CTX // T_LAT) if latent else (T_CTX, N_CTX, 0)
    nk = t + (PAST if latent else 0)
    in_specs = [_seq_spec(t, W_MLA, row0, latent), _layer_spec(l, 1, MLA_QR),
                _layer_spec(l, 1, MLA_KVR), _layer_spec(l, MLA_QR, 384), _layer_spec(l, MLA_KVR, 512)]
    args = [p, gq, gkv, wq, wkv]
    aliases = {}
    o_shape = jax.ShapeDtypeStruct((nseq * t, 256), BF16)
    o_spec = pl.BlockSpec((t, 256), lambda b: (b, 0))
    if latent:
        in_specs += [_table_spec(t, 128)] * 6
        in_specs += [pl.BlockSpec((None, None, PAST, MLA_KVR), lambda b: (b, l, 0, 0)),
                     pl.BlockSpec((None, None, PAST, 128), lambda b: (b, l, 0, 0))]
        args += list(tables_q) + list(tables_k) + [cache_ckv, cache_kr]
    out_shape, out_specs = [o_shape], [o_spec]
    if not latent:
        prev = prev or (None, None)
        _stacked_output(prev[0], (t, MLA_KVR), l, nseq, in_specs, args, out_shape, out_specs, aliases)
        _stacked_output(prev[1], (t, MLA_ROPE), l, nseq, in_specs, args, out_shape, out_specs, aliases)
    scratch = [pltpu.VMEM((t, 256), F32), pltpu.VMEM((t, 128), F32), pltpu.VMEM((nk, 256), F32),
               pltpu.VMEM((nk, 128), F32), pltpu.VMEM((nk, 256), F32)]
    return functools.partial(_mla_kernel, t, latent), in_specs, args, out_shape, out_specs, scratch, aliases


def _gqa_kernel(t, latent, *refs):
    if latent:
        p_ref, sink_ref, cos_ref, slo_ref, shi_ref, ck_ref, cv_ref, o_ref, q_s, k_s = refs
    else:
        p_ref, sink_ref = refs[0:2]
        o_ref, ko_ref, vo_ref = refs[-3:]
        for kvh in range(GQA_KV):
            ko_ref[kvh] = p_ref[:, 256 + kvh * GQA_HD:256 + (kvh + 1) * GQA_HD]
            vo_ref[kvh] = p_ref[:, 384 + kvh * GQA_HD:384 + (kvh + 1) * GQA_HD]
    scale = GQA_HD ** -0.5
    if not latent:
        own = _own_lanes(GQA_H, t, GQA_HD)
        sink = jnp.max(jnp.where(own > 0, sink_ref[...], NEG), axis=-1, keepdims=True)
        s = _mm_nt(_stack_heads(p_ref[:, 0:256], GQA_H, GQA_HD), _per_query_head(p_ref[:, 256:384])) * scale
        m = jnp.maximum(jnp.max(s, axis=-1, keepdims=True), sink)
        e = jnp.exp(s - m)
        o = _mm(e, _per_query_head(p_ref[:, 384:512])) / (jnp.sum(e, axis=-1, keepdims=True) + jnp.exp(sink - m))
        o_ref[...] = _unstack_heads(o, GQA_H, GQA_HD).astype(BF16)
        return
    q_s[...] = _rope(p_ref[:, 0:256], cos_ref[...], slo_ref[...], shi_ref[...], GQA_HD // 2)
    k_s[...] = _rope(p_ref[:, 256:384], cos_ref[:, 0:128], slo_ref[:, 0:128], shi_ref[:, 0:128], GQA_HD // 2)
    kc = _per_query_head(jnp.concatenate([ck_ref[0], ck_ref[1]], axis=1))
    vc = _per_query_head(jnp.concatenate([cv_ref[0], cv_ref[1]], axis=1))
    sink = jnp.max(jnp.where(_own_lanes(GQA_H, QL, GQA_HD) > 0, sink_ref[...], NEG), axis=-1, keepdims=True)
    for r in range(t // QL):
        rows = slice(r * QL, (r + 1) * QL)
        keys = slice(max(0, r * QL - WINDOW), min(t, (r + 1) * QL + WINDOW))
        nkeys = keys.stop - keys.start
        row = lax.broadcasted_iota(jnp.int32, (GQA_H * QL, nkeys), 0) % QL + r * QL
        col = lax.broadcasted_iota(jnp.int32, (GQA_H * QL, nkeys), 1) + keys.start
        near = jnp.abs(row - col) <= WINDOW
        q = _stack_heads(q_s[rows, :], GQA_H, GQA_HD)
        s_loc = jnp.where(near, _mm_nt(q, _per_query_head(k_s[keys, :])) * scale, NEG)
        s_ctx = _mm_nt(q, kc) * scale
        m = jnp.maximum(jnp.maximum(jnp.max(s_loc, axis=-1, keepdims=True), jnp.max(s_ctx, axis=-1, keepdims=True)),
                        sink)
        e_loc, e_ctx = jnp.exp(s_loc - m), jnp.exp(s_ctx - m)
        den = jnp.sum(e_loc, axis=-1, keepdims=True) + jnp.sum(e_ctx, axis=-1, keepdims=True) + jnp.exp(sink - m)
        o = (_mm(e_loc, _per_query_head(p_ref[keys, 384:512])) + _mm(e_ctx, vc)) / den
        o_ref[rows, :] = _unstack_heads(o, GQA_H, GQA_HD).astype(BF16)


def _gqa(p, sink_lanes, tables, cache_k, cache_v, prev, l, latent):
    t, nseq, row0 = (T_LAT, N_LAT, M_CTX // T_LAT) if latent else (T_CTX, N_CTX, 0)
    in_specs = [_seq_spec(t, W_GQA, row0, latent), _layer_spec(l, 1, 256)]
    args = [p, sink_lanes]
    scratch = []
    aliases = {}
    out_shape = [jax.ShapeDtypeStruct((nseq * t, 256), BF16)]
    out_specs = [pl.BlockSpec((t, 256), lambda b: (b, 0))]
    if latent:
        in_specs += [_table_spec(t, 256)] * 3
        in_specs += [pl.BlockSpec((None, None, GQA_KV, PAST, GQA_HD), lambda b: (b, l, 0, 0, 0))] * 2
        args += list(tables) + [cache_k, cache_v]
        scratch = [pltpu.VMEM((t, 256), F32), pltpu.VMEM((t, 128), F32)]
    else:
        prev = prev or (None, None)
        for pv in prev:
            _stacked_output(pv, (GQA_KV, t, GQA_HD), l, nseq, in_specs, args, out_shape, out_specs, aliases)
    return functools.partial(_gqa_kernel, t, latent), in_specs, args, out_shape, out_specs, scratch, aliases


def _outproj_kernel(route, split, tm, *refs):
    is_lat = pl.program_id(0) >= M_CTX // tm
    if split:
        x_in = jnp.where(is_lat, refs[1][...], refs[0][...])
        refs = refs[2:]
    else:
        x_in = refs[0][...]
        refs = refs[1:]
    ctx_refs, lat_refs = refs[0:4], refs[4:8]
    w_ref, gt_ref, g_ref, sh_ref, sc_ref = refs[8:13]
    refs = refs[13:]
    mix = jnp.zeros((tm, D), F32)
    for m in range(4):
        o = jnp.where(is_lat, lat_refs[m][...], ctx_refs[m][...])
        mix = mix + jnp.dot(o, w_ref[m * 256:(m + 1) * 256, :], preferred_element_type=F32)
    x = x_in + gt_ref[...] * mix
    h = (_rms(x) * g_ref[...] * (1.0 + sc_ref[...]) + sh_ref[...]).astype(BF16)
    if not route:
        xo_ref, h_ref = refs
    else:
        r_ref, xo_ref, h_ref, gate_ref, sel_ref = refs
        lane = lax.broadcasted_iota(jnp.int32, gate_ref.shape, 1)
        logits = jnp.where(lane < N_EXP, jnp.dot(h, r_ref[...], preferred_element_type=F32), NEG)
        m1 = jnp.max(logits, axis=-1, keepdims=True)
        i1 = jnp.min(jnp.where(logits == m1, lane, LANE), axis=-1, keepdims=True)
        rest = jnp.where(lane == i1, NEG, logits)
        m2 = jnp.max(rest, axis=-1, keepdims=True)
        i2 = jnp.min(jnp.where(rest == m2, lane, LANE), axis=-1, keepdims=True)
        e2 = jnp.exp(m2 - m1)
        gate_ref[...] = jnp.where(lane == i1, 1.0 / (1.0 + e2), 0.0) + jnp.where(lane == i2, e2 / (1.0 + e2), 0.0)
        sel_ref[...] = jnp.where((lane == i1) | (lane == i2), 1, 0)
    xo_ref[...] = x
    h_ref[...] = h


def _outproj(xs, o_ctx, o_lat, w_out, g2, mod, router, l):
    tm = 512
    route = router is not None
    split = len(xs) == 2
    rows = lambda w: pl.BlockSpec((tm, w), lambda i: (i, 0))
    ctx_spec, lat_spec = _row_split_specs(tm, 256)
    x_specs = _row_split_specs(tm, D) if split else [rows(D)]
    in_specs = x_specs + [ctx_spec] * 4 + [lat_spec] * 4 + [
        _layer_spec(l, D, D), _mod_spec(l, 2, tm), _layer_spec(l, 1, D), _mod_spec(l, 3, tm), _mod_spec(l, 4, tm)]
    args = [*xs, *o_ctx, *o_lat, w_out, mod, g2, mod, mod]
    out_shape = [jax.ShapeDtypeStruct((M_ALL, D), F32), jax.ShapeDtypeStruct((M_ALL, D), BF16)]
    out_specs = [rows(D), rows(D)]
    if route:
        in_specs.append(_layer_spec(l // 2, D, LANE))
        args.append(router)
        out_shape += [jax.ShapeDtypeStruct((M_ALL, LANE), F32), jax.ShapeDtypeStruct((M_ALL, LANE), jnp.int32)]
        out_specs += [rows(LANE), rows(LANE)]
    return pl.pallas_call(
        functools.partial(_outproj_kernel, route, split, tm),
        out_shape=out_shape, grid=(M_ALL // tm,), in_specs=in_specs, out_specs=out_specs,
        compiler_params=_cparams("parallel"),
        name="outproj_route" if route else "outproj",
    )(*args)


def _ffn_kernel(h_ref, x_ref, gt_ref, wg_ref, wu_ref, wd_ref, o_ref, acc_ref):
    f = pl.program_id(1)

    @pl.when(f == 0)
    def _():
        acc_ref[...] = jnp.zeros_like(acc_ref)

    wg, wu, wd = wg_ref[...].astype(BF16), wu_ref[...].astype(BF16), wd_ref[...].astype(BF16)
    half = h_ref.shape[0] // 2
    for rows in (slice(0, half), slice(half, 2 * half)):
        h = h_ref[rows, :]
        g = jnp.dot(h, wg, preferred_element_type=F32)
        u = jnp.dot(h, wu, preferred_element_type=F32)
        acc_ref[rows, :] += jnp.dot((_silu(g) * u).astype(BF16), wd, preferred_element_type=F32)

    @pl.when(f == pl.num_programs(1) - 1)
    def _():
        o_ref[...] = x_ref[...] + gt_ref[...] * acc_ref[...]


def _ffn(h, x, mod, wg, wu, wd, l):
    tm, tf = 1024, 256
    j = l // 2
    return pl.pallas_call(
        _ffn_kernel,
        out_shape=jax.ShapeDtypeStruct((M_ALL, D), F32),
        grid=(M_ALL // tm, D_FF // tf),
        in_specs=[pl.BlockSpec((tm, D), lambda i, f: (i, 0)), pl.BlockSpec((tm, D), lambda i, f: (i, 0)),
                  _mod_spec(l, 5, tm),
                  pl.BlockSpec((None, D, tf), lambda i, f: (j, 0, f)),
                  pl.BlockSpec((None, D, tf), lambda i, f: (j, 0, f)),
                  pl.BlockSpec((None, tf, D), lambda i, f: (j, f, 0))],
        out_specs=pl.BlockSpec((tm, D), lambda i, f: (i, 0)),
        scratch_shapes=[pltpu.VMEM((tm, D), F32)],
        compiler_params=_cparams("parallel", "arbitrary"),
        name="ffn_dense",
    )(h, x, mod, wg, wu, wd)


SUP, SUB, CHUNK = 2048, 256, 256
N_SUB = SUP // SUB
S_MAX = 2 * M_ALL // SUP + N_EXP
P_SLOT = S_MAX * SUP
TC = 128
WIN = TC + 16


def _moe_expert_kernel(se_ref, nt_ref, sblk_ref, clo_ref, chi_ref, pos_ref, h_ref, wg_ref, wu_ref, wd_ref, o_ref,
                       xs_s, acc_s, wg_s, wu_s, wd_s):
    s = pl.program_id(0)
    f = pl.program_id(1)
    n = nt_ref[s]

    @pl.when((f == 0) & (n == 0))
    def _():
        o_ref[...] = jnp.zeros_like(o_ref)

    @pl.when((f == 0) & (n > 0))
    def _():
        row = lax.broadcasted_iota(jnp.int32, (SUB, CHUNK), 0)

        def gather(j, carry):
            row0 = (s * N_SUB + j) * SUB
            acc_s[j] = jnp.zeros((SUB, D), F32)

            def chunk(c, carry):
                tpos = pos_ref[:, pl.ds(pl.multiple_of(c * CHUNK, CHUNK), CHUNK)]
                onehot = jnp.where(tpos - row0 == row, 1.0, 0.0).astype(BF16)
                rows = h_ref[pl.ds(pl.multiple_of(c * CHUNK, CHUNK), CHUNK), :]
                acc_s[j] += jnp.dot(onehot, rows, preferred_element_type=F32)
                return carry

            g = s * N_SUB + j
            lax.fori_loop(clo_ref[g], chi_ref[g] + 1, chunk, 0)
            xs_s[j] = acc_s[j].astype(BF16)
            acc_s[j] = jnp.zeros((SUB, D), F32)
            return carry

        lax.fori_loop(0, n, gather, 0)

    @pl.when(n > 0)
    def _():
        wg_s[...] = wg_ref[...].astype(BF16)
        wu_s[...] = wu_ref[...].astype(BF16)
        wd_s[...] = wd_ref[...].astype(BF16)

        def sub(j):
            x = xs_s[j]
            g = jnp.dot(x, wg_s[...], preferred_element_type=F32)
            u = jnp.dot(x, wu_s[...], preferred_element_type=F32)
            acc_s[j] += jnp.dot((_silu(g) * u).astype(BF16), wd_s[...], preferred_element_type=F32)

        def pair(jj, carry):
            sub(2 * jj)
            sub(2 * jj + 1)
            return carry

        lax.fori_loop(0, n // 2, pair, 0)

        @pl.when(n % 2 == 1)
        def _():
            sub(n - 1)

    @pl.when((f == pl.num_programs(1) - 1) & (n > 0))
    def _():
        for j in range(N_SUB):
            rows = slice(j * SUB, (j + 1) * SUB)

            @pl.when(j < n)
            def _():
                o_ref[rows, :] = acc_s[j].astype(BF16)

            @pl.when(j >= n)
            def _():
                o_ref[rows, :] = jnp.zeros((SUB, D), BF16)


def _moe_experts(h, pos_t, meta, wg, wu, wd, layer):
    tf = 512
    nf = D_FFE // tf
    se, nt, sblk, clo, chi = meta

    def w_up(s, f, se, nt, sblk, clo, chi):
        return (layer, se[s], 0, jnp.where(nt[s] > 0, f, nf - 1))

    def w_down(s, f, se, nt, sblk, clo, chi):
        return (layer, se[s], jnp.where(nt[s] > 0, f, nf - 1), 0)

    grid_spec = pltpu.PrefetchScalarGridSpec(
        num_scalar_prefetch=5,
        grid=(S_MAX, nf),
        in_specs=[pl.BlockSpec((None, 1, M_ALL), lambda s, f, se, nt, sblk, clo, chi: (se[s], 0, 0)),
                  pl.BlockSpec((M_ALL, D), lambda s, f, *_: (0, 0), pipeline_mode=pl.Buffered(1)),
                  pl.BlockSpec((None, None, D, tf), w_up), pl.BlockSpec((None, None, D, tf), w_up),
                  pl.BlockSpec((None, None, tf, D), w_down)],
        out_specs=pl.BlockSpec((SUP, D), lambda s, f, *_: (s, 0)),
        scratch_shapes=[pltpu.VMEM((N_SUB, SUB, D), BF16), pltpu.VMEM((N_SUB, SUB, D), F32),
                        pltpu.VMEM((D, tf), BF16), pltpu.VMEM((D, tf), BF16), pltpu.VMEM((tf, D), BF16)],
    )
    return pl.pallas_call(
        _moe_expert_kernel,
        out_shape=jax.ShapeDtypeStruct((P_SLOT, D), BF16),
        grid_spec=grid_spec,
        compiler_params=pltpu.CompilerParams(dimension_semantics=("arbitrary", "arbitrary"),
                                             vmem_limit_bytes=60 * 1024 * 1024),
        name="moe_experts",
    )(se, nt, sblk, clo, chi, pos_t, h, wg, wu, wd)


def _moe_combine_kernel(off_ref, x_ref, gt_ref, gate_ref, pos_ref, *refs):
    win_refs, o_ref = refs[:N_EXP], refs[N_EXP]
    i = pl.program_id(0)
    lane = lax.broadcasted_iota(jnp.int32, (TC, WIN), 1)
    y = jnp.zeros((TC, D), F32)
    for e in range(N_EXP):
        rel = pos_ref[:, e:e + 1] - off_ref[i * N_EXP + e] * 16
        onehot = jnp.where(rel == lane, 1.0, 0.0).astype(BF16)
        y = y + gate_ref[:, e:e + 1] * jnp.dot(onehot, win_refs[e][...], preferred_element_type=F32)
    o_ref[...] = x_ref[...] + gt_ref[...] * y


def _moe_combine(x, mod, gates, pos, off, slots, l):
    def win_spec(e):
        return pl.BlockSpec((pl.Element(WIN), pl.Element(D)), lambda i, off: (off[i * N_EXP + e] * 16, 0))

    grid_spec = pltpu.PrefetchScalarGridSpec(
        num_scalar_prefetch=1,
        grid=(M_ALL // TC,),
        in_specs=[pl.BlockSpec((TC, D), lambda i, off: (i, 0)),
                  _mod_spec(l, 5, TC),
                  pl.BlockSpec((TC, LANE), lambda i, off: (i, 0)), pl.BlockSpec((TC, LANE), lambda i, off: (i, 0))]
        + [win_spec(e) for e in range(N_EXP)],
        out_specs=pl.BlockSpec((TC, D), lambda i, off: (i, 0)),
    )
    return pl.pallas_call(
        _moe_combine_kernel,
        out_shape=jax.ShapeDtypeStruct((M_ALL, D), F32),
        grid_spec=grid_spec,
        compiler_params=_cparams("arbitrary"),
        name="moe_combine",
    )(off, x, mod, gates, pos, *([slots] * N_EXP))


def _moe_plan(sel):
    i32 = jnp.int32
    sel_t = sel[:, :N_EXP].T
    csum = jnp.cumsum(sel_t, axis=1)
    rank = csum - sel_t
    n_e = csum[:, -1]
    ns_e = (n_e + SUP - 1) // SUP
    end_e = jnp.cumsum(ns_e)
    start_e = end_e - ns_e
    n_used = end_e[-1]
    pos_t = jnp.where(sel_t > 0, start_e[:, None] * SUP + rank, -1)
    s_ids = jnp.arange(S_MAX, dtype=i32)
    sblk = jnp.minimum(s_ids, n_used - 1)
    se = jnp.sum((end_e[None, :] <= sblk[:, None]).astype(i32), axis=1)
    nv = jnp.clip(n_e[se] - (sblk - start_e[se]) * SUP, 0, SUP)
    nt = jnp.where(s_ids < n_used, (nv + SUB - 1) // SUB, 0)
    g_ids = jnp.arange(S_MAX * N_SUB, dtype=i32)
    s_g, e_g = g_ids // N_SUB, se[g_ids // N_SUB]
    r0 = (sblk[s_g] - start_e[e_g]) * SUP + (g_ids % N_SUB) * SUB
    r1 = jnp.minimum(r0 + SUB, n_e[e_g])
    live = (s_g < n_used) & (r0 < n_e[e_g])
    cs_g = csum[e_g]
    t_first = jnp.sum((cs_g <= r0[:, None]).astype(i32), axis=1)
    t_last = jnp.sum((cs_g < r1[:, None]).astype(i32), axis=1)
    clo = jnp.where(live, t_first // CHUNK, 0)
    chi = jnp.where(live, t_last // CHUNK, -1)
    before = jnp.concatenate([jnp.zeros((N_EXP, 1), i32), csum[:, TC - 1::TC][:, :-1]], axis=1)
    base = start_e[:, None] * SUP + before
    off = jnp.minimum(base // 16, (P_SLOT - WIN) // 16).T.reshape(-1)
    pos128 = jnp.pad(pos_t.T, ((0, 0), (0, LANE - N_EXP)), constant_values=-1)
    meta = (se.astype(i32), nt.astype(i32), sblk.astype(i32), clo.astype(i32), chi.astype(i32))
    return meta, pos_t.reshape(N_EXP, 1, M_ALL), pos128, off.astype(i32)


def _moe(h, x, mod, gates, sel, wg, wu, wd, l):
    meta, pos_t, pos, off = _moe_plan(sel)
    slots = _moe_experts(h, pos_t, meta, wg, wu, wd, l // 2)
    return _moe_combine(x, mod, gates, pos, off, slots, l)


def _final_kernel(tm, x_ref, g_ref, oc_ref, ol_ref):
    y = _rms(x_ref[...]) * g_ref[...]
    is_lat = pl.program_id(0) >= M_CTX // tm

    @pl.when(jnp.logical_not(is_lat))
    def _():
        oc_ref[...] = y

    @pl.when(is_lat)
    def _():
        ol_ref[...] = y


def _final_norm(x, g):
    tm = 1024
    return pl.pallas_call(
        functools.partial(_final_kernel, tm),
        out_shape=[jax.ShapeDtypeStruct((M_CTX, D), F32), jax.ShapeDtypeStruct((M_LAT, D), F32)],
        grid=(M_ALL // tm,),
        in_specs=[pl.BlockSpec((tm, D), lambda i: (i, 0)), pl.BlockSpec((1, D), lambda i: (0, 0))],
        out_specs=_row_split_specs(tm, D),
        compiler_params=_cparams("arbitrary"),
        name="final_norm",
    )(x, g)


def kernel(x_prompt, x_sample, state_gla, state_ret, cache_mla_ckv, cache_mla_krope, cache_gqa_k, cache_gqa_v,
           c, c_ctx, norm1_g, norm2_g, final_norm_g, w_mod, b_mod, w_in, w_out, gla_gate_w, gla_gate_b,
           mla_q_norm_g, mla_w_q_up, mla_kv_norm_g, mla_w_kv_up, ret_decay, gqa_sink,
           ffn_w_gate, ffn_w_up, ffn_w_down, moe_router, moe_w_gate, moe_w_up, moe_w_down):
    xs = (x_prompt.reshape(M_CTX, D), x_sample.reshape(M_LAT, D))

    cond = jnp.concatenate([c_ctx[None], c, jnp.zeros((8 - 1 - N_LAT, D), F32)], axis=0)
    mod = _modulation(cond, w_mod, b_mod)
    mod = mod[:, :1 + N_LAT].reshape(DEPTH, (1 + N_LAT) * 6, 1, D)

    w_out_b = w_out.astype(BF16)
    wq = mla_w_q_up.reshape(DEPTH, MLA_QR, MLA_H, MLA_NOPE + MLA_ROPE)
    wq = jnp.concatenate([wq[..., :MLA_NOPE].reshape(DEPTH, MLA_QR, MLA_H * MLA_NOPE),
                          wq[..., MLA_NOPE:].reshape(DEPTH, MLA_QR, MLA_H * MLA_ROPE)], axis=-1).astype(BF16)
    wkv = mla_w_kv_up.reshape(DEPTH, MLA_KVR, MLA_H, MLA_NOPE + MLA_DV)
    wkv = jnp.concatenate([wkv[..., :MLA_NOPE].reshape(DEPTH, MLA_KVR, MLA_H * MLA_NOPE),
                           wkv[..., MLA_NOPE:].reshape(DEPTH, MLA_KVR, MLA_H * MLA_DV)], axis=-1).astype(BF16)
    router = jnp.pad(moe_router, ((0, 0), (0, 0), (0, LANE - N_EXP))).astype(BF16)
    dec_lanes = jnp.repeat(ret_decay, RET_DV, axis=-1)
    sink_lanes = jnp.repeat(gqa_sink, GQA_HD, axis=-1).reshape(DEPTH, 1, 256)
    gate_b = gla_gate_b.reshape(DEPTH, 2, 1, GLA_H * GLA_DK)
    g1, g2 = norm1_g.reshape(DEPTH, 1, D), norm2_g.reshape(DEPTH, 1, D)
    gq, gkv = mla_q_norm_g.reshape(DEPTH, 1, MLA_QR), mla_kv_norm_g.reshape(DEPTH, 1, MLA_KVR)
    eye = jnp.eye(GLA_H, dtype=F32)
    s0_gla = jnp.einsum('bldhkv,hg->bldhvgk', state_gla, eye).reshape(
        N_LAT, DEPTH, 2, GLA_H * GLA_DV, GLA_H * GLA_DK)
    s0_ret = jnp.einsum('bldhkv,hg->bldhkgv', state_ret, jnp.eye(RET_H, dtype=F32)).reshape(
        N_LAT, DEPTH, 2, RET_H * RET_DK, RET_H * RET_DV)
    cache_kr = jnp.pad(cache_mla_krope, ((0, 0), (0, 0), (0, 0), (0, 128 - MLA_ROPE)))

    rope64 = _rope_tables(T_LAT, 64, 256)
    rope32_q = _rope_tables(T_LAT, 32, 128)
    ck, sl, sh = _rope_tables(T_LAT, 32, 128)
    live = jnp.asarray((np.arange(128) < MLA_ROPE).astype(np.float32))[None]
    rope32_k = (ck * live, sl * live, sh * live)

    st_gla = st_ret = caches_mla = caches_gqa = None
    for l in range(DEPTH):
        p_gla, p_mla, p_ret, p_gqa = _inproj(xs, g1, mod, w_in, l)

        o_gla_c, st_gla, o_mla_c, ckv_all, kr_all, o_ret_c, st_ret, o_gqa_c, gk_all, gv_all = _fused_call([
            _gla(p_gla, gla_gate_w, gate_b, None, st_gla, l, False),
            _mla(p_mla, gq, gkv, wq, wkv, None, None, None, None, caches_mla, l, False),
            _ret(p_ret, dec_lanes, None, None, st_ret, l, False),
            _gqa(p_gqa, sink_lanes, None, None, None, caches_gqa, l, False)], N_CTX, "mixers_context")
        caches_mla, caches_gqa = (ckv_all, kr_all), (gk_all, gv_all)
        o_gla_s, o_ret_s = _fused_call([
            _gla(p_gla, gla_gate_w, gate_b, s0_gla, None, l, True),
            _ret(p_ret, dec_lanes, rope64, s0_ret, None, l, True)], N_LAT, "mixers_latent_scan")
        o_mla_s, o_gqa_s = _fused_call([
            _mla(p_mla, gq, gkv, wq, wkv, rope32_q, rope32_k, cache_mla_ckv, cache_kr, None, l, True),
            _gqa(p_gqa, sink_lanes, rope64, cache_gqa_k, cache_gqa_v, None, l, True)], N_LAT, "mixers_latent_attn")
        o_ctx = (o_gla_c, o_mla_c, o_ret_c, o_gqa_c)
        o_lat = (o_gla_s, o_mla_s, o_ret_s, o_gqa_s)
        if l % 2 == 0:
            x, h2 = _outproj(xs, o_ctx, o_lat, w_out_b, g2, mod, None, l)
            x = _ffn(h2, x, mod, ffn_w_gate, ffn_w_up, ffn_w_down, l)
        else:
            x, h2, gates, sel = _outproj(xs, o_ctx, o_lat, w_out_b, g2, mod, router, l)
            x = _moe(h2, x, mod, gates, sel, moe_w_gate, moe_w_up, moe_w_down, l)
        xs = (x,)

    y_ctx, y_lat = _final_norm(x, final_norm_g[None])
    return (y_ctx.reshape(N_CTX, T_CTX, D), y_lat.reshape(N_LAT, T_LAT, D), jnp.swapaxes(st_gla, -1, -2), st_ret,
            *caches_mla, *caches_gqa)
```

```python
import functools

import numpy as np
import jax
import jax.numpy as jnp
from jax import lax
from jax.experimental import pallas as pl
from jax.experimental.pallas import tpu as pltpu

F32 = jnp.float32
BF16 = jnp.bfloat16
HIGHEST = lax.Precision.HIGHEST

D = 1024
N_CTX, T_CTX = 16, 256
N_LAT, T_LAT = 2, 1024
PAST = 256
DEPTH = 4
M_CTX = N_CTX * T_CTX
M_LAT = N_LAT * T_LAT
M_ALL = M_CTX + M_LAT
GRID_W = 64
ROPE_BASE = 10000.0
EPS = 1e-6

GLA_H, GLA_DK, GLA_DV, GLA_RANK, GLA_NORM, GLA_C = 4, 32, 64, 16, 16.0, 64
MLA_H, MLA_QR, MLA_KVR, MLA_NOPE, MLA_ROPE, MLA_DV = 4, 256, 128, 64, 32, 64
RET_H, RET_DK, RET_DV = 4, 64, 64
GQA_H, GQA_KV, GQA_HD, WINDOW = 4, 2, 64, 128
D_FF, N_EXP, D_FFE = 2816, 8, 3584

W_GLA, W_MLA, W_RET, W_GQA = 896, 512, 1024, 512
IN_WIDTH = 2752
IN_GROUPS = ((0, 800), (800, 1216), (1216, 2240), (2240, 2752))
LANE = 128
NEG = -1e30
QB = 256
QL = 128
VMEM_LIMIT = 56 * 1024 * 1024


def _cparams(*sem):
    return pltpu.CompilerParams(dimension_semantics=sem, vmem_limit_bytes=VMEM_LIMIT)


def _mm(a, b):
    return jnp.dot(a.astype(BF16), b.astype(BF16), preferred_element_type=F32)


def _mm_nt(a, b):
    return lax.dot_general(a.astype(BF16), b.astype(BF16), (((1,), (1,)), ((), ())), preferred_element_type=F32)


def _mm_tn(a, b):
    return lax.dot_general(a.astype(BF16), b.astype(BF16), (((0,), (0,)), ((), ())), preferred_element_type=F32)


def _mm_f32(a, b):
    return jnp.dot(a, b, precision=HIGHEST, preferred_element_type=F32)


def _silu(x):
    return x * (1.0 / (1.0 + jnp.exp(-x)))


def _log_sigmoid(x):
    return jnp.minimum(x, 0.0) - jnp.log1p(jnp.exp(-jnp.abs(x)))


def _rms(x):
    return x * lax.rsqrt(jnp.mean(x * x, axis=-1, keepdims=True) + EPS)


def _mod_row(tile, tm):
    return jnp.maximum((tile * tm) // T_LAT - (M_CTX // T_LAT - 1), 0)


def _mod_kernel(c_ref, w_ref, b_ref, o_ref):
    o_ref[...] = _mm(_silu(c_ref[...]), w_ref[...]) + b_ref[...]


def _modulation(cond, w_mod, b_mod):
    tn = 1536
    return pl.pallas_call(
        _mod_kernel,
        out_shape=jax.ShapeDtypeStruct((DEPTH, 8, 6 * D), F32),
        grid=(DEPTH, 6 * D // tn),
        in_specs=[pl.BlockSpec((8, D), lambda l, j: (0, 0)),
                  pl.BlockSpec((None, D, tn), lambda l, j: (l, 0, j)),
                  pl.BlockSpec((None, 1, tn), lambda l, j: (l, 0, j))],
        out_specs=pl.BlockSpec((None, 8, tn), lambda l, j: (l, 0, j)),
        compiler_params=_cparams("parallel", "parallel"),
        name="modulation",
    )(cond, w_mod, b_mod.reshape(DEPTH, 1, 6 * D))


def _inproj_kernel(split, tm, *refs):
    if split:
        xc_ref, xl_ref = refs[0:2]
        x = jnp.where(pl.program_id(0) >= M_CTX // tm, xl_ref[...], xc_ref[...])
        refs = refs[2:]
    else:
        x = refs[0][...]
        refs = refs[1:]
    g_ref, sh_ref, sc_ref, w_ref = refs[0:4]
    out_refs, w_s = refs[4:-1], refs[-1]

    @pl.when(pl.program_id(0) == 0)
    def _():
        dst = 0
        for (lo, hi), o_ref in zip(IN_GROUPS, out_refs):
            width = o_ref.shape[-1]
            w_s[:, dst:dst + hi - lo] = w_ref[:, lo:hi].astype(BF16)
            if width > hi - lo:
                w_s[:, dst + hi - lo:dst + width] = jnp.zeros((D, width - (hi - lo)), BF16)
            dst += width

    h = (_rms(x) * g_ref[...] * (1.0 + sc_ref[...]) + sh_ref[...]).astype(BF16)
    col = 0
    for o_ref in out_refs:
        width = o_ref.shape[-1]
        o_ref[...] = jnp.dot(h, w_s[:, col:col + width], preferred_element_type=F32)
        col += width


def _mod_spec(l, j, tm):
    return pl.BlockSpec((None, None, 1, D), lambda i, *_: (l, _mod_row(i, tm) * 6 + j, 0, 0))


def _layer_spec(l, *shape):
    return pl.BlockSpec((None,) + shape, lambda *_: (l,) + (0,) * len(shape))


def _stacked_output(prev, tail, l, nseq, in_specs, args, out_shape, out_specs, aliases):
    out_shape.append(jax.ShapeDtypeStruct((nseq, DEPTH) + tail, F32))
    out_specs.append(pl.BlockSpec((None, None) + tail, lambda b: (b, l) + (0,) * len(tail)))
    if prev is not None:
        in_specs.append(pl.BlockSpec(memory_space=pl.ANY))
        args.append(prev)
        aliases[len(args) - 1] = len(out_shape) - 1


def _row_split_specs(tm, width):
    n_ctx = M_CTX // tm
    return [pl.BlockSpec((tm, width), lambda i: (jnp.minimum(i, n_ctx - 1), 0)),
            pl.BlockSpec((tm, width), lambda i: (jnp.maximum(i - n_ctx, 0), 0))]


def _inproj(xs, g1, mod, w_in, l):
    tm = 512
    split = len(xs) == 2
    rows = lambda w: pl.BlockSpec((tm, w), lambda i: (i, 0))
    widths = (W_GLA, W_MLA, W_RET, W_GQA)
    x_specs = _row_split_specs(tm, D) if split else [rows(D)]
    w_spec = pl.BlockSpec((None, D, IN_WIDTH), lambda i: (l, 0, 0), pipeline_mode=pl.Buffered(1))
    return pl.pallas_call(
        functools.partial(_inproj_kernel, split, tm),
        out_shape=[jax.ShapeDtypeStruct((M_ALL, w), F32) for w in widths],
        grid=(M_ALL // tm,),
        in_specs=x_specs + [_layer_spec(l, 1, D), _mod_spec(l, 0, tm), _mod_spec(l, 1, tm), w_spec],
        out_specs=[rows(w) for w in widths],
        scratch_shapes=[pltpu.VMEM((D, sum(widths)), BF16)],
        compiler_params=_cparams("arbitrary"),
        name="inproj",
    )(*xs, g1, mod, mod, w_in)


def _head_rms_gate(o, gate):
    r = lax.broadcasted_iota(jnp.int32, (256, 256), 0) // 64
    c = lax.broadcasted_iota(jnp.int32, (256, 256), 1) // 64
    group_mean = jnp.where(r == c, 1.0 / 64.0, 0.0).astype(BF16)
    sq = o * o
    sq_hi = sq.astype(BF16)
    sq_lo = (sq - sq_hi.astype(F32)).astype(BF16)
    ms = (jnp.dot(sq_hi, group_mean, preferred_element_type=F32)
          + jnp.dot(sq_lo, group_mean, preferred_element_type=F32))
    return o * lax.rsqrt(ms + EPS) * _silu(gate)


def _own_lanes(n_heads, rows_per_head, lanes_per_head):
    shape = (n_heads * rows_per_head, n_heads * lanes_per_head)
    return (lax.broadcasted_iota(jnp.int32, shape, 0) // rows_per_head
            == lax.broadcasted_iota(jnp.int32, shape, 1) // lanes_per_head).astype(F32)


def _stack_heads(x, n_heads, lanes_per_head):
    return jnp.concatenate([x] * n_heads, axis=0) * _own_lanes(n_heads, x.shape[0], lanes_per_head)


def _unstack_heads(y, n_heads, lanes_per_head):
    t = y.shape[0] // n_heads
    y = y * _own_lanes(n_heads, t, lanes_per_head)
    out = y[0:t]
    for h in range(1, n_heads):
        out = out + y[h * t:(h + 1) * t]
    return out


def _per_query_head(x):
    lane = lax.broadcasted_iota(jnp.int32, x.shape, 1)
    swapped = pltpu.roll(x, GQA_HD, 1)
    return jnp.concatenate([jnp.where(lane < GQA_HD, x, swapped), jnp.where(lane < GQA_HD, swapped, x)], axis=1)


def _rope(x, cos, sin_lo, sin_hi, half):
    w = x.shape[-1]
    return x * cos + pltpu.roll(x, w - half, 1) * sin_lo + pltpu.roll(x, half, 1) * sin_hi


def _rope_tables(t, head_dim, width):
    half = head_dim // 2
    quarter = head_dim // 4
    pos = np.arange(t)
    rows = (pos // GRID_W).astype(np.float32)
    cols = (pos % GRID_W).astype(np.float32)
    inv = np.power(np.float32(ROPE_BASE), -np.arange(quarter, dtype=np.float32) / np.float32(quarter)).astype(np.float32)
    ang = np.concatenate([rows[:, None] * inv, cols[:, None] * inv], axis=-1).astype(np.float32)
    lane = np.arange(width)
    a = ang[:, lane % half]
    cos, sin = np.cos(a).astype(np.float32), np.sin(a).astype(np.float32)
    low = (lane % head_dim) < half
    return (jnp.asarray(cos), jnp.asarray(np.where(low[None], -sin, 0.0).astype(np.float32)),
            jnp.asarray(np.where(low[None], 0.0, sin).astype(np.float32)))


def _seq_spec(t, width, row0, latent):
    return pl.BlockSpec((t, width), lambda b: (row0 + b, 0), pipeline_mode=pl.Buffered(1) if latent else None)


def _table_spec(t, width):
    return pl.BlockSpec((t, width), lambda b: (0, 0), pipeline_mode=pl.Buffered(1))


def _fused_call(parts, nseq, name):
    in_specs, args, out_shape, out_specs, scratch, aliases, layout = [], [], [], [], [], {}, []
    for fn, p_in, p_args, p_shape, p_out, p_scratch, p_alias in parts:
        layout.append((fn, len(p_args), len(p_shape), len(p_scratch)))
        for k, v in p_alias.items():
            aliases[len(args) + k] = len(out_shape) + v
        in_specs += p_in
        args += p_args
        out_shape += p_shape
        out_specs += p_out
        scratch += p_scratch
    n_in, n_out = len(args), len(out_shape)

    def kernel(*refs):
        i, o, s = 0, n_in, n_in + n_out
        for fn, ni, no, ns in layout:
            fn(*refs[i:i + ni], *refs[o:o + no], *refs[s:s + ns])
            i, o, s = i + ni, o + no, s + ns

    return pl.pallas_call(
        kernel, out_shape=out_shape, grid=(nseq,), in_specs=in_specs, out_specs=out_specs, scratch_shapes=scratch,
        input_output_aliases=aliases, name=name,
        compiler_params=pltpu.CompilerParams(dimension_semantics=("arbitrary",), vmem_limit_bytes=60 * 1024 * 1024),
    )(*args)


def _gla_kernel(t, latent, *refs):
    if latent:
        p_ref, gw_ref, gb_ref, s0_ref, o_ref, la_f, la_b, of_s, ob_s, st_s = refs
    else:
        p_ref, gw_ref, gb_ref = refs[0:3]
        o_ref, st_ref, la_f, la_b, of_s, ob_s, st_s = refs[-7:]
    c = GLA_C
    gr = 256
    nc = gr // c
    n = t // gr
    hd = GLA_H * GLA_DK
    la_f[...] = _log_sigmoid(_mm_f32(p_ref[:, 768:784], gw_ref[0]) + gb_ref[0]) / GLA_NORM
    la_b[...] = _log_sigmoid(_mm_f32(p_ref[:, 784:800], gw_ref[1]) + gb_ref[1]) / GLA_NORM

    ri = lax.broadcasted_iota(jnp.int32, (gr, gr), 0)
    ci = lax.broadcasted_iota(jnp.int32, (gr, gr), 1)
    same = (ri // c) == (ci // c)
    rs = lax.broadcasted_iota(jnp.int32, (GLA_H * gr, gr), 0) % gr
    cs = lax.broadcasted_iota(jnp.int32, (GLA_H * gr, gr), 1)
    same_s = (rs // c) == (cs // c)
    own_chunk = (lax.broadcasted_iota(jnp.int32, (gr, nc * hd), 0) // c
                 == lax.broadcasted_iota(jnp.int32, (gr, nc * hd), 1) // hd).astype(F32)
    own_dk = _own_lanes(GLA_H, GLA_DV, GLA_DK)
    own_dk = jnp.concatenate([own_dk] * nc, axis=1)
    scale = GLA_DK ** -0.5
    if latent:
        st_s[...] = s0_ref[...]
    else:
        st_s[...] = jnp.zeros_like(st_s)

    def group(row0, la_ref, d):
        fwd = d == 0
        ordered = (ci <= ri) if fwd else (ci >= ri)
        first_half = (ci % c < c // 2) if fwd else (ci % c >= c // 2)
        keep = same_s & ((cs <= rs) if fwd else (cs >= rs))
        sums_of = jnp.concatenate([(same & ordered).astype(BF16), same.astype(BF16), (same & first_half).astype(BF16)],
                                  axis=0)
        q = p_ref[pl.ds(row0, gr), 0:128]
        k = p_ref[pl.ds(row0, gr), 128:256] * scale
        v = p_ref[pl.ds(row0, gr), 256:512]
        la = la_ref[pl.ds(row0, gr), :]
        la_hi = la.astype(BF16)
        la_lo = (la - la_hi.astype(F32)).astype(BF16)
        sums = jnp.dot(sums_of, la_hi, preferred_element_type=F32) + jnp.dot(sums_of, la_lo, preferred_element_type=F32)
        bc, tot, mid = sums[0:gr], sums[gr:2 * gr], sums[2 * gr:3 * gr]
        qe, ke = q * jnp.exp(bc - mid), k * jnp.exp(mid - bc)
        q_in, k_out, a = q * jnp.exp(bc), k * jnp.exp(tot - bc), jnp.exp(tot)
        att = jnp.where(keep, _mm_nt(_stack_heads(qe, GLA_H, GLA_DK), ke), 0.0)
        o = _unstack_heads(_mm(att, v), GLA_H, GLA_DV)
        kv = _mm_tn(v, jnp.concatenate([k_out] * nc, axis=1) * own_chunk) * own_dk
        st, entering = st_s[d], [None] * nc
        for cc in (range(nc) if fwd else range(nc - 1, -1, -1)):
            entering[cc] = st
            st = st * a[cc * c:cc * c + 1, :] + kv[:, cc * hd:(cc + 1) * hd]
        st_s[d] = st
        return o + _mm_nt(jnp.concatenate([q_in] * nc, axis=1) * own_chunk, jnp.concatenate(entering, axis=1))

    def body(i, carry):
        rf = pl.multiple_of(i * gr, gr)
        rb = pl.multiple_of((n - 1 - i) * gr, gr)
        of_s[pl.ds(rf, gr), :] = group(rf, la_f, 0)
        ob_s[pl.ds(rb, gr), :] = group(rb, la_b, 1)
        return carry

    lax.fori_loop(0, n, body, 0)
    if not latent:
        for d in range(2):
            for h in range(GLA_H):
                st_ref[d, h] = st_s[d, h * GLA_DV:(h + 1) * GLA_DV, h * GLA_DK:(h + 1) * GLA_DK]
    for r in range(t // QB):
        rows = slice(r * QB, (r + 1) * QB)
        o_ref[rows, :] = _head_rms_gate(of_s[rows, :] + ob_s[rows, :], p_ref[rows, 512:768]).astype(BF16)


def _gla(p, gate_w, gate_b, s0_bd, st_prev, l, latent):
    t, nseq, row0 = (T_LAT, N_LAT, M_CTX // T_LAT) if latent else (T_CTX, N_CTX, 0)
    st_shape = (2, GLA_H * GLA_DV, GLA_H * GLA_DK)
    in_specs = [_seq_spec(t, W_GLA, row0, latent),
                _layer_spec(l, 2, GLA_RANK, GLA_H * GLA_DK), _layer_spec(l, 2, 1, GLA_H * GLA_DK)]
    args = [p, gate_w, gate_b]
    aliases = {}
    o_shape = jax.ShapeDtypeStruct((nseq * t, 256), BF16)
    o_spec = pl.BlockSpec((t, 256), lambda b: (b, 0))
    if latent:
        in_specs.append(pl.BlockSpec((None, None) + st_shape, lambda b: (b, l, 0, 0, 0)))
        args.append(s0_bd)
    out_shape, out_specs = [o_shape], [o_spec]
    if not latent:
        _stacked_output(st_prev, (2, GLA_H, GLA_DV, GLA_DK), l, nseq, in_specs, args, out_shape, out_specs, aliases)
    scratch = [pltpu.VMEM((t, 128), F32), pltpu.VMEM((t, 128), F32),
               pltpu.VMEM((t, 256), F32), pltpu.VMEM((t, 256), F32), pltpu.VMEM(st_shape, F32)]
    return functools.partial(_gla_kernel, t, latent), in_specs, args, out_shape, out_specs, scratch, aliases


def _ret_kernel(t, latent, *refs):
    if latent:
        p_ref, dec_ref, cos_ref, slo_ref, shi_ref, s0_ref, o_ref, q_s, k_s, decay_s = refs
    else:
        p_ref, dec_ref = refs[0:2]
        o_ref, st_ref, decay_s = refs[-3:]
    lg = _log_sigmoid(dec_ref[...])
    scale = RET_DK ** -0.5
    nblk = t // QB

    @pl.when(pl.program_id(0) == 0)
    def _():
        wide = decay_s.shape[-1]
        dist = (lax.broadcasted_iota(jnp.int32, (QB, wide), 0) - lax.broadcasted_iota(jnp.int32, (QB, wide), 1)
                + (nblk - 1) * QB).astype(F32)
        for h in range(RET_H):
            lg_f, lg_b = lg[0:1, h * 64:h * 64 + 1], lg[1:2, h * 64:h * 64 + 1]
            decay_s[h] = (jnp.where(dist >= 0, jnp.exp(jnp.maximum(dist, 0.0) * lg_f), 0.0)
                          + jnp.where(dist <= 0, jnp.exp(jnp.maximum(-dist, 0.0) * lg_b), 0.0))

    if not latent:
        k = p_ref[:, 256:512] * scale
        v = p_ref[:, 512:768]
        decay = jnp.concatenate([decay_s[h] for h in range(RET_H)], axis=0)
        o = _unstack_heads(_mm(_mm_nt(_stack_heads(p_ref[:, 0:256], RET_H, RET_DK), k) * decay, v), RET_H, RET_DV)
        o_ref[...] = _head_rms_gate(o, p_ref[:, 768:1024]).astype(BF16)
        j = lax.broadcasted_iota(jnp.int32, (t, 1), 0).astype(F32)
        kv_f = _mm_tn(k * jnp.exp((float(t - 1) - j) * lg[0:1, :]), v)
        kv_b = _mm_tn(k * jnp.exp(j * lg[1:2, :]), v)
        for h in range(RET_H):
            st_ref[0, h] = kv_f[h * RET_DK:(h + 1) * RET_DK, h * RET_DV:(h + 1) * RET_DV]
            st_ref[1, h] = kv_b[h * RET_DK:(h + 1) * RET_DK, h * RET_DV:(h + 1) * RET_DV]
        return
    q_s[...] = _rope(p_ref[:, 0:256], cos_ref[...], slo_ref[...], shi_ref[...], RET_DK // 2)
    k_s[...] = _rope(p_ref[:, 256:512], cos_ref[...], slo_ref[...], shi_ref[...], RET_DK // 2) * scale
    for r in range(t // QL):
        rows = slice(r * QL, (r + 1) * QL)
        pos = (lax.broadcasted_iota(jnp.int32, (QL, 1), 0) + r * QL).astype(F32)
        strip_rows = slice(r * QL % QB, r * QL % QB + QL)
        off = (nblk - 1 - r * QL // QB) * QB
        decay = jnp.concatenate([decay_s[h, strip_rows, off:off + t] for h in range(RET_H)], axis=0)
        q = q_s[rows, :]
        o = _unstack_heads(_mm(_mm_nt(_stack_heads(q, RET_H, RET_DK), k_s[...]) * decay, p_ref[:, 512:768]),
                           RET_H, RET_DV)
        o = o + _mm(q * jnp.exp((pos + 1.0) * lg[0:1, :]), s0_ref[0])
        o = o + _mm(q * jnp.exp((float(t) - pos) * lg[1:2, :]), s0_ref[1])
        o_ref[rows, :] = _head_rms_gate(o, p_ref[rows, 768:1024]).astype(BF16)


def _ret(p, dec_lanes, tables, s0, st_prev, l, latent):
    t, nseq, row0 = (T_LAT, N_LAT, M_CTX // T_LAT) if latent else (T_CTX, N_CTX, 0)
    st_shape = (2, RET_H, RET_DK, RET_DV)
    in_specs = [_seq_spec(t, W_RET, row0, latent), _layer_spec(l, 2, 256)]
    args = [p, dec_lanes]
    aliases = {}
    o_shape = jax.ShapeDtypeStruct((nseq * t, 256), BF16)
    o_spec = pl.BlockSpec((t, 256), lambda b: (b, 0))
    scratch = []
    if latent:
        in_specs += [_table_spec(t, 256)] * 3
        in_specs.append(pl.BlockSpec((None, None, 2, RET_H * RET_DK, RET_H * RET_DV), lambda b: (b, l, 0, 0, 0)))
        args += list(tables) + [s0]
        scratch = [pltpu.VMEM((t, 256), F32), pltpu.VMEM((t, 256), F32)]
    scratch.append(pltpu.VMEM((RET_H, QB, 2 * t - QB), F32))
    out_shape, out_specs = [o_shape], [o_spec]
    if not latent:
        _stacked_output(st_prev, st_shape, l, nseq, in_specs, args, out_shape, out_specs, aliases)
    return functools.partial(_ret_kernel, t, latent), in_specs, args, out_shape, out_specs, scratch, aliases


def _mla_kernel(t, latent, *refs):
    if latent:
        (p_ref, gq_ref, gkv_ref, wq_ref, wkv_ref, cq_ref, slq_ref, shq_ref, ck_ref, slk_ref, shk_ref,
         cckv_ref, ckr_ref, o_ref, qn_s, qr_s, kn_s, kr_s, v_s) = refs
    else:
        p_ref, gq_ref, gkv_ref, wq_ref, wkv_ref = refs[0:5]
        o_ref, ckv_ref, kro_ref, qn_s, qr_s, kn_s, kr_s, v_s = refs[-8:]
    nk = t + (PAST if latent else 0)
    qh = _mm(_rms(p_ref[:, 0:256]) * gq_ref[...], wq_ref[...])
    ckv = _rms(p_ref[:, 256:384]) * gkv_ref[...]
    kv = _mm(ckv, wkv_ref[...])
    qn_s[...] = qh[:, 0:256]
    kn_s[0:t, :] = kv[:, 0:256]
    v_s[0:t, :] = kv[:, 256:512]
    if latent:
        qr_s[...] = _rope(qh[:, 256:384], cq_ref[...], slq_ref[...], shq_ref[...], MLA_ROPE // 2)
        kr_s[0:t, :] = _rope(p_ref[:, 384:512], ck_ref[...], slk_ref[...], shk_ref[...], MLA_ROPE // 2)
        kvc = _mm(cckv_ref[...], wkv_ref[...])
        kn_s[t:nk, :] = kvc[:, 0:256]
        v_s[t:nk, :] = kvc[:, 256:512]
        kr_s[t:nk, :] = ckr_ref[...]
    scale = (MLA_NOPE + MLA_ROPE) ** -0.5
    if not latent:
        ckv_ref[...] = ckv
        kro_ref[...] = p_ref[:, 384:384 + MLA_ROPE]
        kr = p_ref[:, 384:512]
        kr4 = kr + pltpu.roll(kr, MLA_ROPE, 1) + pltpu.roll(kr, 2 * MLA_ROPE, 1) + pltpu.roll(kr, 3 * MLA_ROPE, 1)
        s = (_mm_nt(_stack_heads(qh[:, 0:256], MLA_H, MLA_NOPE), kv[:, 0:256])
             + _mm_nt(_stack_heads(qh[:, 256:384], MLA_H, MLA_ROPE), kr4)) * scale
        e = jnp.exp(s - jnp.max(s, axis=-1, keepdims=True))
        o = _mm(e, kv[:, 256:512]) / jnp.sum(e, axis=-1, keepdims=True)
        o_ref[...] = _unstack_heads(o, MLA_H, MLA_DV).astype(BF16)
        return
    kr = kr_s[...]
    kr4 = kr + pltpu.roll(kr, MLA_ROPE, 1) + pltpu.roll(kr, 2 * MLA_ROPE, 1) + pltpu.roll(kr, 3 * MLA_ROPE, 1)
    for r in range(t // QL):
        rows = slice(r * QL, (r + 1) * QL)
        s = (_mm_nt(_stack_heads(qn_s[rows, :], MLA_H, MLA_NOPE), kn_s[...])
             + _mm_nt(_stack_heads(qr_s[rows, :], MLA_H, MLA_ROPE), kr4)) * scale
        e = jnp.exp(s - jnp.max(s, axis=-1, keepdims=True))
        o = _mm(e, v_s[...]) / jnp.sum(e, axis=-1, keepdims=True)
        o_ref[rows, :] = _unstack_heads(o, MLA_H, MLA_DV).astype(BF16)


def _mla(p, gq, gkv, wq, wkv, tables_q, tables_k, cache_ckv, cache_kr, prev, l, latent):
    t, nseq, row0 = (T_LAT, N_LAT, M_CTX // T_LAT) if latent else (T_CTX, N_CTX, 0)
    nk = t + (PAST if latent else 0)
    in_specs = [_seq_spec(t, W_MLA, row0, latent), _layer_spec(l, 1, MLA_QR),
                _layer_spec(l, 1, MLA_KVR), _layer_spec(l, MLA_QR, 384), _layer_spec(l, MLA_KVR, 512)]
    args = [p, gq, gkv, wq, wkv]
    aliases = {}
    o_shape = jax.ShapeDtypeStruct((nseq * t, 256), BF16)
    o_spec = pl.BlockSpec((t, 256), lambda b: (b, 0))
    if latent:
        in_specs += [_table_spec(t, 128)] * 6
        in_specs += [pl.BlockSpec((None, None, PAST, MLA_KVR), lambda b: (b, l, 0, 0)),
                     pl.BlockSpec((None, None, PAST, 128), lambda b: (b, l, 0, 0))]
        args += list(tables_q) + list(tables_k) + [cache_ckv, cache_kr]
    out_shape, out_specs = [o_shape], [o_spec]
    if not latent:
        prev = prev or (None, None)
        _stacked_output(prev[0], (t, MLA_KVR), l, nseq, in_specs, args, out_shape, out_specs, aliases)
        _stacked_output(prev[1], (t, MLA_ROPE), l, nseq, in_specs, args, out_shape, out_specs, aliases)
    scratch = [pltpu.VMEM((t, 256), F32), pltpu.VMEM((t, 128), F32), pltpu.VMEM((nk, 256), F32),
               pltpu.VMEM((nk, 128), F32), pltpu.VMEM((nk, 256), F32)]
    return functools.partial(_mla_kernel, t, latent), in_specs, args, out_shape, out_specs, scratch, aliases


def _gqa_kernel(t, latent, *refs):
    if latent:
        p_ref, sink_ref, cos_ref, slo_ref, shi_ref, ck_ref, cv_ref, o_ref, q_s, k_s = refs
    else:
        p_ref, sink_ref = refs[0:2]
        o_ref, ko_ref, vo_ref = refs[-3:]
        for kvh in range(GQA_KV):
            ko_ref[kvh] = p_ref[:, 256 + kvh * GQA_HD:256 + (kvh + 1) * GQA_HD]
            vo_ref[kvh] = p_ref[:, 384 + kvh * GQA_HD:384 + (kvh + 1) * GQA_HD]
    scale = GQA_HD ** -0.5
    if not latent:
        own = _own_lanes(GQA_H, t, GQA_HD)
        sink = jnp.max(jnp.where(own > 0, sink_ref[...], NEG), axis=-1, keepdims=True)
        s = _mm_nt(_stack_heads(p_ref[:, 0:256], GQA_H, GQA_HD), _per_query_head(p_ref[:, 256:384])) * scale
        m = jnp.maximum(jnp.max(s, axis=-1, keepdims=True), sink)
        e = jnp.exp(s - m)
        o = _mm(e, _per_query_head(p_ref[:, 384:512])) / (jnp.sum(e, axis=-1, keepdims=True) + jnp.exp(sink - m))
        o_ref[...] = _unstack_heads(o, GQA_H, GQA_HD).astype(BF16)
        return
    q_s[...] = _rope(p_ref[:, 0:256], cos_ref[...], slo_ref[...], shi_ref[...], GQA_HD // 2)
    k_s[...] = _rope(p_ref[:, 256:384], cos_ref[:, 0:128], slo_ref[:, 0:128], shi_ref[:, 0:128], GQA_HD // 2)
    kc = _per_query_head(jnp.concatenate([ck_ref[0], ck_ref[1]], axis=1))
    vc = _per_query_head(jnp.concatenate([cv_ref[0], cv_ref[1]], axis=1))
    sink = jnp.max(jnp.where(_own_lanes(GQA_H, QL, GQA_HD) > 0, sink_ref[...], NEG), axis=-1, keepdims=True)
    for r in range(t // QL):
        rows = slice(r * QL, (r + 1) * QL)
        keys = slice(max(0, r * QL - WINDOW), min(t, (r + 1) * QL + WINDOW))
        nkeys = keys.stop - keys.start
        row = lax.broadcasted_iota(jnp.int32, (GQA_H * QL, nkeys), 0) % QL + r * QL
        col = lax.broadcasted_iota(jnp.int32, (GQA_H * QL, nkeys), 1) + keys.start
        near = jnp.abs(row - col) <= WINDOW
        q = _stack_heads(q_s[rows, :], GQA_H, GQA_HD)
        s_loc = jnp.where(near, _mm_nt(q, _per_query_head(k_s[keys, :])) * scale, NEG)
        s_ctx = _mm_nt(q, kc) * scale
        m = jnp.maximum(jnp.maximum(jnp.max(s_loc, axis=-1, keepdims=True), jnp.max(s_ctx, axis=-1, keepdims=True)),
                        sink)
        e_loc, e_ctx = jnp.exp(s_loc - m), jnp.exp(s_ctx - m)
        den = jnp.sum(e_loc, axis=-1, keepdims=True) + jnp.sum(e_ctx, axis=-1, keepdims=True) + jnp.exp(sink - m)
        o = (_mm(e_loc, _per_query_head(p_ref[keys, 384:512])) + _mm(e_ctx, vc)) / den
        o_ref[rows, :] = _unstack_heads(o, GQA_H, GQA_HD).astype(BF16)


def _gqa(p, sink_lanes, tables, cache_k, cache_v, prev, l, latent):
    t, nseq, row0 = (T_LAT, N_LAT, M_CTX // T_LAT) if latent else (T_CTX, N_CTX, 0)
    in_specs = [_seq_spec(t, W_GQA, row0, latent), _layer_spec(l, 1, 256)]
    args = [p, sink_lanes]
    scratch = []
    aliases = {}
    out_shape = [jax.ShapeDtypeStruct((nseq * t, 256), BF16)]
    out_specs = [pl.BlockSpec((t, 256), lambda b: (b, 0))]
    if latent:
        in_specs += [_table_spec(t, 256)] * 3
        in_specs += [pl.BlockSpec((None, None, GQA_KV, PAST, GQA_HD), lambda b: (b, l, 0, 0, 0))] * 2
        args += list(tables) + [cache_k, cache_v]
        scratch = [pltpu.VMEM((t, 256), F32), pltpu.VMEM((t, 128), F32)]
    else:
        prev = prev or (None, None)
        for pv in prev:
            _stacked_output(pv, (GQA_KV, t, GQA_HD), l, nseq, in_specs, args, out_shape, out_specs, aliases)
    return functools.partial(_gqa_kernel, t, latent), in_specs, args, out_shape, out_specs, scratch, aliases


def _outproj_kernel(route, split, tm, *refs):
    is_lat = pl.program_id(0) >= M_CTX // tm
    if split:
        x_in = jnp.where(is_lat, refs[1][...], refs[0][...])
        refs = refs[2:]
    else:
        x_in = refs[0][...]
        refs = refs[1:]
    ctx_refs, lat_refs = refs[0:4], refs[4:8]
    w_ref, gt_ref, g_ref, sh_ref, sc_ref = refs[8:13]
    refs = refs[13:]
    mix = jnp.zeros((tm, D), F32)
    for m in range(4):
        o = jnp.where(is_lat, lat_refs[m][...], ctx_refs[m][...])
        mix = mix + jnp.dot(o, w_ref[m * 256:(m + 1) * 256, :], preferred_element_type=F32)
    x = x_in + gt_ref[...] * mix
    h = (_rms(x) * g_ref[...] * (1.0 + sc_ref[...]) + sh_ref[...]).astype(BF16)
    if not route:
        xo_ref, h_ref = refs
    else:
        r_ref, xo_ref, h_ref, gate_ref, sel_ref = refs
        lane = lax.broadcasted_iota(jnp.int32, gate_ref.shape, 1)
        logits = jnp.where(lane < N_EXP, jnp.dot(h, r_ref[...], preferred_element_type=F32), NEG)
        m1 = jnp.max(logits, axis=-1, keepdims=True)
        i1 = jnp.min(jnp.where(logits == m1, lane, LANE), axis=-1, keepdims=True)
        rest = jnp.where(lane == i1, NEG, logits)
        m2 = jnp.max(rest, axis=-1, keepdims=True)
        i2 = jnp.min(jnp.where(rest == m2, lane, LANE), axis=-1, keepdims=True)
        e2 = jnp.exp(m2 - m1)
        gate_ref[...] = jnp.where(lane == i1, 1.0 / (1.0 + e2), 0.0) + jnp.where(lane == i2, e2 / (1.0 + e2), 0.0)
        sel_ref[...] = jnp.where((lane == i1) | (lane == i2), 1, 0)
    xo_ref[...] = x
    h_ref[...] = h


def _outproj(xs, o_ctx, o_lat, w_out, g2, mod, router, l):
    tm = 512
    route = router is not None
    split = len(xs) == 2
    rows = lambda w: pl.BlockSpec((tm, w), lambda i: (i, 0))
    ctx_spec, lat_spec = _row_split_specs(tm, 256)
    x_specs = _row_split_specs(tm, D) if split else [rows(D)]
    in_specs = x_specs + [ctx_spec] * 4 + [lat_spec] * 4 + [
        _layer_spec(l, D, D), _mod_spec(l, 2, tm), _layer_spec(l, 1, D), _mod_spec(l, 3, tm), _mod_spec(l, 4, tm)]
    args = [*xs, *o_ctx, *o_lat, w_out, mod, g2, mod, mod]
    out_shape = [jax.ShapeDtypeStruct((M_ALL, D), F32), jax.ShapeDtypeStruct((M_ALL, D), BF16)]
    out_specs = [rows(D), rows(D)]
    if route:
        in_specs.append(_layer_spec(l // 2, D, LANE))
        args.append(router)
        out_shape += [jax.ShapeDtypeStruct((M_ALL, LANE), F32), jax.ShapeDtypeStruct((M_ALL, LANE), jnp.int32)]
        out_specs += [rows(LANE), rows(LANE)]
    return pl.pallas_call(
        functools.partial(_outproj_kernel, route, split, tm),
        out_shape=out_shape, grid=(M_ALL // tm,), in_specs=in_specs, out_specs=out_specs,
        compiler_params=_cparams("parallel"),
        name="outproj_route" if route else "outproj",
    )(*args)


def _ffn_kernel(h_ref, x_ref, gt_ref, wg_ref, wu_ref, wd_ref, o_ref, acc_ref):
    f = pl.program_id(1)

    @pl.when(f == 0)
    def _():
        acc_ref[...] = jnp.zeros_like(acc_ref)

    wg, wu, wd = wg_ref[...].astype(BF16), wu_ref[...].astype(BF16), wd_ref[...].astype(BF16)
    half = h_ref.shape[0] // 2
    for rows in (slice(0, half), slice(half, 2 * half)):
        h = h_ref[rows, :]
        g = jnp.dot(h, wg, preferred_element_type=F32)
        u = jnp.dot(h, wu, preferred_element_type=F32)
        acc_ref[rows, :] += jnp.dot((_silu(g) * u).astype(BF16), wd, preferred_element_type=F32)

    @pl.when(f == pl.num_programs(1) - 1)
    def _():
        o_ref[...] = x_ref[...] + gt_ref[...] * acc_ref[...]


def _ffn(h, x, mod, wg, wu, wd, l):
    tm, tf = 1024, 256
    j = l // 2
    return pl.pallas_call(
        _ffn_kernel,
        out_shape=jax.ShapeDtypeStruct((M_ALL, D), F32),
        grid=(M_ALL // tm, D_FF // tf),
        in_specs=[pl.BlockSpec((tm, D), lambda i, f: (i, 0)), pl.BlockSpec((tm, D), lambda i, f: (i, 0)),
                  _mod_spec(l, 5, tm),
                  pl.BlockSpec((None, D, tf), lambda i, f: (j, 0, f)),
                  pl.BlockSpec((None, D, tf), lambda i, f: (j, 0, f)),
                  pl.BlockSpec((None, tf, D), lambda i, f: (j, f, 0))],
        out_specs=pl.BlockSpec((tm, D), lambda i, f: (i, 0)),
        scratch_shapes=[pltpu.VMEM((tm, D), F32)],
        compiler_params=_cparams("parallel", "arbitrary"),
        name="ffn_dense",
    )(h, x, mod, wg, wu, wd)


SUP, SUB, CHUNK = 2048, 256, 256
N_SUB = SUP // SUB
S_MAX = 2 * M_ALL // SUP + N_EXP
P_SLOT = S_MAX * SUP
TC = 128
WIN = TC + 16


def _moe_expert_kernel(se_ref, nt_ref, sblk_ref, clo_ref, chi_ref, pos_ref, h_ref, wg_ref, wu_ref, wd_ref, o_ref,
                       xs_s, acc_s, wg_s, wu_s, wd_s):
    s = pl.program_id(0)
    f = pl.program_id(1)
    n = nt_ref[s]

    @pl.when((f == 0) & (n == 0))
    def _():
        o_ref[...] = jnp.zeros_like(o_ref)

    @pl.when((f == 0) & (n > 0))
    def _():
        row = lax.broadcasted_iota(jnp.int32, (SUB, CHUNK), 0)

        def gather(j, carry):
            row0 = (s * N_SUB + j) * SUB
            xs_s[j] = jnp.zeros((SUB, D), BF16)

            def chunk(c, carry):
                tpos = pos_ref[:, pl.ds(pl.multiple_of(c * CHUNK, CHUNK), CHUNK)]
                onehot = jnp.where(tpos - row0 == row, 1.0, 0.0).astype(BF16)
                rows = h_ref[pl.ds(pl.multiple_of(c * CHUNK, CHUNK), CHUNK), :]
                xs_s[j] += jnp.dot(onehot, rows, preferred_element_type=F32).astype(BF16)
                return carry

            g = s * N_SUB + j
            lax.fori_loop(clo_ref[g], chi_ref[g] + 1, chunk, 0)
            acc_s[j] = jnp.zeros((SUB, D), F32)
            return carry

        lax.fori_loop(0, n, gather, 0)

    @pl.when(n > 0)
    def _():
        wg_s[...] = wg_ref[...].astype(BF16)
        wu_s[...] = wu_ref[...].astype(BF16)
        wd_s[...] = wd_ref[...].astype(BF16)

        def sub(j):
            x = xs_s[j]
            g = jnp.dot(x, wg_s[...], preferred_element_type=F32)
            u = jnp.dot(x, wu_s[...], preferred_element_type=F32)
            acc_s[j] += jnp.dot((_silu(g) * u).astype(BF16), wd_s[...], preferred_element_type=F32)

        def pair(jj, carry):
            sub(2 * jj)
            sub(2 * jj + 1)
            return carry

        lax.fori_loop(0, n // 2, pair, 0)

        @pl.when(n % 2 == 1)
        def _():
            sub(n - 1)

    @pl.when((f == pl.num_programs(1) - 1) & (n > 0))
    def _():
        for j in range(N_SUB):
            rows = slice(j * SUB, (j + 1) * SUB)

            @pl.when(j < n)
            def _():
                o_ref[rows, :] = acc_s[j].astype(BF16)

            @pl.when(j >= n)
            def _():
                o_ref[rows, :] = jnp.zeros((SUB, D), BF16)


def _moe_experts(h, pos_t, meta, wg, wu, wd, layer):
    tf = 512
    nf = D_FFE // tf
    se, nt, sblk, clo, chi = meta

    def w_up(s, f, se, nt, sblk, clo, chi):
        return (layer, se[s], 0, jnp.where(nt[s] > 0, f, nf - 1))

    def w_down(s, f, se, nt, sblk, clo, chi):
        return (layer, se[s], jnp.where(nt[s] > 0, f, nf - 1), 0)

    grid_spec = pltpu.PrefetchScalarGridSpec(
        num_scalar_prefetch=5,
        grid=(S_MAX, nf),
        in_specs=[pl.BlockSpec((None, 1, M_ALL), lambda s, f, se, nt, sblk, clo, chi: (se[s], 0, 0)),
                  pl.BlockSpec((M_ALL, D), lambda s, f, *_: (0, 0), pipeline_mode=pl.Buffered(1)),
                  pl.BlockSpec((None, None, D, tf), w_up), pl.BlockSpec((None, None, D, tf), w_up),
                  pl.BlockSpec((None, None, tf, D), w_down)],
        out_specs=pl.BlockSpec((SUP, D), lambda s, f, *_: (s, 0)),
        scratch_shapes=[pltpu.VMEM((N_SUB, SUB, D), BF16), pltpu.VMEM((N_SUB, SUB, D), F32),
                        pltpu.VMEM((D, tf), BF16), pltpu.VMEM((D, tf), BF16), pltpu.VMEM((tf, D), BF16)],
    )
    return pl.pallas_call(
        _moe_expert_kernel,
        out_shape=jax.ShapeDtypeStruct((P_SLOT, D), BF16),
        grid_spec=grid_spec,
        compiler_params=pltpu.CompilerParams(dimension_semantics=("arbitrary", "arbitrary"),
                                             vmem_limit_bytes=60 * 1024 * 1024),
        name="moe_experts",
    )(se, nt, sblk, clo, chi, pos_t, h, wg, wu, wd)


def _moe_combine_kernel(off_ref, x_ref, gt_ref, gate_ref, pos_ref, *refs):
    win_refs, o_ref = refs[:N_EXP], refs[N_EXP]
    i = pl.program_id(0)
    lane = lax.broadcasted_iota(jnp.int32, (TC, WIN), 1)
    y = jnp.zeros((TC, D), F32)
    for e in range(N_EXP):
        rel = pos_ref[:, e:e + 1] - off_ref[i * N_EXP + e] * 16
        onehot = jnp.where(rel == lane, 1.0, 0.0).astype(BF16)
        y = y + gate_ref[:, e:e + 1] * jnp.dot(onehot, win_refs[e][...], preferred_element_type=F32)
    o_ref[...] = x_ref[...] + gt_ref[...] * y


def _moe_combine(x, mod, gates, pos, off, slots, l):
    def win_spec(e):
        return pl.BlockSpec((pl.Element(WIN), pl.Element(D)), lambda i, off: (off[i * N_EXP + e] * 16, 0))

    grid_spec = pltpu.PrefetchScalarGridSpec(
        num_scalar_prefetch=1,
        grid=(M_ALL // TC,),
        in_specs=[pl.BlockSpec((TC, D), lambda i, off: (i, 0)),
                  _mod_spec(l, 5, TC),
                  pl.BlockSpec((TC, LANE), lambda i, off: (i, 0)), pl.BlockSpec((TC, LANE), lambda i, off: (i, 0))]
        + [win_spec(e) for e in range(N_EXP)],
        out_specs=pl.BlockSpec((TC, D), lambda i, off: (i, 0)),
    )
    return pl.pallas_call(
        _moe_combine_kernel,
        out_shape=jax.ShapeDtypeStruct((M_ALL, D), F32),
        grid_spec=grid_spec,
        compiler_params=_cparams("arbitrary"),
        name="moe_combine",
    )(off, x, mod, gates, pos, *([slots] * N_EXP))


def _moe_plan(sel):
    i32 = jnp.int32
    sel_t = sel[:, :N_EXP].T
    csum = jnp.cumsum(sel_t, axis=1)
    rank = csum - sel_t
    n_e = csum[:, -1]
    ns_e = (n_e + SUP - 1) // SUP
    end_e = jnp.cumsum(ns_e)
    start_e = end_e - ns_e
    n_used = end_e[-1]
    pos_t = jnp.where(sel_t > 0, start_e[:, None] * SUP + rank, -1)
    s_ids = jnp.arange(S_MAX, dtype=i32)
    sblk = jnp.minimum(s_ids, n_used - 1)
    se = jnp.sum((end_e[None, :] <= sblk[:, None]).astype(i32), axis=1)
    nv = jnp.clip(n_e[se] - (sblk - start_e[se]) * SUP, 0, SUP)
    nt = jnp.where(s_ids < n_used, (nv + SUB - 1) // SUB, 0)
    g_ids = jnp.arange(S_MAX * N_SUB, dtype=i32)
    s_g, e_g = g_ids // N_SUB, se[g_ids // N_SUB]
    r0 = (sblk[s_g] - start_e[e_g]) * SUP + (g_ids % N_SUB) * SUB
    r1 = jnp.minimum(r0 + SUB, n_e[e_g])
    live = (s_g < n_used) & (r0 < n_e[e_g])
    chunk_end = csum[:, CHUNK - 1::CHUNK][e_g]
    clo = jnp.where(live, jnp.sum((chunk_end <= r0[:, None]).astype(i32), axis=1), 0)
    chi = jnp.where(live, jnp.sum((chunk_end < r1[:, None]).astype(i32), axis=1), -1)
    before = jnp.concatenate([jnp.zeros((N_EXP, 1), i32), csum[:, TC - 1::TC][:, :-1]], axis=1)
    base = start_e[:, None] * SUP + before
    off = jnp.minimum(base // 16, (P_SLOT - WIN) // 16).T.reshape(-1)
    pos128 = jnp.pad(pos_t.T, ((0, 0), (0, LANE - N_EXP)), constant_values=-1)
    meta = (se.astype(i32), nt.astype(i32), sblk.astype(i32), clo.astype(i32), chi.astype(i32))
    return meta, pos_t.reshape(N_EXP, 1, M_ALL), pos128, off.astype(i32)


def _moe(h, x, mod, gates, sel, wg, wu, wd, l):
    meta, pos_t, pos, off = _moe_plan(sel)
    slots = _moe_experts(h, pos_t, meta, wg, wu, wd, l // 2)
    return _moe_combine(x, mod, gates, pos, off, slots, l)


def _final_kernel(tm, x_ref, g_ref, oc_ref, ol_ref):
    y = _rms(x_ref[...]) * g_ref[...]
    is_lat = pl.program_id(0) >= M_CTX // tm

    @pl.when(jnp.logical_not(is_lat))
    def _():
        oc_ref[...] = y

    @pl.when(is_lat)
    def _():
        ol_ref[...] = y


def _final_norm(x, g):
    tm = 1024
    return pl.pallas_call(
        functools.partial(_final_kernel, tm),
        out_shape=[jax.ShapeDtypeStruct((M_CTX, D), F32), jax.ShapeDtypeStruct((M_LAT, D), F32)],
        grid=(M_ALL // tm,),
        in_specs=[pl.BlockSpec((tm, D), lambda i: (i, 0)), pl.BlockSpec((1, D), lambda i: (0, 0))],
        out_specs=_row_split_specs(tm, D),
        compiler_params=_cparams("arbitrary"),
        name="final_norm",
    )(x, g)


def kernel(x_prompt, x_sample, state_gla, state_ret, cache_mla_ckv, cache_mla_krope, cache_gqa_k, cache_gqa_v,
           c, c_ctx, norm1_g, norm2_g, final_norm_g, w_mod, b_mod, w_in, w_out, gla_gate_w, gla_gate_b,
           mla_q_norm_g, mla_w_q_up, mla_kv_norm_g, mla_w_kv_up, ret_decay, gqa_sink,
           ffn_w_gate, ffn_w_up, ffn_w_down, moe_router, moe_w_gate, moe_w_up, moe_w_down):
    xs = (x_prompt.reshape(M_CTX, D), x_sample.reshape(M_LAT, D))

    cond = jnp.concatenate([c_ctx[None], c, jnp.zeros((8 - 1 - N_LAT, D), F32)], axis=0)
    mod = _modulation(cond, w_mod, b_mod)
    mod = mod[:, :1 + N_LAT].reshape(DEPTH, (1 + N_LAT) * 6, 1, D)

    w_out_b = w_out.astype(BF16)
    wq = mla_w_q_up.reshape(DEPTH, MLA_QR, MLA_H, MLA_NOPE + MLA_ROPE)
    wq = jnp.concatenate([wq[..., :MLA_NOPE].reshape(DEPTH, MLA_QR, MLA_H * MLA_NOPE),
                          wq[..., MLA_NOPE:].reshape(DEPTH, MLA_QR, MLA_H * MLA_ROPE)], axis=-1).astype(BF16)
    wkv = mla_w_kv_up.reshape(DEPTH, MLA_KVR, MLA_H, MLA_NOPE + MLA_DV)
    wkv = jnp.concatenate([wkv[..., :MLA_NOPE].reshape(DEPTH, MLA_KVR, MLA_H * MLA_NOPE),
                           wkv[..., MLA_NOPE:].reshape(DEPTH, MLA_KVR, MLA_H * MLA_DV)], axis=-1).astype(BF16)
    router = jnp.pad(moe_router, ((0, 0), (0, 0), (0, LANE - N_EXP))).astype(BF16)
    dec_lanes = jnp.repeat(ret_decay, RET_DV, axis=-1)
    sink_lanes = jnp.repeat(gqa_sink, GQA_HD, axis=-1).reshape(DEPTH, 1, 256)
    gate_b = gla_gate_b.reshape(DEPTH, 2, 1, GLA_H * GLA_DK)
    g1, g2 = norm1_g.reshape(DEPTH, 1, D), norm2_g.reshape(DEPTH, 1, D)
    gq, gkv = mla_q_norm_g.reshape(DEPTH, 1, MLA_QR), mla_kv_norm_g.reshape(DEPTH, 1, MLA_KVR)
    eye = jnp.eye(GLA_H, dtype=F32)
    s0_gla = jnp.einsum('bldhkv,hg->bldhvgk', state_gla, eye).reshape(
        N_LAT, DEPTH, 2, GLA_H * GLA_DV, GLA_H * GLA_DK)
    s0_ret = jnp.einsum('bldhkv,hg->bldhkgv', state_ret, jnp.eye(RET_H, dtype=F32)).reshape(
        N_LAT, DEPTH, 2, RET_H * RET_DK, RET_H * RET_DV)
    cache_kr = jnp.pad(cache_mla_krope, ((0, 0), (0, 0), (0, 0), (0, 128 - MLA_ROPE)))

    rope64 = _rope_tables(T_LAT, 64, 256)
    rope32_q = _rope_tables(T_LAT, 32, 128)
    ck, sl, sh = _rope_tables(T_LAT, 32, 128)
    live = jnp.asarray((np.arange(128) < MLA_ROPE).astype(np.float32))[None]
    rope32_k = (ck * live, sl * live, sh * live)

    st_gla = st_ret = caches_mla = caches_gqa = None
    for l in range(DEPTH):
        p_gla, p_mla, p_ret, p_gqa = _inproj(xs, g1, mod, w_in, l)

        o_gla_c, st_gla, o_mla_c, ckv_all, kr_all, o_ret_c, st_ret, o_gqa_c, gk_all, gv_all = _fused_call([
            _gla(p_gla, gla_gate_w, gate_b, None, st_gla, l, False),
            _mla(p_mla, gq, gkv, wq, wkv, None, None, None, None, caches_mla, l, False),
            _ret(p_ret, dec_lanes, None, None, st_ret, l, False),
            _gqa(p_gqa, sink_lanes, None, None, None, caches_gqa, l, False)], N_CTX, "mixers_context")
        caches_mla, caches_gqa = (ckv_all, kr_all), (gk_all, gv_all)
        o_gla_s, o_ret_s = _fused_call([
            _gla(p_gla, gla_gate_w, gate_b, s0_gla, None, l, True),
            _ret(p_ret, dec_lanes, rope64, s0_ret, None, l, True)], N_LAT, "mixers_latent_scan")
        o_mla_s, o_gqa_s = _fused_call([
            _mla(p_mla, gq, gkv, wq, wkv, rope32_q, rope32_k, cache_mla_ckv, cache_kr, None, l, True),
            _gqa(p_gqa, sink_lanes, rope64, cache_gqa_k, cache_gqa_v, None, l, True)], N_LAT, "mixers_latent_attn")
        o_ctx = (o_gla_c, o_mla_c, o_ret_c, o_gqa_c)
        o_lat = (o_gla_s, o_mla_s, o_ret_s, o_gqa_s)
        if l % 2 == 0:
            x, h2 = _outproj(xs, o_ctx, o_lat, w_out_b, g2, mod, None, l)
            x = _ffn(h2, x, mod, ffn_w_gate, ffn_w_up, ffn_w_down, l)
        else:
            x, h2, gates, sel = _outproj(xs, o_ctx, o_lat, w_out_b, g2, mod, router, l)
            x = _moe(h2, x, mod, gates, sel, moe_w_gate, moe_w_up, moe_w_down, l)
        xs = (x,)

    y_ctx, y_lat = _final_norm(x, final_norm_g[None])
    return (y_ctx.reshape(N_CTX, T_CTX, D), y_lat.reshape(N_LAT, T_LAT, D), jnp.swapaxes(st_gla, -1, -2), st_ret,
            *caches_mla, *caches_gqa)
```

```python
import functools

import numpy as np
import jax
import jax.numpy as jnp
from jax import lax
from jax.experimental import pallas as pl
from jax.experimental.pallas import tpu as pltpu

F32 = jnp.float32
BF16 = jnp.bfloat16
HIGHEST = lax.Precision.HIGHEST

D = 1024
N_CTX, T_CTX = 16, 256
N_LAT, T_LAT = 2, 1024
PAST = 256
DEPTH = 4
M_CTX = N_CTX * T_CTX
M_LAT = N_LAT * T_LAT
M_ALL = M_CTX + M_LAT
GRID_W = 64
ROPE_BASE = 10000.0
EPS = 1e-6

GLA_H, GLA_DK, GLA_DV, GLA_RANK, GLA_NORM, GLA_C = 4, 32, 64, 16, 16.0, 64
MLA_H, MLA_QR, MLA_KVR, MLA_NOPE, MLA_ROPE, MLA_DV = 4, 256, 128, 64, 32, 64
RET_H, RET_DK, RET_DV = 4, 64, 64
GQA_H, GQA_KV, GQA_HD, WINDOW = 4, 2, 64, 128
D_FF, N_EXP, D_FFE = 2816, 8, 3584

W_GLA, W_MLA, W_RET, W_GQA = 896, 512, 1024, 512
IN_WIDTH = 2752
IN_GROUPS = ((0, 800), (800, 1216), (1216, 2240), (2240, 2752))
LANE = 128
NEG = -1e30
QB = 256
QL = 128
VMEM_LIMIT = 56 * 1024 * 1024


def _cparams(*sem):
    return pltpu.CompilerParams(dimension_semantics=sem, vmem_limit_bytes=VMEM_LIMIT)


def _mm(a, b):
    return jnp.dot(a.astype(BF16), b.astype(BF16), preferred_element_type=F32)


def _mm_nt(a, b):
    return lax.dot_general(a.astype(BF16), b.astype(BF16), (((1,), (1,)), ((), ())), preferred_element_type=F32)


def _mm_tn(a, b):
    return lax.dot_general(a.astype(BF16), b.astype(BF16), (((0,), (0,)), ((), ())), preferred_element_type=F32)


def _mm_f32(a, b):
    return jnp.dot(a, b, precision=HIGHEST, preferred_element_type=F32)


def _silu(x):
    return x * (1.0 / (1.0 + jnp.exp(-x)))


def _log_sigmoid(x):
    return jnp.minimum(x, 0.0) - jnp.log1p(jnp.exp(-jnp.abs(x)))


def _rms(x):
    return x * lax.rsqrt(jnp.mean(x * x, axis=-1, keepdims=True) + EPS)


def _mod_row(tile, tm):
    return jnp.maximum((tile * tm) // T_LAT - (M_CTX // T_LAT - 1), 0)


def _mod_kernel(c_ref, w_ref, b_ref, o_ref):
    o_ref[...] = _mm(_silu(c_ref[...]), w_ref[...]) + b_ref[...]


def _modulation(cond, w_mod, b_mod):
    tn = 1536
    return pl.pallas_call(
        _mod_kernel,
        out_shape=jax.ShapeDtypeStruct((DEPTH, 8, 6 * D), F32),
        grid=(DEPTH, 6 * D // tn),
        in_specs=[pl.BlockSpec((8, D), lambda l, j: (0, 0)),
                  pl.BlockSpec((None, D, tn), lambda l, j: (l, 0, j)),
                  pl.BlockSpec((None, 1, tn), lambda l, j: (l, 0, j))],
        out_specs=pl.BlockSpec((None, 8, tn), lambda l, j: (l, 0, j)),
        compiler_params=_cparams("parallel", "parallel"),
        name="modulation",
    )(cond, w_mod, b_mod.reshape(DEPTH, 1, 6 * D))


def _inproj_kernel(split, tm, *refs):
    if split:
        xc_ref, xl_ref = refs[0:2]
        x = jnp.where(pl.program_id(0) >= M_CTX // tm, xl_ref[...], xc_ref[...])
        refs = refs[2:]
    else:
        x = refs[0][...]
        refs = refs[1:]
    g_ref, sh_ref, sc_ref, w_ref = refs[0:4]
    out_refs, w_s = refs[4:-1], refs[-1]

    @pl.when(pl.program_id(0) == 0)
    def _():
        dst = 0
        for (lo, hi), o_ref in zip(IN_GROUPS, out_refs):
            width = o_ref.shape[-1]
            w_s[:, dst:dst + hi - lo] = w_ref[:, lo:hi].astype(BF16)
            if width > hi - lo:
                w_s[:, dst + hi - lo:dst + width] = jnp.zeros((D, width - (hi - lo)), BF16)
            dst += width

    h = (_rms(x) * g_ref[...] * (1.0 + sc_ref[...]) + sh_ref[...]).astype(BF16)
    col = 0
    for o_ref in out_refs:
        width = o_ref.shape[-1]
        o_ref[...] = jnp.dot(h, w_s[:, col:col + width], preferred_element_type=F32)
        col += width


def _mod_spec(l, j, tm):
    return pl.BlockSpec((None, None, 1, D), lambda i, *_: (l, _mod_row(i, tm) * 6 + j, 0, 0))


def _layer_spec(l, *shape):
    return pl.BlockSpec((None,) + shape, lambda *_: (l,) + (0,) * len(shape))


def _stacked_output(prev, tail, l, nseq, in_specs, args, out_shape, out_specs, aliases):
    out_shape.append(jax.ShapeDtypeStruct((nseq, DEPTH) + tail, F32))
    out_specs.append(pl.BlockSpec((None, None) + tail, lambda b: (b, l) + (0,) * len(tail)))
    if prev is not None:
        in_specs.append(pl.BlockSpec(memory_space=pl.ANY))
        args.append(prev)
        aliases[len(args) - 1] = len(out_shape) - 1


def _row_split_specs(tm, width):
    n_ctx = M_CTX // tm
    return [pl.BlockSpec((tm, width), lambda i: (jnp.minimum(i, n_ctx - 1), 0)),
            pl.BlockSpec((tm, width), lambda i: (jnp.maximum(i - n_ctx, 0), 0))]


def _inproj(xs, g1, mod, w_in, l):
    tm = 512
    split = len(xs) == 2
    rows = lambda w: pl.BlockSpec((tm, w), lambda i: (i, 0))
    widths = (W_GLA, W_MLA, W_RET, W_GQA)
    x_specs = _row_split_specs(tm, D) if split else [rows(D)]
    w_spec = pl.BlockSpec((None, D, IN_WIDTH), lambda i: (l, 0, 0), pipeline_mode=pl.Buffered(1))
    return pl.pallas_call(
        functools.partial(_inproj_kernel, split, tm),
        out_shape=[jax.ShapeDtypeStruct((M_ALL, w), F32) for w in widths],
        grid=(M_ALL // tm,),
        in_specs=x_specs + [_layer_spec(l, 1, D), _mod_spec(l, 0, tm), _mod_spec(l, 1, tm), w_spec],
        out_specs=[rows(w) for w in widths],
        scratch_shapes=[pltpu.VMEM((D, sum(widths)), BF16)],
        compiler_params=_cparams("arbitrary"),
        name="inproj",
    )(*xs, g1, mod, mod, w_in)


def _head_rms_gate(o, gate):
    r = lax.broadcasted_iota(jnp.int32, (256, 256), 0) // 64
    c = lax.broadcasted_iota(jnp.int32, (256, 256), 1) // 64
    group_mean = jnp.where(r == c, 1.0 / 64.0, 0.0).astype(BF16)
    sq = o * o
    sq_hi = sq.astype(BF16)
    sq_lo = (sq - sq_hi.astype(F32)).astype(BF16)
    ms = (jnp.dot(sq_hi, group_mean, preferred_element_type=F32)
          + jnp.dot(sq_lo, group_mean, preferred_element_type=F32))
    return o * lax.rsqrt(ms + EPS) * _silu(gate)


def _own_lanes(n_heads, rows_per_head, lanes_per_head):
    shape = (n_heads * rows_per_head, n_heads * lanes_per_head)
    return (lax.broadcasted_iota(jnp.int32, shape, 0) // rows_per_head
            == lax.broadcasted_iota(jnp.int32, shape, 1) // lanes_per_head).astype(F32)


def _stack_heads(x, n_heads, lanes_per_head):
    return jnp.concatenate([x] * n_heads, axis=0) * _own_lanes(n_heads, x.shape[0], lanes_per_head)


def _unstack_heads(y, n_heads, lanes_per_head):
    t = y.shape[0] // n_heads
    y = y * _own_lanes(n_heads, t, lanes_per_head)
    out = y[0:t]
    for h in range(1, n_heads):
        out = out + y[h * t:(h + 1) * t]
    return out


def _per_query_head(x):
    lane = lax.broadcasted_iota(jnp.int32, x.shape, 1)
    swapped = pltpu.roll(x, GQA_HD, 1)
    return jnp.concatenate([jnp.where(lane < GQA_HD, x, swapped), jnp.where(lane < GQA_HD, swapped, x)], axis=1)


def _rope(x, cos, sin_lo, sin_hi, half):
    w = x.shape[-1]
    return x * cos + pltpu.roll(x, w - half, 1) * sin_lo + pltpu.roll(x, half, 1) * sin_hi


def _rope_tables(t, head_dim, width):
    half = head_dim // 2
    quarter = head_dim // 4
    pos = np.arange(t)
    rows = (pos // GRID_W).astype(np.float32)
    cols = (pos % GRID_W).astype(np.float32)
    inv = np.power(np.float32(ROPE_BASE), -np.arange(quarter, dtype=np.float32) / np.float32(quarter)).astype(np.float32)
    ang = np.concatenate([rows[:, None] * inv, cols[:, None] * inv], axis=-1).astype(np.float32)
    lane = np.arange(width)
    a = ang[:, lane % half]
    cos, sin = np.cos(a).astype(np.float32), np.sin(a).astype(np.float32)
    low = (lane % head_dim) < half
    return (jnp.asarray(cos), jnp.asarray(np.where(low[None], -sin, 0.0).astype(np.float32)),
            jnp.asarray(np.where(low[None], 0.0, sin).astype(np.float32)))


def _seq_spec(t, width, row0, latent):
    return pl.BlockSpec((t, width), lambda b: (row0 + b, 0), pipeline_mode=pl.Buffered(1) if latent else None)


def _table_spec(t, width):
    return pl.BlockSpec((t, width), lambda b: (0, 0), pipeline_mode=pl.Buffered(1))


def _fused_call(parts, nseq, name):
    in_specs, args, out_shape, out_specs, scratch, aliases, layout = [], [], [], [], [], {}, []
    for fn, p_in, p_args, p_shape, p_out, p_scratch, p_alias in parts:
        layout.append((fn, len(p_args), len(p_shape), len(p_scratch)))
        for k, v in p_alias.items():
            aliases[len(args) + k] = len(out_shape) + v
        in_specs += p_in
        args += p_args
        out_shape += p_shape
        out_specs += p_out
        scratch += p_scratch
    n_in, n_out = len(args), len(out_shape)

    def kernel(*refs):
        i, o, s = 0, n_in, n_in + n_out
        for fn, ni, no, ns in layout:
            fn(*refs[i:i + ni], *refs[o:o + no], *refs[s:s + ns])
            i, o, s = i + ni, o + no, s + ns

    return pl.pallas_call(
        kernel, out_shape=out_shape, grid=(nseq,), in_specs=in_specs, out_specs=out_specs, scratch_shapes=scratch,
        input_output_aliases=aliases, name=name,
        compiler_params=pltpu.CompilerParams(dimension_semantics=("arbitrary",), vmem_limit_bytes=60 * 1024 * 1024),
    )(*args)


def _gla_kernel(t, latent, *refs):
    if latent:
        p_ref, gw_ref, gb_ref, s0_ref, o_ref, la_f, la_b, of_s, ob_s, st_s = refs
    else:
        p_ref, gw_ref, gb_ref = refs[0:3]
        o_ref, st_ref, la_f, la_b, of_s, ob_s, st_s = refs[-7:]
    c = GLA_C
    gr = 256
    nc = gr // c
    n = t // gr
    hd = GLA_H * GLA_DK
    la_f[...] = _log_sigmoid(_mm_f32(p_ref[:, 768:784], gw_ref[0]) + gb_ref[0]) / GLA_NORM
    la_b[...] = _log_sigmoid(_mm_f32(p_ref[:, 784:800], gw_ref[1]) + gb_ref[1]) / GLA_NORM

    ri = lax.broadcasted_iota(jnp.int32, (gr, gr), 0)
    ci = lax.broadcasted_iota(jnp.int32, (gr, gr), 1)
    same = (ri // c) == (ci // c)
    rs = lax.broadcasted_iota(jnp.int32, (GLA_H * gr, gr), 0) % gr
    cs = lax.broadcasted_iota(jnp.int32, (GLA_H * gr, gr), 1)
    same_s = (rs // c) == (cs // c)
    own_chunk = (lax.broadcasted_iota(jnp.int32, (gr, nc * hd), 0) // c
                 == lax.broadcasted_iota(jnp.int32, (gr, nc * hd), 1) // hd).astype(F32)
    own_dk = _own_lanes(GLA_H, GLA_DV, GLA_DK)
    own_dk = jnp.concatenate([own_dk] * nc, axis=1)
    scale = GLA_DK ** -0.5
    if latent:
        st_s[...] = s0_ref[...]
    else:
        st_s[...] = jnp.zeros_like(st_s)

    def group(row0, la_ref, d):
        fwd = d == 0
        ordered = (ci <= ri) if fwd else (ci >= ri)
        first_half = (ci % c < c // 2) if fwd else (ci % c >= c // 2)
        keep = same_s & ((cs <= rs) if fwd else (cs >= rs))
        sums_of = jnp.concatenate([(same & ordered).astype(BF16), same.astype(BF16), (same & first_half).astype(BF16)],
                                  axis=0)
        q = p_ref[pl.ds(row0, gr), 0:128]
        k = p_ref[pl.ds(row0, gr), 128:256] * scale
        v = p_ref[pl.ds(row0, gr), 256:512]
        la = la_ref[pl.ds(row0, gr), :]
        la_hi = la.astype(BF16)
        la_lo = (la - la_hi.astype(F32)).astype(BF16)
        sums = jnp.dot(sums_of, la_hi, preferred_element_type=F32) + jnp.dot(sums_of, la_lo, preferred_element_type=F32)
        bc, tot, mid = sums[0:gr], sums[gr:2 * gr], sums[2 * gr:3 * gr]
        qe, ke = q * jnp.exp(bc - mid), k * jnp.exp(mid - bc)
        q_in, k_out, a = q * jnp.exp(bc), k * jnp.exp(tot - bc), jnp.exp(tot)
        att = jnp.where(keep, _mm_nt(_stack_heads(qe, GLA_H, GLA_DK), ke), 0.0)
        o = _unstack_heads(_mm(att, v), GLA_H, GLA_DV)
        kv = _mm_tn(v, jnp.concatenate([k_out] * nc, axis=1) * own_chunk) * own_dk
        st, entering = st_s[d], [None] * nc
        for cc in (range(nc) if fwd else range(nc - 1, -1, -1)):
            entering[cc] = st
            st = st * a[cc * c:cc * c + 1, :] + kv[:, cc * hd:(cc + 1) * hd]
        st_s[d] = st
        return o + _mm_nt(jnp.concatenate([q_in] * nc, axis=1) * own_chunk, jnp.concatenate(entering, axis=1))

    def body(i, carry):
        rf = pl.multiple_of(i * gr, gr)
        rb = pl.multiple_of((n - 1 - i) * gr, gr)
        of_s[pl.ds(rf, gr), :] = group(rf, la_f, 0)
        ob_s[pl.ds(rb, gr), :] = group(rb, la_b, 1)
        return carry

    lax.fori_loop(0, n, body, 0)
    if not latent:
        for d in range(2):
            for h in range(GLA_H):
                st_ref[d, h] = st_s[d, h * GLA_DV:(h + 1) * GLA_DV, h * GLA_DK:(h + 1) * GLA_DK]
    for r in range(t // QB):
        rows = slice(r * QB, (r + 1) * QB)
        o_ref[rows, :] = _head_rms_gate(of_s[rows, :] + ob_s[rows, :], p_ref[rows, 512:768]).astype(BF16)


def _gla(p, gate_w, gate_b, s0_bd, st_prev, l, latent):
    t, nseq, row0 = (T_LAT, N_LAT, M_CTX // T_LAT) if latent else (T_CTX, N_CTX, 0)
    st_shape = (2, GLA_H * GLA_DV, GLA_H * GLA_DK)
    in_specs = [_seq_spec(t, W_GLA, row0, latent),
                _layer_spec(l, 2, GLA_RANK, GLA_H * GLA_DK), _layer_spec(l, 2, 1, GLA_H * GLA_DK)]
    args = [p, gate_w, gate_b]
    aliases = {}
    o_shape = jax.ShapeDtypeStruct((nseq * t, 256), BF16)
    o_spec = pl.BlockSpec((t, 256), lambda b: (b, 0))
    if latent:
        in_specs.append(pl.BlockSpec((None, None) + st_shape, lambda b: (b, l, 0, 0, 0)))
        args.append(s0_bd)
    out_shape, out_specs = [o_shape], [o_spec]
    if not latent:
        _stacked_output(st_prev, (2, GLA_H, GLA_DV, GLA_DK), l, nseq, in_specs, args, out_shape, out_specs, aliases)
    scratch = [pltpu.VMEM((t, 128), F32), pltpu.VMEM((t, 128), F32),
               pltpu.VMEM((t, 256), F32), pltpu.VMEM((t, 256), F32), pltpu.VMEM(st_shape, F32)]
    return functools.partial(_gla_kernel, t, latent), in_specs, args, out_shape, out_specs, scratch, aliases


def _ret_kernel(t, latent, *refs):
    if latent:
        p_ref, dec_ref, cos_ref, slo_ref, shi_ref, s0_ref, o_ref, q_s, k_s, decay_s = refs
    else:
        p_ref, dec_ref = refs[0:2]
        o_ref, st_ref, decay_s = refs[-3:]
    lg = _log_sigmoid(dec_ref[...])
    scale = RET_DK ** -0.5
    nblk = t // QB

    @pl.when(pl.program_id(0) == 0)
    def _():
        wide = decay_s.shape[-1]
        dist = (lax.broadcasted_iota(jnp.int32, (QB, wide), 0) - lax.broadcasted_iota(jnp.int32, (QB, wide), 1)
                + (nblk - 1) * QB).astype(F32)
        for h in range(RET_H):
            lg_f, lg_b = lg[0:1, h * 64:h * 64 + 1], lg[1:2, h * 64:h * 64 + 1]
            decay_s[h] = (jnp.where(dist >= 0, jnp.exp(jnp.maximum(dist, 0.0) * lg_f), 0.0)
                          + jnp.where(dist <= 0, jnp.exp(jnp.maximum(-dist, 0.0) * lg_b), 0.0))

    if not latent:
        k = p_ref[:, 256:512] * scale
        v = p_ref[:, 512:768]
        decay = jnp.concatenate([decay_s[h] for h in range(RET_H)], axis=0)
        o = _unstack_heads(_mm(_mm_nt(_stack_heads(p_ref[:, 0:256], RET_H, RET_DK), k) * decay, v), RET_H, RET_DV)
        o_ref[...] = _head_rms_gate(o, p_ref[:, 768:1024]).astype(BF16)
        j = lax.broadcasted_iota(jnp.int32, (t, 1), 0).astype(F32)
        kv_f = _mm_tn(k * jnp.exp((float(t - 1) - j) * lg[0:1, :]), v)
        kv_b = _mm_tn(k * jnp.exp(j * lg[1:2, :]), v)
        for h in range(RET_H):
            st_ref[0, h] = kv_f[h * RET_DK:(h + 1) * RET_DK, h * RET_DV:(h + 1) * RET_DV]
            st_ref[1, h] = kv_b[h * RET_DK:(h + 1) * RET_DK, h * RET_DV:(h + 1) * RET_DV]
        return
    q_s[...] = _rope(p_ref[:, 0:256], cos_ref[...], slo_ref[...], shi_ref[...], RET_DK // 2)
    k_s[...] = _rope(p_ref[:, 256:512], cos_ref[...], slo_ref[...], shi_ref[...], RET_DK // 2) * scale
    for r in range(t // QL):
        rows = slice(r * QL, (r + 1) * QL)
        pos = (lax.broadcasted_iota(jnp.int32, (QL, 1), 0) + r * QL).astype(F32)
        strip_rows = slice(r * QL % QB, r * QL % QB + QL)
        off = (nblk - 1 - r * QL // QB) * QB
        decay = jnp.concatenate([decay_s[h, strip_rows, off:off + t] for h in range(RET_H)], axis=0)
        q = q_s[rows, :]
        o = _unstack_heads(_mm(_mm_nt(_stack_heads(q, RET_H, RET_DK), k_s[...]) * decay, p_ref[:, 512:768]),
                           RET_H, RET_DV)
        o = o + _mm(q * jnp.exp((pos + 1.0) * lg[0:1, :]), s0_ref[0])
        o = o + _mm(q * jnp.exp((float(t) - pos) * lg[1:2, :]), s0_ref[1])
        o_ref[rows, :] = _head_rms_gate(o, p_ref[rows, 768:1024]).astype(BF16)


def _ret(p, dec_lanes, tables, s0, st_prev, l, latent):
    t, nseq, row0 = (T_LAT, N_LAT, M_CTX // T_LAT) if latent else (T_CTX, N_CTX, 0)
    st_shape = (2, RET_H, RET_DK, RET_DV)
    in_specs = [_seq_spec(t, W_RET, row0, latent), _layer_spec(l, 2, 256)]
    args = [p, dec_lanes]
    aliases = {}
    o_shape = jax.ShapeDtypeStruct((nseq * t, 256), BF16)
    o_spec = pl.BlockSpec((t, 256), lambda b: (b, 0))
    scratch = []
    if latent:
        in_specs += [_table_spec(t, 256)] * 3
        in_specs.append(pl.BlockSpec((None, None, 2, RET_H * RET_DK, RET_H * RET_DV), lambda b: (b, l, 0, 0, 0)))
        args += list(tables) + [s0]
        scratch = [pltpu.VMEM((t, 256), F32), pltpu.VMEM((t, 256), F32)]
    scratch.append(pltpu.VMEM((RET_H, QB, 2 * t - QB), F32))
    out_shape, out_specs = [o_shape], [o_spec]
    if not latent:
        _stacked_output(st_prev, st_shape, l, nseq, in_specs, args, out_shape, out_specs, aliases)
    return functools.partial(_ret_kernel, t, latent), in_specs, args, out_shape, out_specs, scratch, aliases


def _mla_kernel(t, latent, *refs):
    if latent:
        (p_ref, gq_ref, gkv_ref, wq_ref, wkv_ref, cq_ref, slq_ref, shq_ref, ck_ref, slk_ref, shk_ref,
         cckv_ref, ckr_ref, o_ref, qn_s, qr_s, kn_s, kr_s, v_s) = refs
    else:
        p_ref, gq_ref, gkv_ref, wq_ref, wkv_ref = refs[0:5]
        o_ref, ckv_ref, kro_ref, qn_s, qr_s, kn_s, kr_s, v_s = refs[-8:]
    nk = t + (PAST if latent else 0)
    qh = _mm(_rms(p_ref[:, 0:256]) * gq_ref[...], wq_ref[...])
    ckv = _rms(p_ref[:, 256:384]) * gkv_ref[...]
    kv = _mm(ckv, wkv_ref[...])
    qn_s[...] = qh[:, 0:256]
    kn_s[0:t, :] = kv[:, 0:256]
    v_s[0:t, :] = kv[:, 256:512]
    if latent:
        qr_s[...] = _rope(qh[:, 256:384], cq_ref[...], slq_ref[...], shq_ref[...], MLA_ROPE // 2)
        kr_s[0:t, :] = _rope(p_ref[:, 384:512], ck_ref[...], slk_ref[...], shk_ref[...], MLA_ROPE // 2)
        kvc = _mm(cckv_ref[...], wkv_ref[...])
        kn_s[t:nk, :] = kvc[:, 0:256]
        v_s[t:nk, :] = kvc[:, 256:512]
        kr_s[t:nk, :] = ckr_ref[...]
    scale = (MLA_NOPE + MLA_ROPE) ** -0.5
    if not latent:
        ckv_ref[...] = ckv
        kro_ref[...] = p_ref[:, 384:384 + MLA_ROPE]
        kr = p_ref[:, 384:512]
        kr4 = kr + pltpu.roll(kr, MLA_ROPE, 1) + pltpu.roll(kr, 2 * MLA_ROPE, 1) + pltpu.roll(kr, 3 * MLA_ROPE, 1)
        s = (_mm_nt(_stack_heads(qh[:, 0:256], MLA_H, MLA_NOPE), kv[:, 0:256])
             + _mm_nt(_stack_heads(qh[:, 256:384], MLA_H, MLA_ROPE), kr4)) * scale
        e = jnp.exp(s - jnp.max(s, axis=-1, keepdims=True))
        o = _mm(e, kv[:, 256:512]) / jnp.sum(e, axis=-1, keepdims=True)
        o_ref[...] = _unstack_heads(o, MLA_H, MLA_DV).astype(BF16)
        return
    kr = kr_s[...]
    kr4 = kr + pltpu.roll(kr, MLA_ROPE, 1) + pltpu.roll(kr, 2 * MLA_ROPE, 1) + pltpu.roll(kr, 3 * MLA_ROPE, 1)
    for r in range(t // QL):
        rows = slice(r * QL, (r + 1) * QL)
        s = (_mm_nt(_stack_heads(qn_s[rows, :], MLA_H, MLA_NOPE), kn_s[...])
             + _mm_nt(_stack_heads(qr_s[rows, :], MLA_H, MLA_ROPE), kr4)) * scale
        e = jnp.exp(s - jnp.max(s, axis=-1, keepdims=True))
        o = _mm(e, v_s[...]) / jnp.sum(e, axis=-1, keepdims=True)
        o_ref[rows, :] = _unstack_heads(o, MLA_H, MLA_DV).astype(BF16)


def _mla(p, gq, gkv, wq, wkv, tables_q, tables_k, cache_ckv, cache_kr, prev, l, latent):
    t, nseq, row0 = (T_LAT, N_LAT, M_CTX // T_LAT) if latent else (T_CTX, N_CTX, 0)
    nk = t + (PAST if latent else 0)
    in_specs = [_seq_spec(t, W_MLA, row0, latent), _layer_spec(l, 1, MLA_QR),
                _layer_spec(l, 1, MLA_KVR), _layer_spec(l, MLA_QR, 384), _layer_spec(l, MLA_KVR, 512)]
    args = [p, gq, gkv, wq, wkv]
    aliases = {}
    o_shape = jax.ShapeDtypeStruct((nseq * t, 256), BF16)
    o_spec = pl.BlockSpec((t, 256), lambda b: (b, 0))
    if latent:
        in_specs += [_table_spec(t, 128)] * 6
        in_specs += [pl.BlockSpec((None, None, PAST, MLA_KVR), lambda b: (b, l, 0, 0)),
                     pl.BlockSpec((None, None, PAST, 128), lambda b: (b, l, 0, 0))]
        args += list(tables_q) + list(tables_k) + [cache_ckv, cache_kr]
    out_shape, out_specs = [o_shape], [o_spec]
    if not latent:
        prev = prev or (None, None)
        _stacked_output(prev[0], (t, MLA_KVR), l, nseq, in_specs, args, out_shape, out_specs, aliases)
        _stacked_output(prev[1], (t, MLA_ROPE), l, nseq, in_specs, args, out_shape, out_specs, aliases)
    scratch = [pltpu.VMEM((t, 256), F32), pltpu.VMEM((t, 128), F32), pltpu.VMEM((nk, 256), F32),
               pltpu.VMEM((nk, 128), F32), pltpu.VMEM((nk, 256), F32)]
    return functools.partial(_mla_kernel, t, latent), in_specs, args, out_shape, out_specs, scratch, aliases


def _gqa_kernel(t, latent, *refs):
    if latent:
        p_ref, sink_ref, cos_ref, slo_ref, shi_ref, ck_ref, cv_ref, o_ref, q_s, k_s = refs
    else:
        p_ref, sink_ref = refs[0:2]
        o_ref, ko_ref, vo_ref = refs[-3:]
        for kvh in range(GQA_KV):
            ko_ref[kvh] = p_ref[:, 256 + kvh * GQA_HD:256 + (kvh + 1) * GQA_HD]
            vo_ref[kvh] = p_ref[:, 384 + kvh * GQA_HD:384 + (kvh + 1) * GQA_HD]
    scale = GQA_HD ** -0.5
    if not latent:
        own = _own_lanes(GQA_H, t, GQA_HD)
        sink = jnp.max(jnp.where(own > 0, sink_ref[...], NEG), axis=-1, keepdims=True)
        s = _mm_nt(_stack_heads(p_ref[:, 0:256], GQA_H, GQA_HD), _per_query_head(p_ref[:, 256:384])) * scale
        m = jnp.maximum(jnp.max(s, axis=-1, keepdims=True), sink)
        e = jnp.exp(s - m)
        o = _mm(e, _per_query_head(p_ref[:, 384:512])) / (jnp.sum(e, axis=-1, keepdims=True) + jnp.exp(sink - m))
        o_ref[...] = _unstack_heads(o, GQA_H, GQA_HD).astype(BF16)
        return
    q_s[...] = _rope(p_ref[:, 0:256], cos_ref[...], slo_ref[...], shi_ref[...], GQA_HD // 2)
    k_s[...] = _rope(p_ref[:, 256:384], cos_ref[:, 0:128], slo_ref[:, 0:128], shi_ref[:, 0:128], GQA_HD // 2)
    kc = _per_query_head(jnp.concatenate([ck_ref[0], ck_ref[1]], axis=1))
    vc = _per_query_head(jnp.concatenate([cv_ref[0], cv_ref[1]], axis=1))
    sink = jnp.max(jnp.where(_own_lanes(GQA_H, QL, GQA_HD) > 0, sink_ref[...], NEG), axis=-1, keepdims=True)
    for r in range(t // QL):
        rows = slice(r * QL, (r + 1) * QL)
        keys = slice(max(0, r * QL - WINDOW), min(t, (r + 1) * QL + WINDOW))
        nkeys = keys.stop - keys.start
        row = lax.broadcasted_iota(jnp.int32, (GQA_H * QL, nkeys), 0) % QL + r * QL
        col = lax.broadcasted_iota(jnp.int32, (GQA_H * QL, nkeys), 1) + keys.start
        near = jnp.abs(row - col) <= WINDOW
        q = _stack_heads(q_s[rows, :], GQA_H, GQA_HD)
        s_loc = jnp.where(near, _mm_nt(q, _per_query_head(k_s[keys, :])) * scale, NEG)
        s_ctx = _mm_nt(q, kc) * scale
        m = jnp.maximum(jnp.maximum(jnp.max(s_loc, axis=-1, keepdims=True), jnp.max(s_ctx, axis=-1, keepdims=True)),
                        sink)
        e_loc, e_ctx = jnp.exp(s_loc - m), jnp.exp(s_ctx - m)
        den = jnp.sum(e_loc, axis=-1, keepdims=True) + jnp.sum(e_ctx, axis=-1, keepdims=True) + jnp.exp(sink - m)
        o = (_mm(e_loc, _per_query_head(p_ref[keys, 384:512])) + _mm(e_ctx, vc)) / den
        o_ref[rows, :] = _unstack_heads(o, GQA_H, GQA_HD).astype(BF16)


def _gqa(p, sink_lanes, tables, cache_k, cache_v, prev, l, latent):
    t, nseq, row0 = (T_LAT, N_LAT, M_CTX // T_LAT) if latent else (T_CTX, N_CTX, 0)
    in_specs = [_seq_spec(t, W_GQA, row0, latent), _layer_spec(l, 1, 256)]
    args = [p, sink_lanes]
    scratch = []
    aliases = {}
    out_shape = [jax.ShapeDtypeStruct((nseq * t, 256), BF16)]
    out_specs = [pl.BlockSpec((t, 256), lambda b: (b, 0))]
    if latent:
        in_specs += [_table_spec(t, 256)] * 3
        in_specs += [pl.BlockSpec((None, None, GQA_KV, PAST, GQA_HD), lambda b: (b, l, 0, 0, 0))] * 2
        args += list(tables) + [cache_k, cache_v]
        scratch = [pltpu.VMEM((t, 256), F32), pltpu.VMEM((t, 128), F32)]
    else:
        prev = prev or (None, None)
        for pv in prev:
            _stacked_output(pv, (GQA_KV, t, GQA_HD), l, nseq, in_specs, args, out_shape, out_specs, aliases)
    return functools.partial(_gqa_kernel, t, latent), in_specs, args, out_shape, out_specs, scratch, aliases


def _outproj_kernel(route, split, tm, *refs):
    is_lat = pl.program_id(0) >= M_CTX // tm
    if split:
        x_in = jnp.where(is_lat, refs[1][...], refs[0][...])
        refs = refs[2:]
    else:
        x_in = refs[0][...]
        refs = refs[1:]
    ctx_refs, lat_refs = refs[0:4], refs[4:8]
    w_ref, gt_ref, g_ref, sh_ref, sc_ref = refs[8:13]
    refs = refs[13:]
    mixed = jnp.concatenate([jnp.where(is_lat, lat_refs[m][...], ctx_refs[m][...]) for m in range(4)], axis=1)
    mix = jnp.dot(mixed, w_ref[...], preferred_element_type=F32)
    x = x_in + gt_ref[...] * mix
    h = (_rms(x) * g_ref[...] * (1.0 + sc_ref[...]) + sh_ref[...]).astype(BF16)
    if not route:
        xo_ref, h_ref = refs
    else:
        r_ref, xo_ref, h_ref, gate_ref, sel_ref = refs
        lane = lax.broadcasted_iota(jnp.int32, gate_ref.shape, 1)
        logits = jnp.where(lane < N_EXP, jnp.dot(h, r_ref[...], preferred_element_type=F32), NEG)
        m1 = jnp.max(logits, axis=-1, keepdims=True)
        i1 = jnp.min(jnp.where(logits == m1, lane, LANE), axis=-1, keepdims=True)
        rest = jnp.where(lane == i1, NEG, logits)
        m2 = jnp.max(rest, axis=-1, keepdims=True)
        i2 = jnp.min(jnp.where(rest == m2, lane, LANE), axis=-1, keepdims=True)
        e2 = jnp.exp(m2 - m1)
        gate_ref[...] = jnp.where(lane == i1, 1.0 / (1.0 + e2), 0.0) + jnp.where(lane == i2, e2 / (1.0 + e2), 0.0)
        sel_ref[...] = jnp.where((lane == i1) | (lane == i2), 1, 0)
    xo_ref[...] = x
    h_ref[...] = h


def _outproj(xs, o_ctx, o_lat, w_out, g2, mod, router, l):
    tm = 512
    route = router is not None
    split = len(xs) == 2
    rows = lambda w: pl.BlockSpec((tm, w), lambda i: (i, 0))
    ctx_spec, lat_spec = _row_split_specs(tm, 256)
    x_specs = _row_split_specs(tm, D) if split else [rows(D)]
    in_specs = x_specs + [ctx_spec] * 4 + [lat_spec] * 4 + [
        _layer_spec(l, D, D), _mod_spec(l, 2, tm), _layer_spec(l, 1, D), _mod_spec(l, 3, tm), _mod_spec(l, 4, tm)]
    args = [*xs, *o_ctx, *o_lat, w_out, mod, g2, mod, mod]
    out_shape = [jax.ShapeDtypeStruct((M_ALL, D), F32), jax.ShapeDtypeStruct((M_ALL, D), BF16)]
    out_specs = [rows(D), rows(D)]
    if route:
        in_specs.append(_layer_spec(l // 2, D, LANE))
        args.append(router)
        out_shape += [jax.ShapeDtypeStruct((M_ALL, LANE), F32), jax.ShapeDtypeStruct((M_ALL, LANE), jnp.int32)]
        out_specs += [rows(LANE), rows(LANE)]
    return pl.pallas_call(
        functools.partial(_outproj_kernel, route, split, tm),
        out_shape=out_shape, grid=(M_ALL // tm,), in_specs=in_specs, out_specs=out_specs,
        compiler_params=_cparams("parallel"),
        name="outproj_route" if route else "outproj",
    )(*args)


def _ffn_kernel(h_ref, x_ref, gt_ref, wg_ref, wu_ref, wd_ref, o_ref, acc_ref):
    f = pl.program_id(1)

    @pl.when(f == 0)
    def _():
        acc_ref[...] = jnp.zeros_like(acc_ref)

    wg, wu, wd = wg_ref[...].astype(BF16), wu_ref[...].astype(BF16), wd_ref[...].astype(BF16)
    half = h_ref.shape[0] // 2
    for rows in (slice(0, half), slice(half, 2 * half)):
        h = h_ref[rows, :]
        g = jnp.dot(h, wg, preferred_element_type=F32)
        u = jnp.dot(h, wu, preferred_element_type=F32)
        acc_ref[rows, :] += jnp.dot((_silu(g) * u).astype(BF16), wd, preferred_element_type=F32)

    @pl.when(f == pl.num_programs(1) - 1)
    def _():
        o_ref[...] = x_ref[...] + gt_ref[...] * acc_ref[...]


def _ffn(h, x, mod, wg, wu, wd, l):
    tm, tf = 1024, 256
    j = l // 2
    return pl.pallas_call(
        _ffn_kernel,
        out_shape=jax.ShapeDtypeStruct((M_ALL, D), F32),
        grid=(M_ALL // tm, D_FF // tf),
        in_specs=[pl.BlockSpec((tm, D), lambda i, f: (i, 0)), pl.BlockSpec((tm, D), lambda i, f: (i, 0)),
                  _mod_spec(l, 5, tm),
                  pl.BlockSpec((None, D, tf), lambda i, f: (j, 0, f)),
                  pl.BlockSpec((None, D, tf), lambda i, f: (j, 0, f)),
                  pl.BlockSpec((None, tf, D), lambda i, f: (j, f, 0))],
        out_specs=pl.BlockSpec((tm, D), lambda i, f: (i, 0)),
        scratch_shapes=[pltpu.VMEM((tm, D), F32)],
        compiler_params=_cparams("parallel", "arbitrary"),
        name="ffn_dense",
    )(h, x, mod, wg, wu, wd)


SUP, SUB, CHUNK = 2048, 256, 256
N_SUB = SUP // SUB
S_MAX = 2 * M_ALL // SUP + N_EXP
P_SLOT = S_MAX * SUP
TC = 128
WIN = TC + 16


def _moe_expert_kernel(se_ref, nt_ref, sblk_ref, clo_ref, chi_ref, pos_ref, h_ref, wg_ref, wu_ref, wd_ref, o_ref,
                       xs_s, acc_s, wg_s, wu_s, wd_s):
    s = pl.program_id(0)
    f = pl.program_id(1)
    n = nt_ref[s]

    @pl.when((f == 0) & (n == 0))
    def _():
        o_ref[...] = jnp.zeros_like(o_ref)

    @pl.when((f == 0) & (n > 0))
    def _():
        row = lax.broadcasted_iota(jnp.int32, (SUB, CHUNK), 0)

        def gather(j, carry):
            row0 = (s * N_SUB + j) * SUB
            xs_s[j] = jnp.zeros((SUB, D), BF16)

            def chunk(c, carry):
                tpos = pos_ref[:, pl.ds(pl.multiple_of(c * CHUNK, CHUNK), CHUNK)]
                onehot = jnp.where(tpos - row0 == row, 1.0, 0.0).astype(BF16)
                rows = h_ref[pl.ds(pl.multiple_of(c * CHUNK, CHUNK), CHUNK), :]
                xs_s[j] += jnp.dot(onehot, rows, preferred_element_type=F32).astype(BF16)
                return carry

            g = s * N_SUB + j
            lax.fori_loop(clo_ref[g], chi_ref[g] + 1, chunk, 0)
            acc_s[j] = jnp.zeros((SUB, D), F32)
            return carry

        lax.fori_loop(0, n, gather, 0)

    @pl.when(n > 0)
    def _():
        wg_s[...] = wg_ref[...].astype(BF16)
        wu_s[...] = wu_ref[...].astype(BF16)
        wd_s[...] = wd_ref[...].astype(BF16)

        def sub(j):
            x = xs_s[j]
            g = jnp.dot(x, wg_s[...], preferred_element_type=F32)
            u = jnp.dot(x, wu_s[...], preferred_element_type=F32)
            acc_s[j] += jnp.dot((_silu(g) * u).astype(BF16), wd_s[...], preferred_element_type=F32)

        def pair(jj, carry):
            sub(2 * jj)
            sub(2 * jj + 1)
            return carry

        lax.fori_loop(0, n // 2, pair, 0)

        @pl.when(n % 2 == 1)
        def _():
            sub(n - 1)

    @pl.when((f == pl.num_programs(1) - 1) & (n > 0))
    def _():
        for j in range(N_SUB):
            rows = slice(j * SUB, (j + 1) * SUB)

            @pl.when(j < n)
            def _():
                o_ref[rows, :] = acc_s[j].astype(BF16)

            @pl.when(j >= n)
            def _():
                o_ref[rows, :] = jnp.zeros((SUB, D), BF16)


def _moe_experts(h, pos_t, meta, wg, wu, wd, layer):
    tf = 512
    nf = D_FFE // tf
    se, nt, sblk, clo, chi = meta

    def w_up(s, f, se, nt, sblk, clo, chi):
        return (layer, se[s], 0, jnp.where(nt[s] > 0, f, nf - 1))

    def w_down(s, f, se, nt, sblk, clo, chi):
        return (layer, se[s], jnp.where(nt[s] > 0, f, nf - 1), 0)

    grid_spec = pltpu.PrefetchScalarGridSpec(
        num_scalar_prefetch=5,
        grid=(S_MAX, nf),
        in_specs=[pl.BlockSpec((None, 1, M_ALL), lambda s, f, se, nt, sblk, clo, chi: (se[s], 0, 0)),
                  pl.BlockSpec((M_ALL, D), lambda s, f, *_: (0, 0), pipeline_mode=pl.Buffered(1)),
                  pl.BlockSpec((None, None, D, tf), w_up), pl.BlockSpec((None, None, D, tf), w_up),
                  pl.BlockSpec((None, None, tf, D), w_down)],
        out_specs=pl.BlockSpec((SUP, D), lambda s, f, *_: (s, 0)),
        scratch_shapes=[pltpu.VMEM((N_SUB, SUB, D), BF16), pltpu.VMEM((N_SUB, SUB, D), F32),
                        pltpu.VMEM((D, tf), BF16), pltpu.VMEM((D, tf), BF16), pltpu.VMEM((tf, D), BF16)],
    )
    return pl.pallas_call(
        _moe_expert_kernel,
        out_shape=jax.ShapeDtypeStruct((P_SLOT, D), BF16),
        grid_spec=grid_spec,
        compiler_params=pltpu.CompilerParams(dimension_semantics=("arbitrary", "arbitrary"),
                                             vmem_limit_bytes=60 * 1024 * 1024),
        name="moe_experts",
    )(se, nt, sblk, clo, chi, pos_t, h, wg, wu, wd)


def _moe_combine_kernel(off_ref, x_ref, gt_ref, gate_ref, pos_ref, *refs):
    win_refs, o_ref = refs[:N_EXP], refs[N_EXP]
    i = pl.program_id(0)
    lane = lax.broadcasted_iota(jnp.int32, (TC, WIN), 1)
    y = jnp.zeros((TC, D), F32)
    for e in range(N_EXP):
        rel = pos_ref[:, e:e + 1] - off_ref[i * N_EXP + e] * 16
        onehot = jnp.where(rel == lane, 1.0, 0.0).astype(BF16)
        y = y + gate_ref[:, e:e + 1] * jnp.dot(onehot, win_refs[e][...], preferred_element_type=F32)
    o_ref[...] = x_ref[...] + gt_ref[...] * y


def _moe_combine(x, mod, gates, pos, off, slots, l):
    def win_spec(e):
        return pl.BlockSpec((pl.Element(WIN), pl.Element(D)), lambda i, off: (off[i * N_EXP + e] * 16, 0))

    grid_spec = pltpu.PrefetchScalarGridSpec(
        num_scalar_prefetch=1,
        grid=(M_ALL // TC,),
        in_specs=[pl.BlockSpec((TC, D), lambda i, off: (i, 0)),
                  _mod_spec(l, 5, TC),
                  pl.BlockSpec((TC, LANE), lambda i, off: (i, 0)), pl.BlockSpec((TC, LANE), lambda i, off: (i, 0))]
        + [win_spec(e) for e in range(N_EXP)],
        out_specs=pl.BlockSpec((TC, D), lambda i, off: (i, 0)),
    )
    return pl.pallas_call(
        _moe_combine_kernel,
        out_shape=jax.ShapeDtypeStruct((M_ALL, D), F32),
        grid_spec=grid_spec,
        compiler_params=_cparams("arbitrary"),
        name="moe_combine",
    )(off, x, mod, gates, pos, *([slots] * N_EXP))


def _moe_plan(sel):
    i32 = jnp.int32
    sel_t = sel[:, :N_EXP].T
    csum = jnp.cumsum(sel_t, axis=1)
    rank = csum - sel_t
    n_e = csum[:, -1]
    ns_e = (n_e + SUP - 1) // SUP
    end_e = jnp.cumsum(ns_e)
    start_e = end_e - ns_e
    n_used = end_e[-1]
    pos_t = jnp.where(sel_t > 0, start_e[:, None] * SUP + rank, -1)
    s_ids = jnp.arange(S_MAX, dtype=i32)
    sblk = jnp.minimum(s_ids, n_used - 1)
    se = jnp.sum((end_e[None, :] <= sblk[:, None]).astype(i32), axis=1)
    nv = jnp.clip(n_e[se] - (sblk - start_e[se]) * SUP, 0, SUP)
    nt = jnp.where(s_ids < n_used, (nv + SUB - 1) // SUB, 0)
    g_ids = jnp.arange(S_MAX * N_SUB, dtype=i32)
    s_g, e_g = g_ids // N_SUB, se[g_ids // N_SUB]
    r0 = (sblk[s_g] - start_e[e_g]) * SUP + (g_ids % N_SUB) * SUB
    r1 = jnp.minimum(r0 + SUB, n_e[e_g])
    live = (s_g < n_used) & (r0 < n_e[e_g])
    chunk_end = csum[:, CHUNK - 1::CHUNK][e_g]
    clo = jnp.where(live, jnp.sum((chunk_end <= r0[:, None]).astype(i32), axis=1), 0)
    chi = jnp.where(live, jnp.sum((chunk_end < r1[:, None]).astype(i32), axis=1), -1)
    before = jnp.concatenate([jnp.zeros((N_EXP, 1), i32), csum[:, TC - 1::TC][:, :-1]], axis=1)
    base = start_e[:, None] * SUP + before
    off = jnp.minimum(base // 16, (P_SLOT - WIN) // 16).T.reshape(-1)
    pos128 = jnp.pad(pos_t.T, ((0, 0), (0, LANE - N_EXP)), constant_values=-1)
    meta = (se.astype(i32), nt.astype(i32), sblk.astype(i32), clo.astype(i32), chi.astype(i32))
    return meta, pos_t.reshape(N_EXP, 1, M_ALL), pos128, off.astype(i32)


def _moe(h, x, mod, gates, sel, wg, wu, wd, l):
    meta, pos_t, pos, off = _moe_plan(sel)
    slots = _moe_experts(h, pos_t, meta, wg, wu, wd, l // 2)
    return _moe_combine(x, mod, gates, pos, off, slots, l)


def _final_kernel(tm, x_ref, g_ref, oc_ref, ol_ref):
    y = _rms(x_ref[...]) * g_ref[...]
    is_lat = pl.program_id(0) >= M_CTX // tm

    @pl.when(jnp.logical_not(is_lat))
    def _():
        oc_ref[...] = y

    @pl.when(is_lat)
    def _():
        ol_ref[...] = y


def _final_norm(x, g):
    tm = 1024
    return pl.pallas_call(
        functools.partial(_final_kernel, tm),
        out_shape=[jax.ShapeDtypeStruct((M_CTX, D), F32), jax.ShapeDtypeStruct((M_LAT, D), F32)],
        grid=(M_ALL // tm,),
        in_specs=[pl.BlockSpec((tm, D), lambda i: (i, 0)), pl.BlockSpec((1, D), lambda i: (0, 0))],
        out_specs=_row_split_specs(tm, D),
        compiler_params=_cparams("arbitrary"),
        name="final_norm",
    )(x, g)


def kernel(x_prompt, x_sample, state_gla, state_ret, cache_mla_ckv, cache_mla_krope, cache_gqa_k, cache_gqa_v,
           c, c_ctx, norm1_g, norm2_g, final_norm_g, w_mod, b_mod, w_in, w_out, gla_gate_w, gla_gate_b,
           mla_q_norm_g, mla_w_q_up, mla_kv_norm_g, mla_w_kv_up, ret_decay, gqa_sink,
           ffn_w_gate, ffn_w_up, ffn_w_down, moe_router, moe_w_gate, moe_w_up, moe_w_down):
    xs = (x_prompt.reshape(M_CTX, D), x_sample.reshape(M_LAT, D))

    cond = jnp.concatenate([c_ctx[None], c, jnp.zeros((8 - 1 - N_LAT, D), F32)], axis=0)
    mod = _modulation(cond, w_mod, b_mod)
    mod = mod[:, :1 + N_LAT].reshape(DEPTH, (1 + N_LAT) * 6, 1, D)

    w_out_b = w_out.astype(BF16)
    wq = mla_w_q_up.reshape(DEPTH, MLA_QR, MLA_H, MLA_NOPE + MLA_ROPE)
    wq = jnp.concatenate([wq[..., :MLA_NOPE].reshape(DEPTH, MLA_QR, MLA_H * MLA_NOPE),
                          wq[..., MLA_NOPE:].reshape(DEPTH, MLA_QR, MLA_H * MLA_ROPE)], axis=-1).astype(BF16)
    wkv = mla_w_kv_up.reshape(DEPTH, MLA_KVR, MLA_H, MLA_NOPE + MLA_DV)
    wkv = jnp.concatenate([wkv[..., :MLA_NOPE].reshape(DEPTH, MLA_KVR, MLA_H * MLA_NOPE),
                           wkv[..., MLA_NOPE:].reshape(DEPTH, MLA_KVR, MLA_H * MLA_DV)], axis=-1).astype(BF16)
    router = jnp.pad(moe_router, ((0, 0), (0, 0), (0, LANE - N_EXP))).astype(BF16)
    dec_lanes = jnp.repeat(ret_decay, RET_DV, axis=-1)
    sink_lanes = jnp.repeat(gqa_sink, GQA_HD, axis=-1).reshape(DEPTH, 1, 256)
    gate_b = gla_gate_b.reshape(DEPTH, 2, 1, GLA_H * GLA_DK)
    g1, g2 = norm1_g.reshape(DEPTH, 1, D), norm2_g.reshape(DEPTH, 1, D)
    gq, gkv = mla_q_norm_g.reshape(DEPTH, 1, MLA_QR), mla_kv_norm_g.reshape(DEPTH, 1, MLA_KVR)
    eye = jnp.eye(GLA_H, dtype=F32)
    s0_gla = jnp.einsum('bldhkv,hg->bldhvgk', state_gla, eye).reshape(
        N_LAT, DEPTH, 2, GLA_H * GLA_DV, GLA_H * GLA_DK)
    s0_ret = jnp.einsum('bldhkv,hg->bldhkgv', state_ret, jnp.eye(RET_H, dtype=F32)).reshape(
        N_LAT, DEPTH, 2, RET_H * RET_DK, RET_H * RET_DV)
    cache_kr = jnp.pad(cache_mla_krope, ((0, 0), (0, 0), (0, 0), (0, 128 - MLA_ROPE)))

    rope64 = _rope_tables(T_LAT, 64, 256)
    rope32_q = _rope_tables(T_LAT, 32, 128)
    ck, sl, sh = _rope_tables(T_LAT, 32, 128)
    live = jnp.asarray((np.arange(128) < MLA_ROPE).astype(np.float32))[None]
    rope32_k = (ck * live, sl * live, sh * live)

    st_gla = st_ret = caches_mla = caches_gqa = None
    for l in range(DEPTH):
        p_gla, p_mla, p_ret, p_gqa = _inproj(xs, g1, mod, w_in, l)

        o_gla_c, st_gla, o_mla_c, ckv_all, kr_all, o_ret_c, st_ret, o_gqa_c, gk_all, gv_all = _fused_call([
            _gla(p_gla, gla_gate_w, gate_b, None, st_gla, l, False),
            _mla(p_mla, gq, gkv, wq, wkv, None, None, None, None, caches_mla, l, False),
            _ret(p_ret, dec_lanes, None, None, st_ret, l, False),
            _gqa(p_gqa, sink_lanes, None, None, None, caches_gqa, l, False)], N_CTX, "mixers_context")
        caches_mla, caches_gqa = (ckv_all, kr_all), (gk_all, gv_all)
        o_gla_s, o_ret_s = _fused_call([
            _gla(p_gla, gla_gate_w, gate_b, s0_gla, None, l, True),
            _ret(p_ret, dec_lanes, rope64, s0_ret, None, l, True)], N_LAT, "mixers_latent_scan")
        o_mla_s, o_gqa_s = _fused_call([
            _mla(p_mla, gq, gkv, wq, wkv, rope32_q, rope32_k, cache_mla_ckv, cache_kr, None, l, True),
            _gqa(p_gqa, sink_lanes, rope64, cache_gqa_k, cache_gqa_v, None, l, True)], N_LAT, "mixers_latent_attn")
        o_ctx = (o_gla_c, o_mla_c, o_ret_c, o_gqa_c)
        o_lat = (o_gla_s, o_mla_s, o_ret_s, o_gqa_s)
        if l % 2 == 0:
            x, h2 = _outproj(xs, o_ctx, o_lat, w_out_b, g2, mod, None, l)
            x = _ffn(h2, x, mod, ffn_w_gate, ffn_w_up, ffn_w_down, l)
        else:
            x, h2, gates, sel = _outproj(xs, o_ctx, o_lat, w_out_b, g2, mod, router, l)
            x = _moe(h2, x, mod, gates, sel, moe_w_gate, moe_w_up, moe_w_down, l)
        xs = (x,)

    y_ctx, y_lat = _final_norm(x, final_norm_g[None])
    return (y_ctx.reshape(N_CTX, T_CTX, D), y_lat.reshape(N_LAT, T_LAT, D), jnp.swapaxes(st_gla, -1, -2), st_ret,
            *caches_mla, *caches_gqa)
```

```python
import functools

import numpy as np
import jax
import jax.numpy as jnp
from jax import lax
from jax.experimental import pallas as pl
from jax.experimental.pallas import tpu as pltpu

F32 = jnp.float32
BF16 = jnp.bfloat16
HIGHEST = lax.Precision.HIGHEST

D = 1024
N_CTX, T_CTX = 16, 256
N_LAT, T_LAT = 2, 1024
PAST = 256
DEPTH = 4
M_CTX = N_CTX * T_CTX
M_LAT = N_LAT * T_LAT
M_ALL = M_CTX + M_LAT
GRID_W = 64
ROPE_BASE = 10000.0
EPS = 1e-6

GLA_H, GLA_DK, GLA_DV, GLA_RANK, GLA_NORM, GLA_C = 4, 32, 64, 16, 16.0, 64
MLA_H, MLA_QR, MLA_KVR, MLA_NOPE, MLA_ROPE, MLA_DV = 4, 256, 128, 64, 32, 64
RET_H, RET_DK, RET_DV = 4, 64, 64
GQA_H, GQA_KV, GQA_HD, WINDOW = 4, 2, 64, 128
D_FF, N_EXP, D_FFE = 2816, 8, 3584

W_GLA, W_MLA, W_RET, W_GQA = 896, 512, 1024, 512
IN_WIDTH = 2752
IN_GROUPS = ((0, 800), (800, 1216), (1216, 2240), (2240, 2752))
LANE = 128
NEG = -1e30
QB = 256
QL = 128
VMEM_LIMIT = 56 * 1024 * 1024


def _cparams(*sem):
    return pltpu.CompilerParams(dimension_semantics=sem, vmem_limit_bytes=VMEM_LIMIT)


def _mm(a, b):
    return jnp.dot(a.astype(BF16), b.astype(BF16), preferred_element_type=F32)


def _mm_nt(a, b):
    return lax.dot_general(a.astype(BF16), b.astype(BF16), (((1,), (1,)), ((), ())), preferred_element_type=F32)


def _mm_tn(a, b):
    return lax.dot_general(a.astype(BF16), b.astype(BF16), (((0,), (0,)), ((), ())), preferred_element_type=F32)


def _mm_f32(a, b):
    return jnp.dot(a, b, precision=HIGHEST, preferred_element_type=F32)


def _silu(x):
    return x * (1.0 / (1.0 + jnp.exp(-x)))


def _log_sigmoid(x):
    return jnp.minimum(x, 0.0) - jnp.log1p(jnp.exp(-jnp.abs(x)))


def _rms(x):
    return x * lax.rsqrt(jnp.mean(x * x, axis=-1, keepdims=True) + EPS)


def _mod_row(tile, tm):
    return jnp.maximum((tile * tm) // T_LAT - (M_CTX // T_LAT - 1), 0)


def _mod_kernel(c_ref, w_ref, b_ref, o_ref):
    o_ref[...] = _mm(_silu(c_ref[...]), w_ref[...]) + b_ref[...]


def _modulation(cond, w_mod, b_mod):
    tn = 1536
    return pl.pallas_call(
        _mod_kernel,
        out_shape=jax.ShapeDtypeStruct((DEPTH, 8, 6 * D), F32),
        grid=(DEPTH, 6 * D // tn),
        in_specs=[pl.BlockSpec((8, D), lambda l, j: (0, 0)),
                  pl.BlockSpec((None, D, tn), lambda l, j: (l, 0, j)),
                  pl.BlockSpec((None, 1, tn), lambda l, j: (l, 0, j))],
        out_specs=pl.BlockSpec((None, 8, tn), lambda l, j: (l, 0, j)),
        compiler_params=_cparams("parallel", "parallel"),
        name="modulation",
    )(cond, w_mod, b_mod.reshape(DEPTH, 1, 6 * D))


def _inproj_kernel(split, tm, *refs):
    if split:
        xc_ref, xl_ref = refs[0:2]
        x = jnp.where(pl.program_id(0) >= M_CTX // tm, xl_ref[...], xc_ref[...])
        refs = refs[2:]
    else:
        x = refs[0][...]
        refs = refs[1:]
    g_ref, sh_ref, sc_ref, w_ref = refs[0:4]
    out_refs, w_s = refs[4:-1], refs[-1]

    @pl.when(pl.program_id(0) == 0)
    def _():
        dst = 0
        for (lo, hi), o_ref in zip(IN_GROUPS, out_refs):
            width = o_ref.shape[-1]
            w_s[:, dst:dst + hi - lo] = w_ref[:, lo:hi].astype(BF16)
            if width > hi - lo:
                w_s[:, dst + hi - lo:dst + width] = jnp.zeros((D, width - (hi - lo)), BF16)
            dst += width

    h = (_rms(x) * g_ref[...] * (1.0 + sc_ref[...]) + sh_ref[...]).astype(BF16)
    col = 0
    for o_ref in out_refs:
        width = o_ref.shape[-1]
        o_ref[...] = jnp.dot(h, w_s[:, col:col + width], preferred_element_type=F32)
        col += width


def _mod_spec(l, j, tm):
    return pl.BlockSpec((None, None, 1, D), lambda i, *_: (l, _mod_row(i, tm) * 6 + j, 0, 0))


def _layer_spec(l, *shape):
    return pl.BlockSpec((None,) + shape, lambda *_: (l,) + (0,) * len(shape))


def _stacked_output(prev, tail, l, nseq, in_specs, args, out_shape, out_specs, aliases):
    out_shape.append(jax.ShapeDtypeStruct((nseq, DEPTH) + tail, F32))
    out_specs.append(pl.BlockSpec((None, None) + tail, lambda b: (b, l) + (0,) * len(tail)))
    if prev is not None:
        in_specs.append(pl.BlockSpec(memory_space=pl.ANY))
        args.append(prev)
        aliases[len(args) - 1] = len(out_shape) - 1


def _row_split_specs(tm, width):
    n_ctx = M_CTX // tm
    return [pl.BlockSpec((tm, width), lambda i: (jnp.minimum(i, n_ctx - 1), 0)),
            pl.BlockSpec((tm, width), lambda i: (jnp.maximum(i - n_ctx, 0), 0))]


def _inproj(xs, g1, mod, w_in, l):
    tm = 512
    split = len(xs) == 2
    rows = lambda w: pl.BlockSpec((tm, w), lambda i: (i, 0))
    widths = (W_GLA, W_MLA, W_RET, W_GQA)
    x_specs = _row_split_specs(tm, D) if split else [rows(D)]
    w_spec = pl.BlockSpec((None, D, IN_WIDTH), lambda i: (l, 0, 0), pipeline_mode=pl.Buffered(1))
    return pl.pallas_call(
        functools.partial(_inproj_kernel, split, tm),
        out_shape=[jax.ShapeDtypeStruct((M_ALL, w), F32) for w in widths],
        grid=(M_ALL // tm,),
        in_specs=x_specs + [_layer_spec(l, 1, D), _mod_spec(l, 0, tm), _mod_spec(l, 1, tm), w_spec],
        out_specs=[rows(w) for w in widths],
        scratch_shapes=[pltpu.VMEM((D, sum(widths)), BF16)],
        compiler_params=_cparams("arbitrary"),
        name="inproj",
    )(*xs, g1, mod, mod, w_in)


def _head_rms_gate(o, gate):
    r = lax.broadcasted_iota(jnp.int32, (256, 256), 0) // 64
    c = lax.broadcasted_iota(jnp.int32, (256, 256), 1) // 64
    group_mean = jnp.where(r == c, 1.0 / 64.0, 0.0).astype(BF16)
    sq = o * o
    sq_hi = sq.astype(BF16)
    sq_lo = (sq - sq_hi.astype(F32)).astype(BF16)
    ms = (jnp.dot(sq_hi, group_mean, preferred_element_type=F32)
          + jnp.dot(sq_lo, group_mean, preferred_element_type=F32))
    return o * lax.rsqrt(ms + EPS) * _silu(gate)


def _own_lanes(n_heads, rows_per_head, lanes_per_head):
    shape = (n_heads * rows_per_head, n_heads * lanes_per_head)
    return (lax.broadcasted_iota(jnp.int32, shape, 0) // rows_per_head
            == lax.broadcasted_iota(jnp.int32, shape, 1) // lanes_per_head).astype(F32)


def _stack_heads(x, n_heads, lanes_per_head):
    return jnp.concatenate([x] * n_heads, axis=0) * _own_lanes(n_heads, x.shape[0], lanes_per_head)


def _unstack_heads(y, n_heads, lanes_per_head):
    t = y.shape[0] // n_heads
    y = y * _own_lanes(n_heads, t, lanes_per_head)
    out = y[0:t]
    for h in range(1, n_heads):
        out = out + y[h * t:(h + 1) * t]
    return out


def _per_query_head(x):
    lane = lax.broadcasted_iota(jnp.int32, x.shape, 1)
    swapped = pltpu.roll(x, GQA_HD, 1)
    return jnp.concatenate([jnp.where(lane < GQA_HD, x, swapped), jnp.where(lane < GQA_HD, swapped, x)], axis=1)


def _rope(x, cos, sin_lo, sin_hi, half):
    w = x.shape[-1]
    return x * cos + pltpu.roll(x, w - half, 1) * sin_lo + pltpu.roll(x, half, 1) * sin_hi


def _rope_tables(t, head_dim, width):
    half = head_dim // 2
    quarter = head_dim // 4
    pos = np.arange(t)
    rows = (pos // GRID_W).astype(np.float32)
    cols = (pos % GRID_W).astype(np.float32)
    inv = np.power(np.float32(ROPE_BASE), -np.arange(quarter, dtype=np.float32) / np.float32(quarter)).astype(np.float32)
    ang = np.concatenate([rows[:, None] * inv, cols[:, None] * inv], axis=-1).astype(np.float32)
    lane = np.arange(width)
    a = ang[:, lane % half]
    cos, sin = np.cos(a).astype(np.float32), np.sin(a).astype(np.float32)
    low = (lane % head_dim) < half
    return (jnp.asarray(cos), jnp.asarray(np.where(low[None], -sin, 0.0).astype(np.float32)),
            jnp.asarray(np.where(low[None], 0.0, sin).astype(np.float32)))


def _seq_spec(t, width, row0, latent):
    return pl.BlockSpec((t, width), lambda b: (row0 + b, 0), pipeline_mode=pl.Buffered(1) if latent else None)


def _table_spec(t, width):
    return pl.BlockSpec((t, width), lambda b: (0, 0), pipeline_mode=pl.Buffered(1))


def _fused_call(parts, nseq, name):
    in_specs, args, out_shape, out_specs, scratch, aliases, layout = [], [], [], [], [], {}, []
    for fn, p_in, p_args, p_shape, p_out, p_scratch, p_alias in parts:
        layout.append((fn, len(p_args), len(p_shape), len(p_scratch)))
        for k, v in p_alias.items():
            aliases[len(args) + k] = len(out_shape) + v
        in_specs += p_in
        args += p_args
        out_shape += p_shape
        out_specs += p_out
        scratch += p_scratch
    n_in, n_out = len(args), len(out_shape)

    def kernel(*refs):
        i, o, s = 0, n_in, n_in + n_out
        for fn, ni, no, ns in layout:
            fn(*refs[i:i + ni], *refs[o:o + no], *refs[s:s + ns])
            i, o, s = i + ni, o + no, s + ns

    return pl.pallas_call(
        kernel, out_shape=out_shape, grid=(nseq,), in_specs=in_specs, out_specs=out_specs, scratch_shapes=scratch,
        input_output_aliases=aliases, name=name,
        compiler_params=pltpu.CompilerParams(dimension_semantics=("arbitrary",), vmem_limit_bytes=60 * 1024 * 1024),
    )(*args)


def _gla_group_scan(t, p_ref, la_f, la_b, of_s, ob_s, st_s):
    c, gr = GLA_C, 256
    nc, n, hd = gr // c, t // gr, GLA_H * GLA_DK
    ri = lax.broadcasted_iota(jnp.int32, (gr, gr), 0)
    ci = lax.broadcasted_iota(jnp.int32, (gr, gr), 1)
    same = (ri // c) == (ci // c)
    rs = lax.broadcasted_iota(jnp.int32, (GLA_H * gr, gr), 0) % gr
    cs = lax.broadcasted_iota(jnp.int32, (GLA_H * gr, gr), 1)
    same_s = (rs // c) == (cs // c)
    own_chunk = (lax.broadcasted_iota(jnp.int32, (gr, nc * hd), 0) // c
                 == lax.broadcasted_iota(jnp.int32, (gr, nc * hd), 1) // hd).astype(F32)
    own_dk = jnp.concatenate([_own_lanes(GLA_H, GLA_DV, GLA_DK)] * nc, axis=1)
    scale = GLA_DK ** -0.5

    def group(row0, la_ref, d):
        fwd = d == 0
        ordered = (ci <= ri) if fwd else (ci >= ri)
        first_half = (ci % c < c // 2) if fwd else (ci % c >= c // 2)
        keep = same_s & ((cs <= rs) if fwd else (cs >= rs))
        sums_of = jnp.concatenate([(same & ordered).astype(BF16), same.astype(BF16), (same & first_half).astype(BF16)],
                                  axis=0)
        q = p_ref[pl.ds(row0, gr), 0:128]
        k = p_ref[pl.ds(row0, gr), 128:256] * scale
        v = p_ref[pl.ds(row0, gr), 256:512]
        la = la_ref[pl.ds(row0, gr), :]
        la_hi = la.astype(BF16)
        la_lo = (la - la_hi.astype(F32)).astype(BF16)
        sums = jnp.dot(sums_of, la_hi, preferred_element_type=F32) + jnp.dot(sums_of, la_lo, preferred_element_type=F32)
        bc, tot, mid = sums[0:gr], sums[gr:2 * gr], sums[2 * gr:3 * gr]
        qe, ke = q * jnp.exp(bc - mid), k * jnp.exp(mid - bc)
        q_in, k_out, a = q * jnp.exp(bc), k * jnp.exp(tot - bc), jnp.exp(tot)
        att = jnp.where(keep, _mm_nt(_stack_heads(qe, GLA_H, GLA_DK), ke), 0.0)
        o = _unstack_heads(_mm(att, v), GLA_H, GLA_DV)
        kv = _mm_tn(v, jnp.concatenate([k_out] * nc, axis=1) * own_chunk) * own_dk
        st, entering = st_s[d], [None] * nc
        for cc in (range(nc) if fwd else range(nc - 1, -1, -1)):
            entering[cc] = st
            st = st * a[cc * c:cc * c + 1, :] + kv[:, cc * hd:(cc + 1) * hd]
        st_s[d] = st
        return o + _mm_nt(jnp.concatenate([q_in] * nc, axis=1) * own_chunk, jnp.concatenate(entering, axis=1))

    def body(i, carry):
        rf = pl.multiple_of(i * gr, gr)
        rb = pl.multiple_of((n - 1 - i) * gr, gr)
        of_s[pl.ds(rf, gr), :] = group(rf, la_f, 0)
        ob_s[pl.ds(rb, gr), :] = group(rb, la_b, 1)
        return carry

    lax.fori_loop(0, n, body, 0)


def _gla_kernel(t, latent, *refs):
    if latent:
        p_ref, gw_ref, gb_ref, s0_ref, o_ref, la_f, la_b, of_s, ob_s, st_s = refs
    else:
        p_ref, gw_ref, gb_ref = refs[0:3]
        o_ref, st_ref, la_f, la_b, of_s, ob_s, st_s = refs[-7:]
    c = GLA_C
    n = t // c
    hd = GLA_H * GLA_DK
    la_f[...] = _log_sigmoid(_mm_f32(p_ref[:, 768:784], gw_ref[0]) + gb_ref[0]) / GLA_NORM
    la_b[...] = _log_sigmoid(_mm_f32(p_ref[:, 784:800], gw_ref[1]) + gb_ref[1]) / GLA_NORM

    ri = lax.broadcasted_iota(jnp.int32, (GLA_H * c, c), 0) % c
    ci = lax.broadcasted_iota(jnp.int32, (GLA_H * c, c), 1)
    tri_r = lax.broadcasted_iota(jnp.int32, (c, c), 0)
    tri_c = lax.broadcasted_iota(jnp.int32, (c, c), 1)
    head_rows = lax.broadcasted_iota(jnp.int32, (GLA_H * c, hd), 0) // c
    own_dk = (head_rows == lax.broadcasted_iota(jnp.int32, (GLA_H * c, hd), 1) // GLA_DK).astype(F32)
    own_dv = (lax.broadcasted_iota(jnp.int32, (GLA_H * c, GLA_H * GLA_DV), 0) // c
              == lax.broadcasted_iota(jnp.int32, (GLA_H * c, GLA_H * GLA_DV), 1) // GLA_DV).astype(F32)
    scale = GLA_DK ** -0.5
    if latent:
        st_s[...] = s0_ref[...]
    else:
        st_s[...] = jnp.zeros_like(st_s)

    def chunk(row0, la_ref, d):
        fwd = d == 0
        keep = (ci <= ri) if fwd else (ci >= ri)
        tri = ((tri_c <= tri_r) if fwd else (tri_c >= tri_r)).astype(BF16)
        q = p_ref[pl.ds(row0, c), 0:128]
        k = p_ref[pl.ds(row0, c), 128:256] * scale
        v = p_ref[pl.ds(row0, c), 256:512]
        la = la_ref[pl.ds(row0, c), :]
        la_hi = la.astype(BF16)
        la_lo = (la - la_hi.astype(F32)).astype(BF16)
        bc = (jnp.dot(tri, la_hi, preferred_element_type=F32) + jnp.dot(tri, la_lo, preferred_element_type=F32))
        tot = bc[c - 1:c, :] if fwd else bc[0:1, :]
        mid = bc[c // 2 - 1:c // 2, :] if fwd else bc[c // 2:c // 2 + 1, :]
        qe, ke = q * jnp.exp(bc - mid), k * jnp.exp(mid - bc)
        q_in, k_out, a = q * jnp.exp(bc), k * jnp.exp(tot - bc), jnp.exp(tot)
        q_rows = jnp.concatenate([qe] * GLA_H, axis=0) * own_dk
        att = jnp.where(keep, _mm_nt(q_rows, ke), 0.0)
        o_all = _mm(att, v) * own_dv
        o = o_all[0:c] + o_all[c:2 * c] + o_all[2 * c:3 * c] + o_all[3 * c:4 * c]
        st = st_s[d]
        o = o + _mm_nt(q_in, st)
        st_s[d] = st * a + _mm_tn(v, k_out) * own_dk
        return o

    unroll = 4

    def body(i, carry):
        for u in range(unroll):
            rf = pl.multiple_of((i * unroll + u) * c, c)
            rb = pl.multiple_of((n - 1 - i * unroll - u) * c, c)
            of_s[pl.ds(rf, c), :] = chunk(rf, la_f, 0)
            ob_s[pl.ds(rb, c), :] = chunk(rb, la_b, 1)
        return carry

    if latent:
        lax.fori_loop(0, n // unroll, body, 0)
    else:
        _gla_group_scan(t, p_ref, la_f, la_b, of_s, ob_s, st_s)
    if not latent:
        for d in range(2):
            for h in range(GLA_H):
                st_ref[d, h] = st_s[d, h * GLA_DV:(h + 1) * GLA_DV, h * GLA_DK:(h + 1) * GLA_DK]
    for r in range(t // QB):
        rows = slice(r * QB, (r + 1) * QB)
        o_ref[rows, :] = _head_rms_gate(of_s[rows, :] + ob_s[rows, :], p_ref[rows, 512:768]).astype(BF16)


def _gla(p, gate_w, gate_b, s0_bd, st_prev, l, latent):
    t, nseq, row0 = (T_LAT, N_LAT, M_CTX // T_LAT) if latent else (T_CTX, N_CTX, 0)
    st_shape = (2, GLA_H * GLA_DV, GLA_H * GLA_DK)
    in_specs = [_seq_spec(t, W_GLA, row0, latent),
                _layer_spec(l, 2, GLA_RANK, GLA_H * GLA_DK), _layer_spec(l, 2, 1, GLA_H * GLA_DK)]
    args = [p, gate_w, gate_b]
    aliases = {}
    o_shape = jax.ShapeDtypeStruct((nseq * t, 256), BF16)
    o_spec = pl.BlockSpec((t, 256), lambda b: (b, 0))
    if latent:
        in_specs.append(pl.BlockSpec((None, None) + st_shape, lambda b: (b, l, 0, 0, 0)))
        args.append(s0_bd)
    out_shape, out_specs = [o_shape], [o_spec]
    if not latent:
        _stacked_output(st_prev, (2, GLA_H, GLA_DV, GLA_DK), l, nseq, in_specs, args, out_shape, out_specs, aliases)
    scratch = [pltpu.VMEM((t, 128), F32), pltpu.VMEM((t, 128), F32),
               pltpu.VMEM((t, 256), F32), pltpu.VMEM((t, 256), F32), pltpu.VMEM(st_shape, F32)]
    return functools.partial(_gla_kernel, t, latent), in_specs, args, out_shape, out_specs, scratch, aliases


def _ret_kernel(t, latent, *refs):
    if latent:
        p_ref, dec_ref, cos_ref, slo_ref, shi_ref, s0_ref, o_ref, q_s, k_s, decay_s = refs
    else:
        p_ref, dec_ref = refs[0:2]
        o_ref, st_ref, decay_s = refs[-3:]
    lg = _log_sigmoid(dec_ref[...])
    scale = RET_DK ** -0.5
    nblk = t // QB

    @pl.when(pl.program_id(0) == 0)
    def _():
        wide = decay_s.shape[-1]
        dist = (lax.broadcasted_iota(jnp.int32, (QB, wide), 0) - lax.broadcasted_iota(jnp.int32, (QB, wide), 1)
                + (nblk - 1) * QB).astype(F32)
        for h in range(RET_H):
            lg_f, lg_b = lg[0:1, h * 64:h * 64 + 1], lg[1:2, h * 64:h * 64 + 1]
            decay_s[h] = (jnp.where(dist >= 0, jnp.exp(jnp.maximum(dist, 0.0) * lg_f), 0.0)
                          + jnp.where(dist <= 0, jnp.exp(jnp.maximum(-dist, 0.0) * lg_b), 0.0))

    if not latent:
        k = p_ref[:, 256:512] * scale
        v = p_ref[:, 512:768]
        decay = jnp.concatenate([decay_s[h] for h in range(RET_H)], axis=0)
        o = _unstack_heads(_mm(_mm_nt(_stack_heads(p_ref[:, 0:256], RET_H, RET_DK), k) * decay, v), RET_H, RET_DV)
        o_ref[...] = _head_rms_gate(o, p_ref[:, 768:1024]).astype(BF16)
        j = lax.broadcasted_iota(jnp.int32, (t, 1), 0).astype(F32)
        kv_f = _mm_tn(k * jnp.exp((float(t - 1) - j) * lg[0:1, :]), v)
        kv_b = _mm_tn(k * jnp.exp(j * lg[1:2, :]), v)
        for h in range(RET_H):
            st_ref[0, h] = kv_f[h * RET_DK:(h + 1) * RET_DK, h * RET_DV:(h + 1) * RET_DV]
            st_ref[1, h] = kv_b[h * RET_DK:(h + 1) * RET_DK, h * RET_DV:(h + 1) * RET_DV]
        return
    q_s[...] = _rope(p_ref[:, 0:256], cos_ref[...], slo_ref[...], shi_ref[...], RET_DK // 2)
    k_s[...] = _rope(p_ref[:, 256:512], cos_ref[...], slo_ref[...], shi_ref[...], RET_DK // 2) * scale
    for r in range(t // QL):
        rows = slice(r * QL, (r + 1) * QL)
        pos = (lax.broadcasted_iota(jnp.int32, (QL, 1), 0) + r * QL).astype(F32)
        strip_rows = slice(r * QL % QB, r * QL % QB + QL)
        off = (nblk - 1 - r * QL // QB) * QB
        decay = jnp.concatenate([decay_s[h, strip_rows, off:off + t] for h in range(RET_H)], axis=0)
        q = q_s[rows, :]
        o = _unstack_heads(_mm(_mm_nt(_stack_heads(q, RET_H, RET_DK), k_s[...]) * decay, p_ref[:, 512:768]),
                           RET_H, RET_DV)
        o = o + _mm(q * jnp.exp((pos + 1.0) * lg[0:1, :]), s0_ref[0])
        o = o + _mm(q * jnp.exp((float(t) - pos) * lg[1:2, :]), s0_ref[1])
        o_ref[rows, :] = _head_rms_gate(o, p_ref[rows, 768:1024]).astype(BF16)


def _ret(p, dec_lanes, tables, s0, st_prev, l, latent):
    t, nseq, row0 = (T_LAT, N_LAT, M_CTX // T_LAT) if latent else (T_CTX, N_CTX, 0)
    st_shape = (2, RET_H, RET_DK, RET_DV)
    in_specs = [_seq_spec(t, W_RET, row0, latent), _layer_spec(l, 2, 256)]
    args = [p, dec_lanes]
    aliases = {}
    o_shape = jax.ShapeDtypeStruct((nseq * t, 256), BF16)
    o_spec = pl.BlockSpec((t, 256), lambda b: (b, 0))
    scratch = []
    if latent:
        in_specs += [_table_spec(t, 256)] * 3
        in_specs.append(pl.BlockSpec((None, None, 2, RET_H * RET_DK, RET_H * RET_DV), lambda b: (b, l, 0, 0, 0)))
        args += list(tables) + [s0]
        scratch = [pltpu.VMEM((t, 256), F32), pltpu.VMEM((t, 256), F32)]
    scratch.append(pltpu.VMEM((RET_H, QB, 2 * t - QB), F32))
    out_shape, out_specs = [o_shape], [o_spec]
    if not latent:
        _stacked_output(st_prev, st_shape, l, nseq, in_specs, args, out_shape, out_specs, aliases)
    return functools.partial(_ret_kernel, t, latent), in_specs, args, out_shape, out_specs, scratch, aliases


def _mla_kernel(t, latent, *refs):
    if latent:
        (p_ref, gq_ref, gkv_ref, wq_ref, wkv_ref, cq_ref, slq_ref, shq_ref, ck_ref, slk_ref, shk_ref,
         cckv_ref, ckr_ref, o_ref, qn_s, qr_s, kn_s, kr_s, v_s) = refs
    else:
        p_ref, gq_ref, gkv_ref, wq_ref, wkv_ref = refs[0:5]
        o_ref, ckv_ref, kro_ref, qn_s, qr_s, kn_s, kr_s, v_s = refs[-8:]
    nk = t + (PAST if latent else 0)
    qh = _mm(_rms(p_ref[:, 0:256]) * gq_ref[...], wq_ref[...])
    ckv = _rms(p_ref[:, 256:384]) * gkv_ref[...]
    kv = _mm(ckv, wkv_ref[...])
    qn_s[...] = qh[:, 0:256]
    kn_s[0:t, :] = kv[:, 0:256]
    v_s[0:t, :] = kv[:, 256:512]
    if latent:
        qr_s[...] = _rope(qh[:, 256:384], cq_ref[...], slq_ref[...], shq_ref[...], MLA_ROPE // 2)
        kr_s[0:t, :] = _rope(p_ref[:, 384:512], ck_ref[...], slk_ref[...], shk_ref[...], MLA_ROPE // 2)
        kvc = _mm(cckv_ref[...], wkv_ref[...])
        kn_s[t:nk, :] = kvc[:, 0:256]
        v_s[t:nk, :] = kvc[:, 256:512]
        kr_s[t:nk, :] = ckr_ref[...]
    scale = (MLA_NOPE + MLA_ROPE) ** -0.5
    if not latent:
        ckv_ref[...] = ckv
        kro_ref[...] = p_ref[:, 384:384 + MLA_ROPE]
        kr = p_ref[:, 384:512]
        kr4 = kr + pltpu.roll(kr, MLA_ROPE, 1) + pltpu.roll(kr, 2 * MLA_ROPE, 1) + pltpu.roll(kr, 3 * MLA_ROPE, 1)
        s = (_mm_nt(_stack_heads(qh[:, 0:256], MLA_H, MLA_NOPE), kv[:, 0:256])
             + _mm_nt(_stack_heads(qh[:, 256:384], MLA_H, MLA_ROPE), kr4)) * scale
        e = jnp.exp(s - jnp.max(s, axis=-1, keepdims=True))
        o = _mm(e, kv[:, 256:512]) / jnp.sum(e, axis=-1, keepdims=True)
        o_ref[...] = _unstack_heads(o, MLA_H, MLA_DV).astype(BF16)
        return
    kr = kr_s[...]
    kr4 = kr + pltpu.roll(kr, MLA_ROPE, 1) + pltpu.roll(kr, 2 * MLA_ROPE, 1) + pltpu.roll(kr, 3 * MLA_ROPE, 1)
    for r in range(t // QL):
        rows = slice(r * QL, (r + 1) * QL)
        s = (_mm_nt(_stack_heads(qn_s[rows, :], MLA_H, MLA_NOPE), kn_s[...])
             + _mm_nt(_stack_heads(qr_s[rows, :], MLA_H, MLA_ROPE), kr4)) * scale
        e = jnp.exp(s - jnp.max(s, axis=-1, keepdims=True))
        o = _mm(e, v_s[...]) / jnp.sum(e, axis=-1, keepdims=True)
        o_ref[rows, :] = _unstack_heads(o, MLA_H, MLA_DV).astype(BF16)


def _mla(p, gq, gkv, wq, wkv, tables_q, tables_k, cache_ckv, cache_kr, prev, l, latent):
    t, nseq, row0 = (T_LAT, N_LAT, M_CTX // T_LAT) if latent else (T_CTX, N_CTX, 0)
    nk = t + (PAST if latent else 0)
    in_specs = [_seq_spec(t, W_MLA, row0, latent), _layer_spec(l, 1, MLA_QR),
                _layer_spec(l, 1, MLA_KVR), _layer_spec(l, MLA_QR, 384), _layer_spec(l, MLA_KVR, 512)]
    args = [p, gq, gkv, wq, wkv]
    aliases = {}
    o_shape = jax.ShapeDtypeStruct((nseq * t, 256), BF16)
    o_spec = pl.BlockSpec((t, 256), lambda b: (b, 0))
    if latent:
        in_specs += [_table_spec(t, 128)] * 6
        in_specs += [pl.BlockSpec((None, None, PAST, MLA_KVR), lambda b: (b, l, 0, 0)),
                     pl.BlockSpec((None, None, PAST, 128), lambda b: (b, l, 0, 0))]
        args += list(tables_q) + list(tables_k) + [cache_ckv, cache_kr]
    out_shape, out_specs = [o_shape], [o_spec]
    if not latent:
        prev = prev or (None, None)
        _stacked_output(prev[0], (t, MLA_KVR), l, nseq, in_specs, args, out_shape, out_specs, aliases)
        _stacked_output(prev[1], (t, MLA_ROPE), l, nseq, in_specs, args, out_shape, out_specs, aliases)
    scratch = [pltpu.VMEM((t, 256), F32), pltpu.VMEM((t, 128), F32), pltpu.VMEM((nk, 256), F32),
               pltpu.VMEM((nk, 128), F32), pltpu.VMEM((nk, 256), F32)]
    return functools.partial(_mla_kernel, t, latent), in_specs, args, out_shape, out_specs, scratch, aliases


def _gqa_kernel(t, latent, *refs):
    if latent:
        p_ref, sink_ref, cos_ref, slo_ref, shi_ref, ck_ref, cv_ref, o_ref, q_s, k_s = refs
    else:
        p_ref, sink_ref = refs[0:2]
        o_ref, ko_ref, vo_ref = refs[-3:]
        for kvh in range(GQA_KV):
            ko_ref[kvh] = p_ref[:, 256 + kvh * GQA_HD:256 + (kvh + 1) * GQA_HD]
            vo_ref[kvh] = p_ref[:, 384 + kvh * GQA_HD:384 + (kvh + 1) * GQA_HD]
    scale = GQA_HD ** -0.5
    if not latent:
        own = _own_lanes(GQA_H, t, GQA_HD)
        sink = jnp.max(jnp.where(own > 0, sink_ref[...], NEG), axis=-1, keepdims=True)
        s = _mm_nt(_stack_heads(p_ref[:, 0:256], GQA_H, GQA_HD), _per_query_head(p_ref[:, 256:384])) * scale
        m = jnp.maximum(jnp.max(s, axis=-1, keepdims=True), sink)
        e = jnp.exp(s - m)
        o = _mm(e, _per_query_head(p_ref[:, 384:512])) / (jnp.sum(e, axis=-1, keepdims=True) + jnp.exp(sink - m))
        o_ref[...] = _unstack_heads(o, GQA_H, GQA_HD).astype(BF16)
        return
    q_s[...] = _rope(p_ref[:, 0:256], cos_ref[...], slo_ref[...], shi_ref[...], GQA_HD // 2)
    k_s[...] = _rope(p_ref[:, 256:384], cos_ref[:, 0:128], slo_ref[:, 0:128], shi_ref[:, 0:128], GQA_HD // 2)
    kc = _per_query_head(jnp.concatenate([ck_ref[0], ck_ref[1]], axis=1))
    vc = _per_query_head(jnp.concatenate([cv_ref[0], cv_ref[1]], axis=1))
    sink = jnp.max(jnp.where(_own_lanes(GQA_H, QL, GQA_HD) > 0, sink_ref[...], NEG), axis=-1, keepdims=True)
    for r in range(t // QL):
        rows = slice(r * QL, (r + 1) * QL)
        keys = slice(max(0, r * QL - WINDOW), min(t, (r + 1) * QL + WINDOW))
        nkeys = keys.stop - keys.start
        row = lax.broadcasted_iota(jnp.int32, (GQA_H * QL, nkeys), 0) % QL + r * QL
        col = lax.broadcasted_iota(jnp.int32, (GQA_H * QL, nkeys), 1) + keys.start
        near = jnp.abs(row - col) <= WINDOW
        q = _stack_heads(q_s[rows, :], GQA_H, GQA_HD)
        s_loc = jnp.where(near, _mm_nt(q, _per_query_head(k_s[keys, :])) * scale, NEG)
        s_ctx = _mm_nt(q, kc) * scale
        m = jnp.maximum(jnp.maximum(jnp.max(s_loc, axis=-1, keepdims=True), jnp.max(s_ctx, axis=-1, keepdims=True)),
                        sink)
        e_loc, e_ctx = jnp.exp(s_loc - m), jnp.exp(s_ctx - m)
        den = jnp.sum(e_loc, axis=-1, keepdims=True) + jnp.sum(e_ctx, axis=-1, keepdims=True) + jnp.exp(sink - m)
        o = (_mm(e_loc, _per_query_head(p_ref[keys, 384:512])) + _mm(e_ctx, vc)) / den
        o_ref[rows, :] = _unstack_heads(o, GQA_H, GQA_HD).astype(BF16)


def _gqa(p, sink_lanes, tables, cache_k, cache_v, prev, l, latent):
    t, nseq, row0 = (T_LAT, N_LAT, M_CTX // T_LAT) if latent else (T_CTX, N_CTX, 0)
    in_specs = [_seq_spec(t, W_GQA, row0, latent), _layer_spec(l, 1, 256)]
    args = [p, sink_lanes]
    scratch = []
    aliases = {}
    out_shape = [jax.ShapeDtypeStruct((nseq * t, 256), BF16)]
    out_specs = [pl.BlockSpec((t, 256), lambda b: (b, 0))]
    if latent:
        in_specs += [_table_spec(t, 256)] * 3
        in_specs += [pl.BlockSpec((None, None, GQA_KV, PAST, GQA_HD), lambda b: (b, l, 0, 0, 0))] * 2
        args += list(tables) + [cache_k, cache_v]
        scratch = [pltpu.VMEM((t, 256), F32), pltpu.VMEM((t, 128), F32)]
    else:
        prev = prev or (None, None)
        for pv in prev:
            _stacked_output(pv, (GQA_KV, t, GQA_HD), l, nseq, in_specs, args, out_shape, out_specs, aliases)
    return functools.partial(_gqa_kernel, t, latent), in_specs, args, out_shape, out_specs, scratch, aliases


def _outproj_kernel(route, split, tm, *refs):
    is_lat = pl.program_id(0) >= M_CTX // tm
    if split:
        x_in = jnp.where(is_lat, refs[1][...], refs[0][...])
        refs = refs[2:]
    else:
        x_in = refs[0][...]
        refs = refs[1:]
    ctx_refs, lat_refs = refs[0:4], refs[4:8]
    w_ref, gt_ref, g_ref, sh_ref, sc_ref = refs[8:13]
    refs = refs[13:]
    mix = jnp.zeros((tm, D), F32)
    for m in range(4):
        o = jnp.where(is_lat, lat_refs[m][...], ctx_refs[m][...])
        mix = mix + jnp.dot(o, w_ref[m * 256:(m + 1) * 256, :], preferred_element_type=F32)
    x = x_in + gt_ref[...] * mix
    h = (_rms(x) * g_ref[...] * (1.0 + sc_ref[...]) + sh_ref[...]).astype(BF16)
    if not route:
        xo_ref, h_ref = refs
    else:
        r_ref, xo_ref, h_ref, gate_ref, sel_ref = refs
        lane = lax.broadcasted_iota(jnp.int32, gate_ref.shape, 1)
        logits = jnp.where(lane < N_EXP, jnp.dot(h, r_ref[...], preferred_element_type=F32), NEG)
        m1 = jnp.max(logits, axis=-1, keepdims=True)
        i1 = jnp.min(jnp.where(logits == m1, lane, LANE), axis=-1, keepdims=True)
        rest = jnp.where(lane == i1, NEG, logits)
        m2 = jnp.max(rest, axis=-1, keepdims=True)
        i2 = jnp.min(jnp.where(rest == m2, lane, LANE), axis=-1, keepdims=True)
        e2 = jnp.exp(m2 - m1)
        gate_ref[...] = jnp.where(lane == i1, 1.0 / (1.0 + e2), 0.0) + jnp.where(lane == i2, e2 / (1.0 + e2), 0.0)
        sel_ref[...] = jnp.where((lane == i1) | (lane == i2), 1, 0)
    xo_ref[...] = x
    h_ref[...] = h


def _outproj(xs, o_ctx, o_lat, w_out, g2, mod, router, l):
    tm = 512
    route = router is not None
    split = len(xs) == 2
    rows = lambda w: pl.BlockSpec((tm, w), lambda i: (i, 0))
    ctx_spec, lat_spec = _row_split_specs(tm, 256)
    x_specs = _row_split_specs(tm, D) if split else [rows(D)]
    in_specs = x_specs + [ctx_spec] * 4 + [lat_spec] * 4 + [
        _layer_spec(l, D, D), _mod_spec(l, 2, tm), _layer_spec(l, 1, D), _mod_spec(l, 3, tm), _mod_spec(l, 4, tm)]
    args = [*xs, *o_ctx, *o_lat, w_out, mod, g2, mod, mod]
    out_shape = [jax.ShapeDtypeStruct((M_ALL, D), F32), jax.ShapeDtypeStruct((M_ALL, D), BF16)]
    out_specs = [rows(D), rows(D)]
    if route:
        in_specs.append(_layer_spec(l // 2, D, LANE))
        args.append(router)
        out_shape += [jax.ShapeDtypeStruct((M_ALL, LANE), F32), jax.ShapeDtypeStruct((M_ALL, LANE), jnp.int32)]
        out_specs += [rows(LANE), rows(LANE)]
    return pl.pallas_call(
        functools.partial(_outproj_kernel, route, split, tm),
        out_shape=out_shape, grid=(M_ALL // tm,), in_specs=in_specs, out_specs=out_specs,
        compiler_params=_cparams("parallel"),
        name="outproj_route" if route else "outproj",
    )(*args)


def _ffn_kernel(h_ref, x_ref, gt_ref, wg_ref, wu_ref, wd_ref, o_ref, acc_ref):
    f = pl.program_id(1)

    @pl.when(f == 0)
    def _():
        acc_ref[...] = jnp.zeros_like(acc_ref)

    wg, wu, wd = wg_ref[...].astype(BF16), wu_ref[...].astype(BF16), wd_ref[...].astype(BF16)
    half = h_ref.shape[0] // 2
    for rows in (slice(0, half), slice(half, 2 * half)):
        h = h_ref[rows, :]
        g = jnp.dot(h, wg, preferred_element_type=F32)
        u = jnp.dot(h, wu, preferred_element_type=F32)
        acc_ref[rows, :] += jnp.dot((_silu(g) * u).astype(BF16), wd, preferred_element_type=F32)

    @pl.when(f == pl.num_programs(1) - 1)
    def _():
        o_ref[...] = x_ref[...] + gt_ref[...] * acc_ref[...]


def _ffn(h, x, mod, wg, wu, wd, l):
    tm, tf = 1024, 256
    j = l // 2
    return pl.pallas_call(
        _ffn_kernel,
        out_shape=jax.ShapeDtypeStruct((M_ALL, D), F32),
        grid=(M_ALL // tm, D_FF // tf),
        in_specs=[pl.BlockSpec((tm, D), lambda i, f: (i, 0)), pl.BlockSpec((tm, D), lambda i, f: (i, 0)),
                  _mod_spec(l, 5, tm),
                  pl.BlockSpec((None, D, tf), lambda i, f: (j, 0, f)),
                  pl.BlockSpec((None, D, tf), lambda i, f: (j, 0, f)),
                  pl.BlockSpec((None, tf, D), lambda i, f: (j, f, 0))],
        out_specs=pl.BlockSpec((tm, D), lambda i, f: (i, 0)),
        scratch_shapes=[pltpu.VMEM((tm, D), F32)],
        compiler_params=_cparams("parallel", "arbitrary"),
        name="ffn_dense",
    )(h, x, mod, wg, wu, wd)


SUP, SUB, CHUNK = 2048, 256, 256
N_SUB = SUP // SUB
S_MAX = 2 * M_ALL // SUP + N_EXP
P_SLOT = S_MAX * SUP
TC = 128
WIN = TC + 16


def _moe_expert_kernel(se_ref, nt_ref, sblk_ref, clo_ref, chi_ref, pos_ref, h_ref, wg_ref, wu_ref, wd_ref, o_ref,
                       xs_s, acc_s, wg_s, wu_s, wd_s):
    s = pl.program_id(0)
    f = pl.program_id(1)
    n = nt_ref[s]

    @pl.when((f == 0) & (n == 0))
    def _():
        o_ref[...] = jnp.zeros_like(o_ref)

    @pl.when((f == 0) & (n > 0))
    def _():
        row = lax.broadcasted_iota(jnp.int32, (SUB, CHUNK), 0)

        def gather(j, carry):
            row0 = (s * N_SUB + j) * SUB
            acc_s[j] = jnp.zeros((SUB, D), F32)

            def chunk(c, carry):
                tpos = pos_ref[:, pl.ds(pl.multiple_of(c * CHUNK, CHUNK), CHUNK)]
                onehot = jnp.where(tpos - row0 == row, 1.0, 0.0).astype(BF16)
                rows = h_ref[pl.ds(pl.multiple_of(c * CHUNK, CHUNK), CHUNK), :]
                acc_s[j] += jnp.dot(onehot, rows, preferred_element_type=F32)
                return carry

            g = s * N_SUB + j
            lax.fori_loop(clo_ref[g], chi_ref[g] + 1, chunk, 0)
            xs_s[j] = acc_s[j].astype(BF16)
            acc_s[j] = jnp.zeros((SUB, D), F32)
            return carry

        lax.fori_loop(0, n, gather, 0)

    @pl.when(n > 0)
    def _():
        wg_s[...] = wg_ref[...].astype(BF16)
        wu_s[...] = wu_ref[...].astype(BF16)
        wd_s[...] = wd_ref[...].astype(BF16)

        def sub(j):
            x = xs_s[j]
            g = jnp.dot(x, wg_s[...], preferred_element_type=F32)
            u = jnp.dot(x, wu_s[...], preferred_element_type=F32)
            acc_s[j] += jnp.dot((_silu(g) * u).astype(BF16), wd_s[...], preferred_element_type=F32)

        def pair(jj, carry):
            sub(2 * jj)
            sub(2 * jj + 1)
            return carry

        lax.fori_loop(0, n // 2, pair, 0)

        @pl.when(n % 2 == 1)
        def _():
            sub(n - 1)

    @pl.when((f == pl.num_programs(1) - 1) & (n > 0))
    def _():
        for j in range(N_SUB):
            rows = slice(j * SUB, (j + 1) * SUB)

            @pl.when(j < n)
            def _():
                o_ref[rows, :] = acc_s[j].astype(BF16)

            @pl.when(j >= n)
            def _():
                o_ref[rows, :] = jnp.zeros((SUB, D), BF16)


def _moe_experts(h, pos_t, meta, wg, wu, wd, layer):
    tf = 512
    nf = D_FFE // tf
    se, nt, sblk, clo, chi = meta

    def w_up(s, f, se, nt, sblk, clo, chi):
        return (layer, se[s], 0, jnp.where(nt[s] > 0, f, nf - 1))

    def w_down(s, f, se, nt, sblk, clo, chi):
        return (layer, se[s], jnp.where(nt[s] > 0, f, nf - 1), 0)

    grid_spec = pltpu.PrefetchScalarGridSpec(
        num_scalar_prefetch=5,
        grid=(S_MAX, nf),
        in_specs=[pl.BlockSpec((None, 1, M_ALL), lambda s, f, se, nt, sblk, clo, chi: (se[s], 0, 0)),
                  pl.BlockSpec((M_ALL, D), lambda s, f, *_: (0, 0), pipeline_mode=pl.Buffered(1)),
                  pl.BlockSpec((None, None, D, tf), w_up), pl.BlockSpec((None, None, D, tf), w_up),
                  pl.BlockSpec((None, None, tf, D), w_down)],
        out_specs=pl.BlockSpec((SUP, D), lambda s, f, *_: (s, 0)),
        scratch_shapes=[pltpu.VMEM((N_SUB, SUB, D), BF16), pltpu.VMEM((N_SUB, SUB, D), F32),
                        pltpu.VMEM((D, tf), BF16), pltpu.VMEM((D, tf), BF16), pltpu.VMEM((tf, D), BF16)],
    )
    return pl.pallas_call(
        _moe_expert_kernel,
        out_shape=jax.ShapeDtypeStruct((P_SLOT, D), BF16),
        grid_spec=grid_spec,
        compiler_params=pltpu.CompilerParams(dimension_semantics=("arbitrary", "arbitrary"),
                                             vmem_limit_bytes=60 * 1024 * 1024),
        name="moe_experts",
    )(se, nt, sblk, clo, chi, pos_t, h, wg, wu, wd)


def _moe_combine_kernel(off_ref, x_ref, gt_ref, gate_ref, pos_ref, *refs):
    win_refs, o_ref = refs[:N_EXP], refs[N_EXP]
    i = pl.program_id(0)
    lane = lax.broadcasted_iota(jnp.int32, (TC, WIN), 1)
    y = jnp.zeros((TC, D), F32)
    for e in range(N_EXP):
        rel = pos_ref[:, e:e + 1] - off_ref[i * N_EXP + e] * 16
        onehot = jnp.where(rel == lane, 1.0, 0.0).astype(BF16)
        y = y + gate_ref[:, e:e + 1] * jnp.dot(onehot, win_refs[e][...], preferred_element_type=F32)
    o_ref[...] = x_ref[...] + gt_ref[...] * y


def _moe_combine(x, mod, gates, pos, off, slots, l):
    def win_spec(e):
        return pl.BlockSpec((pl.Element(WIN), pl.Element(D)), lambda i, off: (off[i * N_EXP + e] * 16, 0))

    grid_spec = pltpu.PrefetchScalarGridSpec(
        num_scalar_prefetch=1,
        grid=(M_ALL // TC,),
        in_specs=[pl.BlockSpec((TC, D), lambda i, off: (i, 0)),
                  _mod_spec(l, 5, TC),
                  pl.BlockSpec((TC, LANE), lambda i, off: (i, 0)), pl.BlockSpec((TC, LANE), lambda i, off: (i, 0))]
        + [win_spec(e) for e in range(N_EXP)],
        out_specs=pl.BlockSpec((TC, D), lambda i, off: (i, 0)),
    )
    return pl.pallas_call(
        _moe_combine_kernel,
        out_shape=jax.ShapeDtypeStruct((M_ALL, D), F32),
        grid_spec=grid_spec,
        compiler_params=_cparams("arbitrary"),
        name="moe_combine",
    )(off, x, mod, gates, pos, *([slots] * N_EXP))


def _moe_plan(sel):
    i32 = jnp.int32
    sel_t = sel[:, :N_EXP].T
    csum = jnp.cumsum(sel_t, axis=1)
    rank = csum - sel_t
    n_e = csum[:, -1]
    ns_e = (n_e + SUP - 1) // SUP
    end_e = jnp.cumsum(ns_e)
    start_e = end_e - ns_e
    n_used = end_e[-1]
    pos_t = jnp.where(sel_t > 0, start_e[:, None] * SUP + rank, -1)
    s_ids = jnp.arange(S_MAX, dtype=i32)
    sblk = jnp.minimum(s_ids, n_used - 1)
    se = jnp.sum((end_e[None, :] <= sblk[:, None]).astype(i32), axis=1)
    nv = jnp.clip(n_e[se] - (sblk - start_e[se]) * SUP, 0, SUP)
    nt = jnp.where(s_ids < n_used, (nv + SUB - 1) // SUB, 0)
    g_ids = jnp.arange(S_MAX * N_SUB, dtype=i32)
    s_g, e_g = g_ids // N_SUB, se[g_ids // N_SUB]
    r0 = (sblk[s_g] - start_e[e_g]) * SUP + (g_ids % N_SUB) * SUB
    r1 = jnp.minimum(r0 + SUB, n_e[e_g])
    live = (s_g < n_used) & (r0 < n_e[e_g])
    cs_g = csum[e_g]
    t_first = jnp.sum((cs_g <= r0[:, None]).astype(i32), axis=1)
    t_last = jnp.sum((cs_g < r1[:, None]).astype(i32), axis=1)
    clo = jnp.where(live, t_first // CHUNK, 0)
    chi = jnp.where(live, t_last // CHUNK, -1)
    before = jnp.concatenate([jnp.zeros((N_EXP, 1), i32), csum[:, TC - 1::TC][:, :-1]], axis=1)
    base = start_e[:, None] * SUP + before
    off = jnp.minimum(base // 16, (P_SLOT - WIN) // 16).T.reshape(-1)
    pos128 = jnp.pad(pos_t.T, ((0, 0), (0, LANE - N_EXP)), constant_values=-1)
    meta = (se.astype(i32), nt.astype(i32), sblk.astype(i32), clo.astype(i32), chi.astype(i32))
    return meta, pos_t.reshape(N_EXP, 1, M_ALL), pos128, off.astype(i32)


def _moe(h, x, mod, gates, sel, wg, wu, wd, l):
    meta, pos_t, pos, off = _moe_plan(sel)
    slots = _moe_experts(h, pos_t, meta, wg, wu, wd, l // 2)
    return _moe_combine(x, mod, gates, pos, off, slots, l)


def _final_kernel(tm, x_ref, g_ref, oc_ref, ol_ref):
    y = _rms(x_ref[...]) * g_ref[...]
    is_lat = pl.program_id(0) >= M_CTX // tm

    @pl.when(jnp.logical_not(is_lat))
    def _():
        oc_ref[...] = y

    @pl.when(is_lat)
    def _():
        ol_ref[...] = y


def _final_norm(x, g):
    tm = 1024
    return pl.pallas_call(
        functools.partial(_final_kernel, tm),
        out_shape=[jax.ShapeDtypeStruct((M_CTX, D), F32), jax.ShapeDtypeStruct((M_LAT, D), F32)],
        grid=(M_ALL // tm,),
        in_specs=[pl.BlockSpec((tm, D), lambda i: (i, 0)), pl.BlockSpec((1, D), lambda i: (0, 0))],
        out_specs=_row_split_specs(tm, D),
        compiler_params=_cparams("arbitrary"),
        name="final_norm",
    )(x, g)


def kernel(x_prompt, x_sample, state_gla, state_ret, cache_mla_ckv, cache_mla_krope, cache_gqa_k, cache_gqa_v,
           c, c_ctx, norm1_g, norm2_g, final_norm_g, w_mod, b_mod, w_in, w_out, gla_gate_w, gla_gate_b,
           mla_q_norm_g, mla_w_q_up, mla_kv_norm_g, mla_w_kv_up, ret_decay, gqa_sink,
           ffn_w_gate, ffn_w_up, ffn_w_down, moe_router, moe_w_gate, moe_w_up, moe_w_down):
    xs = (x_prompt.reshape(M_CTX, D), x_sample.reshape(M_LAT, D))

    cond = jnp.concatenate([c_ctx[None], c, jnp.zeros((8 - 1 - N_LAT, D), F32)], axis=0)
    mod = _modulation(cond, w_mod, b_mod)
    mod = mod[:, :1 + N_LAT].reshape(DEPTH, (1 + N_LAT) * 6, 1, D)

    w_out_b = w_out.astype(BF16)
    wq = mla_w_q_up.reshape(DEPTH, MLA_QR, MLA_H, MLA_NOPE + MLA_ROPE)
    wq = jnp.concatenate([wq[..., :MLA_NOPE].reshape(DEPTH, MLA_QR, MLA_H * MLA_NOPE),
                          wq[..., MLA_NOPE:].reshape(DEPTH, MLA_QR, MLA_H * MLA_ROPE)], axis=-1).astype(BF16)
    wkv = mla_w_kv_up.reshape(DEPTH, MLA_KVR, MLA_H, MLA_NOPE + MLA_DV)
    wkv = jnp.concatenate([wkv[..., :MLA_NOPE].reshape(DEPTH, MLA_KVR, MLA_H * MLA_NOPE),
                           wkv[..., MLA_NOPE:].reshape(DEPTH, MLA_KVR, MLA_H * MLA_DV)], axis=-1).astype(BF16)
    router = jnp.pad(moe_router, ((0, 0), (0, 0), (0, LANE - N_EXP))).astype(BF16)
    dec_lanes = jnp.repeat(ret_decay, RET_DV, axis=-1)
    sink_lanes = jnp.repeat(gqa_sink, GQA_HD, axis=-1).reshape(DEPTH, 1, 256)
    gate_b = gla_gate_b.reshape(DEPTH, 2, 1, GLA_H * GLA_DK)
    g1, g2 = norm1_g.reshape(DEPTH, 1, D), norm2_g.reshape(DEPTH, 1, D)
    gq, gkv = mla_q_norm_g.reshape(DEPTH, 1, MLA_QR), mla_kv_norm_g.reshape(DEPTH, 1, MLA_KVR)
    eye = jnp.eye(GLA_H, dtype=F32)
    s0_gla = jnp.einsum('bldhkv,hg->bldhvgk', state_gla, eye).reshape(
        N_LAT, DEPTH, 2, GLA_H * GLA_DV, GLA_H * GLA_DK)
    s0_ret = jnp.einsum('bldhkv,hg->bldhkgv', state_ret, jnp.eye(RET_H, dtype=F32)).reshape(
        N_LAT, DEPTH, 2, RET_H * RET_DK, RET_H * RET_DV)
    cache_kr = jnp.pad(cache_mla_krope, ((0, 0), (0, 0), (0, 0), (0, 128 - MLA_ROPE)))

    rope64 = _rope_tables(T_LAT, 64, 256)
    rope32_q = _rope_tables(T_LAT, 32, 128)
    ck, sl, sh = _rope_tables(T_LAT, 32, 128)
    live = jnp.asarray((np.arange(128) < MLA_ROPE).astype(np.float32))[None]
    rope32_k = (ck * live, sl * live, sh * live)

    st_gla = st_ret = caches_mla = caches_gqa = None
    for l in range(DEPTH):
        p_gla, p_mla, p_ret, p_gqa = _inproj(xs, g1, mod, w_in, l)

        o_gla_c, st_gla, o_mla_c, ckv_all, kr_all, o_ret_c, st_ret, o_gqa_c, gk_all, gv_all = _fused_call([
            _gla(p_gla, gla_gate_w, gate_b, None, st_gla, l, False),
            _mla(p_mla, gq, gkv, wq, wkv, None, None, None, None, caches_mla, l, False),
            _ret(p_ret, dec_lanes, None, None, st_ret, l, False),
            _gqa(p_gqa, sink_lanes, None, None, None, caches_gqa, l, False)], N_CTX, "mixers_context")
        caches_mla, caches_gqa = (ckv_all, kr_all), (gk_all, gv_all)
        o_gla_s, o_ret_s = _fused_call([
            _gla(p_gla, gla_gate_w, gate_b, s0_gla, None, l, True),
            _ret(p_ret, dec_lanes, rope64, s0_ret, None, l, True)], N_LAT, "mixers_latent_scan")
        o_mla_s, o_gqa_s = _fused_call([
            _mla(p_mla, gq, gkv, wq, wkv, rope32_q, rope32_k, cache_mla_ckv, cache_kr, None, l, True),
            _gqa(p_gqa, sink_lanes, rope64, cache_gqa_k, cache_gqa_v, None, l, True)], N_LAT, "mixers_latent_attn")
        o_ctx = (o_gla_c, o_mla_c, o_ret_c, o_gqa_c)
        o_lat = (o_gla_s, o_mla_s, o_ret_s, o_gqa_s)
        if l % 2 == 0:
            x, h2 = _outproj(xs, o_ctx, o_lat, w_out_b, g2, mod, None, l)
            x = _ffn(h2, x, mod, ffn_w_gate, ffn_w_up, ffn_w_down, l)
        else:
            x, h2, gates, sel = _outproj(xs, o_ctx, o_lat, w_out_b, g2, mod, router, l)
            x = _moe(h2, x, mod, gates, sel, moe_w_gate, moe_w_up, moe_w_down, l)
        xs = (x,)

    y_ctx, y_lat = _final_norm(x, final_norm_g[None])
    return (y_ctx.reshape(N_CTX, T_CTX, D), y_lat.reshape(N_LAT, T_LAT, D), jnp.swapaxes(st_gla, -1, -2), st_ret,
            *caches_mla, *caches_gqa)
```

```python
import functools

import numpy as np
import jax
import jax.numpy as jnp
from jax import lax
from jax.experimental import pallas as pl
from jax.experimental.pallas import tpu as pltpu

F32 = jnp.float32
BF16 = jnp.bfloat16
HIGHEST = lax.Precision.HIGHEST

D = 1024
N_CTX, T_CTX = 16, 256
N_LAT, T_LAT = 2, 1024
PAST = 256
DEPTH = 4
M_CTX = N_CTX * T_CTX
M_LAT = N_LAT * T_LAT
M_ALL = M_CTX + M_LAT
GRID_W = 64
ROPE_BASE = 10000.0
EPS = 1e-6

GLA_H, GLA_DK, GLA_DV, GLA_RANK, GLA_NORM, GLA_C = 4, 32, 64, 16, 16.0, 64
MLA_H, MLA_QR, MLA_KVR, MLA_NOPE, MLA_ROPE, MLA_DV = 4, 256, 128, 64, 32, 64
RET_H, RET_DK, RET_DV = 4, 64, 64
GQA_H, GQA_KV, GQA_HD, WINDOW = 4, 2, 64, 128
D_FF, N_EXP, D_FFE = 2816, 8, 3584

W_GLA, W_MLA, W_RET, W_GQA = 896, 512, 1024, 512
IN_WIDTH = 2752
IN_GROUPS = ((0, 800), (800, 1216), (1216, 2240), (2240, 2752))
LANE = 128
NEG = -1e30
QB = 256
QL = 128
VMEM_LIMIT = 56 * 1024 * 1024


def _cparams(*sem):
    return pltpu.CompilerParams(dimension_semantics=sem, vmem_limit_bytes=VMEM_LIMIT)


def _mm(a, b):
    return jnp.dot(a.astype(BF16), b.astype(BF16), preferred_element_type=F32)


def _mm_nt(a, b):
    return lax.dot_general(a.astype(BF16), b.astype(BF16), (((1,), (1,)), ((), ())), preferred_element_type=F32)


def _mm_tn(a, b):
    return lax.dot_general(a.astype(BF16), b.astype(BF16), (((0,), (0,)), ((), ())), preferred_element_type=F32)


def _mm_f32(a, b):
    return jnp.dot(a, b, precision=HIGHEST, preferred_element_type=F32)


def _silu(x):
    return x * (1.0 / (1.0 + jnp.exp(-x)))


def _log_sigmoid(x):
    return jnp.minimum(x, 0.0) - jnp.log1p(jnp.exp(-jnp.abs(x)))


def _rms(x):
    return x * lax.rsqrt(jnp.mean(x * x, axis=-1, keepdims=True) + EPS)


def _mod_row(tile, tm):
    return jnp.maximum((tile * tm) // T_LAT - (M_CTX // T_LAT - 1), 0)


def _mod_kernel(c_ref, w_ref, b_ref, o_ref):
    o_ref[...] = _mm(_silu(c_ref[...]), w_ref[...]) + b_ref[...]


def _modulation(cond, w_mod, b_mod):
    tn = 1536
    return pl.pallas_call(
        _mod_kernel,
        out_shape=jax.ShapeDtypeStruct((DEPTH, 8, 6 * D), F32),
        grid=(DEPTH, 6 * D // tn),
        in_specs=[pl.BlockSpec((8, D), lambda l, j: (0, 0)),
                  pl.BlockSpec((None, D, tn), lambda l, j: (l, 0, j)),
                  pl.BlockSpec((None, 1, tn), lambda l, j: (l, 0, j))],
        out_specs=pl.BlockSpec((None, 8, tn), lambda l, j: (l, 0, j)),
        compiler_params=_cparams("parallel", "parallel"),
        name="modulation",
    )(cond, w_mod, b_mod.reshape(DEPTH, 1, 6 * D))


def _inproj_kernel(split, tm, *refs):
    if split:
        xc_ref, xl_ref = refs[0:2]
        x = jnp.where(pl.program_id(0) >= M_CTX // tm, xl_ref[...], xc_ref[...])
        refs = refs[2:]
    else:
        x = refs[0][...]
        refs = refs[1:]
    g_ref, sh_ref, sc_ref, w_ref = refs[0:4]
    out_refs, w_s = refs[4:-1], refs[-1]

    @pl.when(pl.program_id(0) == 0)
    def _():
        dst = 0
        for (lo, hi), o_ref in zip(IN_GROUPS, out_refs):
            width = o_ref.shape[-1]
            w_s[:, dst:dst + hi - lo] = w_ref[:, lo:hi].astype(BF16)
            if width > hi - lo:
                w_s[:, dst + hi - lo:dst + width] = jnp.zeros((D, width - (hi - lo)), BF16)
            dst += width

    h = (_rms(x) * g_ref[...] * (1.0 + sc_ref[...]) + sh_ref[...]).astype(BF16)
    col = 0
    for o_ref in out_refs:
        width = o_ref.shape[-1]
        o_ref[...] = jnp.dot(h, w_s[:, col:col + width], preferred_element_type=F32)
        col += width


def _mod_spec(l, j, tm):
    return pl.BlockSpec((None, None, 1, D), lambda i, *_: (l, _mod_row(i, tm) * 6 + j, 0, 0))


def _layer_spec(l, *shape):
    return pl.BlockSpec((None,) + shape, lambda *_: (l,) + (0,) * len(shape))


def _stacked_output(prev, tail, l, nseq, in_specs, args, out_shape, out_specs, aliases):
    out_shape.append(jax.ShapeDtypeStruct((nseq, DEPTH) + tail, F32))
    out_specs.append(pl.BlockSpec((None, None) + tail, lambda b: (b, l) + (0,) * len(tail)))
    if prev is not None:
        in_specs.append(pl.BlockSpec(memory_space=pl.ANY))
        args.append(prev)
        aliases[len(args) - 1] = len(out_shape) - 1


def _row_split_specs(tm, width):
    n_ctx = M_CTX // tm
    return [pl.BlockSpec((tm, width), lambda i: (jnp.minimum(i, n_ctx - 1), 0)),
            pl.BlockSpec((tm, width), lambda i: (jnp.maximum(i - n_ctx, 0), 0))]


def _inproj(xs, g1, mod, w_in, l):
    tm = 512
    split = len(xs) == 2
    rows = lambda w: pl.BlockSpec((tm, w), lambda i: (i, 0))
    widths = (W_GLA, W_MLA, W_RET, W_GQA)
    x_specs = _row_split_specs(tm, D) if split else [rows(D)]
    w_spec = pl.BlockSpec((None, D, IN_WIDTH), lambda i: (l, 0, 0), pipeline_mode=pl.Buffered(1))
    return pl.pallas_call(
        functools.partial(_inproj_kernel, split, tm),
        out_shape=[jax.ShapeDtypeStruct((M_ALL, w), F32) for w in widths],
        grid=(M_ALL // tm,),
        in_specs=x_specs + [_layer_spec(l, 1, D), _mod_spec(l, 0, tm), _mod_spec(l, 1, tm), w_spec],
        out_specs=[rows(w) for w in widths],
        scratch_shapes=[pltpu.VMEM((D, sum(widths)), BF16)],
        compiler_params=_cparams("arbitrary"),
        name="inproj",
    )(*xs, g1, mod, mod, w_in)


def _head_rms_gate(o, gate):
    r = lax.broadcasted_iota(jnp.int32, (256, 256), 0) // 64
    c = lax.broadcasted_iota(jnp.int32, (256, 256), 1) // 64
    group_mean = jnp.where(r == c, 1.0 / 64.0, 0.0).astype(BF16)
    sq = o * o
    sq_hi = sq.astype(BF16)
    sq_lo = (sq - sq_hi.astype(F32)).astype(BF16)
    ms = (jnp.dot(sq_hi, group_mean, preferred_element_type=F32)
          + jnp.dot(sq_lo, group_mean, preferred_element_type=F32))
    return o * lax.rsqrt(ms + EPS) * _silu(gate)


def _own_lanes(n_heads, rows_per_head, lanes_per_head):
    shape = (n_heads * rows_per_head, n_heads * lanes_per_head)
    return (lax.broadcasted_iota(jnp.int32, shape, 0) // rows_per_head
            == lax.broadcasted_iota(jnp.int32, shape, 1) // lanes_per_head).astype(F32)


def _stack_heads(x, n_heads, lanes_per_head):
    return jnp.concatenate([x] * n_heads, axis=0) * _own_lanes(n_heads, x.shape[0], lanes_per_head)


def _unstack_heads(y, n_heads, lanes_per_head):
    t = y.shape[0] // n_heads
    y = y * _own_lanes(n_heads, t, lanes_per_head)
    out = y[0:t]
    for h in range(1, n_heads):
        out = out + y[h * t:(h + 1) * t]
    return out


def _per_query_head(x):
    lane = lax.broadcasted_iota(jnp.int32, x.shape, 1)
    swapped = pltpu.roll(x, GQA_HD, 1)
    return jnp.concatenate([jnp.where(lane < GQA_HD, x, swapped), jnp.where(lane < GQA_HD, swapped, x)], axis=1)


def _rope(x, cos, sin_lo, sin_hi, half):
    w = x.shape[-1]
    return x * cos + pltpu.roll(x, w - half, 1) * sin_lo + pltpu.roll(x, half, 1) * sin_hi


def _rope_tables(t, head_dim, width):
    half = head_dim // 2
    quarter = head_dim // 4
    pos = np.arange(t)
    rows = (pos // GRID_W).astype(np.float32)
    cols = (pos % GRID_W).astype(np.float32)
    inv = np.power(np.float32(ROPE_BASE), -np.arange(quarter, dtype=np.float32) / np.float32(quarter)).astype(np.float32)
    ang = np.concatenate([rows[:, None] * inv, cols[:, None] * inv], axis=-1).astype(np.float32)
    lane = np.arange(width)
    a = ang[:, lane % half]
    cos, sin = np.cos(a).astype(np.float32), np.sin(a).astype(np.float32)
    low = (lane % head_dim) < half
    return (jnp.asarray(cos), jnp.asarray(np.where(low[None], -sin, 0.0).astype(np.float32)),
            jnp.asarray(np.where(low[None], 0.0, sin).astype(np.float32)))


def _seq_spec(t, width, row0, latent):
    return pl.BlockSpec((t, width), lambda b: (row0 + b, 0), pipeline_mode=pl.Buffered(1) if latent else None)


def _table_spec(t, width):
    return pl.BlockSpec((t, width), lambda b: (0, 0), pipeline_mode=pl.Buffered(1))


def _fused_call(parts, nseq, name):
    in_specs, args, out_shape, out_specs, scratch, aliases, layout = [], [], [], [], [], {}, []
    for fn, p_in, p_args, p_shape, p_out, p_scratch, p_alias in parts:
        layout.append((fn, len(p_args), len(p_shape), len(p_scratch)))
        for k, v in p_alias.items():
            aliases[len(args) + k] = len(out_shape) + v
        in_specs += p_in
        args += p_args
        out_shape += p_shape
        out_specs += p_out
        scratch += p_scratch
    n_in, n_out = len(args), len(out_shape)

    def kernel(*refs):
        i, o, s = 0, n_in, n_in + n_out
        for fn, ni, no, ns in layout:
            fn(*refs[i:i + ni], *refs[o:o + no], *refs[s:s + ns])
            i, o, s = i + ni, o + no, s + ns

    return pl.pallas_call(
        kernel, out_shape=out_shape, grid=(nseq,), in_specs=in_specs, out_specs=out_specs, scratch_shapes=scratch,
        input_output_aliases=aliases, name=name,
        compiler_params=pltpu.CompilerParams(dimension_semantics=("arbitrary",), vmem_limit_bytes=60 * 1024 * 1024),
    )(*args)


def _gla_group_scan(t, p_ref, la_f, la_b, of_s, ob_s, st_s):
    c, gr = GLA_C, 256
    nc, n, hd = gr // c, t // gr, GLA_H * GLA_DK
    ri = lax.broadcasted_iota(jnp.int32, (gr, gr), 0)
    ci = lax.broadcasted_iota(jnp.int32, (gr, gr), 1)
    same = (ri // c) == (ci // c)
    rs = lax.broadcasted_iota(jnp.int32, (GLA_H * gr, gr), 0) % gr
    cs = lax.broadcasted_iota(jnp.int32, (GLA_H * gr, gr), 1)
    same_s = (rs // c) == (cs // c)
    own_chunk = (lax.broadcasted_iota(jnp.int32, (gr, nc * hd), 0) // c
                 == lax.broadcasted_iota(jnp.int32, (gr, nc * hd), 1) // hd).astype(F32)
    own_dk = jnp.concatenate([_own_lanes(GLA_H, GLA_DV, GLA_DK)] * nc, axis=1)
    scale = GLA_DK ** -0.5

    def group(row0, la_ref, d):
        fwd = d == 0
        ordered = (ci <= ri) if fwd else (ci >= ri)
        first_half = (ci % c < c // 2) if fwd else (ci % c >= c // 2)
        keep = same_s & ((cs <= rs) if fwd else (cs >= rs))
        sums_of = jnp.concatenate([(same & ordered).astype(BF16), same.astype(BF16), (same & first_half).astype(BF16)],
                                  axis=0)
        q = p_ref[pl.ds(row0, gr), 0:128]
        k = p_ref[pl.ds(row0, gr), 128:256] * scale
        v = p_ref[pl.ds(row0, gr), 256:512]
        la = la_ref[pl.ds(row0, gr), :]
        la_hi = la.astype(BF16)
        la_lo = (la - la_hi.astype(F32)).astype(BF16)
        sums = jnp.dot(sums_of, la_hi, preferred_element_type=F32) + jnp.dot(sums_of, la_lo, preferred_element_type=F32)
        bc, tot, mid = sums[0:gr], sums[gr:2 * gr], sums[2 * gr:3 * gr]
        qe, ke = q * jnp.exp(bc - mid), k * jnp.exp(mid - bc)
        q_in, k_out, a = q * jnp.exp(bc), k * jnp.exp(tot - bc), jnp.exp(tot)
        att = jnp.where(keep, _mm_nt(_stack_heads(qe, GLA_H, GLA_DK), ke), 0.0)
        o = _unstack_heads(_mm(att, v), GLA_H, GLA_DV)
        kv = _mm_tn(v, jnp.concatenate([k_out] * nc, axis=1) * own_chunk) * own_dk
        st, entering = st_s[d], [None] * nc
        for cc in (range(nc) if fwd else range(nc - 1, -1, -1)):
            entering[cc] = st
            st = st * a[cc * c:cc * c + 1, :] + kv[:, cc * hd:(cc + 1) * hd]
        st_s[d] = st
        return o + _mm_nt(jnp.concatenate([q_in] * nc, axis=1) * own_chunk, jnp.concatenate(entering, axis=1))

    def body(i, carry):
        rf = pl.multiple_of(i * gr, gr)
        rb = pl.multiple_of((n - 1 - i) * gr, gr)
        of_s[pl.ds(rf, gr), :] = group(rf, la_f, 0)
        ob_s[pl.ds(rb, gr), :] = group(rb, la_b, 1)
        return carry

    lax.fori_loop(0, n, body, 0)


def _gla_kernel(t, latent, *refs):
    if latent:
        p_ref, gw_ref, gb_ref, s0_ref, o_ref, la_f, la_b, of_s, ob_s, st_s = refs
    else:
        p_ref, gw_ref, gb_ref = refs[0:3]
        o_ref, st_ref, la_f, la_b, of_s, ob_s, st_s = refs[-7:]
    c = GLA_C
    n = t // c
    hd = GLA_H * GLA_DK
    la_f[...] = _log_sigmoid(_mm_f32(p_ref[:, 768:784], gw_ref[0]) + gb_ref[0]) / GLA_NORM
    la_b[...] = _log_sigmoid(_mm_f32(p_ref[:, 784:800], gw_ref[1]) + gb_ref[1]) / GLA_NORM

    ri = lax.broadcasted_iota(jnp.int32, (GLA_H * c, c), 0) % c
    ci = lax.broadcasted_iota(jnp.int32, (GLA_H * c, c), 1)
    tri_r = lax.broadcasted_iota(jnp.int32, (c, c), 0)
    tri_c = lax.broadcasted_iota(jnp.int32, (c, c), 1)
    head_rows = lax.broadcasted_iota(jnp.int32, (GLA_H * c, hd), 0) // c
    own_dk = (head_rows == lax.broadcasted_iota(jnp.int32, (GLA_H * c, hd), 1) // GLA_DK).astype(F32)
    own_dv = (lax.broadcasted_iota(jnp.int32, (GLA_H * c, GLA_H * GLA_DV), 0) // c
              == lax.broadcasted_iota(jnp.int32, (GLA_H * c, GLA_H * GLA_DV), 1) // GLA_DV).astype(F32)
    scale = GLA_DK ** -0.5
    if latent:
        st_s[...] = s0_ref[...]
    else:
        st_s[...] = jnp.zeros_like(st_s)

    def chunk(row0, la_ref, d):
        fwd = d == 0
        keep = (ci <= ri) if fwd else (ci >= ri)
        tri = ((tri_c <= tri_r) if fwd else (tri_c >= tri_r)).astype(BF16)
        q = p_ref[pl.ds(row0, c), 0:128]
        k = p_ref[pl.ds(row0, c), 128:256] * scale
        v = p_ref[pl.ds(row0, c), 256:512]
        la = la_ref[pl.ds(row0, c), :]
        la_hi = la.astype(BF16)
        la_lo = (la - la_hi.astype(F32)).astype(BF16)
        bc = (jnp.dot(tri, la_hi, preferred_element_type=F32) + jnp.dot(tri, la_lo, preferred_element_type=F32))
        tot = bc[c - 1:c, :] if fwd else bc[0:1, :]
        mid = bc[c // 2 - 1:c // 2, :] if fwd else bc[c // 2:c // 2 + 1, :]
        qe, ke = q * jnp.exp(bc - mid), k * jnp.exp(mid - bc)
        q_in, k_out, a = q * jnp.exp(bc), k * jnp.exp(tot - bc), jnp.exp(tot)
        q_rows = jnp.concatenate([qe] * GLA_H, axis=0) * own_dk
        att = jnp.where(keep, _mm_nt(q_rows, ke), 0.0)
        o_all = _mm(att, v) * own_dv
        o = o_all[0:c] + o_all[c:2 * c] + o_all[2 * c:3 * c] + o_all[3 * c:4 * c]
        st = st_s[d]
        o = o + _mm_nt(q_in, st)
        st_s[d] = st * a + _mm_tn(v, k_out) * own_dk
        return o

    unroll = 4

    def body(i, carry):
        for u in range(unroll):
            rf = pl.multiple_of((i * unroll + u) * c, c)
            rb = pl.multiple_of((n - 1 - i * unroll - u) * c, c)
            of_s[pl.ds(rf, c), :] = chunk(rf, la_f, 0)
            ob_s[pl.ds(rb, c), :] = chunk(rb, la_b, 1)
        return carry

    if latent:
        lax.fori_loop(0, n // unroll, body, 0)
    else:
        _gla_group_scan(t, p_ref, la_f, la_b, of_s, ob_s, st_s)
    if not latent:
        for d in range(2):
            for h in range(GLA_H):
                st_ref[d, h] = st_s[d, h * GLA_DV:(h + 1) * GLA_DV, h * GLA_DK:(h + 1) * GLA_DK]
    for r in range(t // QB):
        rows = slice(r * QB, (r + 1) * QB)
        o_ref[rows, :] = _head_rms_gate(of_s[rows, :] + ob_s[rows, :], p_ref[rows, 512:768]).astype(BF16)


def _gla(p, gate_w, gate_b, s0_bd, st_prev, l, latent):
    t, nseq, row0 = (T_LAT, N_LAT, M_CTX // T_LAT) if latent else (T_CTX, N_CTX, 0)
    st_shape = (2, GLA_H * GLA_DV, GLA_H * GLA_DK)
    in_specs = [_seq_spec(t, W_GLA, row0, latent),
                _layer_spec(l, 2, GLA_RANK, GLA_H * GLA_DK), _layer_spec(l, 2, 1, GLA_H * GLA_DK)]
    args = [p, gate_w, gate_b]
    aliases = {}
    o_shape = jax.ShapeDtypeStruct((nseq * t, 256), BF16)
    o_spec = pl.BlockSpec((t, 256), lambda b: (b, 0))
    if latent:
        in_specs.append(pl.BlockSpec((None, None) + st_shape, lambda b: (b, l, 0, 0, 0)))
        args.append(s0_bd)
    out_shape, out_specs = [o_shape], [o_spec]
    if not latent:
        _stacked_output(st_prev, (2, GLA_H, GLA_DV, GLA_DK), l, nseq, in_specs, args, out_shape, out_specs, aliases)
    scratch = [pltpu.VMEM((t, 128), F32), pltpu.VMEM((t, 128), F32),
               pltpu.VMEM((t, 256), F32), pltpu.VMEM((t, 256), F32), pltpu.VMEM(st_shape, F32)]
    return functools.partial(_gla_kernel, t, latent), in_specs, args, out_shape, out_specs, scratch, aliases


def _ret_kernel(t, latent, *refs):
    if latent:
        p_ref, dec_ref, cos_ref, slo_ref, shi_ref, s0_ref, o_ref, q_s, k_s, decay_s = refs
    else:
        p_ref, dec_ref = refs[0:2]
        o_ref, st_ref, decay_s = refs[-3:]
    lg = _log_sigmoid(dec_ref[...])
    scale = RET_DK ** -0.5
    nblk = t // QB

    @pl.when(pl.program_id(0) == 0)
    def _():
        wide = decay_s.shape[-1]
        dist = (lax.broadcasted_iota(jnp.int32, (QB, wide), 0) - lax.broadcasted_iota(jnp.int32, (QB, wide), 1)
                + (nblk - 1) * QB).astype(F32)
        for h in range(RET_H):
            lg_f, lg_b = lg[0:1, h * 64:h * 64 + 1], lg[1:2, h * 64:h * 64 + 1]
            decay_s[h] = (jnp.where(dist >= 0, jnp.exp(jnp.maximum(dist, 0.0) * lg_f), 0.0)
                          + jnp.where(dist <= 0, jnp.exp(jnp.maximum(-dist, 0.0) * lg_b), 0.0))

    if not latent:
        k = p_ref[:, 256:512] * scale
        v = p_ref[:, 512:768]
        decay = jnp.concatenate([decay_s[h] for h in range(RET_H)], axis=0)
        o = _unstack_heads(_mm(_mm_nt(_stack_heads(p_ref[:, 0:256], RET_H, RET_DK), k) * decay, v), RET_H, RET_DV)
        o_ref[...] = _head_rms_gate(o, p_ref[:, 768:1024]).astype(BF16)
        j = lax.broadcasted_iota(jnp.int32, (t, 1), 0).astype(F32)
        kv_f = _mm_tn(k * jnp.exp((float(t - 1) - j) * lg[0:1, :]), v)
        kv_b = _mm_tn(k * jnp.exp(j * lg[1:2, :]), v)
        for h in range(RET_H):
            st_ref[0, h] = kv_f[h * RET_DK:(h + 1) * RET_DK, h * RET_DV:(h + 1) * RET_DV]
            st_ref[1, h] = kv_b[h * RET_DK:(h + 1) * RET_DK, h * RET_DV:(h + 1) * RET_DV]
        return
    q_s[...] = _rope(p_ref[:, 0:256], cos_ref[...], slo_ref[...], shi_ref[...], RET_DK // 2)
    k_s[...] = _rope(p_ref[:, 256:512], cos_ref[...], slo_ref[...], shi_ref[...], RET_DK // 2) * scale
    for r in range(t // QL):
        rows = slice(r * QL, (r + 1) * QL)
        pos = (lax.broadcasted_iota(jnp.int32, (QL, 1), 0) + r * QL).astype(F32)
        strip_rows = slice(r * QL % QB, r * QL % QB + QL)
        off = (nblk - 1 - r * QL // QB) * QB
        decay = jnp.concatenate([decay_s[h, strip_rows, off:off + t] for h in range(RET_H)], axis=0)
        q = q_s[rows, :]
        o = _unstack_heads(_mm(_mm_nt(_stack_heads(q, RET_H, RET_DK), k_s[...]) * decay, p_ref[:, 512:768]),
                           RET_H, RET_DV)
        o = o + _mm(q * jnp.exp((pos + 1.0) * lg[0:1, :]), s0_ref[0])
        o = o + _mm(q * jnp.exp((float(t) - pos) * lg[1:2, :]), s0_ref[1])
        o_ref[rows, :] = _head_rms_gate(o, p_ref[rows, 768:1024]).astype(BF16)


def _ret(p, dec_lanes, tables, s0, st_prev, l, latent):
    t, nseq, row0 = (T_LAT, N_LAT, M_CTX // T_LAT) if latent else (T_CTX, N_CTX, 0)
    st_shape = (2, RET_H, RET_DK, RET_DV)
    in_specs = [_seq_spec(t, W_RET, row0, latent), _layer_spec(l, 2, 256)]
    args = [p, dec_lanes]
    aliases = {}
    o_shape = jax.ShapeDtypeStruct((nseq * t, 256), BF16)
    o_spec = pl.BlockSpec((t, 256), lambda b: (b, 0))
    scratch = []
    if latent:
        in_specs += [_table_spec(t, 256)] * 3
        in_specs.append(pl.BlockSpec((None, None, 2, RET_H * RET_DK, RET_H * RET_DV), lambda b: (b, l, 0, 0, 0)))
        args += list(tables) + [s0]
        scratch = [pltpu.VMEM((t, 256), F32), pltpu.VMEM((t, 256), F32)]
    scratch.append(pltpu.VMEM((RET_H, QB, 2 * t - QB), F32))
    out_shape, out_specs = [o_shape], [o_spec]
    if not latent:
        _stacked_output(st_prev, st_shape, l, nseq, in_specs, args, out_shape, out_specs, aliases)
    return functools.partial(_ret_kernel, t, latent), in_specs, args, out_shape, out_specs, scratch, aliases


def _mla_kernel(t, latent, *refs):
    if latent:
        (p_ref, gq_ref, gkv_ref, wq_ref, wkv_ref, cq_ref, slq_ref, shq_ref, ck_ref, slk_ref, shk_ref,
         cckv_ref, ckr_ref, o_ref, qn_s, qr_s, kn_s, kr_s, v_s) = refs
    else:
        p_ref, gq_ref, gkv_ref, wq_ref, wkv_ref = refs[0:5]
        o_ref, ckv_ref, kro_ref, qn_s, qr_s, kn_s, kr_s, v_s = refs[-8:]
    nk = t + (PAST if latent else 0)
    qh = _mm(_rms(p_ref[:, 0:256]) * gq_ref[...], wq_ref[...])
    ckv = _rms(p_ref[:, 256:384]) * gkv_ref[...]
    kv = _mm(ckv, wkv_ref[...])
    qn_s[...] = qh[:, 0:256]
    kn_s[0:t, :] = kv[:, 0:256]
    v_s[0:t, :] = kv[:, 256:512]
    if latent:
        qr_s[...] = _rope(qh[:, 256:384], cq_ref[...], slq_ref[...], shq_ref[...], MLA_ROPE // 2)
        kr_s[0:t, :] = _rope(p_ref[:, 384:512], ck_ref[...], slk_ref[...], shk_ref[...], MLA_ROPE // 2)
        kvc = _mm(cckv_ref[...], wkv_ref[...])
        kn_s[t:nk, :] = kvc[:, 0:256]
        v_s[t:nk, :] = kvc[:, 256:512]
        kr_s[t:nk, :] = ckr_ref[...]
    scale = (MLA_NOPE + MLA_ROPE) ** -0.5
    if not latent:
        ckv_ref[...] = ckv
        kro_ref[...] = p_ref[:, 384:384 + MLA_ROPE]
        kr = p_ref[:, 384:512]
        kr4 = kr + pltpu.roll(kr, MLA_ROPE, 1) + pltpu.roll(kr, 2 * MLA_ROPE, 1) + pltpu.roll(kr, 3 * MLA_ROPE, 1)
        s = (_mm_nt(_stack_heads(qh[:, 0:256], MLA_H, MLA_NOPE), kv[:, 0:256])
             + _mm_nt(_stack_heads(qh[:, 256:384], MLA_H, MLA_ROPE), kr4)) * scale
        e = jnp.exp(s - jnp.max(s, axis=-1, keepdims=True))
        o = _mm(e, kv[:, 256:512]) / jnp.sum(e, axis=-1, keepdims=True)
        o_ref[...] = _unstack_heads(o, MLA_H, MLA_DV).astype(BF16)
        return
    kr = kr_s[...]
    kr4 = kr + pltpu.roll(kr, MLA_ROPE, 1) + pltpu.roll(kr, 2 * MLA_ROPE, 1) + pltpu.roll(kr, 3 * MLA_ROPE, 1)
    for r in range(t // QL):
        rows = slice(r * QL, (r + 1) * QL)
        s = (_mm_nt(_stack_heads(qn_s[rows, :], MLA_H, MLA_NOPE), kn_s[...])
             + _mm_nt(_stack_heads(qr_s[rows, :], MLA_H, MLA_ROPE), kr4)) * scale
        e = jnp.exp(s - jnp.max(s, axis=-1, keepdims=True))
        o = _mm(e, v_s[...]) / jnp.sum(e, axis=-1, keepdims=True)
        o_ref[rows, :] = _unstack_heads(o, MLA_H, MLA_DV).astype(BF16)


def _mla(p, gq, gkv, wq, wkv, tables_q, tables_k, cache_ckv, cache_kr, prev, l, latent):
    t, nseq, row0 = (T_LAT, N_LAT, M_CTX // T_LAT) if latent else (T_CTX, N_CTX, 0)
    nk = t + (PAST if latent else 0)
    in_specs = [_seq_spec(t, W_MLA, row0, latent), _layer_spec(l, 1, MLA_QR),
                _layer_spec(l, 1, MLA_KVR), _layer_spec(l, MLA_QR, 384), _layer_spec(l, MLA_KVR, 512)]
    args = [p, gq, gkv, wq, wkv]
    aliases = {}
    o_shape = jax.ShapeDtypeStruct((nseq * t, 256), BF16)
    o_spec = pl.BlockSpec((t, 256), lambda b: (b, 0))
    if latent:
        in_specs += [_table_spec(t, 128)] * 6
        in_specs += [pl.BlockSpec((None, None, PAST, MLA_KVR), lambda b: (b, l, 0, 0)),
                     pl.BlockSpec((None, None, PAST, 128), lambda b: (b, l, 0, 0))]
        args += list(tables_q) + list(tables_k) + [cache_ckv, cache_kr]
    out_shape, out_specs = [o_shape], [o_spec]
    if not latent:
        prev = prev or (None, None)
        _stacked_output(prev[0], (t, MLA_KVR), l, nseq, in_specs, args, out_shape, out_specs, aliases)
        _stacked_output(prev[1], (t, MLA_ROPE), l, nseq, in_specs, args, out_shape, out_specs, aliases)
    scratch = [pltpu.VMEM((t, 256), F32), pltpu.VMEM((t, 128), F32), pltpu.VMEM((nk, 256), F32),
               pltpu.VMEM((nk, 128), F32), pltpu.VMEM((nk, 256), F32)]
    return functools.partial(_mla_kernel, t, latent), in_specs, args, out_shape, out_specs, scratch, aliases


def _gqa_kernel(t, latent, *refs):
    if latent:
        p_ref, sink_ref, cos_ref, slo_ref, shi_ref, ck_ref, cv_ref, o_ref, q_s, k_s = refs
    else:
        p_ref, sink_ref = refs[0:2]
        o_ref, ko_ref, vo_ref = refs[-3:]
        for kvh in range(GQA_KV):
            ko_ref[kvh] = p_ref[:, 256 + kvh * GQA_HD:256 + (kvh + 1) * GQA_HD]
            vo_ref[kvh] = p_ref[:, 384 + kvh * GQA_HD:384 + (kvh + 1) * GQA_HD]
    scale = GQA_HD ** -0.5
    if not latent:
        own = _own_lanes(GQA_H, t, GQA_HD)
        sink = jnp.max(jnp.where(own > 0, sink_ref[...], NEG), axis=-1, keepdims=True)
        s = _mm_nt(_stack_heads(p_ref[:, 0:256], GQA_H, GQA_HD), _per_query_head(p_ref[:, 256:384])) * scale
        m = jnp.maximum(jnp.max(s, axis=-1, keepdims=True), sink)
        e = jnp.exp(s - m)
        o = _mm(e, _per_query_head(p_ref[:, 384:512])) / (jnp.sum(e, axis=-1, keepdims=True) + jnp.exp(sink - m))
        o_ref[...] = _unstack_heads(o, GQA_H, GQA_HD).astype(BF16)
        return
    q_s[...] = _rope(p_ref[:, 0:256], cos_ref[...], slo_ref[...], shi_ref[...], GQA_HD // 2)
    k_s[...] = _rope(p_ref[:, 256:384], cos_ref[:, 0:128], slo_ref[:, 0:128], shi_ref[:, 0:128], GQA_HD // 2)
    kc = _per_query_head(jnp.concatenate([ck_ref[0], ck_ref[1]], axis=1))
    vc = _per_query_head(jnp.concatenate([cv_ref[0], cv_ref[1]], axis=1))
    sink = jnp.max(jnp.where(_own_lanes(GQA_H, QL, GQA_HD) > 0, sink_ref[...], NEG), axis=-1, keepdims=True)
    for r in range(t // QL):
        rows = slice(r * QL, (r + 1) * QL)
        keys = slice(max(0, r * QL - WINDOW), min(t, (r + 1) * QL + WINDOW))
        nkeys = keys.stop - keys.start
        row = lax.broadcasted_iota(jnp.int32, (GQA_H * QL, nkeys), 0) % QL + r * QL
        col = lax.broadcasted_iota(jnp.int32, (GQA_H * QL, nkeys), 1) + keys.start
        near = jnp.abs(row - col) <= WINDOW
        q = _stack_heads(q_s[rows, :], GQA_H, GQA_HD)
        s_loc = jnp.where(near, _mm_nt(q, _per_query_head(k_s[keys, :])) * scale, NEG)
        s_ctx = _mm_nt(q, kc) * scale
        m = jnp.maximum(jnp.maximum(jnp.max(s_loc, axis=-1, keepdims=True), jnp.max(s_ctx, axis=-1, keepdims=True)),
                        sink)
        e_loc, e_ctx = jnp.exp(s_loc - m), jnp.exp(s_ctx - m)
        den = jnp.sum(e_loc, axis=-1, keepdims=True) + jnp.sum(e_ctx, axis=-1, keepdims=True) + jnp.exp(sink - m)
        o = (_mm(e_loc, _per_query_head(p_ref[keys, 384:512])) + _mm(e_ctx, vc)) / den
        o_ref[rows, :] = _unstack_heads(o, GQA_H, GQA_HD).astype(BF16)


def _gqa(p, sink_lanes, tables, cache_k, cache_v, prev, l, latent):
    t, nseq, row0 = (T_LAT, N_LAT, M_CTX // T_LAT) if latent else (T_CTX, N_CTX, 0)
    in_specs = [_seq_spec(t, W_GQA, row0, latent), _layer_spec(l, 1, 256)]
    args = [p, sink_lanes]
    scratch = []
    aliases = {}
    out_shape = [jax.ShapeDtypeStruct((nseq * t, 256), BF16)]
    out_specs = [pl.BlockSpec((t, 256), lambda b: (b, 0))]
    if latent:
        in_specs += [_table_spec(t, 256)] * 3
        in_specs += [pl.BlockSpec((None, None, GQA_KV, PAST, GQA_HD), lambda b: (b, l, 0, 0, 0))] * 2
        args += list(tables) + [cache_k, cache_v]
        scratch = [pltpu.VMEM((t, 256), F32), pltpu.VMEM((t, 128), F32)]
    else:
        prev = prev or (None, None)
        for pv in prev:
            _stacked_output(pv, (GQA_KV, t, GQA_HD), l, nseq, in_specs, args, out_shape, out_specs, aliases)
    return functools.partial(_gqa_kernel, t, latent), in_specs, args, out_shape, out_specs, scratch, aliases


def _outproj_kernel(route, split, tm, *refs):
    is_lat = pl.program_id(0) >= M_CTX // tm
    if split:
        x_in = jnp.where(is_lat, refs[1][...], refs[0][...])
        refs = refs[2:]
    else:
        x_in = refs[0][...]
        refs = refs[1:]
    ctx_refs, lat_refs = refs[0:4], refs[4:8]
    w_ref, gt_ref, g_ref, sh_ref, sc_ref = refs[8:13]
    refs = refs[13:]
    mixed = jnp.concatenate([jnp.where(is_lat, lat_refs[m][...], ctx_refs[m][...]) for m in range(4)], axis=1)
    mix = jnp.dot(mixed, w_ref[...], preferred_element_type=F32)
    x = x_in + gt_ref[...] * mix
    h = (_rms(x) * g_ref[...] * (1.0 + sc_ref[...]) + sh_ref[...]).astype(BF16)
    if not route:
        xo_ref, h_ref = refs
    else:
        r_ref, xo_ref, h_ref, gate_ref, sel_ref = refs
        lane = lax.broadcasted_iota(jnp.int32, gate_ref.shape, 1)
        logits = jnp.where(lane < N_EXP, jnp.dot(h, r_ref[...], preferred_element_type=F32), NEG)
        m1 = jnp.max(logits, axis=-1, keepdims=True)
        i1 = jnp.min(jnp.where(logits == m1, lane, LANE), axis=-1, keepdims=True)
        rest = jnp.where(lane == i1, NEG, logits)
        m2 = jnp.max(rest, axis=-1, keepdims=True)
        i2 = jnp.min(jnp.where(rest == m2, lane, LANE), axis=-1, keepdims=True)
        e2 = jnp.exp(m2 - m1)
        gate_ref[...] = jnp.where(lane == i1, 1.0 / (1.0 + e2), 0.0) + jnp.where(lane == i2, e2 / (1.0 + e2), 0.0)
        sel_ref[...] = jnp.where((lane == i1) | (lane == i2), 1, 0)
    xo_ref[...] = x
    h_ref[...] = h


def _outproj(xs, o_ctx, o_lat, w_out, g2, mod, router, l):
    tm = 512
    route = router is not None
    split = len(xs) == 2
    rows = lambda w: pl.BlockSpec((tm, w), lambda i: (i, 0))
    ctx_spec, lat_spec = _row_split_specs(tm, 256)
    x_specs = _row_split_specs(tm, D) if split else [rows(D)]
    in_specs = x_specs + [ctx_spec] * 4 + [lat_spec] * 4 + [
        _layer_spec(l, D, D), _mod_spec(l, 2, tm), _layer_spec(l, 1, D), _mod_spec(l, 3, tm), _mod_spec(l, 4, tm)]
    args = [*xs, *o_ctx, *o_lat, w_out, mod, g2, mod, mod]
    out_shape = [jax.ShapeDtypeStruct((M_ALL, D), F32), jax.ShapeDtypeStruct((M_ALL, D), BF16)]
    out_specs = [rows(D), rows(D)]
    if route:
        in_specs.append(_layer_spec(l // 2, D, LANE))
        args.append(router)
        out_shape += [jax.ShapeDtypeStruct((M_ALL, LANE), F32), jax.ShapeDtypeStruct((M_ALL, LANE), jnp.int32)]
        out_specs += [rows(LANE), rows(LANE)]
    return pl.pallas_call(
        functools.partial(_outproj_kernel, route, split, tm),
        out_shape=out_shape, grid=(M_ALL // tm,), in_specs=in_specs, out_specs=out_specs,
        compiler_params=_cparams("parallel"),
        name="outproj_route" if route else "outproj",
    )(*args)


def _ffn_kernel(h_ref, x_ref, gt_ref, wg_ref, wu_ref, wd_ref, o_ref, acc_ref):
    f = pl.program_id(1)

    @pl.when(f == 0)
    def _():
        acc_ref[...] = jnp.zeros_like(acc_ref)

    wg, wu, wd = wg_ref[...].astype(BF16), wu_ref[...].astype(BF16), wd_ref[...].astype(BF16)
    half = h_ref.shape[0] // 2
    for rows in (slice(0, half), slice(half, 2 * half)):
        h = h_ref[rows, :]
        g = jnp.dot(h, wg, preferred_element_type=F32)
        u = jnp.dot(h, wu, preferred_element_type=F32)
        acc_ref[rows, :] += jnp.dot((_silu(g) * u).astype(BF16), wd, preferred_element_type=F32)

    @pl.when(f == pl.num_programs(1) - 1)
    def _():
        o_ref[...] = x_ref[...] + gt_ref[...] * acc_ref[...]


def _ffn(h, x, mod, wg, wu, wd, l):
    tm, tf = 1024, 256
    j = l // 2
    return pl.pallas_call(
        _ffn_kernel,
        out_shape=jax.ShapeDtypeStruct((M_ALL, D), F32),
        grid=(M_ALL // tm, D_FF // tf),
        in_specs=[pl.BlockSpec((tm, D), lambda i, f: (i, 0)), pl.BlockSpec((tm, D), lambda i, f: (i, 0)),
                  _mod_spec(l, 5, tm),
                  pl.BlockSpec((None, D, tf), lambda i, f: (j, 0, f)),
                  pl.BlockSpec((None, D, tf), lambda i, f: (j, 0, f)),
                  pl.BlockSpec((None, tf, D), lambda i, f: (j, f, 0))],
        out_specs=pl.BlockSpec((tm, D), lambda i, f: (i, 0)),
        scratch_shapes=[pltpu.VMEM((tm, D), F32)],
        compiler_params=_cparams("parallel", "arbitrary"),
        name="ffn_dense",
    )(h, x, mod, wg, wu, wd)


SUP, SUB, CHUNK = 2048, 256, 256
N_SUB = SUP // SUB
S_MAX = 2 * M_ALL // SUP + N_EXP
P_SLOT = S_MAX * SUP
TC = 128
WIN = TC + 16


def _moe_expert_kernel(se_ref, nt_ref, sblk_ref, clo_ref, chi_ref, pos_ref, h_ref, wg_ref, wu_ref, wd_ref, o_ref,
                       xs_s, acc_s, wg_s, wu_s, wd_s):
    s = pl.program_id(0)
    f = pl.program_id(1)
    n = nt_ref[s]

    @pl.when((f == 0) & (n == 0))
    def _():
        o_ref[...] = jnp.zeros_like(o_ref)

    @pl.when((f == 0) & (n > 0))
    def _():
        row = lax.broadcasted_iota(jnp.int32, (SUB, CHUNK), 0)

        def gather(j, carry):
            row0 = (s * N_SUB + j) * SUB
            acc_s[j] = jnp.zeros((SUB, D), F32)

            def chunk(c, carry):
                tpos = pos_ref[:, pl.ds(pl.multiple_of(c * CHUNK, CHUNK), CHUNK)]
                onehot = jnp.where(tpos - row0 == row, 1.0, 0.0).astype(BF16)
                rows = h_ref[pl.ds(pl.multiple_of(c * CHUNK, CHUNK), CHUNK), :]
                acc_s[j] += jnp.dot(onehot, rows, preferred_element_type=F32)
                return carry

            g = s * N_SUB + j
            lax.fori_loop(clo_ref[g], chi_ref[g] + 1, chunk, 0)
            xs_s[j] = acc_s[j].astype(BF16)
            acc_s[j] = jnp.zeros((SUB, D), F32)
            return carry

        lax.fori_loop(0, n, gather, 0)

    @pl.when(n > 0)
    def _():
        wg_s[...] = wg_ref[...].astype(BF16)
        wu_s[...] = wu_ref[...].astype(BF16)
        wd_s[...] = wd_ref[...].astype(BF16)

        def sub(j):
            x = xs_s[j]
            g = jnp.dot(x, wg_s[...], preferred_element_type=F32)
            u = jnp.dot(x, wu_s[...], preferred_element_type=F32)
            acc_s[j] += jnp.dot((_silu(g) * u).astype(BF16), wd_s[...], preferred_element_type=F32)

        def pair(jj, carry):
            sub(2 * jj)
            sub(2 * jj + 1)
            return carry

        lax.fori_loop(0, n // 2, pair, 0)

        @pl.when(n % 2 == 1)
        def _():
            sub(n - 1)

    @pl.when((f == pl.num_programs(1) - 1) & (n > 0))
    def _():
        for j in range(N_SUB):
            rows = slice(j * SUB, (j + 1) * SUB)

            @pl.when(j < n)
            def _():
                o_ref[rows, :] = acc_s[j].astype(BF16)

            @pl.when(j >= n)
            def _():
                o_ref[rows, :] = jnp.zeros((SUB, D), BF16)


def _moe_experts(h, pos_t, meta, wg, wu, wd, layer):
    tf = 512
    nf = D_FFE // tf
    se, nt, sblk, clo, chi = meta

    def w_up(s, f, se, nt, sblk, clo, chi):
        return (layer, se[s], 0, jnp.where(nt[s] > 0, f, nf - 1))

    def w_down(s, f, se, nt, sblk, clo, chi):
        return (layer, se[s], jnp.where(nt[s] > 0, f, nf - 1), 0)

    grid_spec = pltpu.PrefetchScalarGridSpec(
        num_scalar_prefetch=5,
        grid=(S_MAX, nf),
        in_specs=[pl.BlockSpec((None, 1, M_ALL), lambda s, f, se, nt, sblk, clo, chi: (se[s], 0, 0)),
                  pl.BlockSpec((M_ALL, D), lambda s, f, *_: (0, 0), pipeline_mode=pl.Buffered(1)),
                  pl.BlockSpec((None, None, D, tf), w_up), pl.BlockSpec((None, None, D, tf), w_up),
                  pl.BlockSpec((None, None, tf, D), w_down)],
        out_specs=pl.BlockSpec((SUP, D), lambda s, f, *_: (s, 0)),
        scratch_shapes=[pltpu.VMEM((N_SUB, SUB, D), BF16), pltpu.VMEM((N_SUB, SUB, D), F32),
                        pltpu.VMEM((D, tf), BF16), pltpu.VMEM((D, tf), BF16), pltpu.VMEM((tf, D), BF16)],
    )
    return pl.pallas_call(
        _moe_expert_kernel,
        out_shape=jax.ShapeDtypeStruct((P_SLOT, D), BF16),
        grid_spec=grid_spec,
        compiler_params=pltpu.CompilerParams(dimension_semantics=("arbitrary", "arbitrary"),
                                             vmem_limit_bytes=60 * 1024 * 1024),
        name="moe_experts",
    )(se, nt, sblk, clo, chi, pos_t, h, wg, wu, wd)


def _moe_combine_kernel(off_ref, x_ref, gt_ref, gate_ref, pos_ref, *refs):
    win_refs, o_ref = refs[:N_EXP], refs[N_EXP]
    i = pl.program_id(0)
    lane = lax.broadcasted_iota(jnp.int32, (TC, WIN), 1)
    y = jnp.zeros((TC, D), F32)
    for e in range(N_EXP):
        rel = pos_ref[:, e:e + 1] - off_ref[i * N_EXP + e] * 16
        onehot = jnp.where(rel == lane, 1.0, 0.0).astype(BF16)
        y = y + gate_ref[:, e:e + 1] * jnp.dot(onehot, win_refs[e][...], preferred_element_type=F32)
    o_ref[...] = x_ref[...] + gt_ref[...] * y


def _moe_combine(x, mod, gates, pos, off, slots, l):
    def win_spec(e):
        return pl.BlockSpec((pl.Element(WIN), pl.Element(D)), lambda i, off: (off[i * N_EXP + e] * 16, 0))

    grid_spec = pltpu.PrefetchScalarGridSpec(
        num_scalar_prefetch=1,
        grid=(M_ALL // TC,),
        in_specs=[pl.BlockSpec((TC, D), lambda i, off: (i, 0)),
                  _mod_spec(l, 5, TC),
                  pl.BlockSpec((TC, LANE), lambda i, off: (i, 0)), pl.BlockSpec((TC, LANE), lambda i, off: (i, 0))]
        + [win_spec(e) for e in range(N_EXP)],
        out_specs=pl.BlockSpec((TC, D), lambda i, off: (i, 0)),
    )
    return pl.pallas_call(
        _moe_combine_kernel,
        out_shape=jax.ShapeDtypeStruct((M_ALL, D), F32),
        grid_spec=grid_spec,
        compiler_params=_cparams("arbitrary"),
        name="moe_combine",
    )(off, x, mod, gates, pos, *([slots] * N_EXP))


def _moe_plan(sel):
    i32 = jnp.int32
    sel_t = sel[:, :N_EXP].T
    csum = jnp.cumsum(sel_t, axis=1)
    rank = csum - sel_t
    n_e = csum[:, -1]
    ns_e = (n_e + SUP - 1) // SUP
    end_e = jnp.cumsum(ns_e)
    start_e = end_e - ns_e
    n_used = end_e[-1]
    pos_t = jnp.where(sel_t > 0, start_e[:, None] * SUP + rank, -1)
    s_ids = jnp.arange(S_MAX, dtype=i32)
    sblk = jnp.minimum(s_ids, n_used - 1)
    se = jnp.sum((end_e[None, :] <= sblk[:, None]).astype(i32), axis=1)
    nv = jnp.clip(n_e[se] - (sblk - start_e[se]) * SUP, 0, SUP)
    nt = jnp.where(s_ids < n_used, (nv + SUB - 1) // SUB, 0)
    g_ids = jnp.arange(S_MAX * N_SUB, dtype=i32)
    s_g, e_g = g_ids // N_SUB, se[g_ids // N_SUB]
    r0 = (sblk[s_g] - start_e[e_g]) * SUP + (g_ids % N_SUB) * SUB
    r1 = jnp.minimum(r0 + SUB, n_e[e_g])
    live = (s_g < n_used) & (r0 < n_e[e_g])
    chunk_end = csum[:, CHUNK - 1::CHUNK][e_g]
    clo = jnp.where(live, jnp.sum((chunk_end <= r0[:, None]).astype(i32), axis=1), 0)
    chi = jnp.where(live, jnp.sum((chunk_end < r1[:, None]).astype(i32), axis=1), -1)
    before = jnp.concatenate([jnp.zeros((N_EXP, 1), i32), csum[:, TC - 1::TC][:, :-1]], axis=1)
    base = start_e[:, None] * SUP + before
    off = jnp.minimum(base // 16, (P_SLOT - WIN) // 16).T.reshape(-1)
    pos128 = jnp.pad(pos_t.T, ((0, 0), (0, LANE - N_EXP)), constant_values=-1)
    meta = (se.astype(i32), nt.astype(i32), sblk.astype(i32), clo.astype(i32), chi.astype(i32))
    return meta, pos_t.reshape(N_EXP, 1, M_ALL), pos128, off.astype(i32)


def _moe(h, x, mod, gates, sel, wg, wu, wd, l):
    meta, pos_t, pos, off = _moe_plan(sel)
    slots = _moe_experts(h, pos_t, meta, wg, wu, wd, l // 2)
    return _moe_combine(x, mod, gates, pos, off, slots, l)


def _final_kernel(tm, x_ref, g_ref, oc_ref, ol_ref):
    y = _rms(x_ref[...]) * g_ref[...]
    is_lat = pl.program_id(0) >= M_CTX // tm

    @pl.when(jnp.logical_not(is_lat))
    def _():
        oc_ref[...] = y

    @pl.when(is_lat)
    def _():
        ol_ref[...] = y


def _final_norm(x, g):
    tm = 1024
    return pl.pallas_call(
        functools.partial(_final_kernel, tm),
        out_shape=[jax.ShapeDtypeStruct((M_CTX, D), F32), jax.ShapeDtypeStruct((M_LAT, D), F32)],
        grid=(M_ALL // tm,),
        in_specs=[pl.BlockSpec((tm, D), lambda i: (i, 0)), pl.BlockSpec((1, D), lambda i: (0, 0))],
        out_specs=_row_split_specs(tm, D),
        compiler_params=_cparams("arbitrary"),
        name="final_norm",
    )(x, g)


def kernel(x_prompt, x_sample, state_gla, state_ret, cache_mla_ckv, cache_mla_krope, cache_gqa_k, cache_gqa_v,
           c, c_ctx, norm1_g, norm2_g, final_norm_g, w_mod, b_mod, w_in, w_out, gla_gate_w, gla_gate_b,
           mla_q_norm_g, mla_w_q_up, mla_kv_norm_g, mla_w_kv_up, ret_decay, gqa_sink,
           ffn_w_gate, ffn_w_up, ffn_w_down, moe_router, moe_w_gate, moe_w_up, moe_w_down):
    xs = (x_prompt.reshape(M_CTX, D), x_sample.reshape(M_LAT, D))

    cond = jnp.concatenate([c_ctx[None], c, jnp.zeros((8 - 1 - N_LAT, D), F32)], axis=0)
    mod = _modulation(cond, w_mod, b_mod)
    mod = mod[:, :1 + N_LAT].reshape(DEPTH, (1 + N_LAT) * 6, 1, D)

    w_out_b = w_out.astype(BF16)
    wq = mla_w_q_up.reshape(DEPTH, MLA_QR, MLA_H, MLA_NOPE + MLA_ROPE)
    wq = jnp.concatenate([wq[..., :MLA_NOPE].reshape(DEPTH, MLA_QR, MLA_H * MLA_NOPE),
                          wq[..., MLA_NOPE:].reshape(DEPTH, MLA_QR, MLA_H * MLA_ROPE)], axis=-1).astype(BF16)
    wkv = mla_w_kv_up.reshape(DEPTH, MLA_KVR, MLA_H, MLA_NOPE + MLA_DV)
    wkv = jnp.concatenate([wkv[..., :MLA_NOPE].reshape(DEPTH, MLA_KVR, MLA_H * MLA_NOPE),
                           wkv[..., MLA_NOPE:].reshape(DEPTH, MLA_KVR, MLA_H * MLA_DV)], axis=-1).astype(BF16)
    router = jnp.pad(moe_router, ((0, 0), (0, 0), (0, LANE - N_EXP))).astype(BF16)
    dec_lanes = jnp.repeat(ret_decay, RET_DV, axis=-1)
    sink_lanes = jnp.repeat(gqa_sink, GQA_HD, axis=-1).reshape(DEPTH, 1, 256)
    gate_b = gla_gate_b.reshape(DEPTH, 2, 1, GLA_H * GLA_DK)
    g1, g2 = norm1_g.reshape(DEPTH, 1, D), norm2_g.reshape(DEPTH, 1, D)
    gq, gkv = mla_q_norm_g.reshape(DEPTH, 1, MLA_QR), mla_kv_norm_g.reshape(DEPTH, 1, MLA_KVR)
    eye = jnp.eye(GLA_H, dtype=F32)
    s0_gla = jnp.einsum('bldhkv,hg->bldhvgk', state_gla, eye).reshape(
        N_LAT, DEPTH, 2, GLA_H * GLA_DV, GLA_H * GLA_DK)
    s0_ret = jnp.einsum('bldhkv,hg->bldhkgv', state_ret, jnp.eye(RET_H, dtype=F32)).reshape(
        N_LAT, DEPTH, 2, RET_H * RET_DK, RET_H * RET_DV)
    cache_kr = jnp.pad(cache_mla_krope, ((0, 0), (0, 0), (0, 0), (0, 128 - MLA_ROPE)))

    rope64 = _rope_tables(T_LAT, 64, 256)
    rope32_q = _rope_tables(T_LAT, 32, 128)
    ck, sl, sh = _rope_tables(T_LAT, 32, 128)
    live = jnp.asarray((np.arange(128) < MLA_ROPE).astype(np.float32))[None]
    rope32_k = (ck * live, sl * live, sh * live)

    st_gla = st_ret = caches_mla = caches_gqa = None
    for l in range(DEPTH):
        p_gla, p_mla, p_ret, p_gqa = _inproj(xs, g1, mod, w_in, l)

        o_gla_c, st_gla, o_mla_c, ckv_all, kr_all, o_ret_c, st_ret, o_gqa_c, gk_all, gv_all = _fused_call([
            _gla(p_gla, gla_gate_w, gate_b, None, st_gla, l, False),
            _mla(p_mla, gq, gkv, wq, wkv, None, None, None, None, caches_mla, l, False),
            _ret(p_ret, dec_lanes, None, None, st_ret, l, False),
            _gqa(p_gqa, sink_lanes, None, None, None, caches_gqa, l, False)], N_CTX, "mixers_context")
        caches_mla, caches_gqa = (ckv_all, kr_all), (gk_all, gv_all)
        o_gla_s, o_ret_s = _fused_call([
            _gla(p_gla, gla_gate_w, gate_b, s0_gla, None, l, True),
            _ret(p_ret, dec_lanes, rope64, s0_ret, None, l, True)], N_LAT, "mixers_latent_scan")
        o_mla_s, o_gqa_s = _fused_call([
            _mla(p_mla, gq, gkv, wq, wkv, rope32_q, rope32_k, cache_mla_ckv, cache_kr, None, l, True),
            _gqa(p_gqa, sink_lanes, rope64, cache_gqa_k, cache_gqa_v, None, l, True)], N_LAT, "mixers_latent_attn")
        o_ctx = (o_gla_c, o_mla_c, o_ret_c, o_gqa_c)
        o_lat = (o_gla_s, o_mla_s, o_ret_s, o_gqa_s)
        if l % 2 == 0:
            x, h2 = _outproj(xs, o_ctx, o_lat, w_out_b, g2, mod, None, l)
            x = _ffn(h2, x, mod, ffn_w_gate, ffn_w_up, ffn_w_down, l)
        else:
            x, h2, gates, sel = _outproj(xs, o_ctx, o_lat, w_out_b, g2, mod, router, l)
            x = _moe(h2, x, mod, gates, sel, moe_w_gate, moe_w_up, moe_w_down, l)
        xs = (x,)

    y_ctx, y_lat = _final_norm(x, final_norm_g[None])
    return (y_ctx.reshape(N_CTX, T_CTX, D), y_lat.reshape(N_LAT, T_LAT, D), jnp.swapaxes(st_gla, -1, -2), st_ret,
            *caches_mla, *caches_gqa)
```
